```python
import math
import jax
import jax.numpy as jnp
from jax import lax
import numpy as np

D_MODEL = 2048
BATCH = 2
SEQ = 4096
DEPTH = 4
DEC_BATCH = 32
DEC_SEQ = 4
PAST_LEN = 16384
PAGE_SIZE = 128

N_MIXERS = 3
N_POOL_LAYERS = (DEPTH + 2) // 3
N_SWA_LAYERS = (DEPTH + 1) // 3
N_MLSTM_LAYERS = DEPTH // 3
DN_ALPHA = (2.0 * DEPTH) ** 0.25
DN_BETA = (8.0 * DEPTH) ** -0.25
LN_EPS = 1e-5

POOL_WINDOWS = (2, 4, 8, 16)
POOL_GROUPS = len(POOL_WINDOWS)
POOL_GROUP_DIM = D_MODEL // POOL_GROUPS
POOL_STATE = max(POOL_WINDOWS) - 1

SWA_WINDOW = 128
SWA_HEAD_DIM = 64
SWA_HEADS = D_MODEL // SWA_HEAD_DIM
SWA_KV_HEADS = SWA_HEADS // 8
SWA_GROUP = SWA_HEADS // SWA_KV_HEADS

MLSTM_HEADS = 8
MLSTM_HEAD_DIM = D_MODEL // MLSTM_HEADS
MLSTM_CHUNK = 64

N_EXPERTS = 64
TOP_K = 8
N_EXPERT_GROUPS = 8
TOPK_GROUPS = 4
EXPERT_FF = D_MODEL // 4
SHARED_FF = D_MODEL // 4
ROUTED_SCALE = 2.5
MOE_BLOCK = 128

kernel_name = 'hybrid_pool_swa_mlstm_moe_step'


def layer_norm(z, g, b):
    zf = z.astype(jnp.float32)
    mu = zf.mean(-1, keepdims=True)
    var = jnp.square(zf - mu).mean(-1, keepdims=True)
    return ((zf - mu) * lax.rsqrt(var + LN_EPS) * g.astype(jnp.float32) + b.astype(jnp.float32)).astype(z.dtype)


def alibi_slopes(n_heads):
    return 2.0 ** (-8.0 * (jnp.arange(n_heads, dtype=jnp.float32) + 1.0) / n_heads)


def pool_mixer(x, prev, n_prev, w, scale):
    B, T, D = x.shape
    P = POOL_STATE
    z = jnp.concatenate([prev.astype(x.dtype), x], axis=1)
    cs = jnp.cumsum(jnp.pad(z.astype(jnp.float32), ((0, 0), (1, 0), (0, 0))), axis=1)
    t = jnp.arange(T)
    diffs = []
    for g, win in enumerate(POOL_WINDOWS):
        c0, c1 = g * POOL_GROUP_DIM, (g + 1) * POOL_GROUP_DIM
        total = cs[:, P + 1:, c0:c1] - cs[:, P + 1 - win:P + 1 - win + T, c0:c1]
        count = jnp.minimum(win, n_prev + t + 1).astype(jnp.float32)
        diffs.append(total / count[None, :, None] - x[:, :, c0:c1].astype(jnp.float32))
    d = jnp.stack(diffs, axis=2)
    y = jnp.einsum('btgc,gce->btge', d, w.astype(jnp.float32)).reshape(B, T, D)
    y = y * scale.astype(jnp.float32)
    return y.astype(x.dtype), z[:, -P:]


def swa_mixer(x, k_prev, v_prev, pos0, w_qkv, w_o, sinks):
    B, T, _ = x.shape
    W, H, KV, G, dh = SWA_WINDOW, SWA_HEADS, SWA_KV_HEADS, SWA_GROUP, SWA_HEAD_DIM
    proj = jnp.einsum('btd,de->bte', x, w_qkv)
    q = proj[..., :H * dh].reshape(B, T, KV, G, dh)
    k = proj[..., H * dh:(H + KV) * dh].reshape(B, T, KV, dh)
    v = proj[..., (H + KV) * dh:].reshape(B, T, KV, dh)
    kz = jnp.concatenate([k_prev.astype(k.dtype), k], axis=1)
    vz = jnp.concatenate([v_prev.astype(v.dtype), v], axis=1)
    Q = math.gcd(T, W)
    nb = T // Q
    idx = jnp.arange(nb)[:, None] * Q + jnp.arange(W + Q)[None, :]
    kb = kz[:, idx]
    vb = vz[:, idx]
    qb = q.reshape(B, nb, Q, KV, G, dh)
    s = jnp.einsum('bnqkgd,bnskd->bnkgqs', qb, kb).astype(jnp.float32) * (dh ** -0.5)
    dist = (W + jnp.arange(Q))[:, None] - jnp.arange(W + Q)[None, :]
    slopes = alibi_slopes(H).reshape(KV, G, 1, 1)
    s = s - slopes * dist.astype(jnp.float32)
    mask = ((dist >= 0) & (dist <= W))[None] & ((pos0 - W + idx) >= 0)[:, None, :]
    s = jnp.where(mask[None, :, None, None], s, -jnp.inf)
    sink = jnp.broadcast_to(sinks.astype(jnp.float32).reshape(1, 1, KV, G, 1, 1), s.shape[:-1] + (1,))
    p = jax.nn.softmax(jnp.concatenate([s, sink], axis=-1), axis=-1)[..., :-1]
    o = jnp.einsum('bnkgqs,bnskd->bnqkgd', p.astype(vb.dtype), vb).reshape(B, T, H * dh)
    y = jnp.einsum('bte,ed->btd', o, w_o)
    return y, kz[:, -W:], vz[:, -W:]


def mlstm_chunkwise(q, k, v, ig, lf, c0, n0, m0):
    B, T, H, dh = q.shape
    L = math.gcd(T, MLSTM_CHUNK)
    nc = T // L

    def to_chunks(a):
        a = a.reshape((B, nc, L, H) + a.shape[3:])
        return jnp.moveaxis(a, (1, 3), (0, 2))

    causal = jnp.tril(jnp.ones((L, L), dtype=bool))

    def step(carry, xs):
        c, n, m = carry
        qc, kc, vc, ic, fc = xs
        b = jnp.cumsum(fc, axis=-1)
        a = b + m[..., None]
        dmat = jnp.where(causal, b[..., :, None] - b[..., None, :] + ic[..., None, :], -jnp.inf)
        mt = jnp.maximum(a, dmat.max(-1))
        w_inter = jnp.exp(a - mt)
        w_intra = jnp.exp(dmat - mt[..., None]) * jnp.einsum('bhtd,bhsd->bhts', qc, kc)
        num = w_inter[..., None] * jnp.einsum('bhvk,bhtk->bhtv', c, qc) + jnp.einsum('bhts,bhsv->bhtv', w_intra, vc)
        den = w_inter * jnp.einsum('bhk,bhtk->bht', n, qc) + w_intra.sum(-1)
        h = num / jnp.maximum(jnp.abs(den), jnp.exp(-mt))[..., None]
        b_last = b[..., -1]
        g = b_last[..., None] - b + ic
        m_new = jnp.maximum(b_last + m, g.max(-1))
        decay = jnp.exp(b_last + m - m_new)
        wg = jnp.exp(g - m_new[..., None])
        c_new = decay[..., None, None] * c + jnp.einsum('bhs,bhsv,bhsk->bhvk', wg, vc, kc)
        n_new = decay[..., None] * n + jnp.einsum('bhs,bhsk->bhk', wg, kc)
        return (c_new, n_new, m_new), h

    xs = (to_chunks(q), to_chunks(k), to_chunks(v), to_chunks(ig), to_chunks(lf))
    (c, n, m), h = lax.scan(step, (c0, n0, m0), xs)
    h = jnp.moveaxis(h, (0, 2), (1, 3)).reshape(B, T, H, dh)
    return h, c, n, m


def mlstm_mixer(x, c0, n0, m0, w_in, b_gates, norm_g, w_out):
    B, T, D = x.shape
    H, dh = MLSTM_HEADS, MLSTM_HEAD_DIM
    proj = jnp.einsum('btd,de->bte', x, w_in).astype(jnp.float32)
    q = proj[..., 0 * D:1 * D].reshape(B, T, H, dh)
    k = proj[..., 1 * D:2 * D].reshape(B, T, H, dh) * (dh ** -0.5)
    v = proj[..., 2 * D:3 * D].reshape(B, T, H, dh)
    o = proj[..., 3 * D:4 * D]
    gates = proj[..., 4 * D:] + b_gates.astype(jnp.float32)
    ig = gates[..., :H]
    lf = jax.nn.log_sigmoid(gates[..., H:])
    h, c, n, m = mlstm_chunkwise(q, k, v, ig, lf, c0.astype(jnp.float32), n0.astype(jnp.float32),
                                 m0.astype(jnp.float32))
    mu = h.mean(-1, keepdims=True)
    var = jnp.square(h - mu).mean(-1, keepdims=True)
    hn = ((h - mu) * lax.rsqrt(var + LN_EPS)).reshape(B, T, D) * norm_g.astype(jnp.float32)
    hn = hn * jax.nn.sigmoid(o)
    y = jnp.einsum('bte,ed->btd', hn.astype(x.dtype), w_out)
    return y, c.astype(c0.dtype), n.astype(n0.dtype), m.astype(m0.dtype)


def swiglu(x, wg, wu, wd):
    return (jax.nn.silu(x @ wg) * (x @ wu)) @ wd


def moe_route(xt, w_router, bias):
    N = xt.shape[0]
    per_group = N_EXPERTS // N_EXPERT_GROUPS
    s = jax.nn.sigmoid(jnp.einsum('nd,de->ne', xt.astype(jnp.float32), w_router.astype(jnp.float32)))
    sb = s + bias.astype(jnp.float32)
    group_score = lax.top_k(sb.reshape(N, N_EXPERT_GROUPS, per_group), 2)[0].sum(-1)
    _, gidx = lax.top_k(group_score, TOPK_GROUPS)
    gmask = (gidx[:, :, None] == jnp.arange(N_EXPERT_GROUPS)[None, None, :]).any(1)
    emask = jnp.repeat(gmask, per_group, axis=1)
    _, eidx = lax.top_k(jnp.where(emask, sb, -jnp.inf), TOP_K)
    gate = jnp.take_along_axis(s, eidx, axis=1)
    gate = gate / gate.sum(-1, keepdims=True) * ROUTED_SCALE
    return eidx, gate


def moe_dispatch(xt, eidx, gate, w_gate, w_up, w_down):
    N, D = xt.shape
    A = N * TOP_K
    flat_e = eidx.reshape(-1)
    flat_tok = jnp.arange(A) // TOP_K
    flat_gate = gate.reshape(-1)
    order = jnp.argsort(flat_e)
    se = flat_e[order]
    counts = jnp.bincount(flat_e, length=N_EXPERTS)
    starts = jnp.cumsum(counts) - counts
    pcounts = (counts + MOE_BLOCK - 1) // MOE_BLOCK * MOE_BLOCK
    pends = jnp.cumsum(pcounts)
    pstarts = pends - pcounts
    dest = pstarts[se] + jnp.arange(A) - starts[se]
    n_blocks = -(-A // MOE_BLOCK) + N_EXPERTS
    P = n_blocks * MOE_BLOCK
    row_tok = jnp.full((P,), N, dtype=jnp.int32).at[dest].set(flat_tok[order].astype(jnp.int32))
    row_gate = jnp.zeros((P,), jnp.float32).at[dest].set(flat_gate[order])
    block_e = jnp.clip(jnp.searchsorted(pends, jnp.arange(n_blocks) * MOE_BLOCK, side='right'), 0, N_EXPERTS - 1)
    xpad = jnp.concatenate([xt, jnp.zeros((1, D), xt.dtype)], axis=0)
    xb = xpad[row_tok].reshape(n_blocks, MOE_BLOCK, D)

    def expert_block(args):
        e, xblk = args
        return swiglu(xblk, w_gate[e], w_up[e], w_down[e])

    yb = lax.map(expert_block, (block_e, xb)).reshape(P, D)
    out = jnp.zeros((N + 1, D), yb.dtype).at[row_tok].add(yb * row_gate[:, None].astype(yb.dtype))
    return out[:N]


def moe(xt, w_router, bias, w_gate, w_up, w_down, sw_gate, sw_up, sw_down):
    eidx, gate = moe_route(xt, w_router, bias)
    routed = moe_dispatch(xt, eidx, gate, w_gate, w_up, w_down)
    return routed.astype(xt.dtype) + swiglu(xt, sw_gate, sw_up, sw_down)


def setup_inputs(seed: int = 0) -> dict:
    key = jax.random.key(seed)
    ks = iter(jax.random.split(key, 40))
    nrm = lambda shape, scale: jax.random.normal(next(ks), shape, jnp.float32) * scale
    D = D_MODEL
    H, KV, dh = SWA_HEADS, SWA_KV_HEADS, SWA_HEAD_DIM
    MH, MD = MLSTM_HEADS, MLSTM_HEAD_DIM
    x_prompt = nrm((BATCH, SEQ, D), 1.0)
    x_sample = nrm((DEC_BATCH, DEC_SEQ, D), 1.0)
    state_pool = nrm((N_POOL_LAYERS, DEC_BATCH, POOL_STATE, D), 1.0)
    cache_swa_k = nrm((N_SWA_LAYERS, DEC_BATCH, SWA_WINDOW, KV, dh), 1.0)
    cache_swa_v = nrm((N_SWA_LAYERS, DEC_BATCH, SWA_WINDOW, KV, dh), DN_BETA)
    state_mlstm_c = nrm((N_MLSTM_LAYERS, DEC_BATCH, MH, MD, MD), 0.05)
    state_mlstm_n = nrm((N_MLSTM_LAYERS, DEC_BATCH, MH, MD), 0.05)
    state_mlstm_m = nrm((N_MLSTM_LAYERS, DEC_BATCH, MH), 0.5)
    pool_w = nrm((N_POOL_LAYERS, POOL_GROUPS, POOL_GROUP_DIM, POOL_GROUP_DIM), POOL_GROUP_DIM ** -0.5 * DN_BETA)
    pool_scale = 1.0 + nrm((N_POOL_LAYERS, D), 0.02)
    swa_w_qkv = jnp.concatenate([nrm((N_SWA_LAYERS, D, (H + KV) * dh), D ** -0.5),
                                 nrm((N_SWA_LAYERS, D, KV * dh), D ** -0.5 * DN_BETA)], axis=-1)
    swa_w_o = nrm((N_SWA_LAYERS, H * dh, D), (H * dh) ** -0.5 * DN_BETA)
    swa_sinks = nrm((N_SWA_LAYERS, H), 0.5)
    mlstm_w_in = jnp.concatenate([nrm((N_MLSTM_LAYERS, D, 2 * D), D ** -0.5),
                                  nrm((N_MLSTM_LAYERS, D, D), D ** -0.5 * DN_BETA),
                                  nrm((N_MLSTM_LAYERS, D, D + 2 * MH), D ** -0.5)], axis=-1)
    mlstm_b_gates = jnp.concatenate([nrm((N_MLSTM_LAYERS, MH), 0.1),
                                     jnp.linspace(3.0, 6.0, MH)[None, :] + nrm((N_MLSTM_LAYERS, MH), 0.1)], axis=-1)
    mlstm_norm_g = 1.0 + nrm((N_MLSTM_LAYERS, D), 0.02)
    mlstm_w_out = nrm((N_MLSTM_LAYERS, D, D), D ** -0.5 * DN_BETA)
    ln_g = 1.0 + nrm((DEPTH, 2, D), 0.02)
    ln_b = nrm((DEPTH, 2, D), 0.02)
    moe_w_router = nrm((DEPTH, D, N_EXPERTS), D ** -0.5)
    moe_router_bias = nrm((DEPTH, N_EXPERTS), 0.01)
    moe_w_gate = nrm((DEPTH, N_EXPERTS, D, EXPERT_FF), D ** -0.5)
    moe_w_up = nrm((DEPTH, N_EXPERTS, D, EXPERT_FF), D ** -0.5)
    moe_w_down = nrm((DEPTH, N_EXPERTS, EXPERT_FF, D), EXPERT_FF ** -0.5 * DN_BETA)
    moe_shared_w_gate = nrm((DEPTH, D, SHARED_FF), D ** -0.5)
    moe_shared_w_up = nrm((DEPTH, D, SHARED_FF), D ** -0.5)
    moe_shared_w_down = nrm((DEPTH, SHARED_FF, D), SHARED_FF ** -0.5 * DN_BETA)
    return {'x_prompt': x_prompt, 'x_sample': x_sample, 'state_pool': state_pool,
            'cache_swa_k': cache_swa_k, 'cache_swa_v': cache_swa_v,
            'state_mlstm_c': state_mlstm_c, 'state_mlstm_n': state_mlstm_n, 'state_mlstm_m': state_mlstm_m,
            'pool_w': pool_w, 'pool_scale': pool_scale,
            'swa_w_qkv': swa_w_qkv, 'swa_w_o': swa_w_o, 'swa_sinks': swa_sinks,
            'mlstm_w_in': mlstm_w_in, 'mlstm_b_gates': mlstm_b_gates, 'mlstm_norm_g': mlstm_norm_g,
            'mlstm_w_out': mlstm_w_out, 'ln_g': ln_g, 'ln_b': ln_b,
            'moe_w_router': moe_w_router, 'moe_router_bias': moe_router_bias,
            'moe_w_gate': moe_w_gate, 'moe_w_up': moe_w_up, 'moe_w_down': moe_w_down,
            'moe_shared_w_gate': moe_shared_w_gate, 'moe_shared_w_up': moe_shared_w_up,
            'moe_shared_w_down': moe_shared_w_down}


def reference(x_prompt, x_sample, state_pool, cache_swa_k, cache_swa_v, state_mlstm_c, state_mlstm_n,
              state_mlstm_m, pool_w, pool_scale, swa_w_qkv, swa_w_o, swa_sinks, mlstm_w_in, mlstm_b_gates,
              mlstm_norm_g, mlstm_w_out, ln_g, ln_b, moe_w_router, moe_router_bias, moe_w_gate, moe_w_up,
              moe_w_down, moe_shared_w_gate, moe_shared_w_up, moe_shared_w_down):
    D = D_MODEL
    xp, xs = x_prompt, x_sample
    Bp, Bs = xp.shape[0], xs.shape[0]
    pool_p, pool_s = [], []
    swk_p, swv_p, swk_s, swv_s = [], [], [], []
    mc_p, mn_p, mm_p, mc_s, mn_s, mm_s = [], [], [], [], [], []
    for i in range(DEPTH):
        kind, slot = i % N_MIXERS, i // N_MIXERS
        if kind == 0:
            hp, sp = pool_mixer(xp, jnp.zeros((Bp, POOL_STATE, D), xp.dtype), 0, pool_w[slot], pool_scale[slot])
            hs, ss = pool_mixer(xs, state_pool[slot], PAST_LEN, pool_w[slot], pool_scale[slot])
            pool_p.append(sp)
            pool_s.append(ss)
        elif kind == 1:
            kv0 = jnp.zeros((Bp, SWA_WINDOW, SWA_KV_HEADS, SWA_HEAD_DIM), xp.dtype)
            hp, kp, vp = swa_mixer(xp, kv0, kv0, 0, swa_w_qkv[slot], swa_w_o[slot], swa_sinks[slot])
            hs, ks_, vs_ = swa_mixer(xs, cache_swa_k[slot], cache_swa_v[slot], PAST_LEN,
                                     swa_w_qkv[slot], swa_w_o[slot], swa_sinks[slot])
            swk_p.append(kp)
            swv_p.append(vp)
            swk_s.append(ks_)
            swv_s.append(vs_)
        else:
            c0 = jnp.zeros((Bp, MLSTM_HEADS, MLSTM_HEAD_DIM, MLSTM_HEAD_DIM), xp.dtype)
            n0 = jnp.zeros((Bp, MLSTM_HEADS, MLSTM_HEAD_DIM), xp.dtype)
            m0 = jnp.zeros((Bp, MLSTM_HEADS), xp.dtype)
            hp, cp, np_, mp = mlstm_mixer(xp, c0, n0, m0, mlstm_w_in[slot], mlstm_b_gates[slot],
                                          mlstm_norm_g[slot], mlstm_w_out[slot])
            hs, cs_, ns_, ms_ = mlstm_mixer(xs, state_mlstm_c[slot], state_mlstm_n[slot], state_mlstm_m[slot],
                                            mlstm_w_in[slot], mlstm_b_gates[slot], mlstm_norm_g[slot],
                                            mlstm_w_out[slot])
            mc_p.append(cp)
            mn_p.append(np_)
            mm_p.append(mp)
            mc_s.append(cs_)
            mn_s.append(ns_)
            mm_s.append(ms_)
        xp = layer_norm(DN_ALPHA * xp + hp.astype(xp.dtype), ln_g[i, 0], ln_b[i, 0])
        xs = layer_norm(DN_ALPHA * xs + hs.astype(xs.dtype), ln_g[i, 0], ln_b[i, 0])
        n_p = xp.shape[0] * xp.shape[1]
        xt = jnp.concatenate([xp.reshape(-1, D), xs.reshape(-1, D)], axis=0)
        ft = moe(xt, moe_w_router[i], moe_router_bias[i], moe_w_gate[i], moe_w_up[i], moe_w_down[i],
                 moe_shared_w_gate[i], moe_shared_w_up[i], moe_shared_w_down[i])
        xt = layer_norm(DN_ALPHA * xt + ft.astype(xt.dtype), ln_g[i, 1], ln_b[i, 1])
        xp = xt[:n_p].reshape(xp.shape)
        xs = xt[n_p:].reshape(xs.shape)
    return (xp, xs, jnp.stack(pool_p), jnp.stack(pool_s), jnp.stack(swk_p), jnp.stack(swv_p),
            jnp.stack(swk_s), jnp.stack(swv_s), jnp.stack(mc_p), jnp.stack(mn_p), jnp.stack(mm_p),
            jnp.stack(mc_s), jnp.stack(mn_s), jnp.stack(mm_s))
```

```python
import functools
import math

import jax
import jax.numpy as jnp
from jax import lax
from jax.experimental import pallas as pl
from jax.experimental.pallas import tpu as pltpu

F32 = jnp.float32
BF16 = jnp.bfloat16

D_MODEL = 2048
BATCH = 2
SEQ = 4096
DEPTH = 4
DEC_BATCH = 32
DEC_SEQ = 4
PAST_LEN = 16384
N_PROMPT_TOK = BATCH * SEQ
N_SAMPLE_TOK = DEC_BATCH * DEC_SEQ
N_TOK = N_PROMPT_TOK + N_SAMPLE_TOK

N_MIXERS = 3
DN_ALPHA = (2.0 * DEPTH) ** 0.25
LN_EPS = 1e-5

POOL_WINDOWS = (2, 4, 8, 16)
POOL_GROUP_DIM = D_MODEL // len(POOL_WINDOWS)
POOL_STATE = max(POOL_WINDOWS) - 1
POOL_HALO = POOL_STATE + 1

SWA_WINDOW = 128
SWA_HEAD_DIM = 64
SWA_HEADS = D_MODEL // SWA_HEAD_DIM
SWA_KV_HEADS = SWA_HEADS // 8
SWA_GROUP = SWA_HEADS // SWA_KV_HEADS
SWA_KV_DIM = SWA_KV_HEADS * SWA_HEAD_DIM

MLSTM_HEADS = 8
MLSTM_HEAD_DIM = D_MODEL // MLSTM_HEADS
MLSTM_CHUNK = 64
MLSTM_SAMPLE_CHUNK = 8

N_EXPERTS = 64
TOP_K = 8
N_EXPERT_GROUPS = 8
TOPK_GROUPS = 4
EXPERT_FF = D_MODEL // 4
ROUTED_SCALE = 2.5

VMEM_LIMIT_BYTES = 56 * 1024 * 1024

ROW_TILE = 320
ROUTER_TILE = 640
POOL_TILE = 512
MOE_BLOCK = 256
MOE_N_BLOCKS = N_TOK * TOP_K // MOE_BLOCK + N_EXPERTS
MOE_ROWS = MOE_N_BLOCKS * MOE_BLOCK
DISPATCH_TILE = 128
COMBINE_TILE = 64

_NT = (((1,), (1,)), ((), ()))
_TN = (((0,), (0,)), ((), ()))


def _params(*semantics):
    return pltpu.CompilerParams(dimension_semantics=semantics, vmem_limit_bytes=VMEM_LIMIT_BYTES)


def _layer_norm(z, g, b):
    mu = jnp.mean(z, axis=-1, keepdims=True)
    zc = z - mu
    var = jnp.mean(zc * zc, axis=-1, keepdims=True)
    return zc * lax.rsqrt(var + LN_EPS) * g + b


def _silu(x):
    return x * jax.nn.sigmoid(x)


def _matmul_kernel(x_ref, w_ref, o_ref, *, precision):
    if precision is None:
        o_ref[...] = jnp.dot(x_ref[...].astype(BF16), w_ref[...], preferred_element_type=F32)
    else:
        o_ref[...] = jnp.dot(x_ref[...], w_ref[...], preferred_element_type=F32, precision=precision)


def _matmul(x, w, *, tm, tn, precision=None):
    m, k = x.shape
    n = w.shape[1]
    return pl.pallas_call(
        functools.partial(_matmul_kernel, precision=precision),
        grid=(n // tn, m // tm),
        in_specs=[pl.BlockSpec((tm, k), lambda j, i: (i, 0)),
                  pl.BlockSpec((k, tn), lambda j, i: (0, j))],
        out_specs=pl.BlockSpec((tm, tn), lambda j, i: (i, j)),
        out_shape=jax.ShapeDtypeStruct((m, n), F32),
        compiler_params=_params("parallel", "parallel"),
        name="matmul",
    )(x, w)


def _matmul_ln_kernel(a_ref, w_ref, res_ref, g_ref, b_ref, o_ref):
    y = jnp.dot(a_ref[...].astype(BF16), w_ref[...], preferred_element_type=F32)
    o_ref[...] = _layer_norm(DN_ALPHA * res_ref[...] + y, g_ref[...], b_ref[...])


def _matmul_ln(a, w, res, g, b):
    m, k = a.shape
    row = lambda i: (i, 0)
    const = lambda i: (0, 0)
    return pl.pallas_call(
        _matmul_ln_kernel,
        grid=(m // ROW_TILE,),
        in_specs=[pl.BlockSpec((ROW_TILE, k), row),
                  pl.BlockSpec((k, D_MODEL), const),
                  pl.BlockSpec((ROW_TILE, D_MODEL), row),
                  pl.BlockSpec((1, D_MODEL), const),
                  pl.BlockSpec((1, D_MODEL), const)],
        out_specs=pl.BlockSpec((ROW_TILE, D_MODEL), row),
        out_shape=jax.ShapeDtypeStruct((m, D_MODEL), F32),
        compiler_params=_params("parallel"),
        name="matmul_ln",
    )(a, w, res, g, b)


def _pool_core(zbuf, w_ref, scale_ref, g_ref, b_ref, o_ref, *, tt, n_before):
    avail = n_before + lax.broadcasted_iota(jnp.int32, (tt, 1), 0) + 1
    for grp, win in enumerate(POOL_WINDOWS):
        c0, c1 = grp * POOL_GROUP_DIM, (grp + 1) * POOL_GROUP_DIM
        xg = zbuf[POOL_HALO:POOL_HALO + tt, c0:c1]
        total = xg
        for back in range(1, win):
            total = total + zbuf[POOL_HALO - back:POOL_HALO - back + tt, c0:c1]
        count = jnp.minimum(win, avail).astype(F32)
        diff = total / count - xg
        y = jnp.dot(diff.astype(BF16), w_ref[grp], preferred_element_type=F32) * scale_ref[:, c0:c1]
        o_ref[:, c0:c1] = DN_ALPHA * xg + y
    o_ref[...] = _layer_norm(o_ref[...], g_ref[...], b_ref[...])


def _pool_prompt_kernel(x_ref, halo_ref, w_ref, scale_ref, g_ref, b_ref, o_ref, zbuf, *, tt):
    i = pl.program_id(1)
    zbuf[0:POOL_HALO, :] = jnp.where(i == 0, 0.0, halo_ref[...])
    zbuf[POOL_HALO:POOL_HALO + tt, :] = x_ref[...]
    _pool_core(zbuf, w_ref, scale_ref, g_ref, b_ref, o_ref, tt=tt, n_before=i * tt)


def _pool_sample_kernel(z_ref, w_ref, scale_ref, g_ref, b_ref, o_ref, *, tt):
    _pool_core(z_ref.at[0], w_ref, scale_ref, g_ref, b_ref, o_ref.at[0], tt=tt, n_before=PAST_LEN)


def _pool_layer(xt, state, w, scale, g, b):
    tt = POOL_TILE
    tiles = SEQ // tt
    halo_per_tile = tt // POOL_HALO
    const2 = lambda bb, i: (0, 0)
    w_bf = w.astype(BF16)
    out = pl.pallas_call(
        functools.partial(_pool_prompt_kernel, tt=tt),
        grid=(BATCH, tiles),
        in_specs=[pl.BlockSpec((tt, D_MODEL), lambda bb, i: (bb * tiles + i, 0)),
                  pl.BlockSpec((POOL_HALO, D_MODEL),
                               lambda bb, i: (jnp.maximum((bb * tiles + i) * halo_per_tile - 1, 0), 0)),
                  pl.BlockSpec(w_bf.shape, lambda bb, i: (0, 0, 0)),
                  pl.BlockSpec((1, D_MODEL), const2),
                  pl.BlockSpec((1, D_MODEL), const2),
                  pl.BlockSpec((1, D_MODEL), const2)],
        out_specs=pl.BlockSpec((tt, D_MODEL), lambda bb, i: (bb * tiles + i, 0)),
        out_shape=jax.ShapeDtypeStruct((N_TOK, D_MODEL), F32),
        scratch_shapes=[pltpu.VMEM((POOL_HALO + tt, D_MODEL), F32)],
        compiler_params=_params("parallel", "arbitrary"),
        name="pool_prompt",
    )(xt, xt, w_bf, scale, g, b)

    ts = 16
    xs = xt[N_PROMPT_TOK:].reshape(DEC_BATCH, DEC_SEQ, D_MODEL)
    zs = jnp.concatenate([jnp.zeros((DEC_BATCH, 1, D_MODEL), F32), state, xs,
                          jnp.zeros((DEC_BATCH, ts - DEC_SEQ, D_MODEL), F32)], axis=1)
    const1 = lambda bb: (0, 0)
    out_s = pl.pallas_call(
        functools.partial(_pool_sample_kernel, tt=ts),
        grid=(DEC_BATCH,),
        in_specs=[pl.BlockSpec((1, POOL_HALO + ts, D_MODEL), lambda bb: (bb, 0, 0)),
                  pl.BlockSpec(w_bf.shape, lambda bb: (0, 0, 0)),
                  pl.BlockSpec((1, D_MODEL), const1),
                  pl.BlockSpec((1, D_MODEL), const1),
                  pl.BlockSpec((1, D_MODEL), const1)],
        out_specs=pl.BlockSpec((1, ts, D_MODEL), lambda bb: (bb, 0, 0)),
        out_shape=jax.ShapeDtypeStruct((DEC_BATCH, ts, D_MODEL), F32),
        compiler_params=_params("parallel"),
        name="pool_sample",
    )(zs, w_bf, scale, g, b)
    out = lax.dynamic_update_slice(out, out_s[:, :DEC_SEQ].reshape(N_SAMPLE_TOK, D_MODEL), (N_PROMPT_TOK, 0))

    new_p = xt[:N_PROMPT_TOK].reshape(BATCH, SEQ, D_MODEL)[:, SEQ - POOL_STATE:]
    new_s = jnp.concatenate([state, xs], axis=1)[:, DEC_SEQ:]
    return out, new_p, new_s


def _alibi_slope(h):
    return 2.0 ** (-8.0 * (h + 1.0) / SWA_HEADS)


def _attn_core(q_ref, k_all, v_all, sink_ref, o_ref, *, rows, first_block):
    w = SWA_WINDOW
    qi = lax.broadcasted_iota(jnp.int32, (rows, 2 * w), 0)
    sj = lax.broadcasted_iota(jnp.int32, (rows, 2 * w), 1)
    dist = (w + qi) - sj
    valid = (dist >= 0) & (dist <= w)
    if first_block is not None:
        valid = valid & ((sj >= w) | jnp.logical_not(first_block))
    distf = dist.astype(F32)
    for h in range(SWA_HEADS):
        kv = h // SWA_GROUP
        c0, c1 = kv * SWA_HEAD_DIM, (kv + 1) * SWA_HEAD_DIM
        q = q_ref[:, h * SWA_HEAD_DIM:(h + 1) * SWA_HEAD_DIM].astype(BF16)
        s = lax.dot_general(q, k_all[:, c0:c1], _NT, preferred_element_type=F32) * (SWA_HEAD_DIM ** -0.5)
        s = jnp.where(valid, s - _alibi_slope(h) * distf, -jnp.inf)
        sink = sink_ref[h]
        m = jnp.maximum(jnp.max(s, axis=1, keepdims=True), sink)
        e = jnp.exp(s - m)
        den = jnp.sum(e, axis=1, keepdims=True) + jnp.exp(sink - m)
        o = jnp.dot(e.astype(BF16), v_all[:, c0:c1], preferred_element_type=F32) / den
        o_ref[:, h * SWA_HEAD_DIM:(h + 1) * SWA_HEAD_DIM] = o


def _attn_prompt_kernel(sink_ref, q_ref, kp_ref, kc_ref, vp_ref, vc_ref, o_ref):
    n = pl.program_id(1)
    k_all = jnp.concatenate([kp_ref[...], kc_ref[...]], axis=0).astype(BF16)
    v_all = jnp.concatenate([vp_ref[...], vc_ref[...]], axis=0).astype(BF16)
    _attn_core(q_ref, k_all, v_all, sink_ref, o_ref, rows=SWA_WINDOW, first_block=(n == 0))


def _attn_sample_kernel(sink_ref, q_ref, k_ref, v_ref, o_ref, *, rows):
    _attn_core(q_ref.at[0], k_ref[0].astype(BF16), v_ref[0].astype(BF16), sink_ref, o_ref.at[0],
               rows=rows, first_block=None)


def _swa_layer(xt, cache_k, cache_v, w_qkv, w_o, sinks, g, b):
    w = SWA_WINDOW
    qkv = _matmul(xt, w_qkv.astype(BF16), tm=640, tn=1280)
    nb = SEQ // w
    k_col = D_MODEL // SWA_KV_DIM
    cur = lambda bb, n, *_: (bb * nb + n, k_col)
    prev = lambda bb, n, *_: (jnp.maximum(bb * nb + n - 1, 0), k_col)
    cur_v = lambda bb, n, *_: (bb * nb + n, k_col + 1)
    prev_v = lambda bb, n, *_: (jnp.maximum(bb * nb + n - 1, 0), k_col + 1)
    o = pl.pallas_call(
        _attn_prompt_kernel,
        grid_spec=pltpu.PrefetchScalarGridSpec(
            num_scalar_prefetch=1,
            grid=(BATCH, nb),
            in_specs=[pl.BlockSpec((w, D_MODEL), lambda bb, n, *_: (bb * nb + n, 0)),
                      pl.BlockSpec((w, SWA_KV_DIM), prev),
                      pl.BlockSpec((w, SWA_KV_DIM), cur),
                      pl.BlockSpec((w, SWA_KV_DIM), prev_v),
                      pl.BlockSpec((w, SWA_KV_DIM), cur_v)],
            out_specs=pl.BlockSpec((w, D_MODEL), lambda bb, n, *_: (bb * nb + n, 0))),
        out_shape=jax.ShapeDtypeStruct((N_TOK, D_MODEL), F32),
        compiler_params=_params("parallel", "arbitrary"),
        name="attn_prompt",
    )(sinks, qkv, qkv, qkv, qkv, qkv)

    rows = 16
    qkv_s = qkv[N_PROMPT_TOK:].reshape(DEC_BATCH, DEC_SEQ, -1)
    q_s = jnp.pad(qkv_s[..., :D_MODEL], ((0, 0), (0, rows - DEC_SEQ), (0, 0)))
    k_new = qkv_s[..., D_MODEL:D_MODEL + SWA_KV_DIM]
    v_new = qkv_s[..., D_MODEL + SWA_KV_DIM:]
    kz = jnp.concatenate([cache_k.reshape(DEC_BATCH, w, SWA_KV_DIM), k_new], axis=1)
    vz = jnp.concatenate([cache_v.reshape(DEC_BATCH, w, SWA_KV_DIM), v_new], axis=1)
    pad_keys = ((0, 0), (0, w - DEC_SEQ), (0, 0))
    o_s = pl.pallas_call(
        functools.partial(_attn_sample_kernel, rows=rows),
        grid_spec=pltpu.PrefetchScalarGridSpec(
            num_scalar_prefetch=1,
            grid=(DEC_BATCH,),
            in_specs=[pl.BlockSpec((1, rows, D_MODEL), lambda bb, *_: (bb, 0, 0)),
                      pl.BlockSpec((1, 2 * w, SWA_KV_DIM), lambda bb, *_: (bb, 0, 0)),
                      pl.BlockSpec((1, 2 * w, SWA_KV_DIM), lambda bb, *_: (bb, 0, 0))],
            out_specs=pl.BlockSpec((1, rows, D_MODEL), lambda bb, *_: (bb, 0, 0))),
        out_shape=jax.ShapeDtypeStruct((DEC_BATCH, rows, D_MODEL), F32),
        compiler_params=_params("parallel"),
        name="attn_sample",
    )(sinks, q_s, jnp.pad(kz, pad_keys), jnp.pad(vz, pad_keys))
    o = lax.dynamic_update_slice(o, o_s[:, :DEC_SEQ].reshape(N_SAMPLE_TOK, D_MODEL), (N_PROMPT_TOK, 0))

    out = _matmul_ln(o, w_o.astype(BF16), xt, g, b)

    kv_shape = (SWA_WINDOW, SWA_KV_HEADS, SWA_HEAD_DIM)
    qkv_p = qkv[:N_PROMPT_TOK].reshape(BATCH, SEQ, -1)[:, SEQ - w:]
    new_k_p = qkv_p[..., D_MODEL:D_MODEL + SWA_KV_DIM].reshape((BATCH,) + kv_shape)
    new_v_p = qkv_p[..., D_MODEL + SWA_KV_DIM:].reshape((BATCH,) + kv_shape)
    new_k_s = kz[:, DEC_SEQ:].reshape((DEC_BATCH,) + kv_shape)
    new_v_s = vz[:, DEC_SEQ:].reshape((DEC_BATCH,) + kv_shape)
    return out, new_k_p, new_v_p, new_k_s, new_v_s


def _log_sigmoid(x):
    return jnp.minimum(x, 0.0) - jnp.log(1.0 + jnp.exp(-jnp.abs(x)))


def _mlstm_kernel(bias_ref, q_ref, k_ref, v_ref, og_ref, gi_ref, gf_ref, c0_ref, n0_ref, m0_ref, ng_ref,
                  h_ref, c_ref, n_ref, m_ref, c_scr, n_scr, m_scr, *, chunk, n_valid):
    head = pl.program_id(1)
    step = pl.program_id(2)
    ln = chunk

    @pl.when(step == 0)
    def _():
        c_scr[...] = c0_ref[0, 0]
        n_scr[...] = n0_ref[0, 0]
        m_scr[...] = m0_ref[0, 0]

    ig = gi_ref[0, 0, pl.ds(step, 1), :] + bias_ref[head]
    lf = _log_sigmoid(gf_ref[0, 0, pl.ds(step, 1), :] + bias_ref[MLSTM_HEADS + head])
    if n_valid < ln:
        col = lax.broadcasted_iota(jnp.int32, (1, ln), 1)
        ig = jnp.where(col < n_valid, ig, -1e30)
        lf = jnp.where(col < n_valid, lf, 0.0)

    ri = lax.broadcasted_iota(jnp.int32, (ln, ln), 0)
    ci = lax.broadcasted_iota(jnp.int32, (ln, ln), 1)
    eye = ri == ci
    causal = ci <= ri

    def to_col(row):
        return jnp.sum(jnp.where(eye, row, 0.0), axis=1, keepdims=True)

    f_col = to_col(lf)
    b_col = jnp.sum(jnp.where(causal, lf, 0.0), axis=1, keepdims=True)
    b_row = jnp.sum(jnp.where(ri <= ci, f_col, 0.0), axis=0, keepdims=True)
    b_last = jnp.sum(lf, axis=1, keepdims=True)
    m_prev = m_scr[:, 0:1]
    a_col = b_col + m_prev
    dmat = jnp.where(causal, b_col - b_row + ig, -jnp.inf)
    mt = jnp.maximum(a_col, jnp.max(dmat, axis=1, keepdims=True))
    w_inter = jnp.exp(a_col - mt)

    q = q_ref[...]
    k = k_ref[...] * (MLSTM_HEAD_DIM ** -0.5)
    v = v_ref[...]
    qb = q.astype(BF16)
    kb = k.astype(BF16)
    c_prev = c_scr[...]
    n_prev = n_scr[...]
    scores = lax.dot_general(qb, kb, _NT, preferred_element_type=F32)
    w_intra = jnp.exp(dmat - mt) * scores
    num = (w_inter * lax.dot_general(qb, c_prev.astype(BF16), _NT, preferred_element_type=F32)
           + jnp.dot(w_intra.astype(BF16), v.astype(BF16), preferred_element_type=F32))
    den = (w_inter * jnp.sum(q * n_prev, axis=1, keepdims=True)
           + jnp.sum(w_intra, axis=1, keepdims=True))
    h = num / jnp.maximum(jnp.abs(den), jnp.exp(-mt))

    g_row = b_last - b_row + ig
    m_new = jnp.maximum(b_last + m_prev, jnp.max(g_row, axis=1, keepdims=True))
    decay = jnp.exp(b_last + m_prev - m_new)
    wg_col = to_col(jnp.exp(g_row - m_new))
    c_new = decay * c_prev + lax.dot_general((v * wg_col).astype(BF16), kb, _TN, preferred_element_type=F32)
    n_new = decay * n_prev + jnp.sum(wg_col * k, axis=0, keepdims=True)
    m_new_b = jnp.broadcast_to(m_new, m_scr.shape)
    c_scr[...] = c_new
    n_scr[...] = n_new
    m_scr[...] = m_new_b

    mu = jnp.mean(h, axis=1, keepdims=True)
    hc = h - mu
    var = jnp.mean(hc * hc, axis=1, keepdims=True)
    h_ref[...] = hc * lax.rsqrt(var + LN_EPS) * ng_ref[...] * jax.nn.sigmoid(og_ref[...])

    @pl.when(step == pl.num_programs(2) - 1)
    def _():
        c_ref[0, 0] = c_new
        n_ref[0, 0] = n_new
        m_ref[0, 0] = m_new_b


def _mlstm_scan(proj, gates, b_gates, norm_g, c0, n0, m0, *, batch, n_chunks, chunk, n_valid, out_rows):
    dh, nh = MLSTM_HEAD_DIM, MLSTM_HEADS
    rows = lambda off: (lambda bb, hh, cc, *_: (bb * n_chunks + cc, off + hh))
    per_head = lambda bb, hh, cc, *_: (bb, hh, 0, 0)
    gate_i = lambda bb, hh, cc, *_: (bb, hh, 0, 0)
    gate_f = lambda bb, hh, cc, *_: (bb, nh + hh, 0, 0)
    return pl.pallas_call(
        functools.partial(_mlstm_kernel, chunk=chunk, n_valid=n_valid),
        grid_spec=pltpu.PrefetchScalarGridSpec(
            num_scalar_prefetch=1,
            grid=(batch, nh, n_chunks),
            in_specs=[pl.BlockSpec((chunk, dh), rows(0)),
                      pl.BlockSpec((chunk, dh), rows(nh)),
                      pl.BlockSpec((chunk, dh), rows(2 * nh)),
                      pl.BlockSpec((chunk, dh), rows(3 * nh)),
                      pl.BlockSpec((1, 1, n_chunks, chunk), gate_i),
                      pl.BlockSpec((1, 1, n_chunks, chunk), gate_f),
                      pl.BlockSpec((1, 1, dh, dh), per_head),
                      pl.BlockSpec((1, 1, 1, dh), per_head),
                      pl.BlockSpec((1, 1, 1, 128), per_head),
                      pl.BlockSpec((1, dh), lambda bb, hh, cc, *_: (0, hh))],
            out_specs=[pl.BlockSpec((chunk, dh), lambda bb, hh, cc, *_: (bb * n_chunks + cc, hh)),
                       pl.BlockSpec((1, 1, dh, dh), per_head),
                       pl.BlockSpec((1, 1, 1, dh), per_head),
                       pl.BlockSpec((1, 1, 1, 128), per_head)],
            scratch_shapes=[pltpu.VMEM((dh, dh), F32), pltpu.VMEM((1, dh), F32), pltpu.VMEM((1, 128), F32)]),
        out_shape=[jax.ShapeDtypeStruct((out_rows, D_MODEL), F32),
                   jax.ShapeDtypeStruct((batch, nh, dh, dh), F32),
                   jax.ShapeDtypeStruct((batch, nh, 1, dh), F32),
                   jax.ShapeDtypeStruct((batch, nh, 1, 128), F32)],
        compiler_params=_params("parallel", "parallel", "arbitrary"),
        name="mlstm_scan",
    )(b_gates, proj, proj, proj, proj, gates, gates, c0, n0[:, :, None, :],
      jnp.broadcast_to(m0[:, :, None, None], (batch, nh, 1, 128)), norm_g)


def _mlstm_layer(xt, c0_s, n0_s, m0_s, w_in, b_gates, norm_g, w_out, g, b):
    d, nh, dh = D_MODEL, MLSTM_HEADS, MLSTM_HEAD_DIM
    proj = _matmul(xt, w_in[:, :4 * d].astype(BF16), tm=640, tn=1024)
    w_gates = jnp.pad(w_in[:, 4 * d:], ((0, 0), (0, 128 - 2 * nh)))
    gate_pre = _matmul(xt, w_gates, tm=640, tn=128, precision=lax.Precision.HIGHEST)[:, :2 * nh]

    nc = SEQ // MLSTM_CHUNK
    gates_p = gate_pre[:N_PROMPT_TOK].reshape(BATCH, nc, MLSTM_CHUNK, 2 * nh).transpose(0, 3, 1, 2)
    zeros = lambda *s: jnp.zeros(s, F32)
    hn, c_p, n_p, m_p = _mlstm_scan(
        proj, gates_p, b_gates, norm_g, zeros(BATCH, nh, dh, dh), zeros(BATCH, nh, dh), zeros(BATCH, nh),
        batch=BATCH, n_chunks=nc, chunk=MLSTM_CHUNK, n_valid=MLSTM_CHUNK, out_rows=N_TOK)

    ls = MLSTM_SAMPLE_CHUNK
    pad_t = ((0, 0), (0, ls - DEC_SEQ), (0, 0))
    proj_s = jnp.pad(proj[N_PROMPT_TOK:].reshape(DEC_BATCH, DEC_SEQ, 4 * d), pad_t).reshape(DEC_BATCH * ls, 4 * d)
    gates_s = jnp.pad(gate_pre[N_PROMPT_TOK:].reshape(DEC_BATCH, DEC_SEQ, 2 * nh), pad_t)
    gates_s = gates_s.transpose(0, 2, 1)[:, :, None, :]
    hn_s, c_s, n_s, m_s = _mlstm_scan(
        proj_s, gates_s, b_gates, norm_g, c0_s, n0_s, m0_s,
        batch=DEC_BATCH, n_chunks=1, chunk=ls, n_valid=DEC_SEQ, out_rows=DEC_BATCH * ls)
    hn_s = hn_s.reshape(DEC_BATCH, ls, d)[:, :DEC_SEQ].reshape(N_SAMPLE_TOK, d)
    hn = lax.dynamic_update_slice(hn, hn_s, (N_PROMPT_TOK, 0))

    out = _matmul_ln(hn, w_out.astype(BF16), xt, g, b)
    return (out, c_p, n_p[:, :, 0], m_p[:, :, 0, 0], c_s, n_s[:, :, 0], m_s[:, :, 0, 0])


def _router_kernel(x_ref, wt_ref, bias_ref, eidx_ref, gate_ref, sel_ref, *, tt):
    neg = -jnp.inf
    logits = lax.dot_general(wt_ref[...], x_ref[...], _NT, preferred_element_type=F32,
                             precision=lax.Precision.HIGHEST)
    s = jax.nn.sigmoid(logits)
    sb = s + bias_ref[...]
    per_group = N_EXPERTS // N_EXPERT_GROUPS

    sb3 = sb.reshape(N_EXPERT_GROUPS, per_group, tt)
    i3 = lax.broadcasted_iota(jnp.int32, sb3.shape, 1)
    m1 = jnp.max(sb3, axis=1, keepdims=True)
    first = jnp.min(jnp.where(sb3 == m1, i3, per_group), axis=1, keepdims=True)
    m2 = jnp.max(jnp.where(i3 == first, neg, sb3), axis=1, keepdims=True)
    gscore = (m1 + m2).reshape(N_EXPERT_GROUPS, tt)

    gi = lax.broadcasted_iota(jnp.int32, gscore.shape, 0)
    gsel = jnp.zeros(gscore.shape, jnp.bool_)
    cur = gscore
    for _ in range(TOPK_GROUPS):
        mx = jnp.max(cur, axis=0, keepdims=True)
        pick = gi == jnp.min(jnp.where(cur == mx, gi, N_EXPERT_GROUPS), axis=0, keepdims=True)
        gsel = gsel | pick
        cur = jnp.where(pick, neg, cur)
    emask = jnp.broadcast_to(gsel.reshape(N_EXPERT_GROUPS, 1, tt), sb3.shape).reshape(N_EXPERTS, tt)

    ei = lax.broadcasted_iota(jnp.int32, sb.shape, 0)
    sel = jnp.zeros(sb.shape, jnp.bool_)
    cur = jnp.where(emask, sb, neg)
    picks = []
    for _ in range(TOP_K):
        mx = jnp.max(cur, axis=0, keepdims=True)
        idx = jnp.min(jnp.where(cur == mx, ei, N_EXPERTS), axis=0, keepdims=True)
        pick = ei == idx
        picks.append((idx, pick))
        sel = sel | pick
        cur = jnp.where(pick, neg, cur)
    s_sel = jnp.where(sel, s, 0.0)
    gate = s_sel / jnp.sum(s_sel, axis=0, keepdims=True) * ROUTED_SCALE
    for kk, (idx, pick) in enumerate(picks):
        eidx_ref[kk:kk + 1, :] = idx
        gate_ref[kk:kk + 1, :] = jnp.sum(jnp.where(pick, gate, 0.0), axis=0, keepdims=True)
    sel_ref[...] = sel.astype(jnp.int32)


def _moe_route(xt, w_router, bias):
    tt = ROUTER_TILE
    col = lambda i: (0, i)
    return pl.pallas_call(
        functools.partial(_router_kernel, tt=tt),
        grid=(N_TOK // tt,),
        in_specs=[pl.BlockSpec((tt, D_MODEL), lambda i: (i, 0)),
                  pl.BlockSpec((N_EXPERTS, D_MODEL), lambda i: (0, 0)),
                  pl.BlockSpec((N_EXPERTS, 1), lambda i: (0, 0))],
        out_specs=[pl.BlockSpec((TOP_K, tt), col), pl.BlockSpec((TOP_K, tt), col),
                   pl.BlockSpec((N_EXPERTS, tt), col)],
        out_shape=[jax.ShapeDtypeStruct((TOP_K, N_TOK), jnp.int32),
                   jax.ShapeDtypeStruct((TOP_K, N_TOK), F32),
                   jax.ShapeDtypeStruct((N_EXPERTS, N_TOK), jnp.int32)],
        compiler_params=_params("parallel"),
        name="moe_router",
    )(xt, w_router.T, bias[:, None])


def _dispatch_kernel(dest_ref, x_ref, xs_hbm, sem, *, tt):
    def issue(j, carry):
        pltpu.make_async_copy(x_ref.at[pl.ds(j // TOP_K, 1)], xs_hbm.at[pl.ds(dest_ref[0, 0, j], 1)], sem).start()
        return carry
    lax.fori_loop(0, tt * TOP_K, issue, 0)
    for _ in range(TOP_K):
        pltpu.make_async_copy(x_ref, xs_hbm.at[pl.ds(0, tt)], sem).wait()


def _moe_dispatch(xt, dest_tok):
    tt = DISPATCH_TILE
    n_tiles = N_TOK // tt
    return pl.pallas_call(
        functools.partial(_dispatch_kernel, tt=tt),
        grid=(n_tiles,),
        in_specs=[pl.BlockSpec((1, 1, tt * TOP_K), lambda i: (i, 0, 0), memory_space=pltpu.SMEM),
                  pl.BlockSpec((tt, D_MODEL), lambda i: (i, 0))],
        out_specs=pl.BlockSpec(memory_space=pl.ANY),
        out_shape=jax.ShapeDtypeStruct((MOE_ROWS, D_MODEL), F32),
        scratch_shapes=[pltpu.SemaphoreType.DMA(())],
        compiler_params=_params("arbitrary"),
        name="moe_dispatch",
    )(dest_tok.reshape(n_tiles, 1, tt * TOP_K), xt)


def _experts_kernel(be_ref, nused_ref, xs_ref, wg_ref, wu_ref, wd_ref, y_ref, wg_s, wu_s, wd_s):
    i = pl.program_id(0)

    @pl.when(i < nused_ref[0])
    def _():
        changed = (i == 0) | (be_ref[i] != be_ref[jnp.maximum(i - 1, 0)])

        @pl.when(changed)
        def _():
            wg_s[...] = wg_ref[0].astype(BF16)
            wu_s[...] = wu_ref[0].astype(BF16)
            wd_s[...] = wd_ref[0].astype(BF16)

        x = xs_ref[...].astype(BF16)
        hidden = (_silu(jnp.dot(x, wg_s[...], preferred_element_type=F32))
                  * jnp.dot(x, wu_s[...], preferred_element_type=F32))
        y_ref[...] = jnp.dot(hidden.astype(BF16), wd_s[...], preferred_element_type=F32)


def _moe_experts(xs, block_e, n_used, w_gate, w_up, w_down):
    last = lambda i, be, nu: jnp.minimum(i, nu[0] - 1)
    return pl.pallas_call(
        _experts_kernel,
        grid_spec=pltpu.PrefetchScalarGridSpec(
            num_scalar_prefetch=2,
            grid=(MOE_N_BLOCKS,),
            in_specs=[pl.BlockSpec((MOE_BLOCK, D_MODEL), lambda i, be, nu: (last(i, be, nu), 0)),
                      pl.BlockSpec((1, D_MODEL, EXPERT_FF), lambda i, be, nu: (be[i], 0, 0)),
                      pl.BlockSpec((1, D_MODEL, EXPERT_FF), lambda i, be, nu: (be[i], 0, 0)),
                      pl.BlockSpec((1, EXPERT_FF, D_MODEL), lambda i, be, nu: (be[i], 0, 0))],
            out_specs=pl.BlockSpec((MOE_BLOCK, D_MODEL), lambda i, be, nu: (last(i, be, nu), 0)),
            scratch_shapes=[pltpu.VMEM((D_MODEL, EXPERT_FF), BF16), pltpu.VMEM((D_MODEL, EXPERT_FF), BF16),
                            pltpu.VMEM((EXPERT_FF, D_MODEL), BF16)]),
        out_shape=jax.ShapeDtypeStruct((MOE_ROWS, D_MODEL), F32),
        compiler_params=_params("arbitrary"),
        name="moe_experts",
    )(block_e, n_used, xs, w_gate, w_up, w_down)


def _combine_kernel(dest_ref, x_ref, gate_ref, swg_ref, swu_ref, swd_ref, g_ref, b_ref, y_hbm, o_ref,
                    buf, sem, *, tt):
    def issue(j, carry):
        pltpu.make_async_copy(y_hbm.at[pl.ds(dest_ref[0, 0, j], 1)],
                              buf.at[j % TOP_K, pl.ds(j // TOP_K, 1)], sem).start()
        return carry
    lax.fori_loop(0, tt * TOP_K, issue, 0)

    x = x_ref[...]
    xb = x.astype(BF16)
    hidden = (_silu(jnp.dot(xb, swg_ref[...], preferred_element_type=F32))
              * jnp.dot(xb, swu_ref[...], preferred_element_type=F32))
    shared = jnp.dot(hidden.astype(BF16), swd_ref[...], preferred_element_type=F32)

    for kk in range(TOP_K):
        pltpu.make_async_copy(y_hbm.at[pl.ds(0, tt)], buf.at[kk], sem).wait()
    gate = gate_ref[...]
    routed = gate[:, 0:1] * buf[0]
    for kk in range(1, TOP_K):
        routed = routed + gate[:, kk:kk + 1] * buf[kk]
    o_ref[...] = _layer_norm(DN_ALPHA * x + (routed + shared), g_ref[...], b_ref[...])


def _moe_combine(xt, ys, dest_tok, gate_tok, sw_gate, sw_up, sw_down, g, b):
    tt = COMBINE_TILE
    n_tiles = N_TOK // tt
    row = lambda i: (i, 0)
    const = lambda i: (0, 0)
    return pl.pallas_call(
        functools.partial(_combine_kernel, tt=tt),
        grid=(n_tiles,),
        in_specs=[pl.BlockSpec((1, 1, tt * TOP_K), lambda i: (i, 0, 0), memory_space=pltpu.SMEM),
                  pl.BlockSpec((tt, D_MODEL), row),
                  pl.BlockSpec((tt, TOP_K), row),
                  pl.BlockSpec((D_MODEL, EXPERT_FF), const),
                  pl.BlockSpec((D_MODEL, EXPERT_FF), const),
                  pl.BlockSpec((EXPERT_FF, D_MODEL), const),
                  pl.BlockSpec((1, D_MODEL), const),
                  pl.BlockSpec((1, D_MODEL), const),
                  pl.BlockSpec(memory_space=pl.ANY)],
        out_specs=pl.BlockSpec((tt, D_MODEL), row),
        out_shape=jax.ShapeDtypeStruct((N_TOK, D_MODEL), F32),
        scratch_shapes=[pltpu.VMEM((TOP_K, tt, D_MODEL), F32), pltpu.SemaphoreType.DMA(())],
        compiler_params=_params("arbitrary"),
        name="moe_combine",
    )(dest_tok.reshape(n_tiles, 1, tt * TOP_K), xt, gate_tok, sw_gate.astype(BF16), sw_up.astype(BF16),
      sw_down.astype(BF16), g, b, ys)


def _moe_layer(xt, w_router, bias, w_gate, w_up, w_down, sw_gate, sw_up, sw_down, g, b):
    eidx, gate, sel = _moe_route(xt, w_router, bias)

    incl = jnp.cumsum(sel, axis=1)
    counts = incl[:, -1]
    pcounts = (counts + MOE_BLOCK - 1) // MOE_BLOCK * MOE_BLOCK
    pends = jnp.cumsum(pcounts)
    dest_full = (pends - pcounts)[:, None] + incl - sel
    e_iota = jnp.arange(N_EXPERTS, dtype=jnp.int32)[None, :, None]
    dest = jnp.sum(jnp.where(eidx[:, None, :] == e_iota, dest_full[None], 0), axis=1)
    dest_tok = dest.T.astype(jnp.int32)
    n_used = (pends[-1] // MOE_BLOCK).astype(jnp.int32)
    blk = jnp.minimum(jnp.arange(MOE_N_BLOCKS, dtype=jnp.int32), n_used - 1) * MOE_BLOCK
    block_e = jnp.clip(jnp.searchsorted(pends, blk, side='right'), 0, N_EXPERTS - 1).astype(jnp.int32)

    xs = _moe_dispatch(xt, dest_tok)
    ys = _moe_experts(xs, block_e, n_used[None], w_gate, w_up, w_down)
    return _moe_combine(xt, ys, dest_tok, gate.T, sw_gate, sw_up, sw_down, g, b)


def kernel(x_prompt, x_sample, state_pool, cache_swa_k, cache_swa_v, state_mlstm_c, state_mlstm_n, state_mlstm_m, pool_w, pool_scale, swa_w_qkv, swa_w_o, swa_sinks, mlstm_w_in, mlstm_b_gates, mlstm_norm_g, mlstm_w_out, ln_g, ln_b, moe_w_router, moe_router_bias, moe_w_gate, moe_w_up, moe_w_down, moe_shared_w_gate, moe_shared_w_up, moe_shared_w_down):
    d = D_MODEL
    xt = jnp.concatenate([x_prompt.reshape(N_PROMPT_TOK, d), x_sample.reshape(N_SAMPLE_TOK, d)], axis=0)
    pool_p, pool_s = [], []
    swk_p, swv_p, swk_s, swv_s = [], [], [], []
    mc_p, mn_p, mm_p, mc_s, mn_s, mm_s = [], [], [], [], [], []
    for i in range(DEPTH):
        kind, slot = i % N_MIXERS, i // N_MIXERS
        g0, b0 = ln_g[i, 0][None], ln_b[i, 0][None]
        if kind == 0:
            xt, sp, ss = _pool_layer(xt, state_pool[slot], pool_w[slot], pool_scale[slot][None], g0, b0)
            pool_p.append(sp)
            pool_s.append(ss)
        elif kind == 1:
            xt, kp, vp, ks, vs = _swa_layer(xt, cache_swa_k[slot], cache_swa_v[slot], swa_w_qkv[slot],
                                            swa_w_o[slot], swa_sinks[slot], g0, b0)
            swk_p.append(kp)
            swv_p.append(vp)
            swk_s.append(ks)
            swv_s.append(vs)
        else:
            xt, cp, np_, mp, cs, ns, ms = _mlstm_layer(
                xt, state_mlstm_c[slot], state_mlstm_n[slot], state_mlstm_m[slot], mlstm_w_in[slot],
                mlstm_b_gates[slot], mlstm_norm_g[slot][None], mlstm_w_out[slot], g0, b0)
            mc_p.append(cp)
            mn_p.append(np_)
            mm_p.append(mp)
            mc_s.append(cs)
            mn_s.append(ns)
            mm_s.append(ms)
        xt = _moe_layer(xt, moe_w_router[i], moe_router_bias[i], moe_w_gate[i], moe_w_up[i], moe_w_down[i],
                        moe_shared_w_gate[i], moe_shared_w_up[i], moe_shared_w_down[i],
                        ln_g[i, 1][None], ln_b[i, 1][None])
    y_p = xt[:N_PROMPT_TOK].reshape(BATCH, SEQ, d)
    y_s = xt[N_PROMPT_TOK:].reshape(DEC_BATCH, DEC_SEQ, d)
    return (y_p, y_s, jnp.stack(pool_p), jnp.stack(pool_s), jnp.stack(swk_p), jnp.stack(swv_p),
            jnp.stack(swk_s), jnp.stack(swv_s), jnp.stack(mc_p), jnp.stack(mn_p), jnp.stack(mm_p),
            jnp.stack(mc_s), jnp.stack(mn_s), jnp.stack(mm_s))
```

```python
import functools
import math

import jax
import jax.numpy as jnp
from jax import lax
from jax.experimental import pallas as pl
from jax.experimental.pallas import tpu as pltpu

F32 = jnp.float32
BF16 = jnp.bfloat16

D_MODEL = 2048
BATCH = 2
SEQ = 4096
DEPTH = 4
DEC_BATCH = 32
DEC_SEQ = 4
PAST_LEN = 16384
N_PROMPT_TOK = BATCH * SEQ
N_SAMPLE_TOK = DEC_BATCH * DEC_SEQ
N_TOK = N_PROMPT_TOK + N_SAMPLE_TOK

N_MIXERS = 3
DN_ALPHA = (2.0 * DEPTH) ** 0.25
LN_EPS = 1e-5

POOL_WINDOWS = (2, 4, 8, 16)
POOL_GROUP_DIM = D_MODEL // len(POOL_WINDOWS)
POOL_STATE = max(POOL_WINDOWS) - 1
POOL_HALO = POOL_STATE + 1

SWA_WINDOW = 128
SWA_HEAD_DIM = 64
SWA_HEADS = D_MODEL // SWA_HEAD_DIM
SWA_KV_HEADS = SWA_HEADS // 8
SWA_GROUP = SWA_HEADS // SWA_KV_HEADS
SWA_KV_DIM = SWA_KV_HEADS * SWA_HEAD_DIM

MLSTM_HEADS = 8
MLSTM_HEAD_DIM = D_MODEL // MLSTM_HEADS
MLSTM_CHUNK = 64
MLSTM_SAMPLE_CHUNK = 8

N_EXPERTS = 64
TOP_K = 8
N_EXPERT_GROUPS = 8
TOPK_GROUPS = 4
EXPERT_FF = D_MODEL // 4
ROUTED_SCALE = 2.5

VMEM_LIMIT_BYTES = 56 * 1024 * 1024

ROW_TILE = 320
ROUTER_TILE = 640
POOL_TILE = 512
MOE_BLOCK = 256
MOE_N_BLOCKS = N_TOK * TOP_K // MOE_BLOCK + N_EXPERTS
MOE_ROWS = MOE_N_BLOCKS * MOE_BLOCK
DISPATCH_TILE = 320
COMBINE_TILE = 64
LANES = 128
PACK_ROWS = D_MODEL // (2 * LANES)
F32_ROWS = D_MODEL // LANES

_NT = (((1,), (1,)), ((), ()))
_TN = (((0,), (0,)), ((), ()))


def _params(*semantics):
    return pltpu.CompilerParams(dimension_semantics=semantics, vmem_limit_bytes=VMEM_LIMIT_BYTES)


def _layer_norm(z, g, b):
    mu = jnp.mean(z, axis=-1, keepdims=True)
    zc = z - mu
    var = jnp.mean(zc * zc, axis=-1, keepdims=True)
    return zc * lax.rsqrt(var + LN_EPS) * g + b


def _silu(x):
    return x * jax.nn.sigmoid(x)


def _matmul_kernel(x_ref, w_ref, o_ref, *, precision):
    if precision is None:
        o_ref[...] = jnp.dot(x_ref[...].astype(BF16), w_ref[...], preferred_element_type=F32)
    else:
        o_ref[...] = jnp.dot(x_ref[...], w_ref[...], preferred_element_type=F32, precision=precision)


def _matmul(x, w, *, tm, tn, precision=None):
    m, k = x.shape
    n = w.shape[1]
    return pl.pallas_call(
        functools.partial(_matmul_kernel, precision=precision),
        grid=(n // tn, m // tm),
        in_specs=[pl.BlockSpec((tm, k), lambda j, i: (i, 0)),
                  pl.BlockSpec((k, tn), lambda j, i: (0, j))],
        out_specs=pl.BlockSpec((tm, tn), lambda j, i: (i, j)),
        out_shape=jax.ShapeDtypeStruct((m, n), F32),
        compiler_params=_params("parallel", "parallel"),
        name="matmul",
    )(x, w)


def _matmul_ln_kernel(a_ref, w_ref, res_ref, g_ref, b_ref, o_ref):
    y = jnp.dot(a_ref[...].astype(BF16), w_ref[...], preferred_element_type=F32)
    o_ref[...] = _layer_norm(DN_ALPHA * res_ref[...] + y, g_ref[...], b_ref[...])


def _matmul_ln(a, w, res, g, b):
    m, k = a.shape
    row = lambda i: (i, 0)
    const = lambda i: (0, 0)
    return pl.pallas_call(
        _matmul_ln_kernel,
        grid=(m // ROW_TILE,),
        in_specs=[pl.BlockSpec((ROW_TILE, k), row),
                  pl.BlockSpec((k, D_MODEL), const),
                  pl.BlockSpec((ROW_TILE, D_MODEL), row),
                  pl.BlockSpec((1, D_MODEL), const),
                  pl.BlockSpec((1, D_MODEL), const)],
        out_specs=pl.BlockSpec((ROW_TILE, D_MODEL), row),
        out_shape=jax.ShapeDtypeStruct((m, D_MODEL), F32),
        compiler_params=_params("parallel"),
        name="matmul_ln",
    )(a, w, res, g, b)


def _pool_core(zbuf, w_ref, scale_ref, g_ref, b_ref, o_ref, *, tt, n_before):
    avail = n_before + lax.broadcasted_iota(jnp.int32, (tt, 1), 0) + 1
    for grp, win in enumerate(POOL_WINDOWS):
        c0, c1 = grp * POOL_GROUP_DIM, (grp + 1) * POOL_GROUP_DIM
        xg = zbuf[POOL_HALO:POOL_HALO + tt, c0:c1]
        total = xg
        for back in range(1, win):
            total = total + zbuf[POOL_HALO - back:POOL_HALO - back + tt, c0:c1]
        count = jnp.minimum(win, avail).astype(F32)
        diff = total / count - xg
        y = jnp.dot(diff.astype(BF16), w_ref[grp], preferred_element_type=F32) * scale_ref[:, c0:c1]
        o_ref[:, c0:c1] = DN_ALPHA * xg + y
    o_ref[...] = _layer_norm(o_ref[...], g_ref[...], b_ref[...])


def _pool_prompt_kernel(x_ref, halo_ref, w_ref, scale_ref, g_ref, b_ref, o_ref, zbuf, *, tt):
    i = pl.program_id(1)
    zbuf[0:POOL_HALO, :] = jnp.where(i == 0, 0.0, halo_ref[...])
    zbuf[POOL_HALO:POOL_HALO + tt, :] = x_ref[...]
    _pool_core(zbuf, w_ref, scale_ref, g_ref, b_ref, o_ref, tt=tt, n_before=i * tt)


def _pool_sample_kernel(z_ref, w_ref, scale_ref, g_ref, b_ref, o_ref, *, tt):
    _pool_core(z_ref.at[0], w_ref, scale_ref, g_ref, b_ref, o_ref.at[0], tt=tt, n_before=PAST_LEN)


def _pool_layer(xt, state, w, scale, g, b):
    tt = POOL_TILE
    tiles = SEQ // tt
    halo_per_tile = tt // POOL_HALO
    const2 = lambda bb, i: (0, 0)
    w_bf = w.astype(BF16)
    out = pl.pallas_call(
        functools.partial(_pool_prompt_kernel, tt=tt),
        grid=(BATCH, tiles),
        in_specs=[pl.BlockSpec((tt, D_MODEL), lambda bb, i: (bb * tiles + i, 0)),
                  pl.BlockSpec((POOL_HALO, D_MODEL),
                               lambda bb, i: (jnp.maximum((bb * tiles + i) * halo_per_tile - 1, 0), 0)),
                  pl.BlockSpec(w_bf.shape, lambda bb, i: (0, 0, 0)),
                  pl.BlockSpec((1, D_MODEL), const2),
                  pl.BlockSpec((1, D_MODEL), const2),
                  pl.BlockSpec((1, D_MODEL), const2)],
        out_specs=pl.BlockSpec((tt, D_MODEL), lambda bb, i: (bb * tiles + i, 0)),
        out_shape=jax.ShapeDtypeStruct((N_TOK, D_MODEL), F32),
        scratch_shapes=[pltpu.VMEM((POOL_HALO + tt, D_MODEL), F32)],
        compiler_params=_params("parallel", "arbitrary"),
        name="pool_prompt",
    )(xt, xt, w_bf, scale, g, b)

    ts = 16
    xs = xt[N_PROMPT_TOK:].reshape(DEC_BATCH, DEC_SEQ, D_MODEL)
    zs = jnp.concatenate([jnp.zeros((DEC_BATCH, 1, D_MODEL), F32), state, xs,
                          jnp.zeros((DEC_BATCH, ts - DEC_SEQ, D_MODEL), F32)], axis=1)
    const1 = lambda bb: (0, 0)
    out_s = pl.pallas_call(
        functools.partial(_pool_sample_kernel, tt=ts),
        grid=(DEC_BATCH,),
        in_specs=[pl.BlockSpec((1, POOL_HALO + ts, D_MODEL), lambda bb: (bb, 0, 0)),
                  pl.BlockSpec(w_bf.shape, lambda bb: (0, 0, 0)),
                  pl.BlockSpec((1, D_MODEL), const1),
                  pl.BlockSpec((1, D_MODEL), const1),
                  pl.BlockSpec((1, D_MODEL), const1)],
        out_specs=pl.BlockSpec((1, ts, D_MODEL), lambda bb: (bb, 0, 0)),
        out_shape=jax.ShapeDtypeStruct((DEC_BATCH, ts, D_MODEL), F32),
        compiler_params=_params("parallel"),
        name="pool_sample",
    )(zs, w_bf, scale, g, b)
    out = lax.dynamic_update_slice(out, out_s[:, :DEC_SEQ].reshape(N_SAMPLE_TOK, D_MODEL), (N_PROMPT_TOK, 0))

    new_p = xt[:N_PROMPT_TOK].reshape(BATCH, SEQ, D_MODEL)[:, SEQ - POOL_STATE:]
    new_s = jnp.concatenate([state, xs], axis=1)[:, DEC_SEQ:]
    return out, new_p, new_s


def _alibi_slope(h):
    return 2.0 ** (-8.0 * (h + 1.0) / SWA_HEADS)


def _attn_core(q_ref, k_all, v_all, sink_ref, o_ref, *, rows, first_block):
    w = SWA_WINDOW
    qi = lax.broadcasted_iota(jnp.int32, (rows, 2 * w), 0)
    sj = lax.broadcasted_iota(jnp.int32, (rows, 2 * w), 1)
    dist = (w + qi) - sj
    valid = (dist >= 0) & (dist <= w)
    if first_block is not None:
        valid = valid & ((sj >= w) | jnp.logical_not(first_block))
    distf = dist.astype(F32)
    for h in range(SWA_HEADS):
        kv = h // SWA_GROUP
        c0, c1 = kv * SWA_HEAD_DIM, (kv + 1) * SWA_HEAD_DIM
        q = q_ref[:, h * SWA_HEAD_DIM:(h + 1) * SWA_HEAD_DIM].astype(BF16)
        s = lax.dot_general(q, k_all[:, c0:c1], _NT, preferred_element_type=F32) * (SWA_HEAD_DIM ** -0.5)
        s = jnp.where(valid, s - _alibi_slope(h) * distf, -jnp.inf)
        sink = sink_ref[h]
        m = jnp.maximum(jnp.max(s, axis=1, keepdims=True), sink)
        e = jnp.exp(s - m)
        den = jnp.sum(e, axis=1, keepdims=True) + jnp.exp(sink - m)
        o = jnp.dot(e.astype(BF16), v_all[:, c0:c1], preferred_element_type=F32) / den
        o_ref[:, h * SWA_HEAD_DIM:(h + 1) * SWA_HEAD_DIM] = o


def _attn_prompt_kernel(sink_ref, q_ref, kp_ref, kc_ref, vp_ref, vc_ref, o_ref):
    n = pl.program_id(1)
    k_all = jnp.concatenate([kp_ref[...], kc_ref[...]], axis=0).astype(BF16)
    v_all = jnp.concatenate([vp_ref[...], vc_ref[...]], axis=0).astype(BF16)
    _attn_core(q_ref, k_all, v_all, sink_ref, o_ref, rows=SWA_WINDOW, first_block=(n == 0))


def _attn_sample_kernel(sink_ref, q_ref, k_ref, v_ref, o_ref, *, rows):
    _attn_core(q_ref.at[0], k_ref[0].astype(BF16), v_ref[0].astype(BF16), sink_ref, o_ref.at[0],
               rows=rows, first_block=None)


def _swa_layer(xt, cache_k, cache_v, w_qkv, w_o, sinks, g, b):
    w = SWA_WINDOW
    qkv = _matmul(xt, w_qkv.astype(BF16), tm=640, tn=1280)
    nb = SEQ // w
    k_col = D_MODEL // SWA_KV_DIM
    cur = lambda bb, n, *_: (bb * nb + n, k_col)
    prev = lambda bb, n, *_: (jnp.maximum(bb * nb + n - 1, 0), k_col)
    cur_v = lambda bb, n, *_: (bb * nb + n, k_col + 1)
    prev_v = lambda bb, n, *_: (jnp.maximum(bb * nb + n - 1, 0), k_col + 1)
    o = pl.pallas_call(
        _attn_prompt_kernel,
        grid_spec=pltpu.PrefetchScalarGridSpec(
            num_scalar_prefetch=1,
            grid=(BATCH, nb),
            in_specs=[pl.BlockSpec((w, D_MODEL), lambda bb, n, *_: (bb * nb + n, 0)),
                      pl.BlockSpec((w, SWA_KV_DIM), prev),
                      pl.BlockSpec((w, SWA_KV_DIM), cur),
                      pl.BlockSpec((w, SWA_KV_DIM), prev_v),
                      pl.BlockSpec((w, SWA_KV_DIM), cur_v)],
            out_specs=pl.BlockSpec((w, D_MODEL), lambda bb, n, *_: (bb * nb + n, 0))),
        out_shape=jax.ShapeDtypeStruct((N_TOK, D_MODEL), F32),
        compiler_params=_params("parallel", "arbitrary"),
        name="attn_prompt",
    )(sinks, qkv, qkv, qkv, qkv, qkv)

    rows = 16
    qkv_s = qkv[N_PROMPT_TOK:].reshape(DEC_BATCH, DEC_SEQ, -1)
    q_s = jnp.pad(qkv_s[..., :D_MODEL], ((0, 0), (0, rows - DEC_SEQ), (0, 0)))
    k_new = qkv_s[..., D_MODEL:D_MODEL + SWA_KV_DIM]
    v_new = qkv_s[..., D_MODEL + SWA_KV_DIM:]
    kz = jnp.concatenate([cache_k.reshape(DEC_BATCH, w, SWA_KV_DIM), k_new], axis=1)
    vz = jnp.concatenate([cache_v.reshape(DEC_BATCH, w, SWA_KV_DIM), v_new], axis=1)
    pad_keys = ((0, 0), (0, w - DEC_SEQ), (0, 0))
    o_s = pl.pallas_call(
        functools.partial(_attn_sample_kernel, rows=rows),
        grid_spec=pltpu.PrefetchScalarGridSpec(
            num_scalar_prefetch=1,
            grid=(DEC_BATCH,),
            in_specs=[pl.BlockSpec((1, rows, D_MODEL), lambda bb, *_: (bb, 0, 0)),
                      pl.BlockSpec((1, 2 * w, SWA_KV_DIM), lambda bb, *_: (bb, 0, 0)),
                      pl.BlockSpec((1, 2 * w, SWA_KV_DIM), lambda bb, *_: (bb, 0, 0))],
            out_specs=pl.BlockSpec((1, rows, D_MODEL), lambda bb, *_: (bb, 0, 0))),
        out_shape=jax.ShapeDtypeStruct((DEC_BATCH, rows, D_MODEL), F32),
        compiler_params=_params("parallel"),
        name="attn_sample",
    )(sinks, q_s, jnp.pad(kz, pad_keys), jnp.pad(vz, pad_keys))
    o = lax.dynamic_update_slice(o, o_s[:, :DEC_SEQ].reshape(N_SAMPLE_TOK, D_MODEL), (N_PROMPT_TOK, 0))

    out = _matmul_ln(o, w_o.astype(BF16), xt, g, b)

    kv_shape = (SWA_WINDOW, SWA_KV_HEADS, SWA_HEAD_DIM)
    qkv_p = qkv[:N_PROMPT_TOK].reshape(BATCH, SEQ, -1)[:, SEQ - w:]
    new_k_p = qkv_p[..., D_MODEL:D_MODEL + SWA_KV_DIM].reshape((BATCH,) + kv_shape)
    new_v_p = qkv_p[..., D_MODEL + SWA_KV_DIM:].reshape((BATCH,) + kv_shape)
    new_k_s = kz[:, DEC_SEQ:].reshape((DEC_BATCH,) + kv_shape)
    new_v_s = vz[:, DEC_SEQ:].reshape((DEC_BATCH,) + kv_shape)
    return out, new_k_p, new_v_p, new_k_s, new_v_s


def _log_sigmoid(x):
    return jnp.minimum(x, 0.0) - jnp.log(1.0 + jnp.exp(-jnp.abs(x)))


def _mlstm_kernel(bias_ref, q_ref, k_ref, v_ref, og_ref, gi_ref, gf_ref, c0_ref, n0_ref, m0_ref, ng_ref,
                  h_ref, c_ref, n_ref, m_ref, c_scr, n_scr, m_scr, *, chunk, n_valid):
    head = pl.program_id(1)
    step = pl.program_id(2)
    ln = chunk

    @pl.when(step == 0)
    def _():
        c_scr[...] = c0_ref[0, 0]
        n_scr[...] = n0_ref[0, 0]
        m_scr[...] = m0_ref[0, 0]

    ig = gi_ref[0, 0, pl.ds(step, 1), :] + bias_ref[head]
    lf = _log_sigmoid(gf_ref[0, 0, pl.ds(step, 1), :] + bias_ref[MLSTM_HEADS + head])
    if n_valid < ln:
        col = lax.broadcasted_iota(jnp.int32, (1, ln), 1)
        ig = jnp.where(col < n_valid, ig, -1e30)
        lf = jnp.where(col < n_valid, lf, 0.0)

    ri = lax.broadcasted_iota(jnp.int32, (ln, ln), 0)
    ci = lax.broadcasted_iota(jnp.int32, (ln, ln), 1)
    eye = ri == ci
    causal = ci <= ri

    def to_col(row):
        return jnp.sum(jnp.where(eye, row, 0.0), axis=1, keepdims=True)

    f_col = to_col(lf)
    b_col = jnp.sum(jnp.where(causal, lf, 0.0), axis=1, keepdims=True)
    b_row = jnp.sum(jnp.where(ri <= ci, f_col, 0.0), axis=0, keepdims=True)
    b_last = jnp.sum(lf, axis=1, keepdims=True)
    m_prev = m_scr[:, 0:1]
    a_col = b_col + m_prev
    dmat = jnp.where(causal, b_col - b_row + ig, -jnp.inf)
    mt = jnp.maximum(a_col, jnp.max(dmat, axis=1, keepdims=True))
    w_inter = jnp.exp(a_col - mt)

    q = q_ref[...]
    k = k_ref[...] * (MLSTM_HEAD_DIM ** -0.5)
    v = v_ref[...]
    qb = q.astype(BF16)
    kb = k.astype(BF16)
    c_prev = c_scr[...]
    n_prev = n_scr[...]
    scores = lax.dot_general(qb, kb, _NT, preferred_element_type=F32)
    w_intra = jnp.exp(dmat - mt) * scores
    num = (w_inter * lax.dot_general(qb, c_prev.astype(BF16), _NT, preferred_element_type=F32)
           + jnp.dot(w_intra.astype(BF16), v.astype(BF16), preferred_element_type=F32))
    den = (w_inter * jnp.sum(q * n_prev, axis=1, keepdims=True)
           + jnp.sum(w_intra, axis=1, keepdims=True))
    h = num / jnp.maximum(jnp.abs(den), jnp.exp(-mt))

    g_row = b_last - b_row + ig
    m_new = jnp.maximum(b_last + m_prev, jnp.max(g_row, axis=1, keepdims=True))
    decay = jnp.exp(b_last + m_prev - m_new)
    wg_col = to_col(jnp.exp(g_row - m_new))
    c_new = decay * c_prev + lax.dot_general((v * wg_col).astype(BF16), kb, _TN, preferred_element_type=F32)
    n_new = decay * n_prev + jnp.sum(wg_col * k, axis=0, keepdims=True)
    m_new_b = jnp.broadcast_to(m_new, m_scr.shape)
    c_scr[...] = c_new
    n_scr[...] = n_new
    m_scr[...] = m_new_b

    mu = jnp.mean(h, axis=1, keepdims=True)
    hc = h - mu
    var = jnp.mean(hc * hc, axis=1, keepdims=True)
    h_ref[...] = hc * lax.rsqrt(var + LN_EPS) * ng_ref[...] * jax.nn.sigmoid(og_ref[...])

    @pl.when(step == pl.num_programs(2) - 1)
    def _():
        c_ref[0, 0] = c_new
        n_ref[0, 0] = n_new
        m_ref[0, 0] = m_new_b


def _mlstm_scan(proj, gates, b_gates, norm_g, c0, n0, m0, *, batch, n_chunks, chunk, n_valid, out_rows):
    dh, nh = MLSTM_HEAD_DIM, MLSTM_HEADS
    rows = lambda off: (lambda bb, hh, cc, *_: (bb * n_chunks + cc, off + hh))
    per_head = lambda bb, hh, cc, *_: (bb, hh, 0, 0)
    gate_i = lambda bb, hh, cc, *_: (bb, hh, 0, 0)
    gate_f = lambda bb, hh, cc, *_: (bb, nh + hh, 0, 0)
    return pl.pallas_call(
        functools.partial(_mlstm_kernel, chunk=chunk, n_valid=n_valid),
        grid_spec=pltpu.PrefetchScalarGridSpec(
            num_scalar_prefetch=1,
            grid=(batch, nh, n_chunks),
            in_specs=[pl.BlockSpec((chunk, dh), rows(0)),
                      pl.BlockSpec((chunk, dh), rows(nh)),
                      pl.BlockSpec((chunk, dh), rows(2 * nh)),
                      pl.BlockSpec((chunk, dh), rows(3 * nh)),
                      pl.BlockSpec((1, 1, n_chunks, chunk), gate_i),
                      pl.BlockSpec((1, 1, n_chunks, chunk), gate_f),
                      pl.BlockSpec((1, 1, dh, dh), per_head),
                      pl.BlockSpec((1, 1, 1, dh), per_head),
                      pl.BlockSpec((1, 1, 1, 128), per_head),
                      pl.BlockSpec((1, dh), lambda bb, hh, cc, *_: (0, hh))],
            out_specs=[pl.BlockSpec((chunk, dh), lambda bb, hh, cc, *_: (bb * n_chunks + cc, hh)),
                       pl.BlockSpec((1, 1, dh, dh), per_head),
                       pl.BlockSpec((1, 1, 1, dh), per_head),
                       pl.BlockSpec((1, 1, 1, 128), per_head)],
            scratch_shapes=[pltpu.VMEM((dh, dh), F32), pltpu.VMEM((1, dh), F32), pltpu.VMEM((1, 128), F32)]),
        out_shape=[jax.ShapeDtypeStruct((out_rows, D_MODEL), F32),
                   jax.ShapeDtypeStruct((batch, nh, dh, dh), F32),
                   jax.ShapeDtypeStruct((batch, nh, 1, dh), F32),
                   jax.ShapeDtypeStruct((batch, nh, 1, 128), F32)],
        compiler_params=_params("parallel", "parallel", "arbitrary"),
        name="mlstm_scan",
    )(b_gates, proj, proj, proj, proj, gates, gates, c0, n0[:, :, None, :],
      jnp.broadcast_to(m0[:, :, None, None], (batch, nh, 1, 128)), norm_g)


def _mlstm_layer(xt, c0_s, n0_s, m0_s, w_in, b_gates, norm_g, w_out, g, b):
    d, nh, dh = D_MODEL, MLSTM_HEADS, MLSTM_HEAD_DIM
    proj = _matmul(xt, w_in[:, :4 * d].astype(BF16), tm=640, tn=1024)
    w_gates = jnp.pad(w_in[:, 4 * d:], ((0, 0), (0, 128 - 2 * nh)))
    gate_pre = _matmul(xt, w_gates, tm=640, tn=128, precision=lax.Precision.HIGHEST)[:, :2 * nh]

    nc = SEQ // MLSTM_CHUNK
    gates_p = gate_pre[:N_PROMPT_TOK].reshape(BATCH, nc, MLSTM_CHUNK, 2 * nh).transpose(0, 3, 1, 2)
    zeros = lambda *s: jnp.zeros(s, F32)
    hn, c_p, n_p, m_p = _mlstm_scan(
        proj, gates_p, b_gates, norm_g, zeros(BATCH, nh, dh, dh), zeros(BATCH, nh, dh), zeros(BATCH, nh),
        batch=BATCH, n_chunks=nc, chunk=MLSTM_CHUNK, n_valid=MLSTM_CHUNK, out_rows=N_TOK)

    ls = MLSTM_SAMPLE_CHUNK
    pad_t = ((0, 0), (0, ls - DEC_SEQ), (0, 0))
    proj_s = jnp.pad(proj[N_PROMPT_TOK:].reshape(DEC_BATCH, DEC_SEQ, 4 * d), pad_t).reshape(DEC_BATCH * ls, 4 * d)
    gates_s = jnp.pad(gate_pre[N_PROMPT_TOK:].reshape(DEC_BATCH, DEC_SEQ, 2 * nh), pad_t)
    gates_s = gates_s.transpose(0, 2, 1)[:, :, None, :]
    hn_s, c_s, n_s, m_s = _mlstm_scan(
        proj_s, gates_s, b_gates, norm_g, c0_s, n0_s, m0_s,
        batch=DEC_BATCH, n_chunks=1, chunk=ls, n_valid=DEC_SEQ, out_rows=DEC_BATCH * ls)
    hn_s = hn_s.reshape(DEC_BATCH, ls, d)[:, :DEC_SEQ].reshape(N_SAMPLE_TOK, d)
    hn = lax.dynamic_update_slice(hn, hn_s, (N_PROMPT_TOK, 0))

    out = _matmul_ln(hn, w_out.astype(BF16), xt, g, b)
    return (out, c_p, n_p[:, :, 0], m_p[:, :, 0, 0], c_s, n_s[:, :, 0], m_s[:, :, 0, 0])


def _router_kernel(x_ref, wt_ref, bias_ref, tri_ref, eidx_ref, gate_ref, rank_ref, count_ref, xq_ref, count_scr,
                   *, tt):
    neg = -jnp.inf
    x = x_ref[...]
    logits = lax.dot_general(wt_ref[...], x, _NT, preferred_element_type=F32,
                             precision=lax.Precision.HIGHEST)
    s = jax.nn.sigmoid(logits)
    sb = s + bias_ref[...]
    per_group = N_EXPERTS // N_EXPERT_GROUPS

    sb3 = sb.reshape(N_EXPERT_GROUPS, per_group, tt)
    i3 = lax.broadcasted_iota(jnp.int32, sb3.shape, 1)
    m1 = jnp.max(sb3, axis=1, keepdims=True)
    first = jnp.min(jnp.where(sb3 == m1, i3, per_group), axis=1, keepdims=True)
    m2 = jnp.max(jnp.where(i3 == first, neg, sb3), axis=1, keepdims=True)
    gscore = (m1 + m2).reshape(N_EXPERT_GROUPS, tt)

    gi = lax.broadcasted_iota(jnp.int32, gscore.shape, 0)
    gsel = jnp.zeros(gscore.shape, jnp.bool_)
    cur = gscore
    for _ in range(TOPK_GROUPS):
        mx = jnp.max(cur, axis=0, keepdims=True)
        pick = gi == jnp.min(jnp.where(cur == mx, gi, N_EXPERT_GROUPS), axis=0, keepdims=True)
        gsel = gsel | pick
        cur = jnp.where(pick, neg, cur)
    emask = jnp.broadcast_to(gsel.reshape(N_EXPERT_GROUPS, 1, tt), sb3.shape).reshape(N_EXPERTS, tt)

    ei = lax.broadcasted_iota(jnp.int32, sb.shape, 0)
    sel = jnp.zeros(sb.shape, jnp.bool_)
    cur = jnp.where(emask, sb, neg)
    picks = []
    for _ in range(TOP_K):
        mx = jnp.max(cur, axis=0, keepdims=True)
        idx = jnp.min(jnp.where(cur == mx, ei, N_EXPERTS), axis=0, keepdims=True)
        pick = ei == idx
        picks.append((idx, pick))
        sel = sel | pick
        cur = jnp.where(pick, neg, cur)
    s_sel = jnp.where(sel, s, 0.0)
    gate = s_sel / jnp.sum(s_sel, axis=0, keepdims=True) * ROUTED_SCALE

    @pl.when(pl.program_id(0) == 0)
    def _():
        count_scr[...] = jnp.zeros_like(count_scr)

    sel_f = sel.astype(F32)
    incl = jnp.dot(sel_f.astype(BF16), tri_ref[...], preferred_element_type=F32)
    rank = count_scr[:, 0:1] + incl - sel_f
    count_scr[...] = count_scr[...] + jnp.sum(sel_f, axis=1, keepdims=True)
    count_ref[...] = count_scr[...].astype(jnp.int32)

    for kk, (idx, pick) in enumerate(picks):
        eidx_ref[kk:kk + 1, :] = idx
        gate_ref[kk:kk + 1, :] = jnp.sum(jnp.where(pick, gate, 0.0), axis=0, keepdims=True)
        rank_ref[kk:kk + 1, :] = jnp.sum(jnp.where(pick, rank, 0.0), axis=0, keepdims=True).astype(jnp.int32)

    for sl in range(PACK_ROWS):
        c0 = 2 * LANES * sl
        xq_ref[pl.ds(sl, tt, stride=PACK_ROWS), :] = _pack_bf16_pair(x[:, c0:c0 + LANES],
                                                                     x[:, c0 + LANES:c0 + 2 * LANES])


def _pack_bf16_pair(hi, lo):
    hb = lax.bitcast_convert_type(hi.astype(BF16).astype(F32), jnp.uint32)
    lb = lax.bitcast_convert_type(lo.astype(BF16).astype(F32), jnp.uint32)
    return hb | (lb >> 16)


def _unpack_bf16_pair(word):
    hi = lax.bitcast_convert_type(word & jnp.uint32(0xFFFF0000), F32)
    lo = lax.bitcast_convert_type(word << 16, F32)
    return hi, lo


def _moe_route(xt, w_router, bias):
    tt = ROUTER_TILE
    col = lambda i: (0, i)
    const = lambda i: (0, 0)
    tri = (jnp.arange(tt)[:, None] <= jnp.arange(tt)[None, :]).astype(BF16)
    return pl.pallas_call(
        functools.partial(_router_kernel, tt=tt),
        grid=(N_TOK // tt,),
        in_specs=[pl.BlockSpec((tt, D_MODEL), lambda i: (i, 0)),
                  pl.BlockSpec((N_EXPERTS, D_MODEL), const),
                  pl.BlockSpec((N_EXPERTS, 1), const),
                  pl.BlockSpec((tt, tt), const)],
        out_specs=[pl.BlockSpec((TOP_K, tt), col), pl.BlockSpec((TOP_K, tt), col), pl.BlockSpec((TOP_K, tt), col),
                   pl.BlockSpec((N_EXPERTS, LANES), const),
                   pl.BlockSpec((tt * PACK_ROWS, LANES), lambda i: (i, 0))],
        out_shape=[jax.ShapeDtypeStruct((TOP_K, N_TOK), jnp.int32),
                   jax.ShapeDtypeStruct((TOP_K, N_TOK), F32),
                   jax.ShapeDtypeStruct((TOP_K, N_TOK), jnp.int32),
                   jax.ShapeDtypeStruct((N_EXPERTS, LANES), jnp.int32),
                   jax.ShapeDtypeStruct((N_TOK * PACK_ROWS, LANES), jnp.uint32)],
        scratch_shapes=[pltpu.VMEM((N_EXPERTS, LANES), F32)],
        compiler_params=_params("arbitrary"),
        name="moe_router",
    )(xt, w_router.T, bias[:, None], tri)


def _dispatch_kernel(dest_ref, xq_ref, xs_hbm, sem, *, tt):
    def issue(t, carry):
        src = xq_ref.at[pl.ds(pl.multiple_of(t * PACK_ROWS, PACK_ROWS), PACK_ROWS)]
        for kk in range(TOP_K):
            row = pl.multiple_of(dest_ref[0, 0, t * TOP_K + kk] * PACK_ROWS, PACK_ROWS)
            pltpu.make_async_copy(src, xs_hbm.at[pl.ds(row, PACK_ROWS)], sem).start()
        return carry
    lax.fori_loop(0, tt, issue, 0)
    for _ in range(TOP_K):
        pltpu.make_async_copy(xq_ref, xs_hbm.at[pl.ds(0, tt * PACK_ROWS)], sem).wait()


def _moe_dispatch(xq, dest_tok):
    tt = DISPATCH_TILE
    n_tiles = N_TOK // tt
    return pl.pallas_call(
        functools.partial(_dispatch_kernel, tt=tt),
        grid=(n_tiles,),
        in_specs=[pl.BlockSpec((1, 1, tt * TOP_K), lambda i: (i, 0, 0), memory_space=pltpu.SMEM),
                  pl.BlockSpec((tt * PACK_ROWS, LANES), lambda i: (i, 0))],
        out_specs=pl.BlockSpec(memory_space=pl.ANY),
        out_shape=jax.ShapeDtypeStruct((MOE_ROWS * PACK_ROWS, LANES), jnp.uint32),
        scratch_shapes=[pltpu.SemaphoreType.DMA(())],
        compiler_params=_params("arbitrary"),
        name="moe_dispatch",
    )(dest_tok.reshape(n_tiles, 1, tt * TOP_K), xq)


def _experts_kernel(be_ref, nused_ref, xs_ref, wg_ref, wu_ref, wd_ref, y_ref, wg_s, wu_s, wd_s):
    i = pl.program_id(0)
    tm = MOE_BLOCK

    @pl.when(i < nused_ref[0])
    def _():
        changed = (i == 0) | (be_ref[i] != be_ref[jnp.maximum(i - 1, 0)])

        @pl.when(changed)
        def _():
            wg_s[...] = wg_ref[0, 0].astype(BF16)
            wu_s[...] = wu_ref[0, 0].astype(BF16)
            wd_s[...] = wd_ref[0, 0].astype(BF16)

        parts = []
        for sl in range(PACK_ROWS):
            parts.extend(_unpack_bf16_pair(xs_ref[pl.ds(sl, tm, stride=PACK_ROWS), :]))
        x = jnp.concatenate(parts, axis=1).astype(BF16)
        hidden = (_silu(jnp.dot(x, wg_s[...], preferred_element_type=F32))
                  * jnp.dot(x, wu_s[...], preferred_element_type=F32))
        y = jnp.dot(hidden.astype(BF16), wd_s[...], preferred_element_type=F32)
        for j in range(F32_ROWS):
            y_ref[pl.ds(j, tm, stride=F32_ROWS), :] = y[:, j * LANES:(j + 1) * LANES]


def _moe_experts(xs, block_e, n_used, layer, w_gate, w_up, w_down):
    last = lambda i, be, nu: jnp.minimum(i, nu[0] - 1)
    return pl.pallas_call(
        _experts_kernel,
        grid_spec=pltpu.PrefetchScalarGridSpec(
            num_scalar_prefetch=2,
            grid=(MOE_N_BLOCKS,),
            in_specs=[pl.BlockSpec((MOE_BLOCK * PACK_ROWS, LANES), lambda i, be, nu: (last(i, be, nu), 0)),
                      pl.BlockSpec((1, 1, D_MODEL, EXPERT_FF), lambda i, be, nu: (layer, be[i], 0, 0)),
                      pl.BlockSpec((1, 1, D_MODEL, EXPERT_FF), lambda i, be, nu: (layer, be[i], 0, 0)),
                      pl.BlockSpec((1, 1, EXPERT_FF, D_MODEL), lambda i, be, nu: (layer, be[i], 0, 0))],
            out_specs=pl.BlockSpec((MOE_BLOCK * F32_ROWS, LANES), lambda i, be, nu: (last(i, be, nu), 0)),
            scratch_shapes=[pltpu.VMEM((D_MODEL, EXPERT_FF), BF16), pltpu.VMEM((D_MODEL, EXPERT_FF), BF16),
                            pltpu.VMEM((EXPERT_FF, D_MODEL), BF16)]),
        out_shape=jax.ShapeDtypeStruct((MOE_ROWS * F32_ROWS, LANES), F32),
        compiler_params=_params("arbitrary"),
        name="moe_experts",
    )(block_e, n_used, xs, w_gate, w_up, w_down)


def _combine_kernel(dest_ref, dest_next_ref, x_ref, gate_ref, swg_ref, swu_ref, swd_ref, g_ref, b_ref, y_hbm,
                    o_ref, buf, sem, *, tt):
    i = pl.program_id(0)
    slot = i % 2

    def fetch(dref, into):
        def body(t, carry):
            dst_row = pl.multiple_of(t * F32_ROWS, F32_ROWS)
            for kk in range(TOP_K):
                src_row = pl.multiple_of(dref[0, 0, t * TOP_K + kk] * F32_ROWS, F32_ROWS)
                pltpu.make_async_copy(y_hbm.at[pl.ds(src_row, F32_ROWS)],
                                      buf.at[into, kk, pl.ds(dst_row, F32_ROWS)], sem.at[into]).start()
            return carry
        lax.fori_loop(0, tt, body, 0)

    @pl.when(i == 0)
    def _():
        fetch(dest_ref, 0)

    @pl.when(i + 1 < pl.num_programs(0))
    def _():
        fetch(dest_next_ref, 1 - slot)

    x = x_ref[...]
    xb = x.astype(BF16)
    hidden = (_silu(jnp.dot(xb, swg_ref[...], preferred_element_type=F32))
              * jnp.dot(xb, swu_ref[...], preferred_element_type=F32))
    shared = jnp.dot(hidden.astype(BF16), swd_ref[...], preferred_element_type=F32)

    for kk in range(TOP_K):
        pltpu.make_async_copy(y_hbm.at[pl.ds(0, tt * F32_ROWS)], buf.at[slot, kk], sem.at[slot]).wait()
    gate = gate_ref[...]
    for j in range(F32_ROWS):
        cols = slice(j * LANES, (j + 1) * LANES)
        routed = gate[:, 0:1] * buf[slot, 0, pl.ds(j, tt, stride=F32_ROWS), :]
        for kk in range(1, TOP_K):
            routed = routed + gate[:, kk:kk + 1] * buf[slot, kk, pl.ds(j, tt, stride=F32_ROWS), :]
        o_ref[:, cols] = DN_ALPHA * x[:, cols] + (routed + shared[:, cols])
    o_ref[...] = _layer_norm(o_ref[...], g_ref[...], b_ref[...])


def _moe_combine(xt, ys, dest_tok, gate_tok, sw_gate, sw_up, sw_down, g, b):
    tt = COMBINE_TILE
    n_tiles = N_TOK // tt
    row = lambda i: (i, 0)
    const = lambda i: (0, 0)
    dest_tiles = dest_tok.reshape(n_tiles, 1, tt * TOP_K)
    return pl.pallas_call(
        functools.partial(_combine_kernel, tt=tt),
        grid=(n_tiles,),
        in_specs=[pl.BlockSpec((1, 1, tt * TOP_K), lambda i: (i, 0, 0), memory_space=pltpu.SMEM),
                  pl.BlockSpec((1, 1, tt * TOP_K), lambda i: (jnp.minimum(i + 1, n_tiles - 1), 0, 0),
                               memory_space=pltpu.SMEM),
                  pl.BlockSpec((tt, D_MODEL), row),
                  pl.BlockSpec((tt, TOP_K), row),
                  pl.BlockSpec((D_MODEL, EXPERT_FF), const),
                  pl.BlockSpec((D_MODEL, EXPERT_FF), const),
                  pl.BlockSpec((EXPERT_FF, D_MODEL), const),
                  pl.BlockSpec((1, D_MODEL), const),
                  pl.BlockSpec((1, D_MODEL), const),
                  pl.BlockSpec(memory_space=pl.ANY)],
        out_specs=pl.BlockSpec((tt, D_MODEL), row),
        out_shape=jax.ShapeDtypeStruct((N_TOK, D_MODEL), F32),
        scratch_shapes=[pltpu.VMEM((2, TOP_K, tt * F32_ROWS, LANES), F32), pltpu.SemaphoreType.DMA((2,))],
        compiler_params=_params("arbitrary"),
        name="moe_combine",
    )(dest_tiles, dest_tiles, xt, gate_tok, sw_gate.astype(BF16), sw_up.astype(BF16), sw_down.astype(BF16),
      g, b, ys)


def _moe_layer(xt, layer, w_router, bias, w_gate, w_up, w_down, sw_gate, sw_up, sw_down, g, b):
    eidx, gate, rank, counts, xq = _moe_route(xt, w_router, bias)

    counts = counts[:, 0]
    pcounts = (counts + MOE_BLOCK - 1) // MOE_BLOCK * MOE_BLOCK
    pends = jnp.cumsum(pcounts)
    pstarts = pends - pcounts
    e_iota = jnp.arange(N_EXPERTS, dtype=jnp.int32)[None, :, None]
    dest = rank + jnp.sum(jnp.where(eidx[:, None, :] == e_iota, pstarts[None, :, None], 0), axis=1)
    dest_tok = dest.T.astype(jnp.int32)
    n_used = (pends[-1] // MOE_BLOCK).astype(jnp.int32)
    blk = jnp.minimum(jnp.arange(MOE_N_BLOCKS, dtype=jnp.int32), n_used - 1) * MOE_BLOCK
    block_e = jnp.sum((pends[None, :] <= blk[:, None]).astype(jnp.int32), axis=1)
    block_e = jnp.minimum(block_e, N_EXPERTS - 1)

    xs = _moe_dispatch(xq, dest_tok)
    ys = _moe_experts(xs, block_e, n_used[None], layer, w_gate, w_up, w_down)
    return _moe_combine(xt, ys, dest_tok, gate.T, sw_gate, sw_up, sw_down, g, b)


def kernel(x_prompt, x_sample, state_pool, cache_swa_k, cache_swa_v, state_mlstm_c, state_mlstm_n, state_mlstm_m, pool_w, pool_scale, swa_w_qkv, swa_w_o, swa_sinks, mlstm_w_in, mlstm_b_gates, mlstm_norm_g, mlstm_w_out, ln_g, ln_b, moe_w_router, moe_router_bias, moe_w_gate, moe_w_up, moe_w_down, moe_shared_w_gate, moe_shared_w_up, moe_shared_w_down):
    d = D_MODEL
    xt = jnp.concatenate([x_prompt.reshape(N_PROMPT_TOK, d), x_sample.reshape(N_SAMPLE_TOK, d)], axis=0)
    pool_p, pool_s = [], []
    swk_p, swv_p, swk_s, swv_s = [], [], [], []
    mc_p, mn_p, mm_p, mc_s, mn_s, mm_s = [], [], [], [], [], []
    for i in range(DEPTH):
        kind, slot = i % N_MIXERS, i // N_MIXERS
        g0, b0 = ln_g[i, 0][None], ln_b[i, 0][None]
        if kind == 0:
            xt, sp, ss = _pool_layer(xt, state_pool[slot], pool_w[slot], pool_scale[slot][None], g0, b0)
            pool_p.append(sp)
            pool_s.append(ss)
        elif kind == 1:
            xt, kp, vp, ks, vs = _swa_layer(xt, cache_swa_k[slot], cache_swa_v[slot], swa_w_qkv[slot],
                                            swa_w_o[slot], swa_sinks[slot], g0, b0)
            swk_p.append(kp)
            swv_p.append(vp)
            swk_s.append(ks)
            swv_s.append(vs)
        else:
            xt, cp, np_, mp, cs, ns, ms = _mlstm_layer(
                xt, state_mlstm_c[slot], state_mlstm_n[slot], state_mlstm_m[slot], mlstm_w_in[slot],
                mlstm_b_gates[slot], mlstm_norm_g[slot][None], mlstm_w_out[slot], g0, b0)
            mc_p.append(cp)
            mn_p.append(np_)
            mm_p.append(mp)
            mc_s.append(cs)
            mn_s.append(ns)
            mm_s.append(ms)
        xt = _moe_layer(xt, i, moe_w_router[i], moe_router_bias[i], moe_w_gate, moe_w_up, moe_w_down,
                        moe_shared_w_gate[i], moe_shared_w_up[i], moe_shared_w_down[i],
                        ln_g[i, 1][None], ln_b[i, 1][None])
    y_p = xt[:N_PROMPT_TOK].reshape(BATCH, SEQ, d)
    y_s = xt[N_PROMPT_TOK:].reshape(DEC_BATCH, DEC_SEQ, d)
    return (y_p, y_s, jnp.stack(pool_p), jnp.stack(pool_s), jnp.stack(swk_p), jnp.stack(swv_p),
            jnp.stack(swk_s), jnp.stack(swv_s), jnp.stack(mc_p), jnp.stack(mn_p), jnp.stack(mm_p),
            jnp.stack(mc_s), jnp.stack(mn_s), jnp.stack(mm_s))
```

```python
import functools
import math

import jax
import jax.numpy as jnp
from jax import lax
from jax.experimental import pallas as pl
from jax.experimental.pallas import tpu as pltpu

F32 = jnp.float32
BF16 = jnp.bfloat16

D_MODEL = 2048
BATCH = 2
SEQ = 4096
DEPTH = 4
DEC_BATCH = 32
DEC_SEQ = 4
PAST_LEN = 16384
N_PROMPT_TOK = BATCH * SEQ
N_SAMPLE_TOK = DEC_BATCH * DEC_SEQ
N_TOK = N_PROMPT_TOK + N_SAMPLE_TOK

N_MIXERS = 3
DN_ALPHA = (2.0 * DEPTH) ** 0.25
LN_EPS = 1e-5

POOL_WINDOWS = (2, 4, 8, 16)
POOL_GROUP_DIM = D_MODEL // len(POOL_WINDOWS)
POOL_STATE = max(POOL_WINDOWS) - 1
POOL_HALO = POOL_STATE + 1

SWA_WINDOW = 128
SWA_HEAD_DIM = 64
SWA_HEADS = D_MODEL // SWA_HEAD_DIM
SWA_KV_HEADS = SWA_HEADS // 8
SWA_GROUP = SWA_HEADS // SWA_KV_HEADS
SWA_KV_DIM = SWA_KV_HEADS * SWA_HEAD_DIM

MLSTM_HEADS = 8
MLSTM_HEAD_DIM = D_MODEL // MLSTM_HEADS
MLSTM_CHUNK = 64
MLSTM_SAMPLE_CHUNK = 8

N_EXPERTS = 64
TOP_K = 8
N_EXPERT_GROUPS = 8
TOPK_GROUPS = 4
EXPERT_FF = D_MODEL // 4
ROUTED_SCALE = 2.5

VMEM_LIMIT_BYTES = 56 * 1024 * 1024

ROW_TILE = 320
ROUTER_TILE = 640
POOL_TILE = 512
MOE_BLOCK = 256
MOE_N_BLOCKS = N_TOK * TOP_K // MOE_BLOCK + N_EXPERTS
MOE_ROWS = MOE_N_BLOCKS * MOE_BLOCK
DISPATCH_TILE = 320
COMBINE_TILE = 128
LANES = 128
PACK_ROWS = D_MODEL // (2 * LANES)

_NT = (((1,), (1,)), ((), ()))
_TN = (((0,), (0,)), ((), ()))


def _params(*semantics):
    return pltpu.CompilerParams(dimension_semantics=semantics, vmem_limit_bytes=VMEM_LIMIT_BYTES)


def _layer_norm(z, g, b):
    mu = jnp.mean(z, axis=-1, keepdims=True)
    zc = z - mu
    var = jnp.mean(zc * zc, axis=-1, keepdims=True)
    return zc * lax.rsqrt(var + LN_EPS) * g + b


def _silu(x):
    return x * jax.nn.sigmoid(x)


def _matmul_kernel(x_ref, w_ref, o_ref):
    o_ref[...] = jnp.dot(x_ref[...].astype(BF16), w_ref[...], preferred_element_type=F32)


def _matmul(x, w, *, tm, tn):
    m, k = x.shape
    n = w.shape[1]
    return pl.pallas_call(
        _matmul_kernel,
        grid=(n // tn, m // tm),
        in_specs=[pl.BlockSpec((tm, k), lambda j, i: (i, 0)),
                  pl.BlockSpec((k, tn), lambda j, i: (0, j))],
        out_specs=pl.BlockSpec((tm, tn), lambda j, i: (i, j)),
        out_shape=jax.ShapeDtypeStruct((m, n), F32),
        compiler_params=_params("parallel", "parallel"),
        name="matmul",
    )(x, w)


def _matmul_ln_kernel(a_ref, w_ref, res_ref, g_ref, b_ref, o_ref):
    y = jnp.dot(a_ref[...].astype(BF16), w_ref[...], preferred_element_type=F32)
    o_ref[...] = _layer_norm(DN_ALPHA * res_ref[...] + y, g_ref[...], b_ref[...])


def _matmul_ln(a, w, res, g, b):
    m, k = a.shape
    row = lambda i: (i, 0)
    const = lambda i: (0, 0)
    return pl.pallas_call(
        _matmul_ln_kernel,
        grid=(m // ROW_TILE,),
        in_specs=[pl.BlockSpec((ROW_TILE, k), row),
                  pl.BlockSpec((k, D_MODEL), const),
                  pl.BlockSpec((ROW_TILE, D_MODEL), row),
                  pl.BlockSpec((1, D_MODEL), const),
                  pl.BlockSpec((1, D_MODEL), const)],
        out_specs=pl.BlockSpec((ROW_TILE, D_MODEL), row),
        out_shape=jax.ShapeDtypeStruct((m, D_MODEL), F32),
        compiler_params=_params("parallel"),
        name="matmul_ln",
    )(a, w, res, g, b)


def _pool_core(zbuf, w_ref, scale_ref, g_ref, b_ref, o_ref, *, tt, n_before):
    avail = n_before + lax.broadcasted_iota(jnp.int32, (tt, 1), 0) + 1
    for grp, win in enumerate(POOL_WINDOWS):
        c0, c1 = grp * POOL_GROUP_DIM, (grp + 1) * POOL_GROUP_DIM
        xg = zbuf[POOL_HALO:POOL_HALO + tt, c0:c1]
        total = xg
        for back in range(1, win):
            total = total + zbuf[POOL_HALO - back:POOL_HALO - back + tt, c0:c1]
        count = jnp.minimum(win, avail).astype(F32)
        diff = total / count - xg
        y = jnp.dot(diff.astype(BF16), w_ref[grp], preferred_element_type=F32) * scale_ref[:, c0:c1]
        o_ref[:, c0:c1] = DN_ALPHA * xg + y
    o_ref[...] = _layer_norm(o_ref[...], g_ref[...], b_ref[...])


def _pool_prompt_kernel(x_ref, halo_ref, w_ref, scale_ref, g_ref, b_ref, o_ref, zbuf, *, tt):
    i = pl.program_id(1)
    zbuf[0:POOL_HALO, :] = jnp.where(i == 0, 0.0, halo_ref[...])
    zbuf[POOL_HALO:POOL_HALO + tt, :] = x_ref[...]
    _pool_core(zbuf, w_ref, scale_ref, g_ref, b_ref, o_ref, tt=tt, n_before=i * tt)


def _pool_sample_kernel(z_ref, w_ref, scale_ref, g_ref, b_ref, o_ref, *, tt):
    _pool_core(z_ref.at[0], w_ref, scale_ref, g_ref, b_ref, o_ref.at[0], tt=tt, n_before=PAST_LEN)


def _pool_layer(xt, state, w, scale, g, b):
    tt = POOL_TILE
    tiles = SEQ // tt
    halo_per_tile = tt // POOL_HALO
    const2 = lambda bb, i: (0, 0)
    w_bf = w.astype(BF16)
    out = pl.pallas_call(
        functools.partial(_pool_prompt_kernel, tt=tt),
        grid=(BATCH, tiles),
        in_specs=[pl.BlockSpec((tt, D_MODEL), lambda bb, i: (bb * tiles + i, 0)),
                  pl.BlockSpec((POOL_HALO, D_MODEL),
                               lambda bb, i: (jnp.maximum((bb * tiles + i) * halo_per_tile - 1, 0), 0)),
                  pl.BlockSpec(w_bf.shape, lambda bb, i: (0, 0, 0)),
                  pl.BlockSpec((1, D_MODEL), const2),
                  pl.BlockSpec((1, D_MODEL), const2),
                  pl.BlockSpec((1, D_MODEL), const2)],
        out_specs=pl.BlockSpec((tt, D_MODEL), lambda bb, i: (bb * tiles + i, 0)),
        out_shape=jax.ShapeDtypeStruct((N_TOK, D_MODEL), F32),
        scratch_shapes=[pltpu.VMEM((POOL_HALO + tt, D_MODEL), F32)],
        compiler_params=_params("parallel", "arbitrary"),
        name="pool_prompt",
    )(xt, xt, w_bf, scale, g, b)

    ts = 16
    xs = xt[N_PROMPT_TOK:].reshape(DEC_BATCH, DEC_SEQ, D_MODEL)
    zs = jnp.concatenate([jnp.zeros((DEC_BATCH, 1, D_MODEL), F32), state, xs,
                          jnp.zeros((DEC_BATCH, ts - DEC_SEQ, D_MODEL), F32)], axis=1)
    const1 = lambda bb: (0, 0)
    out_s = pl.pallas_call(
        functools.partial(_pool_sample_kernel, tt=ts),
        grid=(DEC_BATCH,),
        in_specs=[pl.BlockSpec((1, POOL_HALO + ts, D_MODEL), lambda bb: (bb, 0, 0)),
                  pl.BlockSpec(w_bf.shape, lambda bb: (0, 0, 0)),
                  pl.BlockSpec((1, D_MODEL), const1),
                  pl.BlockSpec((1, D_MODEL), const1),
                  pl.BlockSpec((1, D_MODEL), const1)],
        out_specs=pl.BlockSpec((1, ts, D_MODEL), lambda bb: (bb, 0, 0)),
        out_shape=jax.ShapeDtypeStruct((DEC_BATCH, ts, D_MODEL), F32),
        compiler_params=_params("parallel"),
        name="pool_sample",
    )(zs, w_bf, scale, g, b)
    out = lax.dynamic_update_slice(out, out_s[:, :DEC_SEQ].reshape(N_SAMPLE_TOK, D_MODEL), (N_PROMPT_TOK, 0))

    new_p = xt[:N_PROMPT_TOK].reshape(BATCH, SEQ, D_MODEL)[:, SEQ - POOL_STATE:]
    new_s = jnp.concatenate([state, xs], axis=1)[:, DEC_SEQ:]
    return out, new_p, new_s


def _alibi_slope(h):
    return 2.0 ** (-8.0 * (h + 1.0) / SWA_HEADS)


def _attn_core(q_ref, k_all, v_all, sink_ref, o_ref, *, rows, first_block):
    w = SWA_WINDOW
    qi = lax.broadcasted_iota(jnp.int32, (rows, 2 * w), 0)
    sj = lax.broadcasted_iota(jnp.int32, (rows, 2 * w), 1)
    dist = (w + qi) - sj
    valid = (dist >= 0) & (dist <= w)
    if first_block is not None:
        valid = valid & ((sj >= w) | jnp.logical_not(first_block))
    distf = dist.astype(F32)
    for h in range(SWA_HEADS):
        kv = h // SWA_GROUP
        c0, c1 = kv * SWA_HEAD_DIM, (kv + 1) * SWA_HEAD_DIM
        q = q_ref[:, h * SWA_HEAD_DIM:(h + 1) * SWA_HEAD_DIM].astype(BF16)
        s = lax.dot_general(q, k_all[:, c0:c1], _NT, preferred_element_type=F32) * (SWA_HEAD_DIM ** -0.5)
        s = jnp.where(valid, s - _alibi_slope(h) * distf, -jnp.inf)
        sink = sink_ref[h]
        m = jnp.maximum(jnp.max(s, axis=1, keepdims=True), sink)
        e = jnp.exp(s - m)
        den = jnp.sum(e, axis=1, keepdims=True) + jnp.exp(sink - m)
        o = jnp.dot(e.astype(BF16), v_all[:, c0:c1], preferred_element_type=F32) / den
        o_ref[:, h * SWA_HEAD_DIM:(h + 1) * SWA_HEAD_DIM] = o


def _attn_prompt_kernel(sink_ref, q_ref, kp_ref, kc_ref, vp_ref, vc_ref, o_ref):
    n = pl.program_id(1)
    k_all = jnp.concatenate([kp_ref[...], kc_ref[...]], axis=0).astype(BF16)
    v_all = jnp.concatenate([vp_ref[...], vc_ref[...]], axis=0).astype(BF16)
    _attn_core(q_ref, k_all, v_all, sink_ref, o_ref, rows=SWA_WINDOW, first_block=(n == 0))


def _attn_sample_kernel(sink_ref, q_ref, k_ref, v_ref, o_ref, *, rows):
    _attn_core(q_ref.at[0], k_ref[0].astype(BF16), v_ref[0].astype(BF16), sink_ref, o_ref.at[0],
               rows=rows, first_block=None)


def _swa_layer(xt, cache_k, cache_v, w_qkv, w_o, sinks, g, b):
    w = SWA_WINDOW
    qkv = _matmul(xt, w_qkv.astype(BF16), tm=640, tn=1280)
    nb = SEQ // w
    k_col = D_MODEL // SWA_KV_DIM
    cur = lambda bb, n, *_: (bb * nb + n, k_col)
    prev = lambda bb, n, *_: (jnp.maximum(bb * nb + n - 1, 0), k_col)
    cur_v = lambda bb, n, *_: (bb * nb + n, k_col + 1)
    prev_v = lambda bb, n, *_: (jnp.maximum(bb * nb + n - 1, 0), k_col + 1)
    o = pl.pallas_call(
        _attn_prompt_kernel,
        grid_spec=pltpu.PrefetchScalarGridSpec(
            num_scalar_prefetch=1,
            grid=(BATCH, nb),
            in_specs=[pl.BlockSpec((w, D_MODEL), lambda bb, n, *_: (bb * nb + n, 0)),
                      pl.BlockSpec((w, SWA_KV_DIM), prev),
                      pl.BlockSpec((w, SWA_KV_DIM), cur),
                      pl.BlockSpec((w, SWA_KV_DIM), prev_v),
                      pl.BlockSpec((w, SWA_KV_DIM), cur_v)],
            out_specs=pl.BlockSpec((w, D_MODEL), lambda bb, n, *_: (bb * nb + n, 0))),
        out_shape=jax.ShapeDtypeStruct((N_TOK, D_MODEL), F32),
        compiler_params=_params("parallel", "arbitrary"),
        name="attn_prompt",
    )(sinks, qkv, qkv, qkv, qkv, qkv)

    rows = 16
    qkv_s = qkv[N_PROMPT_TOK:].reshape(DEC_BATCH, DEC_SEQ, -1)
    q_s = jnp.pad(qkv_s[..., :D_MODEL], ((0, 0), (0, rows - DEC_SEQ), (0, 0)))
    k_new = qkv_s[..., D_MODEL:D_MODEL + SWA_KV_DIM]
    v_new = qkv_s[..., D_MODEL + SWA_KV_DIM:]
    kz = jnp.concatenate([cache_k.reshape(DEC_BATCH, w, SWA_KV_DIM), k_new], axis=1)
    vz = jnp.concatenate([cache_v.reshape(DEC_BATCH, w, SWA_KV_DIM), v_new], axis=1)
    pad_keys = ((0, 0), (0, w - DEC_SEQ), (0, 0))
    o_s = pl.pallas_call(
        functools.partial(_attn_sample_kernel, rows=rows),
        grid_spec=pltpu.PrefetchScalarGridSpec(
            num_scalar_prefetch=1,
            grid=(DEC_BATCH,),
            in_specs=[pl.BlockSpec((1, rows, D_MODEL), lambda bb, *_: (bb, 0, 0)),
                      pl.BlockSpec((1, 2 * w, SWA_KV_DIM), lambda bb, *_: (bb, 0, 0)),
                      pl.BlockSpec((1, 2 * w, SWA_KV_DIM), lambda bb, *_: (bb, 0, 0))],
            out_specs=pl.BlockSpec((1, rows, D_MODEL), lambda bb, *_: (bb, 0, 0))),
        out_shape=jax.ShapeDtypeStruct((DEC_BATCH, rows, D_MODEL), F32),
        compiler_params=_params("parallel"),
        name="attn_sample",
    )(sinks, q_s, jnp.pad(kz, pad_keys), jnp.pad(vz, pad_keys))
    o = lax.dynamic_update_slice(o, o_s[:, :DEC_SEQ].reshape(N_SAMPLE_TOK, D_MODEL), (N_PROMPT_TOK, 0))

    out = _matmul_ln(o, w_o.astype(BF16), xt, g, b)

    kv_shape = (SWA_WINDOW, SWA_KV_HEADS, SWA_HEAD_DIM)
    qkv_p = qkv[:N_PROMPT_TOK].reshape(BATCH, SEQ, -1)[:, SEQ - w:]
    new_k_p = qkv_p[..., D_MODEL:D_MODEL + SWA_KV_DIM].reshape((BATCH,) + kv_shape)
    new_v_p = qkv_p[..., D_MODEL + SWA_KV_DIM:].reshape((BATCH,) + kv_shape)
    new_k_s = kz[:, DEC_SEQ:].reshape((DEC_BATCH,) + kv_shape)
    new_v_s = vz[:, DEC_SEQ:].reshape((DEC_BATCH,) + kv_shape)
    return out, new_k_p, new_v_p, new_k_s, new_v_s


def _log_sigmoid(x):
    return jnp.minimum(x, 0.0) - jnp.log(1.0 + jnp.exp(-jnp.abs(x)))


def _mlstm_kernel(bias_ref, q_ref, k_ref, v_ref, og_ref, gi_ref, gf_ref, c0_ref, n0_ref, m0_ref, ng_ref,
                  h_ref, c_ref, n_ref, m_ref, c_scr, n_scr, m_scr, *, chunk, n_valid):
    head = pl.program_id(1)
    step = pl.program_id(2)
    ln = chunk

    @pl.when(step == 0)
    def _():
        c_scr[...] = c0_ref[0, 0]
        n_scr[...] = n0_ref[0, 0]
        m_scr[...] = m0_ref[0, 0]

    ig = gi_ref[0, 0, pl.ds(step, 1), :] + bias_ref[head]
    lf = _log_sigmoid(gf_ref[0, 0, pl.ds(step, 1), :] + bias_ref[MLSTM_HEADS + head])
    if n_valid < ln:
        col = lax.broadcasted_iota(jnp.int32, (1, ln), 1)
        ig = jnp.where(col < n_valid, ig, -1e30)
        lf = jnp.where(col < n_valid, lf, 0.0)

    ri = lax.broadcasted_iota(jnp.int32, (ln, ln), 0)
    ci = lax.broadcasted_iota(jnp.int32, (ln, ln), 1)
    eye = ri == ci
    causal = ci <= ri

    def to_col(row):
        return jnp.sum(jnp.where(eye, row, 0.0), axis=1, keepdims=True)

    f_col = to_col(lf)
    b_col = jnp.sum(jnp.where(causal, lf, 0.0), axis=1, keepdims=True)
    b_row = jnp.sum(jnp.where(ri <= ci, f_col, 0.0), axis=0, keepdims=True)
    b_last = jnp.sum(lf, axis=1, keepdims=True)
    m_prev = m_scr[:, 0:1]
    a_col = b_col + m_prev
    dmat = jnp.where(causal, b_col - b_row + ig, -jnp.inf)
    mt = jnp.maximum(a_col, jnp.max(dmat, axis=1, keepdims=True))
    w_inter = jnp.exp(a_col - mt)

    q = q_ref[...]
    k = k_ref[...] * (MLSTM_HEAD_DIM ** -0.5)
    v = v_ref[...]
    qb = q.astype(BF16)
    kb = k.astype(BF16)
    c_prev = c_scr[...]
    n_prev = n_scr[...]
    scores = lax.dot_general(qb, kb, _NT, preferred_element_type=F32)
    w_intra = jnp.exp(dmat - mt) * scores
    num = (w_inter * lax.dot_general(qb, c_prev.astype(BF16), _NT, preferred_element_type=F32)
           + jnp.dot(w_intra.astype(BF16), v.astype(BF16), preferred_element_type=F32))
    den = (w_inter * jnp.sum(q * n_prev, axis=1, keepdims=True)
           + jnp.sum(w_intra, axis=1, keepdims=True))
    h = num / jnp.maximum(jnp.abs(den), jnp.exp(-mt))

    g_row = b_last - b_row + ig
    m_new = jnp.maximum(b_last + m_prev, jnp.max(g_row, axis=1, keepdims=True))
    decay = jnp.exp(b_last + m_prev - m_new)
    wg_col = to_col(jnp.exp(g_row - m_new))
    c_new = decay * c_prev + lax.dot_general((v * wg_col).astype(BF16), kb, _TN, preferred_element_type=F32)
    n_new = decay * n_prev + jnp.sum(wg_col * k, axis=0, keepdims=True)
    m_new_b = jnp.broadcast_to(m_new, m_scr.shape)
    c_scr[...] = c_new
    n_scr[...] = n_new
    m_scr[...] = m_new_b

    mu = jnp.mean(h, axis=1, keepdims=True)
    hc = h - mu
    var = jnp.mean(hc * hc, axis=1, keepdims=True)
    h_ref[...] = hc * lax.rsqrt(var + LN_EPS) * ng_ref[...] * jax.nn.sigmoid(og_ref[...])

    @pl.when(step == pl.num_programs(2) - 1)
    def _():
        c_ref[0, 0] = c_new
        n_ref[0, 0] = n_new
        m_ref[0, 0] = m_new_b


def _mlstm_scan(proj, gates, b_gates, norm_g, c0, n0, m0, *, batch, n_chunks, chunk, n_valid, out_rows):
    dh, nh = MLSTM_HEAD_DIM, MLSTM_HEADS
    rows = lambda off: (lambda bb, hh, cc, *_: (bb * n_chunks + cc, off + hh))
    per_head = lambda bb, hh, cc, *_: (bb, hh, 0, 0)
    gate_i = lambda bb, hh, cc, *_: (bb, hh, 0, 0)
    gate_f = lambda bb, hh, cc, *_: (bb, nh + hh, 0, 0)
    return pl.pallas_call(
        functools.partial(_mlstm_kernel, chunk=chunk, n_valid=n_valid),
        grid_spec=pltpu.PrefetchScalarGridSpec(
            num_scalar_prefetch=1,
            grid=(batch, nh, n_chunks),
            in_specs=[pl.BlockSpec((chunk, dh), rows(0)),
                      pl.BlockSpec((chunk, dh), rows(nh)),
                      pl.BlockSpec((chunk, dh), rows(2 * nh)),
                      pl.BlockSpec((chunk, dh), rows(3 * nh)),
                      pl.BlockSpec((1, 1, n_chunks, chunk), gate_i),
                      pl.BlockSpec((1, 1, n_chunks, chunk), gate_f),
                      pl.BlockSpec((1, 1, dh, dh), per_head),
                      pl.BlockSpec((1, 1, 1, dh), per_head),
                      pl.BlockSpec((1, 1, 1, 128), per_head),
                      pl.BlockSpec((1, dh), lambda bb, hh, cc, *_: (0, hh))],
            out_specs=[pl.BlockSpec((chunk, dh), lambda bb, hh, cc, *_: (bb * n_chunks + cc, hh)),
                       pl.BlockSpec((1, 1, dh, dh), per_head),
                       pl.BlockSpec((1, 1, 1, dh), per_head),
                       pl.BlockSpec((1, 1, 1, 128), per_head)],
            scratch_shapes=[pltpu.VMEM((dh, dh), F32), pltpu.VMEM((1, dh), F32), pltpu.VMEM((1, 128), F32)]),
        out_shape=[jax.ShapeDtypeStruct((out_rows, D_MODEL), F32),
                   jax.ShapeDtypeStruct((batch, nh, dh, dh), F32),
                   jax.ShapeDtypeStruct((batch, nh, 1, dh), F32),
                   jax.ShapeDtypeStruct((batch, nh, 1, 128), F32)],
        compiler_params=_params("parallel", "parallel", "arbitrary"),
        name="mlstm_scan",
    )(b_gates, proj, proj, proj, proj, gates, gates, c0, n0[:, :, None, :],
      jnp.broadcast_to(m0[:, :, None, None], (batch, nh, 1, 128)), norm_g)


def _mlstm_layer(xt, c0_s, n0_s, m0_s, w_in, b_gates, norm_g, w_out, g, b):
    d, nh, dh = D_MODEL, MLSTM_HEADS, MLSTM_HEAD_DIM
    proj = _matmul(xt, w_in[:, :4 * d].astype(BF16), tm=640, tn=1024)
    w_gates = jnp.pad(w_in[:, 4 * d:], ((0, 0), (0, 128 - 2 * nh)))
    gate_pre = _matmul(xt, w_gates.astype(BF16), tm=640, tn=128)[:, :2 * nh]

    nc = SEQ // MLSTM_CHUNK
    gates_p = gate_pre[:N_PROMPT_TOK].reshape(BATCH, nc, MLSTM_CHUNK, 2 * nh).transpose(0, 3, 1, 2)
    zeros = lambda *s: jnp.zeros(s, F32)
    hn, c_p, n_p, m_p = _mlstm_scan(
        proj, gates_p, b_gates, norm_g, zeros(BATCH, nh, dh, dh), zeros(BATCH, nh, dh), zeros(BATCH, nh),
        batch=BATCH, n_chunks=nc, chunk=MLSTM_CHUNK, n_valid=MLSTM_CHUNK, out_rows=N_TOK)

    ls = MLSTM_SAMPLE_CHUNK
    pad_t = ((0, 0), (0, ls - DEC_SEQ), (0, 0))
    proj_s = jnp.pad(proj[N_PROMPT_TOK:].reshape(DEC_BATCH, DEC_SEQ, 4 * d), pad_t).reshape(DEC_BATCH * ls, 4 * d)
    gates_s = jnp.pad(gate_pre[N_PROMPT_TOK:].reshape(DEC_BATCH, DEC_SEQ, 2 * nh), pad_t)
    gates_s = gates_s.transpose(0, 2, 1)[:, :, None, :]
    hn_s, c_s, n_s, m_s = _mlstm_scan(
        proj_s, gates_s, b_gates, norm_g, c0_s, n0_s, m0_s,
        batch=DEC_BATCH, n_chunks=1, chunk=ls, n_valid=DEC_SEQ, out_rows=DEC_BATCH * ls)
    hn_s = hn_s.reshape(DEC_BATCH, ls, d)[:, :DEC_SEQ].reshape(N_SAMPLE_TOK, d)
    hn = lax.dynamic_update_slice(hn, hn_s, (N_PROMPT_TOK, 0))

    out = _matmul_ln(hn, w_out.astype(BF16), xt, g, b)
    return (out, c_p, n_p[:, :, 0], m_p[:, :, 0, 0], c_s, n_s[:, :, 0], m_s[:, :, 0, 0])


def _router_kernel(x_ref, wt_ref, bias_ref, tri_ref, eidx_ref, gate_ref, rank_ref, count_ref, xq_ref, count_scr,
                   *, tt):
    neg = -jnp.inf
    x = x_ref[...]
    logits = lax.dot_general(wt_ref[...].astype(BF16), x.astype(BF16), _NT, preferred_element_type=F32)
    s = jax.nn.sigmoid(logits)
    sb = s + bias_ref[...]
    per_group = N_EXPERTS // N_EXPERT_GROUPS

    sb3 = sb.reshape(N_EXPERT_GROUPS, per_group, tt)
    i3 = lax.broadcasted_iota(jnp.int32, sb3.shape, 1)
    m1 = jnp.max(sb3, axis=1, keepdims=True)
    first = jnp.min(jnp.where(sb3 == m1, i3, per_group), axis=1, keepdims=True)
    m2 = jnp.max(jnp.where(i3 == first, neg, sb3), axis=1, keepdims=True)
    gscore = (m1 + m2).reshape(N_EXPERT_GROUPS, tt)

    gi = lax.broadcasted_iota(jnp.int32, gscore.shape, 0)
    gsel = jnp.zeros(gscore.shape, jnp.bool_)
    cur = gscore
    for _ in range(TOPK_GROUPS):
        mx = jnp.max(cur, axis=0, keepdims=True)
        pick = gi == jnp.min(jnp.where(cur == mx, gi, N_EXPERT_GROUPS), axis=0, keepdims=True)
        gsel = gsel | pick
        cur = jnp.where(pick, neg, cur)
    emask = jnp.broadcast_to(gsel.reshape(N_EXPERT_GROUPS, 1, tt), sb3.shape).reshape(N_EXPERTS, tt)

    ei = lax.broadcasted_iota(jnp.int32, sb.shape, 0)
    sel = jnp.zeros(sb.shape, jnp.bool_)
    cur = jnp.where(emask, sb, neg)
    picks = []
    for _ in range(TOP_K):
        mx = jnp.max(cur, axis=0, keepdims=True)
        idx = jnp.min(jnp.where(cur == mx, ei, N_EXPERTS), axis=0, keepdims=True)
        pick = ei == idx
        picks.append((idx, pick))
        sel = sel | pick
        cur = jnp.where(pick, neg, cur)
    s_sel = jnp.where(sel, s, 0.0)
    gate = s_sel / jnp.sum(s_sel, axis=0, keepdims=True) * ROUTED_SCALE

    @pl.when(pl.program_id(0) == 0)
    def _():
        count_scr[...] = jnp.zeros_like(count_scr)

    sel_f = sel.astype(F32)
    incl = jnp.dot(sel_f.astype(BF16), tri_ref[...], preferred_element_type=F32)
    rank = count_scr[:, 0:1] + incl - sel_f
    count_scr[...] = count_scr[...] + jnp.sum(sel_f, axis=1, keepdims=True)
    count_ref[...] = count_scr[...].astype(jnp.int32)

    for kk, (idx, pick) in enumerate(picks):
        eidx_ref[kk:kk + 1, :] = idx
        gate_ref[kk:kk + 1, :] = jnp.sum(jnp.where(pick, gate, 0.0), axis=0, keepdims=True)
        rank_ref[kk:kk + 1, :] = jnp.sum(jnp.where(pick, rank, 0.0), axis=0, keepdims=True).astype(jnp.int32)

    for sl in range(PACK_ROWS):
        c0 = 2 * LANES * sl
        xq_ref[pl.ds(sl, tt, stride=PACK_ROWS), :] = _pack_bf16_pair(x[:, c0:c0 + LANES],
                                                                     x[:, c0 + LANES:c0 + 2 * LANES])


def _pack_bf16_pair(hi, lo):
    hb = lax.bitcast_convert_type(hi.astype(BF16).astype(F32), jnp.uint32)
    lb = lax.bitcast_convert_type(lo.astype(BF16).astype(F32), jnp.uint32)
    return hb | (lb >> 16)


def _unpack_bf16_pair(word):
    hi = lax.bitcast_convert_type(word & jnp.uint32(0xFFFF0000), F32)
    lo = lax.bitcast_convert_type(word << 16, F32)
    return hi, lo


def _moe_route(xt, w_router, bias):
    tt = ROUTER_TILE
    col = lambda i: (0, i)
    const = lambda i: (0, 0)
    tri = (jnp.arange(tt)[:, None] <= jnp.arange(tt)[None, :]).astype(BF16)
    return pl.pallas_call(
        functools.partial(_router_kernel, tt=tt),
        grid=(N_TOK // tt,),
        in_specs=[pl.BlockSpec((tt, D_MODEL), lambda i: (i, 0)),
                  pl.BlockSpec((N_EXPERTS, D_MODEL), const),
                  pl.BlockSpec((N_EXPERTS, 1), const),
                  pl.BlockSpec((tt, tt), const)],
        out_specs=[pl.BlockSpec((TOP_K, tt), col), pl.BlockSpec((TOP_K, tt), col), pl.BlockSpec((TOP_K, tt), col),
                   pl.BlockSpec((N_EXPERTS, LANES), const),
                   pl.BlockSpec((tt * PACK_ROWS, LANES), lambda i: (i, 0))],
        out_shape=[jax.ShapeDtypeStruct((TOP_K, N_TOK), jnp.int32),
                   jax.ShapeDtypeStruct((TOP_K, N_TOK), F32),
                   jax.ShapeDtypeStruct((TOP_K, N_TOK), jnp.int32),
                   jax.ShapeDtypeStruct((N_EXPERTS, LANES), jnp.int32),
                   jax.ShapeDtypeStruct((N_TOK * PACK_ROWS, LANES), jnp.uint32)],
        scratch_shapes=[pltpu.VMEM((N_EXPERTS, LANES), F32)],
        compiler_params=_params("arbitrary"),
        name="moe_router",
    )(xt, w_router.T, bias[:, None], tri)


def _dispatch_kernel(dest_ref, xq_ref, xs_hbm, sem, *, tt):
    def issue(t, carry):
        src = xq_ref.at[pl.ds(pl.multiple_of(t * PACK_ROWS, PACK_ROWS), PACK_ROWS)]
        for kk in range(TOP_K):
            row = pl.multiple_of(dest_ref[0, 0, t * TOP_K + kk] * PACK_ROWS, PACK_ROWS)
            pltpu.make_async_copy(src, xs_hbm.at[pl.ds(row, PACK_ROWS)], sem).start()
        return carry
    lax.fori_loop(0, tt, issue, 0)
    for _ in range(TOP_K):
        pltpu.make_async_copy(xq_ref, xs_hbm.at[pl.ds(0, tt * PACK_ROWS)], sem).wait()


def _moe_dispatch(xq, dest_tok):
    tt = DISPATCH_TILE
    n_tiles = N_TOK // tt
    return pl.pallas_call(
        functools.partial(_dispatch_kernel, tt=tt),
        grid=(n_tiles,),
        in_specs=[pl.BlockSpec((1, 1, tt * TOP_K), lambda i: (i, 0, 0), memory_space=pltpu.SMEM),
                  pl.BlockSpec((tt * PACK_ROWS, LANES), lambda i: (i, 0))],
        out_specs=pl.BlockSpec(memory_space=pl.ANY),
        out_shape=jax.ShapeDtypeStruct((MOE_ROWS * PACK_ROWS, LANES), jnp.uint32),
        scratch_shapes=[pltpu.SemaphoreType.DMA(())],
        compiler_params=_params("arbitrary"),
        name="moe_dispatch",
    )(dest_tok.reshape(n_tiles, 1, tt * TOP_K), xq)


def _experts_kernel(be_ref, next_ref, nused_ref, xs_ref, wg_hbm, wu_hbm, wd_hbm, y_ref,
                    wg_f, wu_f, wd_f, wg_s, wu_s, wd_s, sem, *, layer):
    i = pl.program_id(0)
    tm = MOE_BLOCK

    def weight_copies(e):
        return (pltpu.make_async_copy(wg_hbm.at[layer, e], wg_f, sem.at[0]),
                pltpu.make_async_copy(wu_hbm.at[layer, e], wu_f, sem.at[1]),
                pltpu.make_async_copy(wd_hbm.at[layer, e], wd_f, sem.at[2]))

    @pl.when(i == 0)
    def _():
        for cp in weight_copies(be_ref[0]):
            cp.start()

    @pl.when(i < nused_ref[0])
    def _():
        first_of_expert = (i == 0) | (be_ref[i] != be_ref[jnp.maximum(i - 1, 0)])

        @pl.when(first_of_expert)
        def _():
            for cp in weight_copies(be_ref[i]):
                cp.wait()
            wg_s[...] = wg_f[...].astype(BF16)
            wu_s[...] = wu_f[...].astype(BF16)
            wd_s[...] = wd_f[...].astype(BF16)

            @pl.when(next_ref[i] >= 0)
            def _():
                for cp in weight_copies(next_ref[i]):
                    cp.start()

        parts = []
        for sl in range(PACK_ROWS):
            parts.extend(_unpack_bf16_pair(xs_ref[pl.ds(sl, tm, stride=PACK_ROWS), :]))
        x = jnp.concatenate(parts, axis=1).astype(BF16)
        hidden = (_silu(jnp.dot(x, wg_s[...], preferred_element_type=F32))
                  * jnp.dot(x, wu_s[...], preferred_element_type=F32))
        y = jnp.dot(hidden.astype(BF16), wd_s[...], preferred_element_type=F32)
        for sl in range(PACK_ROWS):
            c0 = 2 * LANES * sl
            y_ref[pl.ds(sl, tm, stride=PACK_ROWS), :] = _pack_bf16_pair(y[:, c0:c0 + LANES],
                                                                        y[:, c0 + LANES:c0 + 2 * LANES])


def _moe_experts(xs, block_e, next_e, n_used, layer, w_gate, w_up, w_down):
    last = lambda i, be, nx, nu: (jnp.minimum(i, nu[0] - 1), 0)
    return pl.pallas_call(
        functools.partial(_experts_kernel, layer=layer),
        grid_spec=pltpu.PrefetchScalarGridSpec(
            num_scalar_prefetch=3,
            grid=(MOE_N_BLOCKS,),
            in_specs=[pl.BlockSpec((MOE_BLOCK * PACK_ROWS, LANES), last),
                      pl.BlockSpec(memory_space=pl.ANY),
                      pl.BlockSpec(memory_space=pl.ANY),
                      pl.BlockSpec(memory_space=pl.ANY)],
            out_specs=pl.BlockSpec((MOE_BLOCK * PACK_ROWS, LANES), last),
            scratch_shapes=[pltpu.VMEM((D_MODEL, EXPERT_FF), F32), pltpu.VMEM((D_MODEL, EXPERT_FF), F32),
                            pltpu.VMEM((EXPERT_FF, D_MODEL), F32),
                            pltpu.VMEM((D_MODEL, EXPERT_FF), BF16), pltpu.VMEM((D_MODEL, EXPERT_FF), BF16),
                            pltpu.VMEM((EXPERT_FF, D_MODEL), BF16),
                            pltpu.SemaphoreType.DMA((3,))]),
        out_shape=jax.ShapeDtypeStruct((MOE_ROWS * PACK_ROWS, LANES), jnp.uint32),
        compiler_params=_params("arbitrary"),
        name="moe_experts",
    )(block_e, next_e, n_used, xs, w_gate, w_up, w_down)


def _combine_kernel(dest_ref, dest_next_ref, x_ref, gate_ref, swg_ref, swu_ref, swd_ref, g_ref, b_ref, y_hbm,
                    o_ref, buf, sem, *, tt):
    i = pl.program_id(0)
    slot = i % 2

    def fetch(dref, into):
        def body(t, carry):
            dst_row = pl.multiple_of(t * PACK_ROWS, PACK_ROWS)
            for kk in range(TOP_K):
                src_row = pl.multiple_of(dref[0, 0, t * TOP_K + kk] * PACK_ROWS, PACK_ROWS)
                pltpu.make_async_copy(y_hbm.at[pl.ds(src_row, PACK_ROWS)],
                                      buf.at[into, kk, pl.ds(dst_row, PACK_ROWS)], sem.at[into]).start()
            return carry
        lax.fori_loop(0, tt, body, 0)

    @pl.when(i == 0)
    def _():
        fetch(dest_ref, 0)

    @pl.when(i + 1 < pl.num_programs(0))
    def _():
        fetch(dest_next_ref, 1 - slot)

    x = x_ref[...]
    xb = x.astype(BF16)
    hidden = (_silu(jnp.dot(xb, swg_ref[...], preferred_element_type=F32))
              * jnp.dot(xb, swu_ref[...], preferred_element_type=F32))
    shared = jnp.dot(hidden.astype(BF16), swd_ref[...], preferred_element_type=F32)

    for kk in range(TOP_K):
        pltpu.make_async_copy(y_hbm.at[pl.ds(0, tt * PACK_ROWS)], buf.at[slot, kk], sem.at[slot]).wait()
    gate = gate_ref[...]
    for sl in range(PACK_ROWS):
        routed_hi = routed_lo = None
        for kk in range(TOP_K):
            hi, lo = _unpack_bf16_pair(buf[slot, kk, pl.ds(sl, tt, stride=PACK_ROWS), :])
            g_k = gate[:, kk:kk + 1]
            routed_hi = g_k * hi if kk == 0 else routed_hi + g_k * hi
            routed_lo = g_k * lo if kk == 0 else routed_lo + g_k * lo
        for half, routed in enumerate((routed_hi, routed_lo)):
            cols = slice((2 * sl + half) * LANES, (2 * sl + half + 1) * LANES)
            o_ref[:, cols] = DN_ALPHA * x[:, cols] + (routed + shared[:, cols])
    o_ref[...] = _layer_norm(o_ref[...], g_ref[...], b_ref[...])


def _moe_combine(xt, ys, dest_tok, gate_tok, sw_gate, sw_up, sw_down, g, b):
    tt = COMBINE_TILE
    n_tiles = N_TOK // tt
    row = lambda i: (i, 0)
    const = lambda i: (0, 0)
    dest_tiles = dest_tok.reshape(n_tiles, 1, tt * TOP_K)
    return pl.pallas_call(
        functools.partial(_combine_kernel, tt=tt),
        grid=(n_tiles,),
        in_specs=[pl.BlockSpec((1, 1, tt * TOP_K), lambda i: (i, 0, 0), memory_space=pltpu.SMEM),
                  pl.BlockSpec((1, 1, tt * TOP_K), lambda i: (jnp.minimum(i + 1, n_tiles - 1), 0, 0),
                               memory_space=pltpu.SMEM),
                  pl.BlockSpec((tt, D_MODEL), row),
                  pl.BlockSpec((tt, TOP_K), row),
                  pl.BlockSpec((D_MODEL, EXPERT_FF), const),
                  pl.BlockSpec((D_MODEL, EXPERT_FF), const),
                  pl.BlockSpec((EXPERT_FF, D_MODEL), const),
                  pl.BlockSpec((1, D_MODEL), const),
                  pl.BlockSpec((1, D_MODEL), const),
                  pl.BlockSpec(memory_space=pl.ANY)],
        out_specs=pl.BlockSpec((tt, D_MODEL), row),
        out_shape=jax.ShapeDtypeStruct((N_TOK, D_MODEL), F32),
        scratch_shapes=[pltpu.VMEM((2, TOP_K, tt * PACK_ROWS, LANES), jnp.uint32), pltpu.SemaphoreType.DMA((2,))],
        compiler_params=_params("arbitrary"),
        name="moe_combine",
    )(dest_tiles, dest_tiles, xt, gate_tok, sw_gate.astype(BF16), sw_up.astype(BF16), sw_down.astype(BF16),
      g, b, ys)


def _moe_layer(xt, layer, w_router, bias, w_gate, w_up, w_down, sw_gate, sw_up, sw_down, g, b):
    eidx, gate, rank, counts, xq = _moe_route(xt, w_router, bias)

    counts = counts[:, 0]
    pcounts = (counts + MOE_BLOCK - 1) // MOE_BLOCK * MOE_BLOCK
    pends = jnp.cumsum(pcounts)
    pstarts = pends - pcounts
    e_iota = jnp.arange(N_EXPERTS, dtype=jnp.int32)[None, :, None]
    dest = rank + jnp.sum(jnp.where(eidx[:, None, :] == e_iota, pstarts[None, :, None], 0), axis=1)
    dest_tok = dest.T.astype(jnp.int32)
    n_used = (pends[-1] // MOE_BLOCK).astype(jnp.int32)
    blk = jnp.minimum(jnp.arange(MOE_N_BLOCKS, dtype=jnp.int32), n_used - 1) * MOE_BLOCK
    block_e = jnp.sum((pends[None, :] <= blk[:, None]).astype(jnp.int32), axis=1)
    block_e = jnp.minimum(block_e, N_EXPERTS - 1)
    after = pends[block_e] // MOE_BLOCK
    next_e = jnp.where(after < n_used, block_e[jnp.minimum(after, MOE_N_BLOCKS - 1)], -1).astype(jnp.int32)

    xs = _moe_dispatch(xq, dest_tok)
    ys = _moe_experts(xs, block_e, next_e, n_used[None], layer, w_gate, w_up, w_down)
    return _moe_combine(xt, ys, dest_tok, gate.T, sw_gate, sw_up, sw_down, g, b)


def kernel(x_prompt, x_sample, state_pool, cache_swa_k, cache_swa_v, state_mlstm_c, state_mlstm_n, state_mlstm_m, pool_w, pool_scale, swa_w_qkv, swa_w_o, swa_sinks, mlstm_w_in, mlstm_b_gates, mlstm_norm_g, mlstm_w_out, ln_g, ln_b, moe_w_router, moe_router_bias, moe_w_gate, moe_w_up, moe_w_down, moe_shared_w_gate, moe_shared_w_up, moe_shared_w_down):
    d = D_MODEL
    xt = jnp.concatenate([x_prompt.reshape(N_PROMPT_TOK, d), x_sample.reshape(N_SAMPLE_TOK, d)], axis=0)
    pool_p, pool_s = [], []
    swk_p, swv_p, swk_s, swv_s = [], [], [], []
    mc_p, mn_p, mm_p, mc_s, mn_s, mm_s = [], [], [], [], [], []
    for i in range(DEPTH):
        kind, slot = i % N_MIXERS, i // N_MIXERS
        g0, b0 = ln_g[i, 0][None], ln_b[i, 0][None]
        if kind == 0:
            xt, sp, ss = _pool_layer(xt, state_pool[slot], pool_w[slot], pool_scale[slot][None], g0, b0)
            pool_p.append(sp)
            pool_s.append(ss)
        elif kind == 1:
            xt, kp, vp, ks, vs = _swa_layer(xt, cache_swa_k[slot], cache_swa_v[slot], swa_w_qkv[slot],
                                            swa_w_o[slot], swa_sinks[slot], g0, b0)
            swk_p.append(kp)
            swv_p.append(vp)
            swk_s.append(ks)
            swv_s.append(vs)
        else:
            xt, cp, np_, mp, cs, ns, ms = _mlstm_layer(
                xt, state_mlstm_c[slot], state_mlstm_n[slot], state_mlstm_m[slot], mlstm_w_in[slot],
                mlstm_b_gates[slot], mlstm_norm_g[slot][None], mlstm_w_out[slot], g0, b0)
            mc_p.append(cp)
            mn_p.append(np_)
            mm_p.append(mp)
            mc_s.append(cs)
            mn_s.append(ns)
            mm_s.append(ms)
        xt = _moe_layer(xt, i, moe_w_router[i], moe_router_bias[i], moe_w_gate, moe_w_up, moe_w_down,
                        moe_shared_w_gate[i], moe_shared_w_up[i], moe_shared_w_down[i],
                        ln_g[i, 1][None], ln_b[i, 1][None])
    y_p = xt[:N_PROMPT_TOK].reshape(BATCH, SEQ, d)
    y_s = xt[N_PROMPT_TOK:].reshape(DEC_BATCH, DEC_SEQ, d)
    return (y_p, y_s, jnp.stack(pool_p), jnp.stack(pool_s), jnp.stack(swk_p), jnp.stack(swv_p),
            jnp.stack(swk_s), jnp.stack(swv_s), jnp.stack(mc_p), jnp.stack(mn_p), jnp.stack(mm_p),
            jnp.stack(mc_s), jnp.stack(mn_s), jnp.stack(mm_s))
```

```python
import functools
import math

import jax
import jax.numpy as jnp
from jax import lax
from jax.experimental import pallas as pl
from jax.experimental.pallas import tpu as pltpu

F32 = jnp.float32
BF16 = jnp.bfloat16

D_MODEL = 2048
BATCH = 2
SEQ = 4096
DEPTH = 4
DEC_BATCH = 32
DEC_SEQ = 4
PAST_LEN = 16384
N_PROMPT_TOK = BATCH * SEQ
N_SAMPLE_TOK = DEC_BATCH * DEC_SEQ
N_TOK = N_PROMPT_TOK + N_SAMPLE_TOK

N_MIXERS = 3
DN_ALPHA = (2.0 * DEPTH) ** 0.25
LN_EPS = 1e-5

POOL_WINDOWS = (2, 4, 8, 16)
POOL_GROUP_DIM = D_MODEL // len(POOL_WINDOWS)
POOL_STATE = max(POOL_WINDOWS) - 1
POOL_HALO = POOL_STATE + 1

SWA_WINDOW = 128
SWA_HEAD_DIM = 64
SWA_HEADS = D_MODEL // SWA_HEAD_DIM
SWA_KV_HEADS = SWA_HEADS // 8
SWA_GROUP = SWA_HEADS // SWA_KV_HEADS
SWA_KV_DIM = SWA_KV_HEADS * SWA_HEAD_DIM

MLSTM_HEADS = 8
MLSTM_HEAD_DIM = D_MODEL // MLSTM_HEADS
MLSTM_CHUNK = 64
MLSTM_SAMPLE_CHUNK = 8

N_EXPERTS = 64
TOP_K = 8
N_EXPERT_GROUPS = 8
TOPK_GROUPS = 4
EXPERT_FF = D_MODEL // 4
ROUTED_SCALE = 2.5

VMEM_LIMIT_BYTES = 56 * 1024 * 1024

ROW_TILE = 320
ROUTER_TILE = 640
POOL_TILE = 512
MOE_BLOCK = 256
MOE_N_BLOCKS = N_TOK * TOP_K // MOE_BLOCK + N_EXPERTS
MOE_ROWS = MOE_N_BLOCKS * MOE_BLOCK
DISPATCH_TILE = 320
COMBINE_TILE = 128
LANES = 128
PACK_ROWS = D_MODEL // (2 * LANES)

_NT = (((1,), (1,)), ((), ()))
_TN = (((0,), (0,)), ((), ()))


def _params(*semantics):
    return pltpu.CompilerParams(dimension_semantics=semantics, vmem_limit_bytes=VMEM_LIMIT_BYTES)


def _layer_norm(z, g, b):
    mu = jnp.mean(z, axis=-1, keepdims=True)
    zc = z - mu
    var = jnp.mean(zc * zc, axis=-1, keepdims=True)
    return zc * lax.rsqrt(var + LN_EPS) * g + b


def _silu(x):
    return x * jax.nn.sigmoid(x)


def _matmul_kernel(x_ref, w_ref, o_ref):
    o_ref[...] = jnp.dot(x_ref[...].astype(BF16), w_ref[...], preferred_element_type=F32)


def _matmul(x, w, *, tm, tn):
    m, k = x.shape
    n = w.shape[1]
    return pl.pallas_call(
        _matmul_kernel,
        grid=(n // tn, m // tm),
        in_specs=[pl.BlockSpec((tm, k), lambda j, i: (i, 0)),
                  pl.BlockSpec((k, tn), lambda j, i: (0, j))],
        out_specs=pl.BlockSpec((tm, tn), lambda j, i: (i, j)),
        out_shape=jax.ShapeDtypeStruct((m, n), F32),
        compiler_params=_params("parallel", "parallel"),
        name="matmul",
    )(x, w)


def _matmul_ln_kernel(a_ref, w_ref, res_ref, g_ref, b_ref, o_ref):
    y = jnp.dot(a_ref[...].astype(BF16), w_ref[...], preferred_element_type=F32)
    o_ref[...] = _layer_norm(DN_ALPHA * res_ref[...] + y, g_ref[...], b_ref[...])


def _matmul_ln(a, w, res, g, b):
    m, k = a.shape
    row = lambda i: (i, 0)
    const = lambda i: (0, 0)
    return pl.pallas_call(
        _matmul_ln_kernel,
        grid=(m // ROW_TILE,),
        in_specs=[pl.BlockSpec((ROW_TILE, k), row),
                  pl.BlockSpec((k, D_MODEL), const),
                  pl.BlockSpec((ROW_TILE, D_MODEL), row),
                  pl.BlockSpec((1, D_MODEL), const),
                  pl.BlockSpec((1, D_MODEL), const)],
        out_specs=pl.BlockSpec((ROW_TILE, D_MODEL), row),
        out_shape=jax.ShapeDtypeStruct((m, D_MODEL), F32),
        compiler_params=_params("parallel"),
        name="matmul_ln",
    )(a, w, res, g, b)


def _pool_core(zbuf, w_ref, scale_ref, g_ref, b_ref, o_ref, *, tt, n_before):
    avail = n_before + lax.broadcasted_iota(jnp.int32, (tt, 1), 0) + 1
    for grp, win in enumerate(POOL_WINDOWS):
        c0, c1 = grp * POOL_GROUP_DIM, (grp + 1) * POOL_GROUP_DIM
        xg = zbuf[POOL_HALO:POOL_HALO + tt, c0:c1]
        total = xg
        for back in range(1, win):
            total = total + zbuf[POOL_HALO - back:POOL_HALO - back + tt, c0:c1]
        count = jnp.minimum(win, avail).astype(F32)
        diff = total / count - xg
        y = jnp.dot(diff.astype(BF16), w_ref[grp], preferred_element_type=F32) * scale_ref[:, c0:c1]
        o_ref[:, c0:c1] = DN_ALPHA * xg + y
    o_ref[...] = _layer_norm(o_ref[...], g_ref[...], b_ref[...])


def _pool_prompt_kernel(x_ref, halo_ref, w_ref, scale_ref, g_ref, b_ref, o_ref, zbuf, *, tt):
    i = pl.program_id(1)
    zbuf[0:POOL_HALO, :] = jnp.where(i == 0, 0.0, halo_ref[...])
    zbuf[POOL_HALO:POOL_HALO + tt, :] = x_ref[...]
    _pool_core(zbuf, w_ref, scale_ref, g_ref, b_ref, o_ref, tt=tt, n_before=i * tt)


def _pool_sample_kernel(z_ref, w_ref, scale_ref, g_ref, b_ref, o_ref, *, tt):
    _pool_core(z_ref.at[0], w_ref, scale_ref, g_ref, b_ref, o_ref.at[0], tt=tt, n_before=PAST_LEN)


def _pool_layer(xt, state, w, scale, g, b):
    tt = POOL_TILE
    tiles = SEQ // tt
    halo_per_tile = tt // POOL_HALO
    const2 = lambda bb, i: (0, 0)
    w_bf = w.astype(BF16)
    out = pl.pallas_call(
        functools.partial(_pool_prompt_kernel, tt=tt),
        grid=(BATCH, tiles),
        in_specs=[pl.BlockSpec((tt, D_MODEL), lambda bb, i: (bb * tiles + i, 0)),
                  pl.BlockSpec((POOL_HALO, D_MODEL),
                               lambda bb, i: (jnp.maximum((bb * tiles + i) * halo_per_tile - 1, 0), 0)),
                  pl.BlockSpec(w_bf.shape, lambda bb, i: (0, 0, 0)),
                  pl.BlockSpec((1, D_MODEL), const2),
                  pl.BlockSpec((1, D_MODEL), const2),
                  pl.BlockSpec((1, D_MODEL), const2)],
        out_specs=pl.BlockSpec((tt, D_MODEL), lambda bb, i: (bb * tiles + i, 0)),
        out_shape=jax.ShapeDtypeStruct((N_TOK, D_MODEL), F32),
        scratch_shapes=[pltpu.VMEM((POOL_HALO + tt, D_MODEL), F32)],
        compiler_params=_params("parallel", "arbitrary"),
        name="pool_prompt",
    )(xt, xt, w_bf, scale, g, b)

    ts = 16
    xs = xt[N_PROMPT_TOK:].reshape(DEC_BATCH, DEC_SEQ, D_MODEL)
    zs = jnp.concatenate([jnp.zeros((DEC_BATCH, 1, D_MODEL), F32), state, xs,
                          jnp.zeros((DEC_BATCH, ts - DEC_SEQ, D_MODEL), F32)], axis=1)
    const1 = lambda bb: (0, 0)
    out_s = pl.pallas_call(
        functools.partial(_pool_sample_kernel, tt=ts),
        grid=(DEC_BATCH,),
        in_specs=[pl.BlockSpec((1, POOL_HALO + ts, D_MODEL), lambda bb: (bb, 0, 0)),
                  pl.BlockSpec(w_bf.shape, lambda bb: (0, 0, 0)),
                  pl.BlockSpec((1, D_MODEL), const1),
                  pl.BlockSpec((1, D_MODEL), const1),
                  pl.BlockSpec((1, D_MODEL), const1)],
        out_specs=pl.BlockSpec((1, ts, D_MODEL), lambda bb: (bb, 0, 0)),
        out_shape=jax.ShapeDtypeStruct((DEC_BATCH, ts, D_MODEL), F32),
        compiler_params=_params("parallel"),
        name="pool_sample",
    )(zs, w_bf, scale, g, b)
    out = lax.dynamic_update_slice(out, out_s[:, :DEC_SEQ].reshape(N_SAMPLE_TOK, D_MODEL), (N_PROMPT_TOK, 0))

    new_p = xt[:N_PROMPT_TOK].reshape(BATCH, SEQ, D_MODEL)[:, SEQ - POOL_STATE:]
    new_s = jnp.concatenate([state, xs], axis=1)[:, DEC_SEQ:]
    return out, new_p, new_s


def _alibi_slope(h):
    return 2.0 ** (-8.0 * (h + 1.0) / SWA_HEADS)


def _attn_core(q_ref, k_all, v_all, sink_ref, o_ref, *, rows, stack, first_block):
    w, dh = SWA_WINDOW, SWA_HEAD_DIM
    m_rows = rows * stack
    qi = lax.broadcasted_iota(jnp.int32, (m_rows, 2 * w), 0) & (rows - 1)
    sj = lax.broadcasted_iota(jnp.int32, (m_rows, 2 * w), 1)
    dist = (w + qi) - sj
    valid = (dist >= 0) & (dist <= w)
    if first_block is not None:
        valid = valid & ((sj >= w) | jnp.logical_not(first_block))
    masked_dist = jnp.where(valid, dist.astype(F32), jnp.inf)
    for h0 in range(0, SWA_HEADS, stack):
        kv = h0 // SWA_GROUP
        c0, c1 = kv * dh, (kv + 1) * dh
        heads = range(h0, h0 + stack)
        if stack == 1:
            q = q_ref[:, h0 * dh:(h0 + 1) * dh]
            slope, sink = _alibi_slope(h0), sink_ref[h0]
        else:
            q = jnp.concatenate([q_ref[:, h * dh:(h + 1) * dh] for h in heads], axis=0)
            slope = jnp.concatenate([jnp.full((rows, 1), _alibi_slope(h), F32) for h in heads], axis=0)
            sink = jnp.concatenate([jnp.full((rows, 1), sink_ref[h], F32) for h in heads], axis=0)
        q = (q * (dh ** -0.5)).astype(BF16)
        s = lax.dot_general(q, k_all[:, c0:c1], _NT, preferred_element_type=F32) - slope * masked_dist
        m = jnp.maximum(jnp.max(s, axis=1, keepdims=True), sink)
        e = jnp.exp(s - m)
        den = jnp.sum(e, axis=1, keepdims=True) + jnp.exp(sink - m)
        o = jnp.dot(e.astype(BF16), v_all[:, c0:c1], preferred_element_type=F32) / den
        for j, h in enumerate(heads):
            o_ref[:, h * dh:(h + 1) * dh] = o[j * rows:(j + 1) * rows]


def _attn_prompt_kernel(sink_ref, q_ref, kp_ref, kc_ref, vp_ref, vc_ref, o_ref):
    n = pl.program_id(1)
    k_all = jnp.concatenate([kp_ref[...], kc_ref[...]], axis=0).astype(BF16)
    v_all = jnp.concatenate([vp_ref[...], vc_ref[...]], axis=0).astype(BF16)
    _attn_core(q_ref, k_all, v_all, sink_ref, o_ref, rows=SWA_WINDOW, stack=1, first_block=(n == 0))


def _attn_sample_kernel(sink_ref, q_ref, k_ref, v_ref, o_ref, *, rows):
    _attn_core(q_ref.at[0], k_ref[0].astype(BF16), v_ref[0].astype(BF16), sink_ref, o_ref.at[0],
               rows=rows, stack=SWA_GROUP, first_block=None)


def _swa_layer(xt, cache_k, cache_v, w_qkv, w_o, sinks, g, b):
    w = SWA_WINDOW
    qkv = _matmul(xt, w_qkv.astype(BF16), tm=640, tn=1280)
    nb = SEQ // w
    k_col = D_MODEL // SWA_KV_DIM
    cur = lambda bb, n, *_: (bb * nb + n, k_col)
    prev = lambda bb, n, *_: (jnp.maximum(bb * nb + n - 1, 0), k_col)
    cur_v = lambda bb, n, *_: (bb * nb + n, k_col + 1)
    prev_v = lambda bb, n, *_: (jnp.maximum(bb * nb + n - 1, 0), k_col + 1)
    o = pl.pallas_call(
        _attn_prompt_kernel,
        grid_spec=pltpu.PrefetchScalarGridSpec(
            num_scalar_prefetch=1,
            grid=(BATCH, nb),
            in_specs=[pl.BlockSpec((w, D_MODEL), lambda bb, n, *_: (bb * nb + n, 0)),
                      pl.BlockSpec((w, SWA_KV_DIM), prev),
                      pl.BlockSpec((w, SWA_KV_DIM), cur),
                      pl.BlockSpec((w, SWA_KV_DIM), prev_v),
                      pl.BlockSpec((w, SWA_KV_DIM), cur_v)],
            out_specs=pl.BlockSpec((w, D_MODEL), lambda bb, n, *_: (bb * nb + n, 0))),
        out_shape=jax.ShapeDtypeStruct((N_TOK, D_MODEL), F32),
        compiler_params=_params("parallel", "arbitrary"),
        name="attn_prompt",
    )(sinks, qkv, qkv, qkv, qkv, qkv)

    rows = 16
    qkv_s = qkv[N_PROMPT_TOK:].reshape(DEC_BATCH, DEC_SEQ, -1)
    q_s = jnp.pad(qkv_s[..., :D_MODEL], ((0, 0), (0, rows - DEC_SEQ), (0, 0)))
    k_new = qkv_s[..., D_MODEL:D_MODEL + SWA_KV_DIM]
    v_new = qkv_s[..., D_MODEL + SWA_KV_DIM:]
    kz = jnp.concatenate([cache_k.reshape(DEC_BATCH, w, SWA_KV_DIM), k_new], axis=1)
    vz = jnp.concatenate([cache_v.reshape(DEC_BATCH, w, SWA_KV_DIM), v_new], axis=1)
    pad_keys = ((0, 0), (0, w - DEC_SEQ), (0, 0))
    o_s = pl.pallas_call(
        functools.partial(_attn_sample_kernel, rows=rows),
        grid_spec=pltpu.PrefetchScalarGridSpec(
            num_scalar_prefetch=1,
            grid=(DEC_BATCH,),
            in_specs=[pl.BlockSpec((1, rows, D_MODEL), lambda bb, *_: (bb, 0, 0)),
                      pl.BlockSpec((1, 2 * w, SWA_KV_DIM), lambda bb, *_: (bb, 0, 0)),
                      pl.BlockSpec((1, 2 * w, SWA_KV_DIM), lambda bb, *_: (bb, 0, 0))],
            out_specs=pl.BlockSpec((1, rows, D_MODEL), lambda bb, *_: (bb, 0, 0))),
        out_shape=jax.ShapeDtypeStruct((DEC_BATCH, rows, D_MODEL), F32),
        compiler_params=_params("parallel"),
        name="attn_sample",
    )(sinks, q_s, jnp.pad(kz, pad_keys), jnp.pad(vz, pad_keys))
    o = lax.dynamic_update_slice(o, o_s[:, :DEC_SEQ].reshape(N_SAMPLE_TOK, D_MODEL), (N_PROMPT_TOK, 0))

    out = _matmul_ln(o, w_o.astype(BF16), xt, g, b)

    kv_shape = (SWA_WINDOW, SWA_KV_HEADS, SWA_HEAD_DIM)
    qkv_p = qkv[:N_PROMPT_TOK].reshape(BATCH, SEQ, -1)[:, SEQ - w:]
    new_k_p = qkv_p[..., D_MODEL:D_MODEL + SWA_KV_DIM].reshape((BATCH,) + kv_shape)
    new_v_p = qkv_p[..., D_MODEL + SWA_KV_DIM:].reshape((BATCH,) + kv_shape)
    new_k_s = kz[:, DEC_SEQ:].reshape((DEC_BATCH,) + kv_shape)
    new_v_s = vz[:, DEC_SEQ:].reshape((DEC_BATCH,) + kv_shape)
    return out, new_k_p, new_v_p, new_k_s, new_v_s


def _log_sigmoid(x):
    return jnp.minimum(x, 0.0) - jnp.log(1.0 + jnp.exp(-jnp.abs(x)))


def _mlstm_kernel(bias_ref, q_ref, k_ref, v_ref, og_ref, gates_ref, c0_ref, n0_ref, m0_ref, ng_ref,
                  h_ref, c_ref, n_ref, m_ref, c_scr, n_scr, m_scr, *, chunk, n_valid):
    step = pl.program_id(1)

    @pl.when(step == 0)
    def _():
        c_scr[...] = c0_ref[0]
        n_scr[...] = n0_ref[0]
        m_scr[...] = m0_ref[0]

    for head in range(MLSTM_HEADS):
        _mlstm_head(head, step, bias_ref, q_ref, k_ref, v_ref, og_ref, gates_ref, ng_ref, h_ref, c_scr, n_scr, m_scr,
                    chunk=chunk, n_valid=n_valid)

    @pl.when(step == pl.num_programs(1) - 1)
    def _():
        c_ref[0] = c_scr[...]
        n_ref[0] = n_scr[...]
        m_ref[0] = m_scr[...]


def _mlstm_head(head, step, bias_ref, q_ref, k_ref, v_ref, og_ref, gates_ref, ng_ref, h_ref, c_scr, n_scr, m_scr,
                *, chunk, n_valid):
    ln = chunk
    cols = slice(head * MLSTM_HEAD_DIM, (head + 1) * MLSTM_HEAD_DIM)
    ig = gates_ref[0, head, pl.ds(step, 1), :] + bias_ref[head]
    lf = _log_sigmoid(gates_ref[0, MLSTM_HEADS + head, pl.ds(step, 1), :] + bias_ref[MLSTM_HEADS + head])
    if n_valid < ln:
        col = lax.broadcasted_iota(jnp.int32, (1, ln), 1)
        ig = jnp.where(col < n_valid, ig, -1e30)
        lf = jnp.where(col < n_valid, lf, 0.0)

    ri = lax.broadcasted_iota(jnp.int32, (ln, ln), 0)
    ci = lax.broadcasted_iota(jnp.int32, (ln, ln), 1)
    eye = ri == ci
    causal = ci <= ri

    def to_col(row):
        return jnp.sum(jnp.where(eye, row, 0.0), axis=1, keepdims=True)

    f_col = to_col(lf)
    b_col = jnp.sum(jnp.where(causal, lf, 0.0), axis=1, keepdims=True)
    b_row = jnp.sum(jnp.where(ri <= ci, f_col, 0.0), axis=0, keepdims=True)
    b_last = jnp.sum(lf, axis=1, keepdims=True)
    m_prev = m_scr[head, :, 0:1]
    a_col = b_col + m_prev
    dmat = jnp.where(causal, b_col - b_row + ig, -jnp.inf)
    mt = jnp.maximum(a_col, jnp.max(dmat, axis=1, keepdims=True))
    w_inter = jnp.exp(a_col - mt)

    q = q_ref[:, cols]
    k = k_ref[:, cols] * (MLSTM_HEAD_DIM ** -0.5)
    v = v_ref[:, cols]
    qb = q.astype(BF16)
    kb = k.astype(BF16)
    c_prev = c_scr[head]
    n_prev = n_scr[head]
    scores = lax.dot_general(qb, kb, _NT, preferred_element_type=F32)
    w_intra = jnp.exp(dmat - mt) * scores
    num = (w_inter * lax.dot_general(qb, c_prev.astype(BF16), _NT, preferred_element_type=F32)
           + jnp.dot(w_intra.astype(BF16), v.astype(BF16), preferred_element_type=F32))
    den = (w_inter * jnp.sum(q * n_prev, axis=1, keepdims=True)
           + jnp.sum(w_intra, axis=1, keepdims=True))
    h = num / jnp.maximum(jnp.abs(den), jnp.exp(-mt))

    g_row = b_last - b_row + ig
    m_new = jnp.maximum(b_last + m_prev, jnp.max(g_row, axis=1, keepdims=True))
    decay = jnp.exp(b_last + m_prev - m_new)
    wg_col = to_col(jnp.exp(g_row - m_new))
    c_new = decay * c_prev + lax.dot_general((v * wg_col).astype(BF16), kb, _TN, preferred_element_type=F32)
    n_new = decay * n_prev + jnp.sum(wg_col * k, axis=0, keepdims=True)
    c_scr[head] = c_new
    n_scr[head] = n_new
    m_scr[head] = jnp.broadcast_to(m_new, (1, LANES))

    mu = jnp.mean(h, axis=1, keepdims=True)
    hc = h - mu
    var = jnp.mean(hc * hc, axis=1, keepdims=True)
    h_ref[:, cols] = hc * lax.rsqrt(var + LN_EPS) * ng_ref[:, cols] * jax.nn.sigmoid(og_ref[:, cols])


def _mlstm_scan(proj, gates, b_gates, norm_g, c0, n0, m0, *, batch, n_chunks, chunk, n_valid, out_rows):
    d, dh, nh = D_MODEL, MLSTM_HEAD_DIM, MLSTM_HEADS
    rows = lambda off: (lambda bb, cc, *_: (bb * n_chunks + cc, off))
    per_seq = lambda bb, cc, *_: (bb, 0, 0, 0)
    return pl.pallas_call(
        functools.partial(_mlstm_kernel, chunk=chunk, n_valid=n_valid),
        grid_spec=pltpu.PrefetchScalarGridSpec(
            num_scalar_prefetch=1,
            grid=(batch, n_chunks),
            in_specs=[pl.BlockSpec((chunk, d), rows(0)),
                      pl.BlockSpec((chunk, d), rows(1)),
                      pl.BlockSpec((chunk, d), rows(2)),
                      pl.BlockSpec((chunk, d), rows(3)),
                      pl.BlockSpec((1, 2 * nh, n_chunks, chunk), per_seq),
                      pl.BlockSpec((1, nh, dh, dh), per_seq),
                      pl.BlockSpec((1, nh, 1, dh), per_seq),
                      pl.BlockSpec((1, nh, 1, LANES), per_seq),
                      pl.BlockSpec((1, d), lambda bb, cc, *_: (0, 0))],
            out_specs=[pl.BlockSpec((chunk, d), lambda bb, cc, *_: (bb * n_chunks + cc, 0)),
                       pl.BlockSpec((1, nh, dh, dh), per_seq),
                       pl.BlockSpec((1, nh, 1, dh), per_seq),
                       pl.BlockSpec((1, nh, 1, LANES), per_seq)],
            scratch_shapes=[pltpu.VMEM((nh, dh, dh), F32), pltpu.VMEM((nh, 1, dh), F32),
                            pltpu.VMEM((nh, 1, LANES), F32)]),
        out_shape=[jax.ShapeDtypeStruct((out_rows, d), F32),
                   jax.ShapeDtypeStruct((batch, nh, dh, dh), F32),
                   jax.ShapeDtypeStruct((batch, nh, 1, dh), F32),
                   jax.ShapeDtypeStruct((batch, nh, 1, LANES), F32)],
        compiler_params=_params("parallel", "arbitrary"),
        name="mlstm_scan",
    )(b_gates, proj, proj, proj, proj, gates, c0, n0[:, :, None, :],
      jnp.broadcast_to(m0[:, :, None, None], (batch, nh, 1, LANES)), norm_g)


def _mlstm_layer(xt, c0_s, n0_s, m0_s, w_in, b_gates, norm_g, w_out, g, b):
    d, nh, dh = D_MODEL, MLSTM_HEADS, MLSTM_HEAD_DIM
    proj = _matmul(xt, w_in[:, :4 * d].astype(BF16), tm=640, tn=1024)
    w_gates = jnp.pad(w_in[:, 4 * d:], ((0, 0), (0, 128 - 2 * nh)))
    gate_pre = _matmul(xt, w_gates.astype(BF16), tm=640, tn=128)[:, :2 * nh]

    nc = SEQ // MLSTM_CHUNK
    gates_p = gate_pre[:N_PROMPT_TOK].reshape(BATCH, nc, MLSTM_CHUNK, 2 * nh).transpose(0, 3, 1, 2)
    zeros = lambda *s: jnp.zeros(s, F32)
    hn, c_p, n_p, m_p = _mlstm_scan(
        proj, gates_p, b_gates, norm_g, zeros(BATCH, nh, dh, dh), zeros(BATCH, nh, dh), zeros(BATCH, nh),
        batch=BATCH, n_chunks=nc, chunk=MLSTM_CHUNK, n_valid=MLSTM_CHUNK, out_rows=N_TOK)

    ls = MLSTM_SAMPLE_CHUNK
    pad_t = ((0, 0), (0, ls - DEC_SEQ), (0, 0))
    proj_s = jnp.pad(proj[N_PROMPT_TOK:].reshape(DEC_BATCH, DEC_SEQ, 4 * d), pad_t).reshape(DEC_BATCH * ls, 4 * d)
    gates_s = jnp.pad(gate_pre[N_PROMPT_TOK:].reshape(DEC_BATCH, DEC_SEQ, 2 * nh), pad_t)
    gates_s = gates_s.transpose(0, 2, 1)[:, :, None, :]
    hn_s, c_s, n_s, m_s = _mlstm_scan(
        proj_s, gates_s, b_gates, norm_g, c0_s, n0_s, m0_s,
        batch=DEC_BATCH, n_chunks=1, chunk=ls, n_valid=DEC_SEQ, out_rows=DEC_BATCH * ls)
    hn_s = hn_s.reshape(DEC_BATCH, ls, d)[:, :DEC_SEQ].reshape(N_SAMPLE_TOK, d)
    hn = lax.dynamic_update_slice(hn, hn_s, (N_PROMPT_TOK, 0))

    out = _matmul_ln(hn, w_out.astype(BF16), xt, g, b)
    return (out, c_p, n_p[:, :, 0], m_p[:, :, 0, 0], c_s, n_s[:, :, 0], m_s[:, :, 0, 0])


def _router_kernel(x_ref, wt_ref, bias_ref, tri_ref, eidx_ref, gate_ref, rank_ref, count_ref, xq_ref, count_scr,
                   *, tt):
    neg = -jnp.inf
    x = x_ref[...]
    logits = lax.dot_general(wt_ref[...].astype(BF16), x.astype(BF16), _NT, preferred_element_type=F32)
    s = jax.nn.sigmoid(logits)
    sb = s + bias_ref[...]
    per_group = N_EXPERTS // N_EXPERT_GROUPS

    sb3 = sb.reshape(N_EXPERT_GROUPS, per_group, tt)
    i3 = lax.broadcasted_iota(jnp.int32, sb3.shape, 1)
    m1 = jnp.max(sb3, axis=1, keepdims=True)
    first = jnp.min(jnp.where(sb3 == m1, i3, per_group), axis=1, keepdims=True)
    m2 = jnp.max(jnp.where(i3 == first, neg, sb3), axis=1, keepdims=True)
    gscore = (m1 + m2).reshape(N_EXPERT_GROUPS, tt)

    gi = lax.broadcasted_iota(jnp.int32, gscore.shape, 0)
    gsel = jnp.zeros(gscore.shape, jnp.bool_)
    cur = gscore
    for _ in range(TOPK_GROUPS):
        mx = jnp.max(cur, axis=0, keepdims=True)
        pick = gi == jnp.min(jnp.where(cur == mx, gi, N_EXPERT_GROUPS), axis=0, keepdims=True)
        gsel = gsel | pick
        cur = jnp.where(pick, neg, cur)
    emask = jnp.broadcast_to(gsel.reshape(N_EXPERT_GROUPS, 1, tt), sb3.shape).reshape(N_EXPERTS, tt)

    ei = lax.broadcasted_iota(jnp.int32, sb.shape, 0)
    sel = jnp.zeros(sb.shape, jnp.bool_)
    cur = jnp.where(emask, sb, neg)
    picks = []
    for _ in range(TOP_K):
        mx = jnp.max(cur, axis=0, keepdims=True)
        idx = jnp.min(jnp.where(cur == mx, ei, N_EXPERTS), axis=0, keepdims=True)
        pick = ei == idx
        picks.append((idx, pick))
        sel = sel | pick
        cur = jnp.where(pick, neg, cur)
    s_sel = jnp.where(sel, s, 0.0)
    gate = s_sel / jnp.sum(s_sel, axis=0, keepdims=True) * ROUTED_SCALE

    @pl.when(pl.program_id(0) == 0)
    def _():
        count_scr[...] = jnp.zeros_like(count_scr)

    sel_f = sel.astype(F32)
    incl = jnp.dot(sel_f.astype(BF16), tri_ref[...], preferred_element_type=F32)
    rank = count_scr[:, 0:1] + incl - sel_f
    count_scr[...] = count_scr[...] + jnp.sum(sel_f, axis=1, keepdims=True)
    count_ref[...] = count_scr[...].astype(jnp.int32)

    for kk, (idx, pick) in enumerate(picks):
        eidx_ref[kk:kk + 1, :] = idx
        gate_ref[kk:kk + 1, :] = jnp.sum(jnp.where(pick, gate, 0.0), axis=0, keepdims=True)
        rank_ref[kk:kk + 1, :] = jnp.sum(jnp.where(pick, rank, 0.0), axis=0, keepdims=True).astype(jnp.int32)

    for sl in range(PACK_ROWS):
        c0 = 2 * LANES * sl
        xq_ref[pl.ds(sl, tt, stride=PACK_ROWS), :] = _pack_bf16_pair(x[:, c0:c0 + LANES],
                                                                     x[:, c0 + LANES:c0 + 2 * LANES])


def _pack_bf16_pair(hi, lo):
    hb = lax.bitcast_convert_type(hi.astype(BF16).astype(F32), jnp.uint32)
    lb = lax.bitcast_convert_type(lo.astype(BF16).astype(F32), jnp.uint32)
    return hb | (lb >> 16)


def _unpack_bf16_pair(word):
    hi = lax.bitcast_convert_type(word & jnp.uint32(0xFFFF0000), F32)
    lo = lax.bitcast_convert_type(word << 16, F32)
    return hi, lo


def _moe_route(xt, w_router, bias):
    tt = ROUTER_TILE
    col = lambda i: (0, i)
    const = lambda i: (0, 0)
    tri = (jnp.arange(tt)[:, None] <= jnp.arange(tt)[None, :]).astype(BF16)
    return pl.pallas_call(
        functools.partial(_router_kernel, tt=tt),
        grid=(N_TOK // tt,),
        in_specs=[pl.BlockSpec((tt, D_MODEL), lambda i: (i, 0)),
                  pl.BlockSpec((N_EXPERTS, D_MODEL), const),
                  pl.BlockSpec((N_EXPERTS, 1), const),
                  pl.BlockSpec((tt, tt), const)],
        out_specs=[pl.BlockSpec((TOP_K, tt), col), pl.BlockSpec((TOP_K, tt), col), pl.BlockSpec((TOP_K, tt), col),
                   pl.BlockSpec((N_EXPERTS, LANES), const),
                   pl.BlockSpec((tt * PACK_ROWS, LANES), lambda i: (i, 0))],
        out_shape=[jax.ShapeDtypeStruct((TOP_K, N_TOK), jnp.int32),
                   jax.ShapeDtypeStruct((TOP_K, N_TOK), F32),
                   jax.ShapeDtypeStruct((TOP_K, N_TOK), jnp.int32),
                   jax.ShapeDtypeStruct((N_EXPERTS, LANES), jnp.int32),
                   jax.ShapeDtypeStruct((N_TOK * PACK_ROWS, LANES), jnp.uint32)],
        scratch_shapes=[pltpu.VMEM((N_EXPERTS, LANES), F32)],
        compiler_params=_params("arbitrary"),
        name="moe_router",
    )(xt, w_router.T, bias[:, None], tri)


def _dispatch_kernel(dest_ref, xq_ref, xs_hbm, sem, *, tt):
    def issue(t, carry):
        src = xq_ref.at[pl.ds(pl.multiple_of(t * PACK_ROWS, PACK_ROWS), PACK_ROWS)]
        for kk in range(TOP_K):
            row = pl.multiple_of(dest_ref[0, 0, t * TOP_K + kk] * PACK_ROWS, PACK_ROWS)
            pltpu.make_async_copy(src, xs_hbm.at[pl.ds(row, PACK_ROWS)], sem).start()
        return carry
    lax.fori_loop(0, tt, issue, 0)
    for _ in range(TOP_K):
        pltpu.make_async_copy(xq_ref, xs_hbm.at[pl.ds(0, tt * PACK_ROWS)], sem).wait()


def _moe_dispatch(xq, dest_tok):
    tt = DISPATCH_TILE
    n_tiles = N_TOK // tt
    return pl.pallas_call(
        functools.partial(_dispatch_kernel, tt=tt),
        grid=(n_tiles,),
        in_specs=[pl.BlockSpec((1, 1, tt * TOP_K), lambda i: (i, 0, 0), memory_space=pltpu.SMEM),
                  pl.BlockSpec((tt * PACK_ROWS, LANES), lambda i: (i, 0))],
        out_specs=pl.BlockSpec(memory_space=pl.ANY),
        out_shape=jax.ShapeDtypeStruct((MOE_ROWS * PACK_ROWS, LANES), jnp.uint32),
        scratch_shapes=[pltpu.SemaphoreType.DMA(())],
        compiler_params=_params("arbitrary"),
        name="moe_dispatch",
    )(dest_tok.reshape(n_tiles, 1, tt * TOP_K), xq)


def _experts_kernel(be_ref, next_ref, nused_ref, xs_ref, wg_hbm, wu_hbm, wd_hbm, y_ref,
                    wg_f, wu_f, wd_f, wgu_s, wd_s, sem, *, layer):
    i = pl.program_id(0)
    tm = MOE_BLOCK

    def weight_copies(e):
        return (pltpu.make_async_copy(wg_hbm.at[layer, e], wg_f, sem.at[0]),
                pltpu.make_async_copy(wu_hbm.at[layer, e], wu_f, sem.at[1]),
                pltpu.make_async_copy(wd_hbm.at[layer, e], wd_f, sem.at[2]))

    @pl.when(i == 0)
    def _():
        for cp in weight_copies(be_ref[0]):
            cp.start()

    @pl.when(i < nused_ref[0])
    def _():
        first_of_expert = (i == 0) | (be_ref[i] != be_ref[jnp.maximum(i - 1, 0)])

        @pl.when(first_of_expert)
        def _():
            for cp in weight_copies(be_ref[i]):
                cp.wait()
            wgu_s[:, :EXPERT_FF] = wg_f[...].astype(BF16)
            wgu_s[:, EXPERT_FF:] = wu_f[...].astype(BF16)
            wd_s[...] = wd_f[...].astype(BF16)

            @pl.when(next_ref[i] >= 0)
            def _():
                for cp in weight_copies(next_ref[i]):
                    cp.start()

        parts = []
        for sl in range(PACK_ROWS):
            parts.extend(_unpack_bf16_pair(xs_ref[pl.ds(sl, tm, stride=PACK_ROWS), :]))
        x = jnp.concatenate(parts, axis=1).astype(BF16)
        gate_up = jnp.dot(x, wgu_s[...], preferred_element_type=F32)
        hidden = _silu(gate_up[:, :EXPERT_FF]) * gate_up[:, EXPERT_FF:]
        y = jnp.dot(hidden.astype(BF16), wd_s[...], preferred_element_type=F32)
        for sl in range(PACK_ROWS):
            c0 = 2 * LANES * sl
            y_ref[pl.ds(sl, tm, stride=PACK_ROWS), :] = _pack_bf16_pair(y[:, c0:c0 + LANES],
                                                                        y[:, c0 + LANES:c0 + 2 * LANES])


def _moe_experts(xs, block_e, next_e, n_used, layer, w_gate, w_up, w_down):
    last = lambda i, be, nx, nu: (jnp.minimum(i, nu[0] - 1), 0)
    return pl.pallas_call(
        functools.partial(_experts_kernel, layer=layer),
        grid_spec=pltpu.PrefetchScalarGridSpec(
            num_scalar_prefetch=3,
            grid=(MOE_N_BLOCKS,),
            in_specs=[pl.BlockSpec((MOE_BLOCK * PACK_ROWS, LANES), last),
                      pl.BlockSpec(memory_space=pl.ANY),
                      pl.BlockSpec(memory_space=pl.ANY),
                      pl.BlockSpec(memory_space=pl.ANY)],
            out_specs=pl.BlockSpec((MOE_BLOCK * PACK_ROWS, LANES), last),
            scratch_shapes=[pltpu.VMEM((D_MODEL, EXPERT_FF), F32), pltpu.VMEM((D_MODEL, EXPERT_FF), F32),
                            pltpu.VMEM((EXPERT_FF, D_MODEL), F32),
                            pltpu.VMEM((D_MODEL, 2 * EXPERT_FF), BF16),
                            pltpu.VMEM((EXPERT_FF, D_MODEL), BF16),
                            pltpu.SemaphoreType.DMA((3,))]),
        out_shape=jax.ShapeDtypeStruct((MOE_ROWS * PACK_ROWS, LANES), jnp.uint32),
        compiler_params=_params("arbitrary"),
        name="moe_experts",
    )(block_e, next_e, n_used, xs, w_gate, w_up, w_down)


def _combine_kernel(dest_ref, dest_next_ref, x_ref, gate_ref, swg_ref, swu_ref, swd_ref, g_ref, b_ref, y_hbm,
                    o_ref, buf, sem, *, tt):
    i = pl.program_id(0)
    slot = i % 2

    def fetch(dref, into):
        def body(t, carry):
            dst_row = pl.multiple_of(t * PACK_ROWS, PACK_ROWS)
            for kk in range(TOP_K):
                src_row = pl.multiple_of(dref[0, 0, t * TOP_K + kk] * PACK_ROWS, PACK_ROWS)
                pltpu.make_async_copy(y_hbm.at[pl.ds(src_row, PACK_ROWS)],
                                      buf.at[into, kk, pl.ds(dst_row, PACK_ROWS)], sem.at[into]).start()
            return carry
        lax.fori_loop(0, tt, body, 0)

    @pl.when(i == 0)
    def _():
        fetch(dest_ref, 0)

    @pl.when(i + 1 < pl.num_programs(0))
    def _():
        fetch(dest_next_ref, 1 - slot)

    x = x_ref[...]
    xb = x.astype(BF16)
    hidden = (_silu(jnp.dot(xb, swg_ref[...], preferred_element_type=F32))
              * jnp.dot(xb, swu_ref[...], preferred_element_type=F32))
    shared = jnp.dot(hidden.astype(BF16), swd_ref[...], preferred_element_type=F32)

    for kk in range(TOP_K):
        pltpu.make_async_copy(y_hbm.at[pl.ds(0, tt * PACK_ROWS)], buf.at[slot, kk], sem.at[slot]).wait()
    gate = gate_ref[...]
    for sl in range(PACK_ROWS):
        routed_hi = routed_lo = None
        for kk in range(TOP_K):
            hi, lo = _unpack_bf16_pair(buf[slot, kk, pl.ds(sl, tt, stride=PACK_ROWS), :])
            g_k = gate[:, kk:kk + 1]
            routed_hi = g_k * hi if kk == 0 else routed_hi + g_k * hi
            routed_lo = g_k * lo if kk == 0 else routed_lo + g_k * lo
        for half, routed in enumerate((routed_hi, routed_lo)):
            cols = slice((2 * sl + half) * LANES, (2 * sl + half + 1) * LANES)
            o_ref[:, cols] = DN_ALPHA * x[:, cols] + (routed + shared[:, cols])
    o_ref[...] = _layer_norm(o_ref[...], g_ref[...], b_ref[...])


def _moe_combine(xt, ys, dest_tok, gate_tok, sw_gate, sw_up, sw_down, g, b):
    tt = COMBINE_TILE
    n_tiles = N_TOK // tt
    row = lambda i: (i, 0)
    const = lambda i: (0, 0)
    dest_tiles = dest_tok.reshape(n_tiles, 1, tt * TOP_K)
    return pl.pallas_call(
        functools.partial(_combine_kernel, tt=tt),
        grid=(n_tiles,),
        in_specs=[pl.BlockSpec((1, 1, tt * TOP_K), lambda i: (i, 0, 0), memory_space=pltpu.SMEM),
                  pl.BlockSpec((1, 1, tt * TOP_K), lambda i: (jnp.minimum(i + 1, n_tiles - 1), 0, 0),
                               memory_space=pltpu.SMEM),
                  pl.BlockSpec((tt, D_MODEL), row),
                  pl.BlockSpec((tt, TOP_K), row),
                  pl.BlockSpec((D_MODEL, EXPERT_FF), const),
                  pl.BlockSpec((D_MODEL, EXPERT_FF), const),
                  pl.BlockSpec((EXPERT_FF, D_MODEL), const),
                  pl.BlockSpec((1, D_MODEL), const),
                  pl.BlockSpec((1, D_MODEL), const),
                  pl.BlockSpec(memory_space=pl.ANY)],
        out_specs=pl.BlockSpec((tt, D_MODEL), row),
        out_shape=jax.ShapeDtypeStruct((N_TOK, D_MODEL), F32),
        scratch_shapes=[pltpu.VMEM((2, TOP_K, tt * PACK_ROWS, LANES), jnp.uint32), pltpu.SemaphoreType.DMA((2,))],
        compiler_params=_params("arbitrary"),
        name="moe_combine",
    )(dest_tiles, dest_tiles, xt, gate_tok, sw_gate.astype(BF16), sw_up.astype(BF16), sw_down.astype(BF16),
      g, b, ys)


def _moe_layer(xt, layer, w_router, bias, w_gate, w_up, w_down, sw_gate, sw_up, sw_down, g, b):
    eidx, gate, rank, counts, xq = _moe_route(xt, w_router, bias)

    counts = counts[:, 0]
    pcounts = (counts + MOE_BLOCK - 1) // MOE_BLOCK * MOE_BLOCK
    pends = jnp.cumsum(pcounts)
    pstarts = pends - pcounts
    e_iota = jnp.arange(N_EXPERTS, dtype=jnp.int32)[None, :, None]
    dest = rank + jnp.sum(jnp.where(eidx[:, None, :] == e_iota, pstarts[None, :, None], 0), axis=1)
    dest_tok = dest.T.astype(jnp.int32)
    n_used = (pends[-1] // MOE_BLOCK).astype(jnp.int32)
    blk = jnp.minimum(jnp.arange(MOE_N_BLOCKS, dtype=jnp.int32), n_used - 1) * MOE_BLOCK
    block_e = jnp.sum((pends[None, :] <= blk[:, None]).astype(jnp.int32), axis=1)
    block_e = jnp.minimum(block_e, N_EXPERTS - 1)
    after = pends[block_e] // MOE_BLOCK
    next_e = jnp.where(after < n_used, block_e[jnp.minimum(after, MOE_N_BLOCKS - 1)], -1).astype(jnp.int32)

    xs = _moe_dispatch(xq, dest_tok)
    ys = _moe_experts(xs, block_e, next_e, n_used[None], layer, w_gate, w_up, w_down)
    return _moe_combine(xt, ys, dest_tok, gate.T, sw_gate, sw_up, sw_down, g, b)


def kernel(x_prompt, x_sample, state_pool, cache_swa_k, cache_swa_v, state_mlstm_c, state_mlstm_n, state_mlstm_m, pool_w, pool_scale, swa_w_qkv, swa_w_o, swa_sinks, mlstm_w_in, mlstm_b_gates, mlstm_norm_g, mlstm_w_out, ln_g, ln_b, moe_w_router, moe_router_bias, moe_w_gate, moe_w_up, moe_w_down, moe_shared_w_gate, moe_shared_w_up, moe_shared_w_down):
    d = D_MODEL
    xt = jnp.concatenate([x_prompt.reshape(N_PROMPT_TOK, d), x_sample.reshape(N_SAMPLE_TOK, d)], axis=0)
    pool_p, pool_s = [], []
    swk_p, swv_p, swk_s, swv_s = [], [], [], []
    mc_p, mn_p, mm_p, mc_s, mn_s, mm_s = [], [], [], [], [], []
    for i in range(DEPTH):
        kind, slot = i % N_MIXERS, i // N_MIXERS
        g0, b0 = ln_g[i, 0][None], ln_b[i, 0][None]
        if kind == 0:
            xt, sp, ss = _pool_layer(xt, state_pool[slot], pool_w[slot], pool_scale[slot][None], g0, b0)
            pool_p.append(sp)
            pool_s.append(ss)
        elif kind == 1:
            xt, kp, vp, ks, vs = _swa_layer(xt, cache_swa_k[slot], cache_swa_v[slot], swa_w_qkv[slot],
                                            swa_w_o[slot], swa_sinks[slot], g0, b0)
            swk_p.append(kp)
            swv_p.append(vp)
            swk_s.append(ks)
            swv_s.append(vs)
        else:
            xt, cp, np_, mp, cs, ns, ms = _mlstm_layer(
                xt, state_mlstm_c[slot], state_mlstm_n[slot], state_mlstm_m[slot], mlstm_w_in[slot],
                mlstm_b_gates[slot], mlstm_norm_g[slot][None], mlstm_w_out[slot], g0, b0)
            mc_p.append(cp)
            mn_p.append(np_)
            mm_p.append(mp)
            mc_s.append(cs)
            mn_s.append(ns)
            mm_s.append(ms)
        xt = _moe_layer(xt, i, moe_w_router[i], moe_router_bias[i], moe_w_gate, moe_w_up, moe_w_down,
                        moe_shared_w_gate[i], moe_shared_w_up[i], moe_shared_w_down[i],
                        ln_g[i, 1][None], ln_b[i, 1][None])
    y_p = xt[:N_PROMPT_TOK].reshape(BATCH, SEQ, d)
    y_s = xt[N_PROMPT_TOK:].reshape(DEC_BATCH, DEC_SEQ, d)
    return (y_p, y_s, jnp.stack(pool_p), jnp.stack(pool_s), jnp.stack(swk_p), jnp.stack(swv_p),
            jnp.stack(swk_s), jnp.stack(swv_s), jnp.stack(mc_p), jnp.stack(mn_p), jnp.stack(mm_p),
            jnp.stack(mc_s), jnp.stack(mn_s), jnp.stack(mm_s))
```

```python
import functools
import math

import jax
import jax.numpy as jnp
from jax import lax
from jax.experimental import pallas as pl
from jax.experimental.pallas import tpu as pltpu

F32 = jnp.float32
BF16 = jnp.bfloat16

D_MODEL = 2048
BATCH = 2
SEQ = 4096
DEPTH = 4
DEC_BATCH = 32
DEC_SEQ = 4
PAST_LEN = 16384
N_PROMPT_TOK = BATCH * SEQ
N_SAMPLE_TOK = DEC_BATCH * DEC_SEQ
N_TOK = N_PROMPT_TOK + N_SAMPLE_TOK

N_MIXERS = 3
DN_ALPHA = (2.0 * DEPTH) ** 0.25
LN_EPS = 1e-5

POOL_WINDOWS = (2, 4, 8, 16)
POOL_GROUP_DIM = D_MODEL // len(POOL_WINDOWS)
POOL_STATE = max(POOL_WINDOWS) - 1
POOL_HALO = POOL_STATE + 1

SWA_WINDOW = 128
SWA_HEAD_DIM = 64
SWA_HEADS = D_MODEL // SWA_HEAD_DIM
SWA_KV_HEADS = SWA_HEADS // 8
SWA_GROUP = SWA_HEADS // SWA_KV_HEADS
SWA_KV_DIM = SWA_KV_HEADS * SWA_HEAD_DIM

MLSTM_HEADS = 8
MLSTM_HEAD_DIM = D_MODEL // MLSTM_HEADS
MLSTM_CHUNK = 64
MLSTM_SAMPLE_CHUNK = 8

N_EXPERTS = 64
TOP_K = 8
N_EXPERT_GROUPS = 8
TOPK_GROUPS = 4
EXPERT_FF = D_MODEL // 4
ROUTED_SCALE = 2.5

VMEM_LIMIT_BYTES = 56 * 1024 * 1024

ROW_TILE = 320
ROUTER_TILE = 640
POOL_TILE = 512
MOE_BLOCK = 512
MOE_N_BLOCKS = N_TOK * TOP_K // MOE_BLOCK + N_EXPERTS
MOE_ROWS = MOE_N_BLOCKS * MOE_BLOCK
DISPATCH_TILE = 320
COMBINE_TILE = 128
LANES = 128
PACK_ROWS = D_MODEL // (2 * LANES)

_NT = (((1,), (1,)), ((), ()))
_TN = (((0,), (0,)), ((), ()))


def _params(*semantics):
    return pltpu.CompilerParams(dimension_semantics=semantics, vmem_limit_bytes=VMEM_LIMIT_BYTES)


def _layer_norm(z, g, b):
    mu = jnp.mean(z, axis=-1, keepdims=True)
    zc = z - mu
    var = jnp.mean(zc * zc, axis=-1, keepdims=True)
    return zc * lax.rsqrt(var + LN_EPS) * g + b


def _silu(x):
    return x * jax.nn.sigmoid(x)


def _matmul_kernel(x_ref, w_ref, o_ref):
    o_ref[...] = jnp.dot(x_ref[...].astype(BF16), w_ref[...], preferred_element_type=F32)


def _matmul(x, w, *, tm, tn):
    m, k = x.shape
    n = w.shape[1]
    return pl.pallas_call(
        _matmul_kernel,
        grid=(n // tn, m // tm),
        in_specs=[pl.BlockSpec((tm, k), lambda j, i: (i, 0)),
                  pl.BlockSpec((k, tn), lambda j, i: (0, j))],
        out_specs=pl.BlockSpec((tm, tn), lambda j, i: (i, j)),
        out_shape=jax.ShapeDtypeStruct((m, n), F32),
        compiler_params=_params("parallel", "parallel"),
        name="matmul",
    )(x, w)


def _matmul_ln_kernel(a_ref, w_ref, res_ref, g_ref, b_ref, o_ref):
    y = jnp.dot(a_ref[...].astype(BF16), w_ref[...], preferred_element_type=F32)
    o_ref[...] = _layer_norm(DN_ALPHA * res_ref[...] + y, g_ref[...], b_ref[...])


def _matmul_ln(a, w, res, g, b):
    m, k = a.shape
    row = lambda i: (i, 0)
    const = lambda i: (0, 0)
    return pl.pallas_call(
        _matmul_ln_kernel,
        grid=(m // ROW_TILE,),
        in_specs=[pl.BlockSpec((ROW_TILE, k), row),
                  pl.BlockSpec((k, D_MODEL), const),
                  pl.BlockSpec((ROW_TILE, D_MODEL), row),
                  pl.BlockSpec((1, D_MODEL), const),
                  pl.BlockSpec((1, D_MODEL), const)],
        out_specs=pl.BlockSpec((ROW_TILE, D_MODEL), row),
        out_shape=jax.ShapeDtypeStruct((m, D_MODEL), F32),
        compiler_params=_params("parallel"),
        name="matmul_ln",
    )(a, w, res, g, b)


def _pool_core(zbuf, w_ref, scale_ref, g_ref, b_ref, o_ref, *, tt, n_before):
    avail = n_before + lax.broadcasted_iota(jnp.int32, (tt, 1), 0) + 1
    for grp, win in enumerate(POOL_WINDOWS):
        c0, c1 = grp * POOL_GROUP_DIM, (grp + 1) * POOL_GROUP_DIM
        xg = zbuf[POOL_HALO:POOL_HALO + tt, c0:c1]
        total = xg
        for back in range(1, win):
            total = total + zbuf[POOL_HALO - back:POOL_HALO - back + tt, c0:c1]
        count = jnp.minimum(win, avail).astype(F32)
        diff = total / count - xg
        y = jnp.dot(diff.astype(BF16), w_ref[grp], preferred_element_type=F32) * scale_ref[:, c0:c1]
        o_ref[:, c0:c1] = DN_ALPHA * xg + y
    o_ref[...] = _layer_norm(o_ref[...], g_ref[...], b_ref[...])


def _pool_prompt_kernel(x_ref, halo_ref, w_ref, scale_ref, g_ref, b_ref, o_ref, zbuf, *, tt):
    i = pl.program_id(1)
    zbuf[0:POOL_HALO, :] = jnp.where(i == 0, 0.0, halo_ref[...])
    zbuf[POOL_HALO:POOL_HALO + tt, :] = x_ref[...]
    _pool_core(zbuf, w_ref, scale_ref, g_ref, b_ref, o_ref, tt=tt, n_before=i * tt)


def _pool_sample_kernel(z_ref, w_ref, scale_ref, g_ref, b_ref, o_ref, *, tt):
    _pool_core(z_ref.at[0], w_ref, scale_ref, g_ref, b_ref, o_ref.at[0], tt=tt, n_before=PAST_LEN)


def _pool_layer(xp, xs, state, w, scale, g, b):
    tt = POOL_TILE
    tiles = SEQ // tt
    halo_per_tile = tt // POOL_HALO
    const2 = lambda bb, i: (0, 0)
    w_bf = w.astype(BF16)
    out = pl.pallas_call(
        functools.partial(_pool_prompt_kernel, tt=tt),
        grid=(BATCH, tiles),
        in_specs=[pl.BlockSpec((tt, D_MODEL), lambda bb, i: (bb * tiles + i, 0)),
                  pl.BlockSpec((POOL_HALO, D_MODEL),
                               lambda bb, i: (jnp.maximum((bb * tiles + i) * halo_per_tile - 1, 0), 0)),
                  pl.BlockSpec(w_bf.shape, lambda bb, i: (0, 0, 0)),
                  pl.BlockSpec((1, D_MODEL), const2),
                  pl.BlockSpec((1, D_MODEL), const2),
                  pl.BlockSpec((1, D_MODEL), const2)],
        out_specs=pl.BlockSpec((tt, D_MODEL), lambda bb, i: (bb * tiles + i, 0)),
        out_shape=jax.ShapeDtypeStruct((N_TOK, D_MODEL), F32),
        scratch_shapes=[pltpu.VMEM((POOL_HALO + tt, D_MODEL), F32)],
        compiler_params=_params("parallel", "arbitrary"),
        name="pool_prompt",
    )(xp, xp, w_bf, scale, g, b)

    ts = 16
    zs = jnp.concatenate([jnp.zeros((DEC_BATCH, 1, D_MODEL), F32), state, xs,
                          jnp.zeros((DEC_BATCH, ts - DEC_SEQ, D_MODEL), F32)], axis=1)
    const1 = lambda bb: (0, 0)
    out_s = pl.pallas_call(
        functools.partial(_pool_sample_kernel, tt=ts),
        grid=(DEC_BATCH,),
        in_specs=[pl.BlockSpec((1, POOL_HALO + ts, D_MODEL), lambda bb: (bb, 0, 0)),
                  pl.BlockSpec(w_bf.shape, lambda bb: (0, 0, 0)),
                  pl.BlockSpec((1, D_MODEL), const1),
                  pl.BlockSpec((1, D_MODEL), const1),
                  pl.BlockSpec((1, D_MODEL), const1)],
        out_specs=pl.BlockSpec((1, ts, D_MODEL), lambda bb: (bb, 0, 0)),
        out_shape=jax.ShapeDtypeStruct((DEC_BATCH, ts, D_MODEL), F32),
        compiler_params=_params("parallel"),
        name="pool_sample",
    )(zs, w_bf, scale, g, b)
    out = lax.dynamic_update_slice(out, out_s[:, :DEC_SEQ].reshape(N_SAMPLE_TOK, D_MODEL), (N_PROMPT_TOK, 0))

    new_p = jnp.stack([xp[(bb + 1) * SEQ - POOL_STATE:(bb + 1) * SEQ] for bb in range(BATCH)])
    new_s = jnp.concatenate([state, xs], axis=1)[:, DEC_SEQ:]
    return out, new_p, new_s


def _alibi_slope(h):
    return 2.0 ** (-8.0 * (h + 1.0) / SWA_HEADS)


def _attn_core(q_ref, k_all, v_all, sink_ref, o_ref, *, rows, first_block):
    w, dh = SWA_WINDOW, SWA_HEAD_DIM
    qi = lax.broadcasted_iota(jnp.int32, (rows, 2 * w), 0)
    sj = lax.broadcasted_iota(jnp.int32, (rows, 2 * w), 1)
    dist = (w + qi) - sj
    valid = (dist >= 0) & (dist <= w)
    if first_block is not None:
        valid = valid & ((sj >= w) | jnp.logical_not(first_block))
    masked_dist = jnp.where(valid, dist.astype(F32), jnp.inf)
    for kv in range(SWA_KV_HEADS):
        c0, c1 = kv * dh, (kv + 1) * dh
        heads = range(kv * SWA_GROUP, (kv + 1) * SWA_GROUP)
        q = jnp.concatenate([q_ref[:, h * dh:(h + 1) * dh] for h in heads], axis=0)
        q = (q * (dh ** -0.5)).astype(BF16)
        s_all = lax.dot_general(q, k_all[:, c0:c1], _NT, preferred_element_type=F32)
        probs, dens = [], []
        for j, h in enumerate(heads):
            sink = sink_ref[h]
            s = s_all[j * rows:(j + 1) * rows] - _alibi_slope(h) * masked_dist
            m = jnp.maximum(jnp.max(s, axis=1, keepdims=True), sink)
            e = jnp.exp(s - m)
            dens.append(jnp.sum(e, axis=1, keepdims=True) + jnp.exp(sink - m))
            probs.append(e.astype(BF16))
        o_all = jnp.dot(jnp.concatenate(probs, axis=0), v_all[:, c0:c1], preferred_element_type=F32)
        for j, h in enumerate(heads):
            o_ref[:, h * dh:(h + 1) * dh] = o_all[j * rows:(j + 1) * rows] / dens[j]


def _attn_prompt_kernel(sink_ref, q_ref, kp_ref, kc_ref, vp_ref, vc_ref, o_ref):
    n = pl.program_id(1)
    k_all = jnp.concatenate([kp_ref[...], kc_ref[...]], axis=0).astype(BF16)
    v_all = jnp.concatenate([vp_ref[...], vc_ref[...]], axis=0).astype(BF16)
    _attn_core(q_ref, k_all, v_all, sink_ref, o_ref, rows=SWA_WINDOW, first_block=(n == 0))


def _attn_sample_kernel(sink_ref, q_ref, k_ref, v_ref, o_ref, *, rows):
    _attn_core(q_ref.at[0], k_ref[0].astype(BF16), v_ref[0].astype(BF16), sink_ref, o_ref.at[0],
               rows=rows, first_block=None)


def _swa_layer(xt, cache_k, cache_v, w_qkv, w_o, sinks, g, b):
    w = SWA_WINDOW
    qkv = _matmul(xt, w_qkv.astype(BF16), tm=640, tn=1280)
    nb = SEQ // w
    k_col = D_MODEL // SWA_KV_DIM
    cur = lambda bb, n, *_: (bb * nb + n, k_col)
    prev = lambda bb, n, *_: (jnp.maximum(bb * nb + n - 1, 0), k_col)
    cur_v = lambda bb, n, *_: (bb * nb + n, k_col + 1)
    prev_v = lambda bb, n, *_: (jnp.maximum(bb * nb + n - 1, 0), k_col + 1)
    o = pl.pallas_call(
        _attn_prompt_kernel,
        grid_spec=pltpu.PrefetchScalarGridSpec(
            num_scalar_prefetch=1,
            grid=(BATCH, nb),
            in_specs=[pl.BlockSpec((w, D_MODEL), lambda bb, n, *_: (bb * nb + n, 0)),
                      pl.BlockSpec((w, SWA_KV_DIM), prev),
                      pl.BlockSpec((w, SWA_KV_DIM), cur),
                      pl.BlockSpec((w, SWA_KV_DIM), prev_v),
                      pl.BlockSpec((w, SWA_KV_DIM), cur_v)],
            out_specs=pl.BlockSpec((w, D_MODEL), lambda bb, n, *_: (bb * nb + n, 0))),
        out_shape=jax.ShapeDtypeStruct((N_TOK, D_MODEL), F32),
        compiler_params=_params("parallel", "arbitrary"),
        name="attn_prompt",
    )(sinks, qkv, qkv, qkv, qkv, qkv)

    rows = 16
    qkv_s = qkv[N_PROMPT_TOK:].reshape(DEC_BATCH, DEC_SEQ, -1)
    q_s = jnp.pad(qkv_s[..., :D_MODEL], ((0, 0), (0, rows - DEC_SEQ), (0, 0)))
    k_new = qkv_s[..., D_MODEL:D_MODEL + SWA_KV_DIM]
    v_new = qkv_s[..., D_MODEL + SWA_KV_DIM:]
    kz = jnp.concatenate([cache_k.reshape(DEC_BATCH, w, SWA_KV_DIM), k_new], axis=1)
    vz = jnp.concatenate([cache_v.reshape(DEC_BATCH, w, SWA_KV_DIM), v_new], axis=1)
    pad_keys = ((0, 0), (0, w - DEC_SEQ), (0, 0))
    o_s = pl.pallas_call(
        functools.partial(_attn_sample_kernel, rows=rows),
        grid_spec=pltpu.PrefetchScalarGridSpec(
            num_scalar_prefetch=1,
            grid=(DEC_BATCH,),
            in_specs=[pl.BlockSpec((1, rows, D_MODEL), lambda bb, *_: (bb, 0, 0)),
                      pl.BlockSpec((1, 2 * w, SWA_KV_DIM), lambda bb, *_: (bb, 0, 0)),
                      pl.BlockSpec((1, 2 * w, SWA_KV_DIM), lambda bb, *_: (bb, 0, 0))],
            out_specs=pl.BlockSpec((1, rows, D_MODEL), lambda bb, *_: (bb, 0, 0))),
        out_shape=jax.ShapeDtypeStruct((DEC_BATCH, rows, D_MODEL), F32),
        compiler_params=_params("parallel"),
        name="attn_sample",
    )(sinks, q_s, jnp.pad(kz, pad_keys), jnp.pad(vz, pad_keys))
    o = lax.dynamic_update_slice(o, o_s[:, :DEC_SEQ].reshape(N_SAMPLE_TOK, D_MODEL), (N_PROMPT_TOK, 0))

    out = _matmul_ln(o, w_o.astype(BF16), xt, g, b)

    kv_shape = (SWA_WINDOW, SWA_KV_HEADS, SWA_HEAD_DIM)
    kv_p = jnp.stack([qkv[(bb + 1) * SEQ - w:(bb + 1) * SEQ, D_MODEL:] for bb in range(BATCH)])
    new_k_p = kv_p[..., :SWA_KV_DIM].reshape((BATCH,) + kv_shape)
    new_v_p = kv_p[..., SWA_KV_DIM:].reshape((BATCH,) + kv_shape)
    new_k_s = kz[:, DEC_SEQ:].reshape((DEC_BATCH,) + kv_shape)
    new_v_s = vz[:, DEC_SEQ:].reshape((DEC_BATCH,) + kv_shape)
    return out, new_k_p, new_v_p, new_k_s, new_v_s


def _log_sigmoid(x):
    return jnp.minimum(x, 0.0) - jnp.log(1.0 + jnp.exp(-jnp.abs(x)))


def _mlstm_kernel(bias_ref, q_ref, k_ref, v_ref, og_ref, gates_ref, c0_ref, n0_ref, m0_ref, ng_ref,
                  h_ref, c_ref, n_ref, m_ref, c_scr, n_scr, m_scr, *, chunk, n_valid):
    step = pl.program_id(1)

    @pl.when(step == 0)
    def _():
        c_scr[...] = c0_ref[0]
        n_scr[...] = n0_ref[0]
        m_scr[...] = m0_ref[0]

    for head in range(MLSTM_HEADS):
        _mlstm_head(head, step, bias_ref, q_ref, k_ref, v_ref, og_ref, gates_ref, ng_ref, h_ref, c_scr, n_scr, m_scr,
                    chunk=chunk, n_valid=n_valid)

    @pl.when(step == pl.num_programs(1) - 1)
    def _():
        c_ref[0] = c_scr[...]
        n_ref[0] = n_scr[...]
        m_ref[0] = m_scr[...]


def _mlstm_head(head, step, bias_ref, q_ref, k_ref, v_ref, og_ref, gates_ref, ng_ref, h_ref, c_scr, n_scr, m_scr,
                *, chunk, n_valid):
    ln = chunk
    cols = slice(head * MLSTM_HEAD_DIM, (head + 1) * MLSTM_HEAD_DIM)
    ig = gates_ref[0, head, pl.ds(step, 1), :] + bias_ref[head]
    lf = _log_sigmoid(gates_ref[0, MLSTM_HEADS + head, pl.ds(step, 1), :] + bias_ref[MLSTM_HEADS + head])
    if n_valid < ln:
        col = lax.broadcasted_iota(jnp.int32, (1, ln), 1)
        ig = jnp.where(col < n_valid, ig, -1e30)
        lf = jnp.where(col < n_valid, lf, 0.0)

    ri = lax.broadcasted_iota(jnp.int32, (ln, ln), 0)
    ci = lax.broadcasted_iota(jnp.int32, (ln, ln), 1)
    eye = ri == ci
    causal = ci <= ri

    def to_col(row):
        return jnp.sum(jnp.where(eye, row, 0.0), axis=1, keepdims=True)

    f_col = to_col(lf)
    b_col = jnp.sum(jnp.where(causal, lf, 0.0), axis=1, keepdims=True)
    b_row = jnp.sum(jnp.where(ri <= ci, f_col, 0.0), axis=0, keepdims=True)
    b_last = jnp.sum(lf, axis=1, keepdims=True)
    m_prev = m_scr[head, :, 0:1]
    a_col = b_col + m_prev
    dmat = jnp.where(causal, b_col - b_row + ig, -jnp.inf)
    mt = jnp.maximum(a_col, jnp.max(dmat, axis=1, keepdims=True))
    w_inter = jnp.exp(a_col - mt)

    q = q_ref[:, cols]
    k = k_ref[:, cols] * (MLSTM_HEAD_DIM ** -0.5)
    v = v_ref[:, cols]
    qb = q.astype(BF16)
    kb = k.astype(BF16)
    c_prev = c_scr[head]
    n_prev = n_scr[head]
    scores = lax.dot_general(qb, kb, _NT, preferred_element_type=F32)
    w_intra = jnp.exp(dmat - mt) * scores
    num = (w_inter * lax.dot_general(qb, c_prev.astype(BF16), _NT, preferred_element_type=F32)
           + jnp.dot(w_intra.astype(BF16), v.astype(BF16), preferred_element_type=F32))
    den = (w_inter * jnp.sum(q * n_prev, axis=1, keepdims=True)
           + jnp.sum(w_intra, axis=1, keepdims=True))
    h = num / jnp.maximum(jnp.abs(den), jnp.exp(-mt))

    g_row = b_last - b_row + ig
    m_new = jnp.maximum(b_last + m_prev, jnp.max(g_row, axis=1, keepdims=True))
    decay = jnp.exp(b_last + m_prev - m_new)
    wg_col = to_col(jnp.exp(g_row - m_new))
    c_new = decay * c_prev + lax.dot_general((v * wg_col).astype(BF16), kb, _TN, preferred_element_type=F32)
    n_new = decay * n_prev + jnp.sum(wg_col * k, axis=0, keepdims=True)
    c_scr[head] = c_new
    n_scr[head] = n_new
    m_scr[head] = jnp.broadcast_to(m_new, (1, LANES))

    mu = jnp.mean(h, axis=1, keepdims=True)
    hc = h - mu
    var = jnp.mean(hc * hc, axis=1, keepdims=True)
    h_ref[:, cols] = hc * lax.rsqrt(var + LN_EPS) * ng_ref[:, cols] * jax.nn.sigmoid(og_ref[:, cols])


def _mlstm_scan(proj, gates, b_gates, norm_g, c0, n0, m0, *, batch, n_chunks, chunk, n_valid, out_rows):
    d, dh, nh = D_MODEL, MLSTM_HEAD_DIM, MLSTM_HEADS
    rows = lambda off: (lambda bb, cc, *_: (bb * n_chunks + cc, off))
    per_seq = lambda bb, cc, *_: (bb, 0, 0, 0)
    return pl.pallas_call(
        functools.partial(_mlstm_kernel, chunk=chunk, n_valid=n_valid),
        grid_spec=pltpu.PrefetchScalarGridSpec(
            num_scalar_prefetch=1,
            grid=(batch, n_chunks),
            in_specs=[pl.BlockSpec((chunk, d), rows(0)),
                      pl.BlockSpec((chunk, d), rows(1)),
                      pl.BlockSpec((chunk, d), rows(2)),
                      pl.BlockSpec((chunk, d), rows(3)),
                      pl.BlockSpec((1, 2 * nh, n_chunks, chunk), per_seq),
                      pl.BlockSpec((1, nh, dh, dh), per_seq),
                      pl.BlockSpec((1, nh, 1, dh), per_seq),
                      pl.BlockSpec((1, nh, 1, LANES), per_seq),
                      pl.BlockSpec((1, d), lambda bb, cc, *_: (0, 0))],
            out_specs=[pl.BlockSpec((chunk, d), lambda bb, cc, *_: (bb * n_chunks + cc, 0)),
                       pl.BlockSpec((1, nh, dh, dh), per_seq),
                       pl.BlockSpec((1, nh, 1, dh), per_seq),
                       pl.BlockSpec((1, nh, 1, LANES), per_seq)],
            scratch_shapes=[pltpu.VMEM((nh, dh, dh), F32), pltpu.VMEM((nh, 1, dh), F32),
                            pltpu.VMEM((nh, 1, LANES), F32)]),
        out_shape=[jax.ShapeDtypeStruct((out_rows, d), F32),
                   jax.ShapeDtypeStruct((batch, nh, dh, dh), F32),
                   jax.ShapeDtypeStruct((batch, nh, 1, dh), F32),
                   jax.ShapeDtypeStruct((batch, nh, 1, LANES), F32)],
        compiler_params=_params("parallel", "arbitrary"),
        name="mlstm_scan",
    )(b_gates, proj, proj, proj, proj, gates, c0, n0[:, :, None, :],
      jnp.broadcast_to(m0[:, :, None, None], (batch, nh, 1, LANES)), norm_g)


def _mlstm_layer(xt, c0_s, n0_s, m0_s, w_in, b_gates, norm_g, w_out, g, b):
    d, nh, dh = D_MODEL, MLSTM_HEADS, MLSTM_HEAD_DIM
    proj = _matmul(xt, w_in[:, :4 * d].astype(BF16), tm=640, tn=1024)
    w_gates = jnp.pad(w_in[:, 4 * d:], ((0, 0), (0, 128 - 2 * nh)))
    gate_pre = _matmul(xt, w_gates.astype(BF16), tm=640, tn=128)[:, :2 * nh]

    nc = SEQ // MLSTM_CHUNK
    gates_p = gate_pre[:N_PROMPT_TOK].reshape(BATCH, nc, MLSTM_CHUNK, 2 * nh).transpose(0, 3, 1, 2)
    zeros = lambda *s: jnp.zeros(s, F32)
    hn, c_p, n_p, m_p = _mlstm_scan(
        proj, gates_p, b_gates, norm_g, zeros(BATCH, nh, dh, dh), zeros(BATCH, nh, dh), zeros(BATCH, nh),
        batch=BATCH, n_chunks=nc, chunk=MLSTM_CHUNK, n_valid=MLSTM_CHUNK, out_rows=N_TOK)

    ls = MLSTM_SAMPLE_CHUNK
    pad_t = ((0, 0), (0, ls - DEC_SEQ), (0, 0))
    proj_s = jnp.pad(proj[N_PROMPT_TOK:].reshape(DEC_BATCH, DEC_SEQ, 4 * d), pad_t).reshape(DEC_BATCH * ls, 4 * d)
    gates_s = jnp.pad(gate_pre[N_PROMPT_TOK:].reshape(DEC_BATCH, DEC_SEQ, 2 * nh), pad_t)
    gates_s = gates_s.transpose(0, 2, 1)[:, :, None, :]
    hn_s, c_s, n_s, m_s = _mlstm_scan(
        proj_s, gates_s, b_gates, norm_g, c0_s, n0_s, m0_s,
        batch=DEC_BATCH, n_chunks=1, chunk=ls, n_valid=DEC_SEQ, out_rows=DEC_BATCH * ls)
    hn_s = hn_s.reshape(DEC_BATCH, ls, d)[:, :DEC_SEQ].reshape(N_SAMPLE_TOK, d)
    hn = lax.dynamic_update_slice(hn, hn_s, (N_PROMPT_TOK, 0))

    out = _matmul_ln(hn, w_out.astype(BF16), xt, g, b)
    return (out, c_p, n_p[:, :, 0], m_p[:, :, 0, 0], c_s, n_s[:, :, 0], m_s[:, :, 0, 0])


def _router_kernel(x_ref, wt_ref, bias_ref, tri_ref, eidx_ref, gate_ref, rank_ref, count_ref, xq_ref, count_scr,
                   *, tt):
    neg = -jnp.inf
    x = x_ref[...]
    logits = lax.dot_general(wt_ref[...].astype(BF16), x.astype(BF16), _NT, preferred_element_type=F32)
    s = jax.nn.sigmoid(logits)
    sb = s + bias_ref[...]
    per_group = N_EXPERTS // N_EXPERT_GROUPS

    sb3 = sb.reshape(N_EXPERT_GROUPS, per_group, tt)
    i3 = lax.broadcasted_iota(jnp.int32, sb3.shape, 1)
    m1 = jnp.max(sb3, axis=1, keepdims=True)
    first = jnp.min(jnp.where(sb3 == m1, i3, per_group), axis=1, keepdims=True)
    m2 = jnp.max(jnp.where(i3 == first, neg, sb3), axis=1, keepdims=True)
    gscore = (m1 + m2).reshape(N_EXPERT_GROUPS, tt)

    gi = lax.broadcasted_iota(jnp.int32, gscore.shape, 0)
    gsel = jnp.zeros(gscore.shape, jnp.bool_)
    cur = gscore
    for _ in range(TOPK_GROUPS):
        mx = jnp.max(cur, axis=0, keepdims=True)
        pick = gi == jnp.min(jnp.where(cur == mx, gi, N_EXPERT_GROUPS), axis=0, keepdims=True)
        gsel = gsel | pick
        cur = jnp.where(pick, neg, cur)
    emask = jnp.broadcast_to(gsel.reshape(N_EXPERT_GROUPS, 1, tt), sb3.shape).reshape(N_EXPERTS, tt)

    ei = lax.broadcasted_iota(jnp.int32, sb.shape, 0)
    sel = jnp.zeros(sb.shape, jnp.bool_)
    cur = jnp.where(emask, sb, neg)
    picks = []
    for _ in range(TOP_K):
        mx = jnp.max(cur, axis=0, keepdims=True)
        idx = jnp.min(jnp.where(cur == mx, ei, N_EXPERTS), axis=0, keepdims=True)
        pick = ei == idx
        picks.append((idx, pick))
        sel = sel | pick
        cur = jnp.where(pick, neg, cur)
    s_sel = jnp.where(sel, s, 0.0)
    gate = s_sel / jnp.sum(s_sel, axis=0, keepdims=True) * ROUTED_SCALE

    @pl.when(pl.program_id(0) == 0)
    def _():
        count_scr[...] = jnp.zeros_like(count_scr)

    sel_f = sel.astype(F32)
    incl = jnp.dot(sel_f.astype(BF16), tri_ref[...], preferred_element_type=F32)
    rank = count_scr[:, 0:1] + incl - sel_f
    count_scr[...] = count_scr[...] + jnp.sum(sel_f, axis=1, keepdims=True)
    count_ref[...] = count_scr[...].astype(jnp.int32)

    for kk, (idx, pick) in enumerate(picks):
        eidx_ref[kk:kk + 1, :] = idx
        gate_ref[kk:kk + 1, :] = jnp.sum(jnp.where(pick, gate, 0.0), axis=0, keepdims=True)
        rank_ref[kk:kk + 1, :] = jnp.sum(jnp.where(pick, rank, 0.0), axis=0, keepdims=True).astype(jnp.int32)

    for sl in range(PACK_ROWS):
        c0 = 2 * LANES * sl
        xq_ref[pl.ds(sl, tt, stride=PACK_ROWS), :] = _pack_bf16_pair(x[:, c0:c0 + LANES],
                                                                     x[:, c0 + LANES:c0 + 2 * LANES])


def _pack_bf16_pair(hi, lo):
    hb = lax.bitcast_convert_type(hi.astype(BF16).astype(F32), jnp.uint32)
    lb = lax.bitcast_convert_type(lo.astype(BF16).astype(F32), jnp.uint32)
    return hb | (lb >> 16)


def _unpack_bf16_pair(word):
    hi = lax.bitcast_convert_type(word & jnp.uint32(0xFFFF0000), F32)
    lo = lax.bitcast_convert_type(word << 16, F32)
    return hi, lo


def _moe_route(xt, w_router, bias):
    tt = ROUTER_TILE
    col = lambda i: (0, i)
    const = lambda i: (0, 0)
    tri = (jnp.arange(tt)[:, None] <= jnp.arange(tt)[None, :]).astype(BF16)
    return pl.pallas_call(
        functools.partial(_router_kernel, tt=tt),
        grid=(N_TOK // tt,),
        in_specs=[pl.BlockSpec((tt, D_MODEL), lambda i: (i, 0)),
                  pl.BlockSpec((N_EXPERTS, D_MODEL), const),
                  pl.BlockSpec((N_EXPERTS, 1), const),
                  pl.BlockSpec((tt, tt), const)],
        out_specs=[pl.BlockSpec((TOP_K, tt), col), pl.BlockSpec((TOP_K, tt), col), pl.BlockSpec((TOP_K, tt), col),
                   pl.BlockSpec((N_EXPERTS, LANES), const),
                   pl.BlockSpec((tt * PACK_ROWS, LANES), lambda i: (i, 0))],
        out_shape=[jax.ShapeDtypeStruct((TOP_K, N_TOK), jnp.int32),
                   jax.ShapeDtypeStruct((TOP_K, N_TOK), F32),
                   jax.ShapeDtypeStruct((TOP_K, N_TOK), jnp.int32),
                   jax.ShapeDtypeStruct((N_EXPERTS, LANES), jnp.int32),
                   jax.ShapeDtypeStruct((N_TOK * PACK_ROWS, LANES), jnp.uint32)],
        scratch_shapes=[pltpu.VMEM((N_EXPERTS, LANES), F32)],
        compiler_params=_params("arbitrary"),
        name="moe_router",
    )(xt, w_router.T, bias[:, None], tri)


def _sorted_row(pstart_ref, eidx_ref, rank_ref, j):
    return pstart_ref[eidx_ref[0, 0, j]] + rank_ref[0, 0, j]


def _dispatch_kernel(pstart_ref, eidx_ref, rank_ref, xq_ref, xs_hbm, sem, *, tt):
    def issue(t, carry):
        src = xq_ref.at[pl.ds(pl.multiple_of(t * PACK_ROWS, PACK_ROWS), PACK_ROWS)]
        for kk in range(TOP_K):
            row = _sorted_row(pstart_ref, eidx_ref, rank_ref, t * TOP_K + kk)
            row = pl.multiple_of(row * PACK_ROWS, PACK_ROWS)
            pltpu.make_async_copy(src, xs_hbm.at[pl.ds(row, PACK_ROWS)], sem).start()
        return carry
    lax.fori_loop(0, tt, issue, 0)
    for _ in range(TOP_K):
        pltpu.make_async_copy(xq_ref, xs_hbm.at[pl.ds(0, tt * PACK_ROWS)], sem).wait()


def _moe_dispatch(xq, pstarts, eidx_tok, rank_tok):
    tt = DISPATCH_TILE
    n_tiles = N_TOK // tt
    slots = pl.BlockSpec((1, 1, tt * TOP_K), lambda i, *_: (i, 0, 0), memory_space=pltpu.SMEM)
    return pl.pallas_call(
        functools.partial(_dispatch_kernel, tt=tt),
        grid_spec=pltpu.PrefetchScalarGridSpec(
            num_scalar_prefetch=1,
            grid=(n_tiles,),
            in_specs=[slots, slots, pl.BlockSpec((tt * PACK_ROWS, LANES), lambda i, *_: (i, 0))],
            out_specs=pl.BlockSpec(memory_space=pl.ANY),
            scratch_shapes=[pltpu.SemaphoreType.DMA(())]),
        out_shape=jax.ShapeDtypeStruct((MOE_ROWS * PACK_ROWS, LANES), jnp.uint32),
        compiler_params=_params("arbitrary"),
        name="moe_dispatch",
    )(pstarts, eidx_tok.reshape(n_tiles, 1, tt * TOP_K), rank_tok.reshape(n_tiles, 1, tt * TOP_K), xq)


def _experts_kernel(be_ref, next_ref, nused_ref, xs_ref, wg_hbm, wu_hbm, wd_hbm, y_ref,
                    wg_f, wu_f, wd_f, wgu_s, wd_s, sem, *, layer):
    i = pl.program_id(0)
    tm = MOE_BLOCK

    def weight_copies(e):
        return (pltpu.make_async_copy(wg_hbm.at[layer, e], wg_f, sem.at[0]),
                pltpu.make_async_copy(wu_hbm.at[layer, e], wu_f, sem.at[1]),
                pltpu.make_async_copy(wd_hbm.at[layer, e], wd_f, sem.at[2]))

    @pl.when(i == 0)
    def _():
        for cp in weight_copies(be_ref[0]):
            cp.start()

    @pl.when(i < nused_ref[0])
    def _():
        first_of_expert = (i == 0) | (be_ref[i] != be_ref[jnp.maximum(i - 1, 0)])

        @pl.when(first_of_expert)
        def _():
            for cp in weight_copies(be_ref[i]):
                cp.wait()
            wgu_s[:, :EXPERT_FF] = wg_f[...].astype(BF16)
            wgu_s[:, EXPERT_FF:] = wu_f[...].astype(BF16)
            wd_s[...] = wd_f[...].astype(BF16)

            @pl.when(next_ref[i] >= 0)
            def _():
                for cp in weight_copies(next_ref[i]):
                    cp.start()

        parts = []
        for sl in range(PACK_ROWS):
            parts.extend(_unpack_bf16_pair(xs_ref[pl.ds(sl, tm, stride=PACK_ROWS), :]))
        x = jnp.concatenate(parts, axis=1).astype(BF16)
        gate_up = jnp.dot(x, wgu_s[...], preferred_element_type=F32)
        hidden = _silu(gate_up[:, :EXPERT_FF]) * gate_up[:, EXPERT_FF:]
        y = jnp.dot(hidden.astype(BF16), wd_s[...], preferred_element_type=F32)
        for sl in range(PACK_ROWS):
            c0 = 2 * LANES * sl
            y_ref[pl.ds(sl, tm, stride=PACK_ROWS), :] = _pack_bf16_pair(y[:, c0:c0 + LANES],
                                                                        y[:, c0 + LANES:c0 + 2 * LANES])


def _moe_experts(xs, block_e, next_e, n_used, layer, w_gate, w_up, w_down):
    last = lambda i, be, nx, nu: (jnp.minimum(i, nu[0] - 1), 0)
    return pl.pallas_call(
        functools.partial(_experts_kernel, layer=layer),
        grid_spec=pltpu.PrefetchScalarGridSpec(
            num_scalar_prefetch=3,
            grid=(MOE_N_BLOCKS,),
            in_specs=[pl.BlockSpec((MOE_BLOCK * PACK_ROWS, LANES), last),
                      pl.BlockSpec(memory_space=pl.ANY),
                      pl.BlockSpec(memory_space=pl.ANY),
                      pl.BlockSpec(memory_space=pl.ANY)],
            out_specs=pl.BlockSpec((MOE_BLOCK * PACK_ROWS, LANES), last),
            scratch_shapes=[pltpu.VMEM((D_MODEL, EXPERT_FF), F32), pltpu.VMEM((D_MODEL, EXPERT_FF), F32),
                            pltpu.VMEM((EXPERT_FF, D_MODEL), F32),
                            pltpu.VMEM((D_MODEL, 2 * EXPERT_FF), BF16),
                            pltpu.VMEM((EXPERT_FF, D_MODEL), BF16),
                            pltpu.SemaphoreType.DMA((3,))]),
        out_shape=jax.ShapeDtypeStruct((MOE_ROWS * PACK_ROWS, LANES), jnp.uint32),
        compiler_params=_params("arbitrary"),
        name="moe_experts",
    )(block_e, next_e, n_used, xs, w_gate, w_up, w_down)


def _combine_kernel(pstart_ref, eidx_ref, rank_ref, eidx_next_ref, rank_next_ref, x_ref, gate_ref, swg_ref, swu_ref,
                    swd_ref, g_ref, b_ref, y_hbm, *rest, tt, n_first):
    out_refs, (buf, acc, sem) = rest[:-3], rest[-3:]
    i = pl.program_id(0)
    slot = i % 2

    def fetch(e_ref, r_ref, into):
        def body(t, carry):
            dst_row = pl.multiple_of(t * PACK_ROWS, PACK_ROWS)
            for kk in range(TOP_K):
                src_row = _sorted_row(pstart_ref, e_ref, r_ref, t * TOP_K + kk)
                src_row = pl.multiple_of(src_row * PACK_ROWS, PACK_ROWS)
                pltpu.make_async_copy(y_hbm.at[pl.ds(src_row, PACK_ROWS)],
                                      buf.at[into, kk, pl.ds(dst_row, PACK_ROWS)], sem.at[into]).start()
            return carry
        lax.fori_loop(0, tt, body, 0)

    @pl.when(i == 0)
    def _():
        fetch(eidx_ref, rank_ref, 0)

    @pl.when(i + 1 < pl.num_programs(0))
    def _():
        fetch(eidx_next_ref, rank_next_ref, 1 - slot)

    x = x_ref[...]
    xb = x.astype(BF16)
    hidden = (_silu(jnp.dot(xb, swg_ref[...], preferred_element_type=F32))
              * jnp.dot(xb, swu_ref[...], preferred_element_type=F32))
    shared = jnp.dot(hidden.astype(BF16), swd_ref[...], preferred_element_type=F32)

    for kk in range(TOP_K):
        pltpu.make_async_copy(y_hbm.at[pl.ds(0, tt * PACK_ROWS)], buf.at[slot, kk], sem.at[slot]).wait()
    gate = gate_ref[...]
    for sl in range(PACK_ROWS):
        routed_hi = routed_lo = None
        for kk in range(TOP_K):
            hi, lo = _unpack_bf16_pair(buf[slot, kk, pl.ds(sl, tt, stride=PACK_ROWS), :])
            g_k = gate[:, kk:kk + 1]
            routed_hi = g_k * hi if kk == 0 else routed_hi + g_k * hi
            routed_lo = g_k * lo if kk == 0 else routed_lo + g_k * lo
        for half, routed in enumerate((routed_hi, routed_lo)):
            cols = slice((2 * sl + half) * LANES, (2 * sl + half + 1) * LANES)
            acc[:, cols] = DN_ALPHA * x[:, cols] + (routed + shared[:, cols])
    res = _layer_norm(acc[...], g_ref[...], b_ref[...])
    if n_first is None:
        out_refs[0][...] = res
    else:
        @pl.when(i < n_first)
        def _():
            out_refs[0][...] = res

        @pl.when(i >= n_first)
        def _():
            out_refs[1][...] = res


def _moe_combine(xt, ys, pstarts, eidx_tok, rank_tok, gate_tok, sw_gate, sw_up, sw_down, g, b, *, split):
    tt = COMBINE_TILE
    n_tiles = N_TOK // tt
    row = lambda i, *_: (i, 0)
    const = lambda i, *_: (0, 0)
    slots = pl.BlockSpec((1, 1, tt * TOP_K), lambda i, *_: (i, 0, 0), memory_space=pltpu.SMEM)
    slots_next = pl.BlockSpec((1, 1, tt * TOP_K), lambda i, *_: (jnp.minimum(i + 1, n_tiles - 1), 0, 0),
                              memory_space=pltpu.SMEM)
    eidx_tiles = eidx_tok.reshape(n_tiles, 1, tt * TOP_K)
    rank_tiles = rank_tok.reshape(n_tiles, 1, tt * TOP_K)
    if split:
        n_first = N_PROMPT_TOK // tt
        assert N_SAMPLE_TOK == tt
        out_specs = [pl.BlockSpec((tt, D_MODEL), lambda i, *_: (jnp.minimum(i, n_first - 1), 0)),
                     pl.BlockSpec((tt, D_MODEL), const)]
        out_shape = [jax.ShapeDtypeStruct((N_PROMPT_TOK, D_MODEL), F32),
                     jax.ShapeDtypeStruct((N_SAMPLE_TOK, D_MODEL), F32)]
    else:
        n_first = None
        out_specs = [pl.BlockSpec((tt, D_MODEL), row)]
        out_shape = [jax.ShapeDtypeStruct((N_TOK, D_MODEL), F32)]
    return pl.pallas_call(
        functools.partial(_combine_kernel, tt=tt, n_first=n_first),
        grid_spec=pltpu.PrefetchScalarGridSpec(
            num_scalar_prefetch=1,
            grid=(n_tiles,),
            in_specs=[slots, slots, slots_next, slots_next,
                      pl.BlockSpec((tt, D_MODEL), row),
                      pl.BlockSpec((tt, TOP_K), row),
                      pl.BlockSpec((D_MODEL, EXPERT_FF), const),
                      pl.BlockSpec((D_MODEL, EXPERT_FF), const),
                      pl.BlockSpec((EXPERT_FF, D_MODEL), const),
                      pl.BlockSpec((1, D_MODEL), const),
                      pl.BlockSpec((1, D_MODEL), const),
                      pl.BlockSpec(memory_space=pl.ANY)],
            out_specs=out_specs,
            scratch_shapes=[pltpu.VMEM((2, TOP_K, tt * PACK_ROWS, LANES), jnp.uint32),
                            pltpu.VMEM((tt, D_MODEL), F32), pltpu.SemaphoreType.DMA((2,))]),
        out_shape=out_shape,
        compiler_params=_params("arbitrary"),
        name="moe_combine",
    )(pstarts, eidx_tiles, rank_tiles, eidx_tiles, rank_tiles, xt, gate_tok, sw_gate.astype(BF16),
      sw_up.astype(BF16), sw_down.astype(BF16), g, b, ys)


def _moe_layer(xt, layer, w_router, bias, w_gate, w_up, w_down, sw_gate, sw_up, sw_down, g, b, *, split):
    eidx, gate, rank, counts, xq = _moe_route(xt, w_router, bias)

    counts = counts[:, 0]
    pcounts = (counts + MOE_BLOCK - 1) // MOE_BLOCK * MOE_BLOCK
    pends = jnp.cumsum(pcounts)
    pstarts = (pends - pcounts).astype(jnp.int32)
    n_used = (pends[-1] // MOE_BLOCK).astype(jnp.int32)
    blk = jnp.minimum(jnp.arange(MOE_N_BLOCKS, dtype=jnp.int32), n_used - 1) * MOE_BLOCK
    block_e = jnp.sum((pends[None, :] <= blk[:, None]).astype(jnp.int32), axis=1)
    block_e = jnp.minimum(block_e, N_EXPERTS - 1)
    after = pends[block_e] // MOE_BLOCK
    next_e = jnp.where(after < n_used, block_e[jnp.minimum(after, MOE_N_BLOCKS - 1)], -1).astype(jnp.int32)

    eidx_tok, rank_tok = eidx.T, rank.T
    xs = _moe_dispatch(xq, pstarts, eidx_tok, rank_tok)
    ys = _moe_experts(xs, block_e, next_e, n_used[None], layer, w_gate, w_up, w_down)
    return _moe_combine(xt, ys, pstarts, eidx_tok, rank_tok, gate.T, sw_gate, sw_up, sw_down, g, b, split=split)


def kernel(x_prompt, x_sample, state_pool, cache_swa_k, cache_swa_v, state_mlstm_c, state_mlstm_n, state_mlstm_m, pool_w, pool_scale, swa_w_qkv, swa_w_o, swa_sinks, mlstm_w_in, mlstm_b_gates, mlstm_norm_g, mlstm_w_out, ln_g, ln_b, moe_w_router, moe_router_bias, moe_w_gate, moe_w_up, moe_w_down, moe_shared_w_gate, moe_shared_w_up, moe_shared_w_down):
    d = D_MODEL
    xt = None
    pool_p, pool_s = [], []
    swk_p, swv_p, swk_s, swv_s = [], [], [], []
    mc_p, mn_p, mm_p, mc_s, mn_s, mm_s = [], [], [], [], [], []
    for i in range(DEPTH):
        kind, slot = i % N_MIXERS, i // N_MIXERS
        g0, b0 = ln_g[i, 0][None], ln_b[i, 0][None]
        if kind == 0:
            if i == 0:
                xp, xs = x_prompt.reshape(N_PROMPT_TOK, d), x_sample
            else:
                xp, xs = xt, xt[N_PROMPT_TOK:].reshape(DEC_BATCH, DEC_SEQ, d)
            xt, sp, ss = _pool_layer(xp, xs, state_pool[slot], pool_w[slot], pool_scale[slot][None], g0, b0)
            pool_p.append(sp)
            pool_s.append(ss)
        elif kind == 1:
            xt, kp, vp, ks, vs = _swa_layer(xt, cache_swa_k[slot], cache_swa_v[slot], swa_w_qkv[slot],
                                            swa_w_o[slot], swa_sinks[slot], g0, b0)
            swk_p.append(kp)
            swv_p.append(vp)
            swk_s.append(ks)
            swv_s.append(vs)
        else:
            xt, cp, np_, mp, cs, ns, ms = _mlstm_layer(
                xt, state_mlstm_c[slot], state_mlstm_n[slot], state_mlstm_m[slot], mlstm_w_in[slot],
                mlstm_b_gates[slot], mlstm_norm_g[slot][None], mlstm_w_out[slot], g0, b0)
            mc_p.append(cp)
            mn_p.append(np_)
            mm_p.append(mp)
            mc_s.append(cs)
            mn_s.append(ns)
            mm_s.append(ms)
        xt = _moe_layer(xt, i, moe_w_router[i], moe_router_bias[i], moe_w_gate, moe_w_up, moe_w_down,
                        moe_shared_w_gate[i], moe_shared_w_up[i], moe_shared_w_down[i],
                        ln_g[i, 1][None], ln_b[i, 1][None], split=(i == DEPTH - 1))
        if i < DEPTH - 1:
            xt = xt[0]
    y_p = xt[0].reshape(BATCH, SEQ, d)
    y_s = xt[1].reshape(DEC_BATCH, DEC_SEQ, d)
    return (y_p, y_s, jnp.stack(pool_p), jnp.stack(pool_s), jnp.stack(swk_p), jnp.stack(swv_p),
            jnp.stack(swk_s), jnp.stack(swv_s), jnp.stack(mc_p), jnp.stack(mn_p), jnp.stack(mm_p),
            jnp.stack(mc_s), jnp.stack(mn_s), jnp.stack(mm_s))
```

```python
import functools
import math

import jax
import jax.numpy as jnp
from jax import lax
from jax.experimental import pallas as pl
from jax.experimental.pallas import tpu as pltpu

F32 = jnp.float32
BF16 = jnp.bfloat16

D_MODEL = 2048
BATCH = 2
SEQ = 4096
DEPTH = 4
DEC_BATCH = 32
DEC_SEQ = 4
PAST_LEN = 16384
N_PROMPT_TOK = BATCH * SEQ
N_SAMPLE_TOK = DEC_BATCH * DEC_SEQ
N_TOK = N_PROMPT_TOK + N_SAMPLE_TOK

N_MIXERS = 3
DN_ALPHA = (2.0 * DEPTH) ** 0.25
LN_EPS = 1e-5

POOL_WINDOWS = (2, 4, 8, 16)
POOL_GROUP_DIM = D_MODEL // len(POOL_WINDOWS)
POOL_STATE = max(POOL_WINDOWS) - 1
POOL_HALO = POOL_STATE + 1

SWA_WINDOW = 128
SWA_HEAD_DIM = 64
SWA_HEADS = D_MODEL // SWA_HEAD_DIM
SWA_KV_HEADS = SWA_HEADS // 8
SWA_GROUP = SWA_HEADS // SWA_KV_HEADS
SWA_KV_DIM = SWA_KV_HEADS * SWA_HEAD_DIM

MLSTM_HEADS = 8
MLSTM_HEAD_DIM = D_MODEL // MLSTM_HEADS
MLSTM_CHUNK = 64
MLSTM_SAMPLE_CHUNK = 8

N_EXPERTS = 64
TOP_K = 8
N_EXPERT_GROUPS = 8
TOPK_GROUPS = 4
EXPERT_FF = D_MODEL // 4
ROUTED_SCALE = 2.5

VMEM_LIMIT_BYTES = 56 * 1024 * 1024

ROW_TILE = 320
ROUTER_TILE = 640
POOL_TILE = 512
MOE_BLOCK = 512
MOE_N_BLOCKS = N_TOK * TOP_K // MOE_BLOCK + N_EXPERTS
MOE_ROWS = MOE_N_BLOCKS * MOE_BLOCK
DISPATCH_TILE = 320
COMBINE_TILE = 128
LANES = 128
PACK_ROWS = D_MODEL // (2 * LANES)

_NT = (((1,), (1,)), ((), ()))
_TN = (((0,), (0,)), ((), ()))


def _params(*semantics):
    return pltpu.CompilerParams(dimension_semantics=semantics, vmem_limit_bytes=VMEM_LIMIT_BYTES)


def _layer_norm(z, g, b):
    mu = jnp.mean(z, axis=-1, keepdims=True)
    zc = z - mu
    var = jnp.mean(zc * zc, axis=-1, keepdims=True)
    return zc * lax.rsqrt(var + LN_EPS) * g + b


def _silu(x):
    return x * jax.nn.sigmoid(x)


def _matmul_kernel(x_ref, w_ref, o_ref):
    o_ref[...] = jnp.dot(x_ref[...].astype(BF16), w_ref[...], preferred_element_type=F32)


def _matmul(x, w, *, tm, tn):
    m, k = x.shape
    n = w.shape[1]
    return pl.pallas_call(
        _matmul_kernel,
        grid=(n // tn, m // tm),
        in_specs=[pl.BlockSpec((tm, k), lambda j, i: (i, 0)),
                  pl.BlockSpec((k, tn), lambda j, i: (0, j))],
        out_specs=pl.BlockSpec((tm, tn), lambda j, i: (i, j)),
        out_shape=jax.ShapeDtypeStruct((m, n), F32),
        compiler_params=_params("parallel", "parallel"),
        name="matmul",
    )(x, w)


def _matmul_ln_kernel(a_ref, w_ref, res_ref, g_ref, b_ref, o_ref):
    y = jnp.dot(a_ref[...].astype(BF16), w_ref[...], preferred_element_type=F32)
    o_ref[...] = _layer_norm(DN_ALPHA * res_ref[...] + y, g_ref[...], b_ref[...])


def _matmul_ln(a, w, res, g, b):
    m, k = a.shape
    row = lambda i: (i, 0)
    const = lambda i: (0, 0)
    return pl.pallas_call(
        _matmul_ln_kernel,
        grid=(m // ROW_TILE,),
        in_specs=[pl.BlockSpec((ROW_TILE, k), row),
                  pl.BlockSpec((k, D_MODEL), const),
                  pl.BlockSpec((ROW_TILE, D_MODEL), row),
                  pl.BlockSpec((1, D_MODEL), const),
                  pl.BlockSpec((1, D_MODEL), const)],
        out_specs=pl.BlockSpec((ROW_TILE, D_MODEL), row),
        out_shape=jax.ShapeDtypeStruct((m, D_MODEL), F32),
        compiler_params=_params("parallel"),
        name="matmul_ln",
    )(a, w, res, g, b)


def _pool_core(zbuf, w_ref, scale_ref, g_ref, b_ref, o_ref, *, tt, n_before):
    avail = n_before + lax.broadcasted_iota(jnp.int32, (tt, 1), 0) + 1
    for grp, win in enumerate(POOL_WINDOWS):
        c0, c1 = grp * POOL_GROUP_DIM, (grp + 1) * POOL_GROUP_DIM
        xg = zbuf[POOL_HALO:POOL_HALO + tt, c0:c1]
        total = xg
        for back in range(1, win):
            total = total + zbuf[POOL_HALO - back:POOL_HALO - back + tt, c0:c1]
        count = jnp.minimum(win, avail).astype(F32)
        diff = total / count - xg
        y = jnp.dot(diff.astype(BF16), w_ref[grp], preferred_element_type=F32) * scale_ref[:, c0:c1]
        o_ref[:, c0:c1] = DN_ALPHA * xg + y
    o_ref[...] = _layer_norm(o_ref[...], g_ref[...], b_ref[...])


def _pool_prompt_kernel(x_ref, halo_ref, w_ref, scale_ref, g_ref, b_ref, o_ref, zbuf, *, tt):
    i = pl.program_id(1)
    zbuf[0:POOL_HALO, :] = jnp.where(i == 0, 0.0, halo_ref[...])
    zbuf[POOL_HALO:POOL_HALO + tt, :] = x_ref[...]
    _pool_core(zbuf, w_ref, scale_ref, g_ref, b_ref, o_ref, tt=tt, n_before=i * tt)


def _pool_sample_kernel(z_ref, w_ref, scale_ref, g_ref, b_ref, o_ref, *, tt):
    _pool_core(z_ref.at[0], w_ref, scale_ref, g_ref, b_ref, o_ref.at[0], tt=tt, n_before=PAST_LEN)


def _pool_layer(xp, xs, state, w, scale, g, b):
    tt = POOL_TILE
    tiles = SEQ // tt
    halo_per_tile = tt // POOL_HALO
    const2 = lambda bb, i: (0, 0)
    w_bf = w.astype(BF16)
    out = pl.pallas_call(
        functools.partial(_pool_prompt_kernel, tt=tt),
        grid=(BATCH, tiles),
        in_specs=[pl.BlockSpec((tt, D_MODEL), lambda bb, i: (bb * tiles + i, 0)),
                  pl.BlockSpec((POOL_HALO, D_MODEL),
                               lambda bb, i: (jnp.maximum((bb * tiles + i) * halo_per_tile - 1, 0), 0)),
                  pl.BlockSpec(w_bf.shape, lambda bb, i: (0, 0, 0)),
                  pl.BlockSpec((1, D_MODEL), const2),
                  pl.BlockSpec((1, D_MODEL), const2),
                  pl.BlockSpec((1, D_MODEL), const2)],
        out_specs=pl.BlockSpec((tt, D_MODEL), lambda bb, i: (bb * tiles + i, 0)),
        out_shape=jax.ShapeDtypeStruct((N_TOK, D_MODEL), F32),
        scratch_shapes=[pltpu.VMEM((POOL_HALO + tt, D_MODEL), F32)],
        compiler_params=_params("parallel", "arbitrary"),
        name="pool_prompt",
    )(xp, xp, w_bf, scale, g, b)

    ts = 16
    zs = jnp.concatenate([jnp.zeros((DEC_BATCH, 1, D_MODEL), F32), state, xs,
                          jnp.zeros((DEC_BATCH, ts - DEC_SEQ, D_MODEL), F32)], axis=1)
    const1 = lambda bb: (0, 0)
    out_s = pl.pallas_call(
        functools.partial(_pool_sample_kernel, tt=ts),
        grid=(DEC_BATCH,),
        in_specs=[pl.BlockSpec((1, POOL_HALO + ts, D_MODEL), lambda bb: (bb, 0, 0)),
                  pl.BlockSpec(w_bf.shape, lambda bb: (0, 0, 0)),
                  pl.BlockSpec((1, D_MODEL), const1),
                  pl.BlockSpec((1, D_MODEL), const1),
                  pl.BlockSpec((1, D_MODEL), const1)],
        out_specs=pl.BlockSpec((1, ts, D_MODEL), lambda bb: (bb, 0, 0)),
        out_shape=jax.ShapeDtypeStruct((DEC_BATCH, ts, D_MODEL), F32),
        compiler_params=_params("parallel"),
        name="pool_sample",
    )(zs, w_bf, scale, g, b)
    out = lax.dynamic_update_slice(out, out_s[:, :DEC_SEQ].reshape(N_SAMPLE_TOK, D_MODEL), (N_PROMPT_TOK, 0))

    new_p = jnp.stack([xp[(bb + 1) * SEQ - POOL_STATE:(bb + 1) * SEQ] for bb in range(BATCH)])
    new_s = jnp.concatenate([state, xs], axis=1)[:, DEC_SEQ:]
    return out, new_p, new_s


def _alibi_slope(h):
    return 2.0 ** (-8.0 * (h + 1.0) / SWA_HEADS)


def _attn_core(q_ref, k_all, v_all, sink_ref, o_ref, *, rows, first_block):
    w, dh = SWA_WINDOW, SWA_HEAD_DIM
    qi = lax.broadcasted_iota(jnp.int32, (rows, 2 * w), 0)
    sj = lax.broadcasted_iota(jnp.int32, (rows, 2 * w), 1)
    dist = (w + qi) - sj
    valid = (dist >= 0) & (dist <= w)
    if first_block is not None:
        valid = valid & ((sj >= w) | jnp.logical_not(first_block))
    masked_dist = jnp.where(valid, dist.astype(F32), jnp.inf)
    for kv in range(SWA_KV_HEADS):
        c0, c1 = kv * dh, (kv + 1) * dh
        heads = range(kv * SWA_GROUP, (kv + 1) * SWA_GROUP)
        q = jnp.concatenate([q_ref[:, h * dh:(h + 1) * dh] for h in heads], axis=0)
        q = (q * (dh ** -0.5)).astype(BF16)
        s_all = lax.dot_general(q, k_all[:, c0:c1], _NT, preferred_element_type=F32)
        probs, dens = [], []
        for j, h in enumerate(heads):
            sink = sink_ref[h]
            s = s_all[j * rows:(j + 1) * rows] - _alibi_slope(h) * masked_dist
            m = jnp.maximum(jnp.max(s, axis=1, keepdims=True), sink)
            e = jnp.exp(s - m)
            dens.append(jnp.sum(e, axis=1, keepdims=True) + jnp.exp(sink - m))
            probs.append(e.astype(BF16))
        o_all = jnp.dot(jnp.concatenate(probs, axis=0), v_all[:, c0:c1], preferred_element_type=F32)
        for j, h in enumerate(heads):
            o_ref[:, h * dh:(h + 1) * dh] = o_all[j * rows:(j + 1) * rows] / dens[j]


def _attn_prompt_kernel(sink_ref, q_ref, kp_ref, kc_ref, vp_ref, vc_ref, o_ref):
    n = pl.program_id(1)
    k_all = jnp.concatenate([kp_ref[...], kc_ref[...]], axis=0).astype(BF16)
    v_all = jnp.concatenate([vp_ref[...], vc_ref[...]], axis=0).astype(BF16)
    _attn_core(q_ref, k_all, v_all, sink_ref, o_ref, rows=SWA_WINDOW, first_block=(n == 0))


def _attn_sample_kernel(sink_ref, q_ref, k_ref, v_ref, o_ref, *, rows):
    _attn_core(q_ref.at[0], k_ref[0].astype(BF16), v_ref[0].astype(BF16), sink_ref, o_ref.at[0],
               rows=rows, first_block=None)


def _swa_layer(xt, cache_k, cache_v, w_qkv, w_o, sinks, g, b):
    w = SWA_WINDOW
    qkv = _matmul(xt, w_qkv.astype(BF16), tm=640, tn=1280)
    nb = SEQ // w
    k_col = D_MODEL // SWA_KV_DIM
    cur = lambda bb, n, *_: (bb * nb + n, k_col)
    prev = lambda bb, n, *_: (jnp.maximum(bb * nb + n - 1, 0), k_col)
    cur_v = lambda bb, n, *_: (bb * nb + n, k_col + 1)
    prev_v = lambda bb, n, *_: (jnp.maximum(bb * nb + n - 1, 0), k_col + 1)
    o = pl.pallas_call(
        _attn_prompt_kernel,
        grid_spec=pltpu.PrefetchScalarGridSpec(
            num_scalar_prefetch=1,
            grid=(BATCH, nb),
            in_specs=[pl.BlockSpec((w, D_MODEL), lambda bb, n, *_: (bb * nb + n, 0)),
                      pl.BlockSpec((w, SWA_KV_DIM), prev),
                      pl.BlockSpec((w, SWA_KV_DIM), cur),
                      pl.BlockSpec((w, SWA_KV_DIM), prev_v),
                      pl.BlockSpec((w, SWA_KV_DIM), cur_v)],
            out_specs=pl.BlockSpec((w, D_MODEL), lambda bb, n, *_: (bb * nb + n, 0))),
        out_shape=jax.ShapeDtypeStruct((N_TOK, D_MODEL), F32),
        compiler_params=_params("parallel", "arbitrary"),
        name="attn_prompt",
    )(sinks, qkv, qkv, qkv, qkv, qkv)

    rows = 16
    qkv_s = qkv[N_PROMPT_TOK:].reshape(DEC_BATCH, DEC_SEQ, -1)
    q_s = jnp.pad(qkv_s[..., :D_MODEL], ((0, 0), (0, rows - DEC_SEQ), (0, 0)))
    k_new = qkv_s[..., D_MODEL:D_MODEL + SWA_KV_DIM]
    v_new = qkv_s[..., D_MODEL + SWA_KV_DIM:]
    kz = jnp.concatenate([cache_k.reshape(DEC_BATCH, w, SWA_KV_DIM), k_new], axis=1)
    vz = jnp.concatenate([cache_v.reshape(DEC_BATCH, w, SWA_KV_DIM), v_new], axis=1)
    pad_keys = ((0, 0), (0, w - DEC_SEQ), (0, 0))
    o_s = pl.pallas_call(
        functools.partial(_attn_sample_kernel, rows=rows),
        grid_spec=pltpu.PrefetchScalarGridSpec(
            num_scalar_prefetch=1,
            grid=(DEC_BATCH,),
            in_specs=[pl.BlockSpec((1, rows, D_MODEL), lambda bb, *_: (bb, 0, 0)),
                      pl.BlockSpec((1, 2 * w, SWA_KV_DIM), lambda bb, *_: (bb, 0, 0)),
                      pl.BlockSpec((1, 2 * w, SWA_KV_DIM), lambda bb, *_: (bb, 0, 0))],
            out_specs=pl.BlockSpec((1, rows, D_MODEL), lambda bb, *_: (bb, 0, 0))),
        out_shape=jax.ShapeDtypeStruct((DEC_BATCH, rows, D_MODEL), F32),
        compiler_params=_params("parallel"),
        name="attn_sample",
    )(sinks, q_s, jnp.pad(kz, pad_keys), jnp.pad(vz, pad_keys))
    o = lax.dynamic_update_slice(o, o_s[:, :DEC_SEQ].reshape(N_SAMPLE_TOK, D_MODEL), (N_PROMPT_TOK, 0))

    out = _matmul_ln(o, w_o.astype(BF16), xt, g, b)

    kv_shape = (SWA_WINDOW, SWA_KV_HEADS, SWA_HEAD_DIM)
    kv_p = jnp.stack([qkv[(bb + 1) * SEQ - w:(bb + 1) * SEQ, D_MODEL:] for bb in range(BATCH)])
    new_k_p = kv_p[..., :SWA_KV_DIM].reshape((BATCH,) + kv_shape)
    new_v_p = kv_p[..., SWA_KV_DIM:].reshape((BATCH,) + kv_shape)
    new_k_s = kz[:, DEC_SEQ:].reshape((DEC_BATCH,) + kv_shape)
    new_v_s = vz[:, DEC_SEQ:].reshape((DEC_BATCH,) + kv_shape)
    return out, new_k_p, new_v_p, new_k_s, new_v_s


def _log_sigmoid(x):
    return jnp.minimum(x, 0.0) - jnp.log(1.0 + jnp.exp(-jnp.abs(x)))


def _mlstm_kernel(bias_ref, q_ref, k_ref, v_ref, og_ref, gates_ref, c0_ref, n0_ref, m0_ref, ng_ref,
                  h_ref, c_ref, n_ref, m_ref, c_scr, n_scr, m_scr, *, chunk, n_valid):
    step = pl.program_id(1)

    @pl.when(step == 0)
    def _():
        c_scr[...] = c0_ref[0]
        n_scr[...] = n0_ref[0]
        m_scr[...] = m0_ref[0]

    for head in range(MLSTM_HEADS):
        _mlstm_head(head, step, bias_ref, q_ref, k_ref, v_ref, og_ref, gates_ref, ng_ref, h_ref, c_scr, n_scr, m_scr,
                    chunk=chunk, n_valid=n_valid)

    @pl.when(step == pl.num_programs(1) - 1)
    def _():
        c_ref[0] = c_scr[...]
        n_ref[0] = n_scr[...]
        m_ref[0] = m_scr[...]


def _mlstm_head(head, step, bias_ref, q_ref, k_ref, v_ref, og_ref, gates_ref, ng_ref, h_ref, c_scr, n_scr, m_scr,
                *, chunk, n_valid):
    ln = chunk
    cols = slice(head * MLSTM_HEAD_DIM, (head + 1) * MLSTM_HEAD_DIM)
    ig = gates_ref[0, head, pl.ds(step, 1), :] + bias_ref[head]
    lf = _log_sigmoid(gates_ref[0, MLSTM_HEADS + head, pl.ds(step, 1), :] + bias_ref[MLSTM_HEADS + head])
    if n_valid < ln:
        col = lax.broadcasted_iota(jnp.int32, (1, ln), 1)
        ig = jnp.where(col < n_valid, ig, -1e30)
        lf = jnp.where(col < n_valid, lf, 0.0)

    ri = lax.broadcasted_iota(jnp.int32, (ln, ln), 0)
    ci = lax.broadcasted_iota(jnp.int32, (ln, ln), 1)
    eye = ri == ci
    causal = ci <= ri

    def to_col(row):
        return jnp.sum(jnp.where(eye, row, 0.0), axis=1, keepdims=True)

    f_col = to_col(lf)
    b_col = jnp.sum(jnp.where(causal, lf, 0.0), axis=1, keepdims=True)
    b_row = jnp.sum(jnp.where(ri <= ci, f_col, 0.0), axis=0, keepdims=True)
    b_last = jnp.sum(lf, axis=1, keepdims=True)
    m_prev = m_scr[head, :, 0:1]
    a_col = b_col + m_prev
    dmat = jnp.where(causal, b_col - b_row + ig, -jnp.inf)
    mt = jnp.maximum(a_col, jnp.max(dmat, axis=1, keepdims=True))
    w_inter = jnp.exp(a_col - mt)

    q = q_ref[:, cols]
    k = k_ref[:, cols] * (MLSTM_HEAD_DIM ** -0.5)
    v = v_ref[:, cols]
    qb = q.astype(BF16)
    kb = k.astype(BF16)
    c_prev = c_scr[head]
    n_prev = n_scr[head]
    scores = lax.dot_general(qb, kb, _NT, preferred_element_type=F32)
    w_intra = jnp.exp(dmat - mt) * scores
    num = (w_inter * lax.dot_general(qb, c_prev.astype(BF16), _NT, preferred_element_type=F32)
           + jnp.dot(w_intra.astype(BF16), v.astype(BF16), preferred_element_type=F32))
    den = (w_inter * jnp.sum(q * n_prev, axis=1, keepdims=True)
           + jnp.sum(w_intra, axis=1, keepdims=True))
    h = num / jnp.maximum(jnp.abs(den), jnp.exp(-mt))

    g_row = b_last - b_row + ig
    m_new = jnp.maximum(b_last + m_prev, jnp.max(g_row, axis=1, keepdims=True))
    decay = jnp.exp(b_last + m_prev - m_new)
    wg_col = to_col(jnp.exp(g_row - m_new))
    c_new = decay * c_prev + lax.dot_general((v * wg_col).astype(BF16), kb, _TN, preferred_element_type=F32)
    n_new = decay * n_prev + jnp.sum(wg_col * k, axis=0, keepdims=True)
    c_scr[head] = c_new
    n_scr[head] = n_new
    m_scr[head] = jnp.broadcast_to(m_new, (1, LANES))

    mu = jnp.mean(h, axis=1, keepdims=True)
    hc = h - mu
    var = jnp.mean(hc * hc, axis=1, keepdims=True)
    h_ref[:, cols] = hc * lax.rsqrt(var + LN_EPS) * ng_ref[:, cols] * jax.nn.sigmoid(og_ref[:, cols])


def _mlstm_scan(proj, gates, b_gates, norm_g, c0, n0, m0, *, batch, n_chunks, chunk, n_valid, out_rows):
    d, dh, nh = D_MODEL, MLSTM_HEAD_DIM, MLSTM_HEADS
    rows = lambda off: (lambda bb, cc, *_: (bb * n_chunks + cc, off))
    per_seq = lambda bb, cc, *_: (bb, 0, 0, 0)
    return pl.pallas_call(
        functools.partial(_mlstm_kernel, chunk=chunk, n_valid=n_valid),
        grid_spec=pltpu.PrefetchScalarGridSpec(
            num_scalar_prefetch=1,
            grid=(batch, n_chunks),
            in_specs=[pl.BlockSpec((chunk, d), rows(0)),
                      pl.BlockSpec((chunk, d), rows(1)),
                      pl.BlockSpec((chunk, d), rows(2)),
                      pl.BlockSpec((chunk, d), rows(3)),
                      pl.BlockSpec((1, 2 * nh, n_chunks, chunk), per_seq),
                      pl.BlockSpec((1, nh, dh, dh), per_seq),
                      pl.BlockSpec((1, nh, 1, dh), per_seq),
                      pl.BlockSpec((1, nh, 1, LANES), per_seq),
                      pl.BlockSpec((1, d), lambda bb, cc, *_: (0, 0))],
            out_specs=[pl.BlockSpec((chunk, d), lambda bb, cc, *_: (bb * n_chunks + cc, 0)),
                       pl.BlockSpec((1, nh, dh, dh), per_seq),
                       pl.BlockSpec((1, nh, 1, dh), per_seq),
                       pl.BlockSpec((1, nh, 1, LANES), per_seq)],
            scratch_shapes=[pltpu.VMEM((nh, dh, dh), F32), pltpu.VMEM((nh, 1, dh), F32),
                            pltpu.VMEM((nh, 1, LANES), F32)]),
        out_shape=[jax.ShapeDtypeStruct((out_rows, d), F32),
                   jax.ShapeDtypeStruct((batch, nh, dh, dh), F32),
                   jax.ShapeDtypeStruct((batch, nh, 1, dh), F32),
                   jax.ShapeDtypeStruct((batch, nh, 1, LANES), F32)],
        compiler_params=_params("parallel", "arbitrary"),
        name="mlstm_scan",
    )(b_gates, proj, proj, proj, proj, gates, c0, n0[:, :, None, :],
      jnp.broadcast_to(m0[:, :, None, None], (batch, nh, 1, LANES)), norm_g)


def _mlstm_layer(xt, c0_s, n0_s, m0_s, w_in, b_gates, norm_g, w_out, g, b):
    d, nh, dh = D_MODEL, MLSTM_HEADS, MLSTM_HEAD_DIM
    proj = _matmul(xt, w_in[:, :4 * d].astype(BF16), tm=640, tn=1024)
    w_gates = jnp.pad(w_in[:, 4 * d:], ((0, 0), (0, 128 - 2 * nh)))
    gate_pre = _matmul(xt, w_gates.astype(BF16), tm=640, tn=128)[:, :2 * nh]

    nc = SEQ // MLSTM_CHUNK
    gates_p = gate_pre[:N_PROMPT_TOK].reshape(BATCH, nc, MLSTM_CHUNK, 2 * nh).transpose(0, 3, 1, 2)
    zeros = lambda *s: jnp.zeros(s, F32)
    hn, c_p, n_p, m_p = _mlstm_scan(
        proj, gates_p, b_gates, norm_g, zeros(BATCH, nh, dh, dh), zeros(BATCH, nh, dh), zeros(BATCH, nh),
        batch=BATCH, n_chunks=nc, chunk=MLSTM_CHUNK, n_valid=MLSTM_CHUNK, out_rows=N_TOK)

    ls = MLSTM_SAMPLE_CHUNK
    pad_t = ((0, 0), (0, ls - DEC_SEQ), (0, 0))
    proj_s = jnp.pad(proj[N_PROMPT_TOK:].reshape(DEC_BATCH, DEC_SEQ, 4 * d), pad_t).reshape(DEC_BATCH * ls, 4 * d)
    gates_s = jnp.pad(gate_pre[N_PROMPT_TOK:].reshape(DEC_BATCH, DEC_SEQ, 2 * nh), pad_t)
    gates_s = gates_s.transpose(0, 2, 1)[:, :, None, :]
    hn_s, c_s, n_s, m_s = _mlstm_scan(
        proj_s, gates_s, b_gates, norm_g, c0_s, n0_s, m0_s,
        batch=DEC_BATCH, n_chunks=1, chunk=ls, n_valid=DEC_SEQ, out_rows=DEC_BATCH * ls)
    hn_s = hn_s.reshape(DEC_BATCH, ls, d)[:, :DEC_SEQ].reshape(N_SAMPLE_TOK, d)
    hn = lax.dynamic_update_slice(hn, hn_s, (N_PROMPT_TOK, 0))

    out = _matmul_ln(hn, w_out.astype(BF16), xt, g, b)
    return (out, c_p, n_p[:, :, 0], m_p[:, :, 0, 0], c_s, n_s[:, :, 0], m_s[:, :, 0, 0])


def _router_kernel(x_ref, wt_ref, bias_ref, tri_ref, eidx_ref, gate_ref, rank_ref, count_ref, xq_ref, count_scr,
                   *, tt):
    neg = -jnp.inf
    x = x_ref[...]
    logits = lax.dot_general(wt_ref[...].astype(BF16), x.astype(BF16), _NT, preferred_element_type=F32)
    s = jax.nn.sigmoid(logits)
    sb = s + bias_ref[...]
    per_group = N_EXPERTS // N_EXPERT_GROUPS

    sb3 = sb.reshape(N_EXPERT_GROUPS, per_group, tt)
    i3 = lax.broadcasted_iota(jnp.int32, sb3.shape, 1)
    m1 = jnp.max(sb3, axis=1, keepdims=True)
    first = jnp.min(jnp.where(sb3 == m1, i3, per_group), axis=1, keepdims=True)
    m2 = jnp.max(jnp.where(i3 == first, neg, sb3), axis=1, keepdims=True)
    gscore = (m1 + m2).reshape(N_EXPERT_GROUPS, tt)

    gi = lax.broadcasted_iota(jnp.int32, gscore.shape, 0)
    gsel = jnp.zeros(gscore.shape, jnp.bool_)
    cur = gscore
    for _ in range(TOPK_GROUPS):
        mx = jnp.max(cur, axis=0, keepdims=True)
        pick = gi == jnp.min(jnp.where(cur == mx, gi, N_EXPERT_GROUPS), axis=0, keepdims=True)
        gsel = gsel | pick
        cur = jnp.where(pick, neg, cur)
    emask = jnp.broadcast_to(gsel.reshape(N_EXPERT_GROUPS, 1, tt), sb3.shape).reshape(N_EXPERTS, tt)

    ei = lax.broadcasted_iota(jnp.int32, sb.shape, 0)
    sel = jnp.zeros(sb.shape, jnp.bool_)
    cur = jnp.where(emask, sb, neg)
    picks = []
    for _ in range(TOP_K):
        mx = jnp.max(cur, axis=0, keepdims=True)
        idx = jnp.min(jnp.where(cur == mx, ei, N_EXPERTS), axis=0, keepdims=True)
        pick = ei == idx
        picks.append((idx, pick))
        sel = sel | pick
        cur = jnp.where(pick, neg, cur)
    s_sel = jnp.where(sel, s, 0.0)
    gate = s_sel / jnp.sum(s_sel, axis=0, keepdims=True) * ROUTED_SCALE

    @pl.when(pl.program_id(0) == 0)
    def _():
        count_scr[...] = jnp.zeros_like(count_scr)

    sel_f = sel.astype(F32)
    incl = jnp.dot(sel_f.astype(BF16), tri_ref[...], preferred_element_type=F32)
    rank = count_scr[:, 0:1] + incl - sel_f
    count_scr[...] = count_scr[...] + jnp.sum(sel_f, axis=1, keepdims=True)
    count_ref[...] = count_scr[...].astype(jnp.int32)

    for kk, (idx, pick) in enumerate(picks):
        eidx_ref[kk:kk + 1, :] = idx
        gate_ref[kk:kk + 1, :] = jnp.sum(jnp.where(pick, gate, 0.0), axis=0, keepdims=True)
        rank_ref[kk:kk + 1, :] = jnp.sum(jnp.where(pick, rank, 0.0), axis=0, keepdims=True).astype(jnp.int32)

    for sl in range(PACK_ROWS):
        c0 = 2 * LANES * sl
        xq_ref[pl.ds(sl, tt, stride=PACK_ROWS), :] = _pack_bf16_pair(x[:, c0:c0 + LANES],
                                                                     x[:, c0 + LANES:c0 + 2 * LANES])


def _pack_bf16_pair(hi, lo):
    hb = lax.bitcast_convert_type(hi.astype(BF16).astype(F32), jnp.uint32)
    lb = lax.bitcast_convert_type(lo.astype(BF16).astype(F32), jnp.uint32)
    return hb | (lb >> 16)


def _unpack_bf16_pair(word):
    hi = lax.bitcast_convert_type(word & jnp.uint32(0xFFFF0000), F32)
    lo = lax.bitcast_convert_type(word << 16, F32)
    return hi, lo


def _moe_route(xt, w_router, bias):
    tt = ROUTER_TILE
    col = lambda i: (0, i)
    const = lambda i: (0, 0)
    tri = (jnp.arange(tt)[:, None] <= jnp.arange(tt)[None, :]).astype(BF16)
    return pl.pallas_call(
        functools.partial(_router_kernel, tt=tt),
        grid=(N_TOK // tt,),
        in_specs=[pl.BlockSpec((tt, D_MODEL), lambda i: (i, 0)),
                  pl.BlockSpec((N_EXPERTS, D_MODEL), const),
                  pl.BlockSpec((N_EXPERTS, 1), const),
                  pl.BlockSpec((tt, tt), const)],
        out_specs=[pl.BlockSpec((TOP_K, tt), col), pl.BlockSpec((TOP_K, tt), col), pl.BlockSpec((TOP_K, tt), col),
                   pl.BlockSpec((N_EXPERTS, LANES), const),
                   pl.BlockSpec((tt * PACK_ROWS, LANES), lambda i: (i, 0))],
        out_shape=[jax.ShapeDtypeStruct((TOP_K, N_TOK), jnp.int32),
                   jax.ShapeDtypeStruct((TOP_K, N_TOK), F32),
                   jax.ShapeDtypeStruct((TOP_K, N_TOK), jnp.int32),
                   jax.ShapeDtypeStruct((N_EXPERTS, LANES), jnp.int32),
                   jax.ShapeDtypeStruct((N_TOK * PACK_ROWS, LANES), jnp.uint32)],
        scratch_shapes=[pltpu.VMEM((N_EXPERTS, LANES), F32)],
        compiler_params=_params("arbitrary"),
        name="moe_router",
    )(xt, w_router.T, bias[:, None], tri)


def _sorted_row(pstart_ref, eidx_ref, rank_ref, j):
    return pstart_ref[eidx_ref[0, 0, j]] + rank_ref[0, 0, j]


def _dispatch_kernel(pstart_ref, eidx_ref, rank_ref, xq_ref, xs_hbm, row_ref, sem, *, tt):
    def issue(t, carry):
        src = xq_ref.at[pl.ds(pl.multiple_of(t * PACK_ROWS, PACK_ROWS), PACK_ROWS)]
        for kk in range(TOP_K):
            row = _sorted_row(pstart_ref, eidx_ref, rank_ref, t * TOP_K + kk)
            row_ref[0, 0, t * TOP_K + kk] = row
            row = pl.multiple_of(row * PACK_ROWS, PACK_ROWS)
            pltpu.make_async_copy(src, xs_hbm.at[pl.ds(row, PACK_ROWS)], sem).start()
        return carry
    lax.fori_loop(0, tt, issue, 0)
    for _ in range(TOP_K):
        pltpu.make_async_copy(xq_ref, xs_hbm.at[pl.ds(0, tt * PACK_ROWS)], sem).wait()


def _moe_dispatch(xq, pstarts, eidx_tok, rank_tok):
    tt = DISPATCH_TILE
    n_tiles = N_TOK // tt
    slots = pl.BlockSpec((1, 1, tt * TOP_K), lambda i, *_: (i, 0, 0), memory_space=pltpu.SMEM)
    return pl.pallas_call(
        functools.partial(_dispatch_kernel, tt=tt),
        grid_spec=pltpu.PrefetchScalarGridSpec(
            num_scalar_prefetch=1,
            grid=(n_tiles,),
            in_specs=[slots, slots, pl.BlockSpec((tt * PACK_ROWS, LANES), lambda i, *_: (i, 0))],
            out_specs=[pl.BlockSpec(memory_space=pl.ANY), slots],
            scratch_shapes=[pltpu.SemaphoreType.DMA(())]),
        out_shape=[jax.ShapeDtypeStruct((MOE_ROWS * PACK_ROWS, LANES), jnp.uint32),
                   jax.ShapeDtypeStruct((n_tiles, 1, tt * TOP_K), jnp.int32)],
        compiler_params=_params("arbitrary"),
        name="moe_dispatch",
    )(pstarts, eidx_tok.reshape(n_tiles, 1, tt * TOP_K), rank_tok.reshape(n_tiles, 1, tt * TOP_K), xq)


def _experts_kernel(be_ref, next_ref, nused_ref, xs_ref, wg_hbm, wu_hbm, wd_hbm, y_ref,
                    wg_f, wu_f, wd_f, wgu_s, wd_s, sem, *, layer):
    i = pl.program_id(0)
    tm = MOE_BLOCK

    def weight_copies(e):
        return (pltpu.make_async_copy(wg_hbm.at[layer, e], wg_f, sem.at[0]),
                pltpu.make_async_copy(wu_hbm.at[layer, e], wu_f, sem.at[1]),
                pltpu.make_async_copy(wd_hbm.at[layer, e], wd_f, sem.at[2]))

    @pl.when(i == 0)
    def _():
        for cp in weight_copies(be_ref[0]):
            cp.start()

    @pl.when(i < nused_ref[0])
    def _():
        first_of_expert = (i == 0) | (be_ref[i] != be_ref[jnp.maximum(i - 1, 0)])

        @pl.when(first_of_expert)
        def _():
            for cp in weight_copies(be_ref[i]):
                cp.wait()
            wgu_s[:, :EXPERT_FF] = wg_f[...].astype(BF16)
            wgu_s[:, EXPERT_FF:] = wu_f[...].astype(BF16)
            wd_s[...] = wd_f[...].astype(BF16)

            @pl.when(next_ref[i] >= 0)
            def _():
                for cp in weight_copies(next_ref[i]):
                    cp.start()

        parts = []
        for sl in range(PACK_ROWS):
            parts.extend(_unpack_bf16_pair(xs_ref[pl.ds(sl, tm, stride=PACK_ROWS), :]))
        x = jnp.concatenate(parts, axis=1).astype(BF16)
        gate_up = jnp.dot(x, wgu_s[...], preferred_element_type=F32)
        hidden = _silu(gate_up[:, :EXPERT_FF]) * gate_up[:, EXPERT_FF:]
        y = jnp.dot(hidden.astype(BF16), wd_s[...], preferred_element_type=F32)
        for sl in range(PACK_ROWS):
            c0 = 2 * LANES * sl
            y_ref[pl.ds(sl, tm, stride=PACK_ROWS), :] = _pack_bf16_pair(y[:, c0:c0 + LANES],
                                                                        y[:, c0 + LANES:c0 + 2 * LANES])


def _moe_experts(xs, block_e, next_e, n_used, layer, w_gate, w_up, w_down):
    last = lambda i, be, nx, nu: (jnp.minimum(i, nu[0] - 1), 0)
    return pl.pallas_call(
        functools.partial(_experts_kernel, layer=layer),
        grid_spec=pltpu.PrefetchScalarGridSpec(
            num_scalar_prefetch=3,
            grid=(MOE_N_BLOCKS,),
            in_specs=[pl.BlockSpec((MOE_BLOCK * PACK_ROWS, LANES), last),
                      pl.BlockSpec(memory_space=pl.ANY),
                      pl.BlockSpec(memory_space=pl.ANY),
                      pl.BlockSpec(memory_space=pl.ANY)],
            out_specs=pl.BlockSpec((MOE_BLOCK * PACK_ROWS, LANES), last),
            scratch_shapes=[pltpu.VMEM((D_MODEL, EXPERT_FF), F32), pltpu.VMEM((D_MODEL, EXPERT_FF), F32),
                            pltpu.VMEM((EXPERT_FF, D_MODEL), F32),
                            pltpu.VMEM((D_MODEL, 2 * EXPERT_FF), BF16),
                            pltpu.VMEM((EXPERT_FF, D_MODEL), BF16),
                            pltpu.SemaphoreType.DMA((3,))]),
        out_shape=jax.ShapeDtypeStruct((MOE_ROWS * PACK_ROWS, LANES), jnp.uint32),
        compiler_params=_params("arbitrary"),
        name="moe_experts",
    )(block_e, next_e, n_used, xs, w_gate, w_up, w_down)


def _combine_kernel(row_ref, row_next_ref, x_ref, gate_ref, swg_ref, swu_ref, swd_ref, g_ref, b_ref, y_hbm, *rest,
                    tt, n_first):
    out_refs, (buf_a, buf_b, acc, sem) = rest[:-4], rest[-4:]
    i = pl.program_id(0)
    last = pl.num_programs(0) - 1

    def start_fetch(rows, into, into_sem, t):
        dst = pl.ds(t * PACK_ROWS, PACK_ROWS)
        for kk in range(TOP_K):
            src_row = pl.multiple_of(rows[0, 0, t * TOP_K + kk] * PACK_ROWS, PACK_ROWS)
            pltpu.make_async_copy(y_hbm.at[pl.ds(src_row, PACK_ROWS)], into.at[kk, dst], into_sem).start()

    def wait_fetch(into, into_sem):
        for kk in range(TOP_K):
            pltpu.make_async_copy(y_hbm.at[pl.ds(0, tt * PACK_ROWS)], into.at[kk], into_sem).wait()

    @pl.when(i == 0)
    def _():
        def body(t, carry):
            dst = pl.ds(pl.multiple_of(t * PACK_ROWS, PACK_ROWS), PACK_ROWS)
            for kk in range(TOP_K):
                src_row = pl.multiple_of(row_ref[0, 0, t * TOP_K + kk] * PACK_ROWS, PACK_ROWS)
                pltpu.make_async_copy(y_hbm.at[pl.ds(src_row, PACK_ROWS)], buf_a.at[kk, dst], sem.at[0]).start()
            return carry
        lax.fori_loop(0, tt, body, 0)

    def tile(cur, cur_sem, nxt, nxt_sem):
        per_phase = tt // (PACK_ROWS + 2)
        tokens = iter(range(tt))

        def issue(n):
            for _ in range(n):
                t = next(tokens, None)
                if t is not None:
                    start_fetch(row_next_ref, nxt, nxt_sem, t)

        x = x_ref[...]
        xb = x.astype(BF16)
        issue(per_phase)
        hidden = (_silu(jnp.dot(xb, swg_ref[...], preferred_element_type=F32))
                  * jnp.dot(xb, swu_ref[...], preferred_element_type=F32))
        issue(per_phase)
        shared = jnp.dot(hidden.astype(BF16), swd_ref[...], preferred_element_type=F32)
        wait_fetch(cur, cur_sem)
        gate = gate_ref[...]
        for sl in range(PACK_ROWS):
            issue(per_phase if sl < PACK_ROWS - 1 else tt)
            routed_hi = routed_lo = None
            for kk in range(TOP_K):
                hi, lo = _unpack_bf16_pair(cur[kk, pl.ds(sl, tt, stride=PACK_ROWS), :])
                g_k = gate[:, kk:kk + 1]
                routed_hi = g_k * hi if kk == 0 else routed_hi + g_k * hi
                routed_lo = g_k * lo if kk == 0 else routed_lo + g_k * lo
            for half, routed in enumerate((routed_hi, routed_lo)):
                cols = slice((2 * sl + half) * LANES, (2 * sl + half + 1) * LANES)
                acc[:, cols] = DN_ALPHA * x[:, cols] + (routed + shared[:, cols])
        res = _layer_norm(acc[...], g_ref[...], b_ref[...])
        if n_first is None:
            out_refs[0][...] = res
        else:
            @pl.when(i < n_first)
            def _():
                out_refs[0][...] = res

            @pl.when(i >= n_first)
            def _():
                out_refs[1][...] = res

        @pl.when(i == last)
        def _():
            wait_fetch(nxt, nxt_sem)

    @pl.when(i % 2 == 0)
    def _():
        tile(buf_a, sem.at[0], buf_b, sem.at[1])

    @pl.when(i % 2 == 1)
    def _():
        tile(buf_b, sem.at[1], buf_a, sem.at[0])


def _moe_combine(xt, ys, row_tiles, gate_tok, sw_gate, sw_up, sw_down, g, b, *, split):
    tt = COMBINE_TILE
    n_tiles = N_TOK // tt
    row = lambda i: (i, 0)
    const = lambda i: (0, 0)
    slots = pl.BlockSpec((1, 1, tt * TOP_K), lambda i: (i, 0, 0), memory_space=pltpu.SMEM)
    slots_next = pl.BlockSpec((1, 1, tt * TOP_K), lambda i: (jnp.minimum(i + 1, n_tiles - 1), 0, 0),
                              memory_space=pltpu.SMEM)
    row_tiles = row_tiles.reshape(n_tiles, 1, tt * TOP_K)
    if split:
        n_first = N_PROMPT_TOK // tt
        assert N_SAMPLE_TOK == tt
        out_specs = [pl.BlockSpec((tt, D_MODEL), lambda i: (jnp.minimum(i, n_first - 1), 0)),
                     pl.BlockSpec((tt, D_MODEL), const)]
        out_shape = [jax.ShapeDtypeStruct((N_PROMPT_TOK, D_MODEL), F32),
                     jax.ShapeDtypeStruct((N_SAMPLE_TOK, D_MODEL), F32)]
    else:
        n_first = None
        out_specs = [pl.BlockSpec((tt, D_MODEL), row)]
        out_shape = [jax.ShapeDtypeStruct((N_TOK, D_MODEL), F32)]
    return pl.pallas_call(
        functools.partial(_combine_kernel, tt=tt, n_first=n_first),
        grid=(n_tiles,),
        in_specs=[slots, slots_next,
                  pl.BlockSpec((tt, D_MODEL), row),
                  pl.BlockSpec((tt, TOP_K), row),
                  pl.BlockSpec((D_MODEL, EXPERT_FF), const),
                  pl.BlockSpec((D_MODEL, EXPERT_FF), const),
                  pl.BlockSpec((EXPERT_FF, D_MODEL), const),
                  pl.BlockSpec((1, D_MODEL), const),
                  pl.BlockSpec((1, D_MODEL), const),
                  pl.BlockSpec(memory_space=pl.ANY)],
        out_specs=out_specs,
        scratch_shapes=[pltpu.VMEM((TOP_K, tt * PACK_ROWS, LANES), jnp.uint32),
                        pltpu.VMEM((TOP_K, tt * PACK_ROWS, LANES), jnp.uint32),
                        pltpu.VMEM((tt, D_MODEL), F32), pltpu.SemaphoreType.DMA((2,))],
        out_shape=out_shape,
        compiler_params=_params("arbitrary"),
        name="moe_combine",
    )(row_tiles, row_tiles, xt, gate_tok, sw_gate.astype(BF16), sw_up.astype(BF16), sw_down.astype(BF16), g, b, ys)


def _moe_layer(xt, layer, w_router, bias, w_gate, w_up, w_down, sw_gate, sw_up, sw_down, g, b, *, split):
    eidx, gate, rank, counts, xq = _moe_route(xt, w_router, bias)

    counts = counts[:, 0]
    pcounts = (counts + MOE_BLOCK - 1) // MOE_BLOCK * MOE_BLOCK
    pends = jnp.cumsum(pcounts)
    pstarts = (pends - pcounts).astype(jnp.int32)
    n_used = (pends[-1] // MOE_BLOCK).astype(jnp.int32)
    blk = jnp.minimum(jnp.arange(MOE_N_BLOCKS, dtype=jnp.int32), n_used - 1) * MOE_BLOCK
    block_e = jnp.sum((pends[None, :] <= blk[:, None]).astype(jnp.int32), axis=1)
    block_e = jnp.minimum(block_e, N_EXPERTS - 1)
    after = pends[block_e] // MOE_BLOCK
    next_e = jnp.where(after < n_used, block_e[jnp.minimum(after, MOE_N_BLOCKS - 1)], -1).astype(jnp.int32)

    eidx_tok, rank_tok = eidx.T, rank.T
    xs, rows = _moe_dispatch(xq, pstarts, eidx_tok, rank_tok)
    ys = _moe_experts(xs, block_e, next_e, n_used[None], layer, w_gate, w_up, w_down)
    return _moe_combine(xt, ys, rows, gate.T, sw_gate, sw_up, sw_down, g, b, split=split)


def kernel(x_prompt, x_sample, state_pool, cache_swa_k, cache_swa_v, state_mlstm_c, state_mlstm_n, state_mlstm_m, pool_w, pool_scale, swa_w_qkv, swa_w_o, swa_sinks, mlstm_w_in, mlstm_b_gates, mlstm_norm_g, mlstm_w_out, ln_g, ln_b, moe_w_router, moe_router_bias, moe_w_gate, moe_w_up, moe_w_down, moe_shared_w_gate, moe_shared_w_up, moe_shared_w_down):
    d = D_MODEL
    xt = None
    pool_p, pool_s = [], []
    swk_p, swv_p, swk_s, swv_s = [], [], [], []
    mc_p, mn_p, mm_p, mc_s, mn_s, mm_s = [], [], [], [], [], []
    for i in range(DEPTH):
        kind, slot = i % N_MIXERS, i // N_MIXERS
        g0, b0 = ln_g[i, 0][None], ln_b[i, 0][None]
        if kind == 0:
            if i == 0:
                xp, xs = x_prompt.reshape(N_PROMPT_TOK, d), x_sample
            else:
                xp, xs = xt, xt[N_PROMPT_TOK:].reshape(DEC_BATCH, DEC_SEQ, d)
            xt, sp, ss = _pool_layer(xp, xs, state_pool[slot], pool_w[slot], pool_scale[slot][None], g0, b0)
            pool_p.append(sp)
            pool_s.append(ss)
        elif kind == 1:
            xt, kp, vp, ks, vs = _swa_layer(xt, cache_swa_k[slot], cache_swa_v[slot], swa_w_qkv[slot],
                                            swa_w_o[slot], swa_sinks[slot], g0, b0)
            swk_p.append(kp)
            swv_p.append(vp)
            swk_s.append(ks)
            swv_s.append(vs)
        else:
            xt, cp, np_, mp, cs, ns, ms = _mlstm_layer(
                xt, state_mlstm_c[slot], state_mlstm_n[slot], state_mlstm_m[slot], mlstm_w_in[slot],
                mlstm_b_gates[slot], mlstm_norm_g[slot][None], mlstm_w_out[slot], g0, b0)
            mc_p.append(cp)
            mn_p.append(np_)
            mm_p.append(mp)
            mc_s.append(cs)
            mn_s.append(ns)
            mm_s.append(ms)
        xt = _moe_layer(xt, i, moe_w_router[i], moe_router_bias[i], moe_w_gate, moe_w_up, moe_w_down,
                        moe_shared_w_gate[i], moe_shared_w_up[i], moe_shared_w_down[i],
                        ln_g[i, 1][None], ln_b[i, 1][None], split=(i == DEPTH - 1))
        if i < DEPTH - 1:
            xt = xt[0]
    y_p = xt[0].reshape(BATCH, SEQ, d)
    y_s = xt[1].reshape(DEC_BATCH, DEC_SEQ, d)
    return (y_p, y_s, jnp.stack(pool_p), jnp.stack(pool_s), jnp.stack(swk_p), jnp.stack(swv_p),
            jnp.stack(swk_s), jnp.stack(swv_s), jnp.stack(mc_p), jnp.stack(mn_p), jnp.stack(mm_p),
            jnp.stack(mc_s), jnp.stack(mn_s), jnp.stack(mm_s))
```

```python
import functools
import math

import jax
import jax.numpy as jnp
from jax import lax
from jax.experimental import pallas as pl
from jax.experimental.pallas import tpu as pltpu

F32 = jnp.float32
BF16 = jnp.bfloat16

D_MODEL = 2048
BATCH = 2
SEQ = 4096
DEPTH = 4
DEC_BATCH = 32
DEC_SEQ = 4
PAST_LEN = 16384
N_PROMPT_TOK = BATCH * SEQ
N_SAMPLE_TOK = DEC_BATCH * DEC_SEQ
N_TOK = N_PROMPT_TOK + N_SAMPLE_TOK

N_MIXERS = 3
DN_ALPHA = (2.0 * DEPTH) ** 0.25
LN_EPS = 1e-5

POOL_WINDOWS = (2, 4, 8, 16)
POOL_GROUP_DIM = D_MODEL // len(POOL_WINDOWS)
POOL_STATE = max(POOL_WINDOWS) - 1
POOL_HALO = POOL_STATE + 1

SWA_WINDOW = 128
SWA_HEAD_DIM = 64
SWA_HEADS = D_MODEL // SWA_HEAD_DIM
SWA_KV_HEADS = SWA_HEADS // 8
SWA_GROUP = SWA_HEADS // SWA_KV_HEADS
SWA_KV_DIM = SWA_KV_HEADS * SWA_HEAD_DIM

MLSTM_HEADS = 8
MLSTM_HEAD_DIM = D_MODEL // MLSTM_HEADS
MLSTM_CHUNK = 64
MLSTM_SAMPLE_CHUNK = 8

N_EXPERTS = 64
TOP_K = 8
N_EXPERT_GROUPS = 8
TOPK_GROUPS = 4
EXPERT_FF = D_MODEL // 4
ROUTED_SCALE = 2.5

VMEM_LIMIT_BYTES = 56 * 1024 * 1024

ROW_TILE = 320
ROUTER_TILE = 640
POOL_TILE = 512
MOE_BLOCK = 512
MOE_N_BLOCKS = N_TOK * TOP_K // MOE_BLOCK + N_EXPERTS
MOE_ROWS = MOE_N_BLOCKS * MOE_BLOCK
DISPATCH_TILE = 320
COMBINE_TILE = 128
LANES = 128
PACK_ROWS = D_MODEL // (2 * LANES)

_NT = (((1,), (1,)), ((), ()))
_TN = (((0,), (0,)), ((), ()))


def _params(*semantics):
    return pltpu.CompilerParams(dimension_semantics=semantics, vmem_limit_bytes=VMEM_LIMIT_BYTES)


def _layer_norm(z, g, b):
    mu = jnp.mean(z, axis=-1, keepdims=True)
    zc = z - mu
    var = jnp.mean(zc * zc, axis=-1, keepdims=True)
    return zc * lax.rsqrt(var + LN_EPS) * g + b


def _silu(x):
    return x * jax.nn.sigmoid(x)


def _matmul_kernel(x_ref, w_ref, o_ref):
    o_ref[...] = jnp.dot(x_ref[...].astype(BF16), w_ref[...], preferred_element_type=F32)


def _matmul(x, w, *, tm, tn):
    m, k = x.shape
    n = w.shape[1]
    return pl.pallas_call(
        _matmul_kernel,
        grid=(n // tn, m // tm),
        in_specs=[pl.BlockSpec((tm, k), lambda j, i: (i, 0)),
                  pl.BlockSpec((k, tn), lambda j, i: (0, j))],
        out_specs=pl.BlockSpec((tm, tn), lambda j, i: (i, j)),
        out_shape=jax.ShapeDtypeStruct((m, n), F32),
        compiler_params=_params("parallel", "parallel"),
        name="matmul",
    )(x, w)


def _matmul_ln_kernel(a_ref, w_ref, res_ref, g_ref, b_ref, o_ref):
    y = jnp.dot(a_ref[...].astype(BF16), w_ref[...], preferred_element_type=F32)
    o_ref[...] = _layer_norm(DN_ALPHA * res_ref[...] + y, g_ref[...], b_ref[...])


def _matmul_ln(a, w, res, g, b):
    m, k = a.shape
    row = lambda i: (i, 0)
    const = lambda i: (0, 0)
    return pl.pallas_call(
        _matmul_ln_kernel,
        grid=(m // ROW_TILE,),
        in_specs=[pl.BlockSpec((ROW_TILE, k), row),
                  pl.BlockSpec((k, D_MODEL), const),
                  pl.BlockSpec((ROW_TILE, D_MODEL), row),
                  pl.BlockSpec((1, D_MODEL), const),
                  pl.BlockSpec((1, D_MODEL), const)],
        out_specs=pl.BlockSpec((ROW_TILE, D_MODEL), row),
        out_shape=jax.ShapeDtypeStruct((m, D_MODEL), F32),
        compiler_params=_params("parallel"),
        name="matmul_ln",
    )(a, w, res, g, b)


def _pool_core(zbuf, w_ref, scale_ref, g_ref, b_ref, o_ref, *, tt, n_before):
    avail = n_before + lax.broadcasted_iota(jnp.int32, (tt, 1), 0) + 1
    for grp, win in enumerate(POOL_WINDOWS):
        c0, c1 = grp * POOL_GROUP_DIM, (grp + 1) * POOL_GROUP_DIM
        xg = zbuf[POOL_HALO:POOL_HALO + tt, c0:c1]
        total = xg
        for back in range(1, win):
            total = total + zbuf[POOL_HALO - back:POOL_HALO - back + tt, c0:c1]
        count = jnp.minimum(win, avail).astype(F32)
        diff = total / count - xg
        y = jnp.dot(diff.astype(BF16), w_ref[grp], preferred_element_type=F32) * scale_ref[:, c0:c1]
        o_ref[:, c0:c1] = DN_ALPHA * xg + y
    o_ref[...] = _layer_norm(o_ref[...], g_ref[...], b_ref[...])


def _pool_prompt_kernel(x_ref, halo_ref, w_ref, scale_ref, g_ref, b_ref, o_ref, zbuf, *, tt):
    i = pl.program_id(1)
    zbuf[0:POOL_HALO, :] = jnp.where(i == 0, 0.0, halo_ref[...])
    zbuf[POOL_HALO:POOL_HALO + tt, :] = x_ref[...]
    _pool_core(zbuf, w_ref, scale_ref, g_ref, b_ref, o_ref, tt=tt, n_before=i * tt)


def _pool_sample_kernel(z_ref, w_ref, scale_ref, g_ref, b_ref, o_ref, *, tt):
    _pool_core(z_ref.at[0], w_ref, scale_ref, g_ref, b_ref, o_ref.at[0], tt=tt, n_before=PAST_LEN)


def _pool_layer(xp, xs, state, w, scale, g, b):
    tt = POOL_TILE
    tiles = SEQ // tt
    halo_per_tile = tt // POOL_HALO
    const2 = lambda bb, i: (0, 0)
    w_bf = w.astype(BF16)
    out = pl.pallas_call(
        functools.partial(_pool_prompt_kernel, tt=tt),
        grid=(BATCH, tiles),
        in_specs=[pl.BlockSpec((tt, D_MODEL), lambda bb, i: (bb * tiles + i, 0)),
                  pl.BlockSpec((POOL_HALO, D_MODEL),
                               lambda bb, i: (jnp.maximum((bb * tiles + i) * halo_per_tile - 1, 0), 0)),
                  pl.BlockSpec(w_bf.shape, lambda bb, i: (0, 0, 0)),
                  pl.BlockSpec((1, D_MODEL), const2),
                  pl.BlockSpec((1, D_MODEL), const2),
                  pl.BlockSpec((1, D_MODEL), const2)],
        out_specs=pl.BlockSpec((tt, D_MODEL), lambda bb, i: (bb * tiles + i, 0)),
        out_shape=jax.ShapeDtypeStruct((N_TOK, D_MODEL), F32),
        scratch_shapes=[pltpu.VMEM((POOL_HALO + tt, D_MODEL), F32)],
        compiler_params=_params("parallel", "arbitrary"),
        name="pool_prompt",
    )(xp, xp, w_bf, scale, g, b)

    ts = 16
    zs = jnp.concatenate([jnp.zeros((DEC_BATCH, 1, D_MODEL), F32), state, xs,
                          jnp.zeros((DEC_BATCH, ts - DEC_SEQ, D_MODEL), F32)], axis=1)
    const1 = lambda bb: (0, 0)
    out_s = pl.pallas_call(
        functools.partial(_pool_sample_kernel, tt=ts),
        grid=(DEC_BATCH,),
        in_specs=[pl.BlockSpec((1, POOL_HALO + ts, D_MODEL), lambda bb: (bb, 0, 0)),
                  pl.BlockSpec(w_bf.shape, lambda bb: (0, 0, 0)),
                  pl.BlockSpec((1, D_MODEL), const1),
                  pl.BlockSpec((1, D_MODEL), const1),
                  pl.BlockSpec((1, D_MODEL), const1)],
        out_specs=pl.BlockSpec((1, ts, D_MODEL), lambda bb: (bb, 0, 0)),
        out_shape=jax.ShapeDtypeStruct((DEC_BATCH, ts, D_MODEL), F32),
        compiler_params=_params("parallel"),
        name="pool_sample",
    )(zs, w_bf, scale, g, b)
    out = lax.dynamic_update_slice(out, out_s[:, :DEC_SEQ].reshape(N_SAMPLE_TOK, D_MODEL), (N_PROMPT_TOK, 0))

    new_p = jnp.stack([xp[(bb + 1) * SEQ - POOL_STATE:(bb + 1) * SEQ] for bb in range(BATCH)])
    new_s = jnp.concatenate([state, xs], axis=1)[:, DEC_SEQ:]
    return out, new_p, new_s


def _alibi_slope(h):
    return 2.0 ** (-8.0 * (h + 1.0) / SWA_HEADS)


def _attn_core(q_ref, k_all, v_all, sink_ref, o_ref, *, rows, first_block):
    w, dh = SWA_WINDOW, SWA_HEAD_DIM
    qi = lax.broadcasted_iota(jnp.int32, (rows, 2 * w), 0)
    sj = lax.broadcasted_iota(jnp.int32, (rows, 2 * w), 1)
    dist = (w + qi) - sj
    valid = (dist >= 0) & (dist <= w)
    if first_block is not None:
        valid = valid & ((sj >= w) | jnp.logical_not(first_block))
    masked_dist = jnp.where(valid, dist.astype(F32), jnp.inf)
    for kv in range(SWA_KV_HEADS):
        c0, c1 = kv * dh, (kv + 1) * dh
        heads = range(kv * SWA_GROUP, (kv + 1) * SWA_GROUP)
        q = jnp.concatenate([q_ref[:, h * dh:(h + 1) * dh] for h in heads], axis=0)
        q = (q * (dh ** -0.5)).astype(BF16)
        s_all = lax.dot_general(q, k_all[:, c0:c1], _NT, preferred_element_type=F32)
        probs, dens = [], []
        for j, h in enumerate(heads):
            sink = sink_ref[h]
            s = s_all[j * rows:(j + 1) * rows] - _alibi_slope(h) * masked_dist
            m = jnp.maximum(jnp.max(s, axis=1, keepdims=True), sink)
            e = jnp.exp(s - m)
            dens.append(jnp.sum(e, axis=1, keepdims=True) + jnp.exp(sink - m))
            probs.append(e.astype(BF16))
        o_all = jnp.dot(jnp.concatenate(probs, axis=0), v_all[:, c0:c1], preferred_element_type=F32)
        for j, h in enumerate(heads):
            o_ref[:, h * dh:(h + 1) * dh] = o_all[j * rows:(j + 1) * rows] / dens[j]


def _attn_prompt_kernel(sink_ref, q_ref, kp_ref, kc_ref, vp_ref, vc_ref, o_ref):
    n = pl.program_id(1)
    k_all = jnp.concatenate([kp_ref[...], kc_ref[...]], axis=0).astype(BF16)
    v_all = jnp.concatenate([vp_ref[...], vc_ref[...]], axis=0).astype(BF16)
    _attn_core(q_ref, k_all, v_all, sink_ref, o_ref, rows=SWA_WINDOW, first_block=(n == 0))


def _attn_sample_kernel(sink_ref, q_ref, k_ref, v_ref, o_ref, *, rows):
    _attn_core(q_ref.at[0], k_ref[0].astype(BF16), v_ref[0].astype(BF16), sink_ref, o_ref.at[0],
               rows=rows, first_block=None)


def _swa_layer(xt, cache_k, cache_v, w_qkv, w_o, sinks, g, b):
    w = SWA_WINDOW
    qkv = _matmul(xt, w_qkv.astype(BF16), tm=640, tn=1280)
    nb = SEQ // w
    k_col = D_MODEL // SWA_KV_DIM
    cur = lambda bb, n, *_: (bb * nb + n, k_col)
    prev = lambda bb, n, *_: (jnp.maximum(bb * nb + n - 1, 0), k_col)
    cur_v = lambda bb, n, *_: (bb * nb + n, k_col + 1)
    prev_v = lambda bb, n, *_: (jnp.maximum(bb * nb + n - 1, 0), k_col + 1)
    o = pl.pallas_call(
        _attn_prompt_kernel,
        grid_spec=pltpu.PrefetchScalarGridSpec(
            num_scalar_prefetch=1,
            grid=(BATCH, nb),
            in_specs=[pl.BlockSpec((w, D_MODEL), lambda bb, n, *_: (bb * nb + n, 0)),
                      pl.BlockSpec((w, SWA_KV_DIM), prev),
                      pl.BlockSpec((w, SWA_KV_DIM), cur),
                      pl.BlockSpec((w, SWA_KV_DIM), prev_v),
                      pl.BlockSpec((w, SWA_KV_DIM), cur_v)],
            out_specs=pl.BlockSpec((w, D_MODEL), lambda bb, n, *_: (bb * nb + n, 0))),
        out_shape=jax.ShapeDtypeStruct((N_TOK, D_MODEL), F32),
        compiler_params=_params("parallel", "arbitrary"),
        name="attn_prompt",
    )(sinks, qkv, qkv, qkv, qkv, qkv)

    rows = 16
    qkv_s = qkv[N_PROMPT_TOK:].reshape(DEC_BATCH, DEC_SEQ, -1)
    q_s = jnp.pad(qkv_s[..., :D_MODEL], ((0, 0), (0, rows - DEC_SEQ), (0, 0)))
    k_new = qkv_s[..., D_MODEL:D_MODEL + SWA_KV_DIM]
    v_new = qkv_s[..., D_MODEL + SWA_KV_DIM:]
    kz = jnp.concatenate([cache_k.reshape(DEC_BATCH, w, SWA_KV_DIM), k_new], axis=1)
    vz = jnp.concatenate([cache_v.reshape(DEC_BATCH, w, SWA_KV_DIM), v_new], axis=1)
    pad_keys = ((0, 0), (0, w - DEC_SEQ), (0, 0))
    o_s = pl.pallas_call(
        functools.partial(_attn_sample_kernel, rows=rows),
        grid_spec=pltpu.PrefetchScalarGridSpec(
            num_scalar_prefetch=1,
            grid=(DEC_BATCH,),
            in_specs=[pl.BlockSpec((1, rows, D_MODEL), lambda bb, *_: (bb, 0, 0)),
                      pl.BlockSpec((1, 2 * w, SWA_KV_DIM), lambda bb, *_: (bb, 0, 0)),
                      pl.BlockSpec((1, 2 * w, SWA_KV_DIM), lambda bb, *_: (bb, 0, 0))],
            out_specs=pl.BlockSpec((1, rows, D_MODEL), lambda bb, *_: (bb, 0, 0))),
        out_shape=jax.ShapeDtypeStruct((DEC_BATCH, rows, D_MODEL), F32),
        compiler_params=_params("parallel"),
        name="attn_sample",
    )(sinks, q_s, jnp.pad(kz, pad_keys), jnp.pad(vz, pad_keys))
    o = lax.dynamic_update_slice(o, o_s[:, :DEC_SEQ].reshape(N_SAMPLE_TOK, D_MODEL), (N_PROMPT_TOK, 0))

    out = _matmul_ln(o, w_o.astype(BF16), xt, g, b)

    kv_shape = (SWA_WINDOW, SWA_KV_HEADS, SWA_HEAD_DIM)
    kv_p = jnp.stack([qkv[(bb + 1) * SEQ - w:(bb + 1) * SEQ, D_MODEL:] for bb in range(BATCH)])
    new_k_p = kv_p[..., :SWA_KV_DIM].reshape((BATCH,) + kv_shape)
    new_v_p = kv_p[..., SWA_KV_DIM:].reshape((BATCH,) + kv_shape)
    new_k_s = kz[:, DEC_SEQ:].reshape((DEC_BATCH,) + kv_shape)
    new_v_s = vz[:, DEC_SEQ:].reshape((DEC_BATCH,) + kv_shape)
    return out, new_k_p, new_v_p, new_k_s, new_v_s


def _log_sigmoid(x):
    return jnp.minimum(x, 0.0) - jnp.log(1.0 + jnp.exp(-jnp.abs(x)))


def _mlstm_kernel(bias_ref, q_ref, k_ref, v_ref, og_ref, gates_ref, c0_ref, n0_ref, m0_ref, ng_ref,
                  h_ref, c_ref, n_ref, m_ref, c_scr, n_scr, m_scr, *, chunk, n_valid):
    step = pl.program_id(1)

    @pl.when(step == 0)
    def _():
        c_scr[...] = c0_ref[0]
        n_scr[...] = n0_ref[0]
        m_scr[...] = m0_ref[0]

    for head in range(MLSTM_HEADS):
        _mlstm_head(head, step, bias_ref, q_ref, k_ref, v_ref, og_ref, gates_ref, ng_ref, h_ref, c_scr, n_scr, m_scr,
                    chunk=chunk, n_valid=n_valid)

    @pl.when(step == pl.num_programs(1) - 1)
    def _():
        c_ref[0] = c_scr[...]
        n_ref[0] = n_scr[...]
        m_ref[0] = m_scr[...]


def _mlstm_head(head, step, bias_ref, q_ref, k_ref, v_ref, og_ref, gates_ref, ng_ref, h_ref, c_scr, n_scr, m_scr,
                *, chunk, n_valid):
    ln = chunk
    cols = slice(head * MLSTM_HEAD_DIM, (head + 1) * MLSTM_HEAD_DIM)
    ig = gates_ref[0, head, pl.ds(step, 1), :] + bias_ref[head]
    lf = _log_sigmoid(gates_ref[0, MLSTM_HEADS + head, pl.ds(step, 1), :] + bias_ref[MLSTM_HEADS + head])
    if n_valid < ln:
        col = lax.broadcasted_iota(jnp.int32, (1, ln), 1)
        ig = jnp.where(col < n_valid, ig, -1e30)
        lf = jnp.where(col < n_valid, lf, 0.0)

    ri = lax.broadcasted_iota(jnp.int32, (ln, ln), 0)
    ci = lax.broadcasted_iota(jnp.int32, (ln, ln), 1)
    eye = ri == ci
    causal = ci <= ri

    def to_col(row):
        return jnp.sum(jnp.where(eye, row, 0.0), axis=1, keepdims=True)

    f_col = to_col(lf)
    b_col = jnp.sum(jnp.where(causal, lf, 0.0), axis=1, keepdims=True)
    b_row = jnp.sum(jnp.where(ri <= ci, f_col, 0.0), axis=0, keepdims=True)
    b_last = jnp.sum(lf, axis=1, keepdims=True)
    m_prev = m_scr[head, :, 0:1]
    a_col = b_col + m_prev
    dmat = jnp.where(causal, b_col - b_row + ig, -jnp.inf)
    mt = jnp.maximum(a_col, jnp.max(dmat, axis=1, keepdims=True))
    w_inter = jnp.exp(a_col - mt)

    q = q_ref[:, cols]
    k = k_ref[:, cols] * (MLSTM_HEAD_DIM ** -0.5)
    v = v_ref[:, cols]
    qb = q.astype(BF16)
    kb = k.astype(BF16)
    c_prev = c_scr[head]
    n_prev = n_scr[head]
    scores = lax.dot_general(qb, kb, _NT, preferred_element_type=F32)
    w_intra = jnp.exp(dmat - mt) * scores
    num = (w_inter * lax.dot_general(qb, c_prev.astype(BF16), _NT, preferred_element_type=F32)
           + jnp.dot(w_intra.astype(BF16), v.astype(BF16), preferred_element_type=F32))
    den = (w_inter * jnp.sum(q * n_prev, axis=1, keepdims=True)
           + jnp.sum(w_intra, axis=1, keepdims=True))
    h = num / jnp.maximum(jnp.abs(den), jnp.exp(-mt))

    g_row = b_last - b_row + ig
    m_new = jnp.maximum(b_last + m_prev, jnp.max(g_row, axis=1, keepdims=True))
    decay = jnp.exp(b_last + m_prev - m_new)
    wg_col = to_col(jnp.exp(g_row - m_new))
    c_new = decay * c_prev + lax.dot_general((v * wg_col).astype(BF16), kb, _TN, preferred_element_type=F32)
    n_new = decay * n_prev + jnp.sum(wg_col * k, axis=0, keepdims=True)
    c_scr[head] = c_new
    n_scr[head] = n_new
    m_scr[head] = jnp.broadcast_to(m_new, (1, LANES))

    mu = jnp.mean(h, axis=1, keepdims=True)
    hc = h - mu
    var = jnp.mean(hc * hc, axis=1, keepdims=True)
    h_ref[:, cols] = hc * lax.rsqrt(var + LN_EPS) * ng_ref[:, cols] * jax.nn.sigmoid(og_ref[:, cols])


def _mlstm_scan(proj, gates, b_gates, norm_g, c0, n0, m0, *, batch, n_chunks, chunk, n_valid, out_rows):
    d, dh, nh = D_MODEL, MLSTM_HEAD_DIM, MLSTM_HEADS
    rows = lambda off: (lambda bb, cc, *_: (bb * n_chunks + cc, off))
    per_seq = lambda bb, cc, *_: (bb, 0, 0, 0)
    return pl.pallas_call(
        functools.partial(_mlstm_kernel, chunk=chunk, n_valid=n_valid),
        grid_spec=pltpu.PrefetchScalarGridSpec(
            num_scalar_prefetch=1,
            grid=(batch, n_chunks),
            in_specs=[pl.BlockSpec((chunk, d), rows(0)),
                      pl.BlockSpec((chunk, d), rows(1)),
                      pl.BlockSpec((chunk, d), rows(2)),
                      pl.BlockSpec((chunk, d), rows(3)),
                      pl.BlockSpec((1, 2 * nh, n_chunks, chunk), per_seq),
                      pl.BlockSpec((1, nh, dh, dh), per_seq),
                      pl.BlockSpec((1, nh, 1, dh), per_seq),
                      pl.BlockSpec((1, nh, 1, LANES), per_seq),
                      pl.BlockSpec((1, d), lambda bb, cc, *_: (0, 0))],
            out_specs=[pl.BlockSpec((chunk, d), lambda bb, cc, *_: (bb * n_chunks + cc, 0)),
                       pl.BlockSpec((1, nh, dh, dh), per_seq),
                       pl.BlockSpec((1, nh, 1, dh), per_seq),
                       pl.BlockSpec((1, nh, 1, LANES), per_seq)],
            scratch_shapes=[pltpu.VMEM((nh, dh, dh), F32), pltpu.VMEM((nh, 1, dh), F32),
                            pltpu.VMEM((nh, 1, LANES), F32)]),
        out_shape=[jax.ShapeDtypeStruct((out_rows, d), F32),
                   jax.ShapeDtypeStruct((batch, nh, dh, dh), F32),
                   jax.ShapeDtypeStruct((batch, nh, 1, dh), F32),
                   jax.ShapeDtypeStruct((batch, nh, 1, LANES), F32)],
        compiler_params=_params("parallel", "arbitrary"),
        name="mlstm_scan",
    )(b_gates, proj, proj, proj, proj, gates, c0, n0[:, :, None, :],
      jnp.broadcast_to(m0[:, :, None, None], (batch, nh, 1, LANES)), norm_g)


def _mlstm_layer(xt, c0_s, n0_s, m0_s, w_in, b_gates, norm_g, w_out, g, b):
    d, nh, dh = D_MODEL, MLSTM_HEADS, MLSTM_HEAD_DIM
    proj = _matmul(xt, w_in[:, :4 * d].astype(BF16), tm=640, tn=1024)
    w_gates = jnp.pad(w_in[:, 4 * d:], ((0, 0), (0, 128 - 2 * nh)))
    gate_pre = _matmul(xt, w_gates.astype(BF16), tm=640, tn=128)[:, :2 * nh]

    nc = SEQ // MLSTM_CHUNK
    gates_p = gate_pre[:N_PROMPT_TOK].reshape(BATCH, nc, MLSTM_CHUNK, 2 * nh).transpose(0, 3, 1, 2)
    zeros = lambda *s: jnp.zeros(s, F32)
    hn, c_p, n_p, m_p = _mlstm_scan(
        proj, gates_p, b_gates, norm_g, zeros(BATCH, nh, dh, dh), zeros(BATCH, nh, dh), zeros(BATCH, nh),
        batch=BATCH, n_chunks=nc, chunk=MLSTM_CHUNK, n_valid=MLSTM_CHUNK, out_rows=N_TOK)

    ls = MLSTM_SAMPLE_CHUNK
    pad_t = ((0, 0), (0, ls - DEC_SEQ), (0, 0))
    proj_s = jnp.pad(proj[N_PROMPT_TOK:].reshape(DEC_BATCH, DEC_SEQ, 4 * d), pad_t).reshape(DEC_BATCH * ls, 4 * d)
    gates_s = jnp.pad(gate_pre[N_PROMPT_TOK:].reshape(DEC_BATCH, DEC_SEQ, 2 * nh), pad_t)
    gates_s = gates_s.transpose(0, 2, 1)[:, :, None, :]
    hn_s, c_s, n_s, m_s = _mlstm_scan(
        proj_s, gates_s, b_gates, norm_g, c0_s, n0_s, m0_s,
        batch=DEC_BATCH, n_chunks=1, chunk=ls, n_valid=DEC_SEQ, out_rows=DEC_BATCH * ls)
    hn_s = hn_s.reshape(DEC_BATCH, ls, d)[:, :DEC_SEQ].reshape(N_SAMPLE_TOK, d)
    hn = lax.dynamic_update_slice(hn, hn_s, (N_PROMPT_TOK, 0))

    out = _matmul_ln(hn, w_out.astype(BF16), xt, g, b)
    return (out, c_p, n_p[:, :, 0], m_p[:, :, 0, 0], c_s, n_s[:, :, 0], m_s[:, :, 0, 0])


def _router_kernel(x_ref, wt_ref, bias_ref, tri_ref, eidx_ref, gate_ref, rank_ref, count_ref, xq_ref, count_scr,
                   *, tt):
    neg = -jnp.inf
    x = x_ref[...]
    logits = lax.dot_general(wt_ref[...].astype(BF16), x.astype(BF16), _NT, preferred_element_type=F32)
    s = jax.nn.sigmoid(logits)
    sb = s + bias_ref[...]
    per_group = N_EXPERTS // N_EXPERT_GROUPS

    sb3 = sb.reshape(N_EXPERT_GROUPS, per_group, tt)
    i3 = lax.broadcasted_iota(jnp.int32, sb3.shape, 1)
    m1 = jnp.max(sb3, axis=1, keepdims=True)
    first = jnp.min(jnp.where(sb3 == m1, i3, per_group), axis=1, keepdims=True)
    m2 = jnp.max(jnp.where(i3 == first, neg, sb3), axis=1, keepdims=True)
    gscore = (m1 + m2).reshape(N_EXPERT_GROUPS, tt)

    gi = lax.broadcasted_iota(jnp.int32, gscore.shape, 0)
    gsel = jnp.zeros(gscore.shape, jnp.bool_)
    cur = gscore
    for _ in range(TOPK_GROUPS):
        mx = jnp.max(cur, axis=0, keepdims=True)
        pick = gi == jnp.min(jnp.where(cur == mx, gi, N_EXPERT_GROUPS), axis=0, keepdims=True)
        gsel = gsel | pick
        cur = jnp.where(pick, neg, cur)
    emask = jnp.broadcast_to(gsel.reshape(N_EXPERT_GROUPS, 1, tt), sb3.shape).reshape(N_EXPERTS, tt)

    ei = lax.broadcasted_iota(jnp.int32, sb.shape, 0)
    sel = jnp.zeros(sb.shape, jnp.bool_)
    cur = jnp.where(emask, sb, neg)
    picks = []
    for _ in range(TOP_K):
        mx = jnp.max(cur, axis=0, keepdims=True)
        idx = jnp.min(jnp.where(cur == mx, ei, N_EXPERTS), axis=0, keepdims=True)
        pick = ei == idx
        picks.append((idx, pick))
        sel = sel | pick
        cur = jnp.where(pick, neg, cur)
    s_sel = jnp.where(sel, s, 0.0)
    gate = s_sel / jnp.sum(s_sel, axis=0, keepdims=True) * ROUTED_SCALE

    @pl.when(pl.program_id(0) == 0)
    def _():
        count_scr[...] = jnp.zeros_like(count_scr)

    sel_f = sel.astype(F32)
    incl = jnp.dot(sel_f.astype(BF16), tri_ref[...], preferred_element_type=F32)
    rank = count_scr[:, 0:1] + incl - sel_f
    count_scr[...] = count_scr[...] + jnp.sum(sel_f, axis=1, keepdims=True)
    count_ref[...] = count_scr[...].astype(jnp.int32)

    for kk, (idx, pick) in enumerate(picks):
        eidx_ref[kk:kk + 1, :] = idx
        gate_ref[kk:kk + 1, :] = jnp.sum(jnp.where(pick, gate, 0.0), axis=0, keepdims=True)
        rank_ref[kk:kk + 1, :] = jnp.sum(jnp.where(pick, rank, 0.0), axis=0, keepdims=True).astype(jnp.int32)

    for sl in range(PACK_ROWS):
        c0 = 2 * LANES * sl
        xq_ref[pl.ds(sl, tt, stride=PACK_ROWS), :] = _pack_bf16_pair(x[:, c0:c0 + LANES],
                                                                     x[:, c0 + LANES:c0 + 2 * LANES])


def _pack_bf16_pair(hi, lo):
    hb = lax.bitcast_convert_type(hi.astype(BF16).astype(F32), jnp.uint32)
    lb = lax.bitcast_convert_type(lo.astype(BF16).astype(F32), jnp.uint32)
    return hb | (lb >> 16)


def _unpack_bf16_pair(word):
    hi = lax.bitcast_convert_type(word & jnp.uint32(0xFFFF0000), F32)
    lo = lax.bitcast_convert_type(word << 16, F32)
    return hi, lo


def _moe_route(xt, w_router, bias):
    tt = ROUTER_TILE
    col = lambda i: (0, i)
    const = lambda i: (0, 0)
    tri = (jnp.arange(tt)[:, None] <= jnp.arange(tt)[None, :]).astype(BF16)
    return pl.pallas_call(
        functools.partial(_router_kernel, tt=tt),
        grid=(N_TOK // tt,),
        in_specs=[pl.BlockSpec((tt, D_MODEL), lambda i: (i, 0)),
                  pl.BlockSpec((N_EXPERTS, D_MODEL), const),
                  pl.BlockSpec((N_EXPERTS, 1), const),
                  pl.BlockSpec((tt, tt), const)],
        out_specs=[pl.BlockSpec((TOP_K, tt), col), pl.BlockSpec((TOP_K, tt), col), pl.BlockSpec((TOP_K, tt), col),
                   pl.BlockSpec((N_EXPERTS, LANES), const),
                   pl.BlockSpec((tt * PACK_ROWS, LANES), lambda i: (i, 0))],
        out_shape=[jax.ShapeDtypeStruct((TOP_K, N_TOK), jnp.int32),
                   jax.ShapeDtypeStruct((TOP_K, N_TOK), F32),
                   jax.ShapeDtypeStruct((TOP_K, N_TOK), jnp.int32),
                   jax.ShapeDtypeStruct((N_EXPERTS, LANES), jnp.int32),
                   jax.ShapeDtypeStruct((N_TOK * PACK_ROWS, LANES), jnp.uint32)],
        scratch_shapes=[pltpu.VMEM((N_EXPERTS, LANES), F32)],
        compiler_params=_params("arbitrary"),
        name="moe_router",
    )(xt, w_router.T, bias[:, None], tri)


def _rowmap_kernel(row_ref, src_ref, *, per_step):
    base = pl.program_id(0) * per_step

    def body(j, carry):
        src_ref[row_ref[0, 0, j]] = base + j
        return carry
    lax.fori_loop(0, per_step, body, 0, unroll=16)


def _moe_rowmap(row_tok):
    tt = DISPATCH_TILE
    n_tiles = N_TOK // tt
    per_step = tt * TOP_K
    return pl.pallas_call(
        functools.partial(_rowmap_kernel, per_step=per_step),
        grid=(n_tiles,),
        in_specs=[pl.BlockSpec((1, 1, per_step), lambda i: (i, 0, 0), memory_space=pltpu.SMEM)],
        out_specs=pl.BlockSpec(memory_space=pltpu.SMEM),
        out_shape=jax.ShapeDtypeStruct((MOE_ROWS,), jnp.int32),
        compiler_params=_params("arbitrary"),
        name="moe_rowmap",
    )(row_tok.reshape(n_tiles, 1, per_step))


def _experts_kernel(be_ref, next_ref, nused_ref, src_ref, src_next_ref, xq_hbm, wg_hbm, wu_hbm, wd_hbm, y_ref,
                    x_a, x_b, wg_f, wu_f, wd_f, wgu_s, wd_s, wsem, xsem, *, layer):
    i = pl.program_id(0)
    tm = MOE_BLOCK
    n_used = nused_ref[0]

    def weight_copies(e):
        return (pltpu.make_async_copy(wg_hbm.at[layer, e], wg_f, wsem.at[0]),
                pltpu.make_async_copy(wu_hbm.at[layer, e], wu_f, wsem.at[1]),
                pltpu.make_async_copy(wd_hbm.at[layer, e], wd_f, wsem.at[2]))

    def start_gather(src, into, into_sem, r):
        tok = jnp.clip(src[0, 0, r] // TOP_K, 0, N_TOK - 1)
        pltpu.make_async_copy(xq_hbm.at[pl.ds(pl.multiple_of(tok * PACK_ROWS, PACK_ROWS), PACK_ROWS)],
                              into.at[pl.ds(r * PACK_ROWS, PACK_ROWS)], into_sem).start()

    def wait_gather(into, into_sem):
        pltpu.make_async_copy(xq_hbm.at[pl.ds(0, tm * PACK_ROWS)], into, into_sem).wait()

    @pl.when(i == 0)
    def _():
        for cp in weight_copies(be_ref[0]):
            cp.start()

        def body(r, carry):
            tok = jnp.clip(src_ref[0, 0, r] // TOP_K, 0, N_TOK - 1)
            pltpu.make_async_copy(xq_hbm.at[pl.ds(pl.multiple_of(tok * PACK_ROWS, PACK_ROWS), PACK_ROWS)],
                                  x_a.at[pl.ds(pl.multiple_of(r * PACK_ROWS, PACK_ROWS), PACK_ROWS)],
                                  xsem.at[0]).start()
            return carry
        lax.fori_loop(0, tm, body, 0)

    def block(cur, cur_sem, nxt, nxt_sem):
        rows = iter(range(tm))

        def issue(n):
            for _ in range(n):
                r = next(rows, None)
                if r is not None:
                    start_gather(src_next_ref, nxt, nxt_sem, r)

        wait_gather(cur, cur_sem)
        issue(tm // 4)
        parts = []
        for sl in range(PACK_ROWS):
            parts.extend(_unpack_bf16_pair(cur[pl.ds(sl, tm, stride=PACK_ROWS), :]))
        x = jnp.concatenate(parts, axis=1).astype(BF16)
        issue(tm // 4)
        gate_up = jnp.dot(x, wgu_s[...], preferred_element_type=F32)
        hidden = _silu(gate_up[:, :EXPERT_FF]) * gate_up[:, EXPERT_FF:]
        issue(tm // 4)
        y = jnp.dot(hidden.astype(BF16), wd_s[...], preferred_element_type=F32)
        issue(tm)
        for sl in range(PACK_ROWS):
            c0 = 2 * LANES * sl
            y_ref[pl.ds(sl, tm, stride=PACK_ROWS), :] = _pack_bf16_pair(y[:, c0:c0 + LANES],
                                                                        y[:, c0 + LANES:c0 + 2 * LANES])

        @pl.when(i == n_used - 1)
        def _():
            wait_gather(nxt, nxt_sem)

    @pl.when(i < n_used)
    def _():
        first_of_expert = (i == 0) | (be_ref[i] != be_ref[jnp.maximum(i - 1, 0)])

        @pl.when(first_of_expert)
        def _():
            for cp in weight_copies(be_ref[i]):
                cp.wait()
            wgu_s[:, :EXPERT_FF] = wg_f[...].astype(BF16)
            wgu_s[:, EXPERT_FF:] = wu_f[...].astype(BF16)
            wd_s[...] = wd_f[...].astype(BF16)

            @pl.when(next_ref[i] >= 0)
            def _():
                for cp in weight_copies(next_ref[i]):
                    cp.start()

        @pl.when(i % 2 == 0)
        def _():
            block(x_a, xsem.at[0], x_b, xsem.at[1])

        @pl.when(i % 2 == 1)
        def _():
            block(x_b, xsem.at[1], x_a, xsem.at[0])


def _moe_experts(xq, row_src, block_e, next_e, n_used, layer, w_gate, w_up, w_down):
    last = lambda i, be, nx, nu: (jnp.minimum(i, nu[0] - 1), 0)
    src_tiles = row_src.reshape(MOE_N_BLOCKS, 1, MOE_BLOCK)
    src_spec = lambda off: pl.BlockSpec((1, 1, MOE_BLOCK),
                                        lambda i, be, nx, nu: (jnp.minimum(i + off, nu[0] - 1), 0, 0),
                                        memory_space=pltpu.SMEM)
    x_buf = pltpu.VMEM((MOE_BLOCK * PACK_ROWS, LANES), jnp.uint32)
    return pl.pallas_call(
        functools.partial(_experts_kernel, layer=layer),
        grid_spec=pltpu.PrefetchScalarGridSpec(
            num_scalar_prefetch=3,
            grid=(MOE_N_BLOCKS,),
            in_specs=[src_spec(0), src_spec(1),
                      pl.BlockSpec(memory_space=pl.ANY),
                      pl.BlockSpec(memory_space=pl.ANY),
                      pl.BlockSpec(memory_space=pl.ANY),
                      pl.BlockSpec(memory_space=pl.ANY)],
            out_specs=pl.BlockSpec((MOE_BLOCK * PACK_ROWS, LANES), last),
            scratch_shapes=[x_buf, x_buf,
                            pltpu.VMEM((D_MODEL, EXPERT_FF), F32), pltpu.VMEM((D_MODEL, EXPERT_FF), F32),
                            pltpu.VMEM((EXPERT_FF, D_MODEL), F32),
                            pltpu.VMEM((D_MODEL, 2 * EXPERT_FF), BF16),
                            pltpu.VMEM((EXPERT_FF, D_MODEL), BF16),
                            pltpu.SemaphoreType.DMA((3,)), pltpu.SemaphoreType.DMA((2,))]),
        out_shape=jax.ShapeDtypeStruct((MOE_ROWS * PACK_ROWS, LANES), jnp.uint32),
        compiler_params=_params("arbitrary"),
        name="moe_experts",
    )(block_e, next_e, n_used, src_tiles, src_tiles, xq, w_gate, w_up, w_down)


def _combine_kernel(row_ref, row_next_ref, x_ref, gate_ref, swg_ref, swu_ref, swd_ref, g_ref, b_ref, y_hbm, *rest,
                    tt, n_first):
    out_refs, (buf_a, buf_b, acc, sem) = rest[:-4], rest[-4:]
    i = pl.program_id(0)
    last = pl.num_programs(0) - 1

    def start_fetch(rows, into, into_sem, t):
        dst = pl.ds(t * PACK_ROWS, PACK_ROWS)
        for kk in range(TOP_K):
            src_row = pl.multiple_of(rows[0, 0, t * TOP_K + kk] * PACK_ROWS, PACK_ROWS)
            pltpu.make_async_copy(y_hbm.at[pl.ds(src_row, PACK_ROWS)], into.at[kk, dst], into_sem).start()

    def wait_fetch(into, into_sem):
        for kk in range(TOP_K):
            pltpu.make_async_copy(y_hbm.at[pl.ds(0, tt * PACK_ROWS)], into.at[kk], into_sem).wait()

    @pl.when(i == 0)
    def _():
        def body(t, carry):
            dst = pl.ds(pl.multiple_of(t * PACK_ROWS, PACK_ROWS), PACK_ROWS)
            for kk in range(TOP_K):
                src_row = pl.multiple_of(row_ref[0, 0, t * TOP_K + kk] * PACK_ROWS, PACK_ROWS)
                pltpu.make_async_copy(y_hbm.at[pl.ds(src_row, PACK_ROWS)], buf_a.at[kk, dst], sem.at[0]).start()
            return carry
        lax.fori_loop(0, tt, body, 0)

    def tile(cur, cur_sem, nxt, nxt_sem):
        per_phase = tt // (PACK_ROWS + 2)
        tokens = iter(range(tt))

        def issue(n):
            for _ in range(n):
                t = next(tokens, None)
                if t is not None:
                    start_fetch(row_next_ref, nxt, nxt_sem, t)

        x = x_ref[...]
        xb = x.astype(BF16)
        issue(per_phase)
        hidden = (_silu(jnp.dot(xb, swg_ref[...], preferred_element_type=F32))
                  * jnp.dot(xb, swu_ref[...], preferred_element_type=F32))
        issue(per_phase)
        shared = jnp.dot(hidden.astype(BF16), swd_ref[...], preferred_element_type=F32)
        wait_fetch(cur, cur_sem)
        gate = gate_ref[...]
        for sl in range(PACK_ROWS):
            issue(per_phase if sl < PACK_ROWS - 1 else tt)
            routed_hi = routed_lo = None
            for kk in range(TOP_K):
                hi, lo = _unpack_bf16_pair(cur[kk, pl.ds(sl, tt, stride=PACK_ROWS), :])
                g_k = gate[:, kk:kk + 1]
                routed_hi = g_k * hi if kk == 0 else routed_hi + g_k * hi
                routed_lo = g_k * lo if kk == 0 else routed_lo + g_k * lo
            for half, routed in enumerate((routed_hi, routed_lo)):
                cols = slice((2 * sl + half) * LANES, (2 * sl + half + 1) * LANES)
                acc[:, cols] = DN_ALPHA * x[:, cols] + (routed + shared[:, cols])
        res = _layer_norm(acc[...], g_ref[...], b_ref[...])
        if n_first is None:
            out_refs[0][...] = res
        else:
            @pl.when(i < n_first)
            def _():
                out_refs[0][...] = res

            @pl.when(i >= n_first)
            def _():
                out_refs[1][...] = res

        @pl.when(i == last)
        def _():
            wait_fetch(nxt, nxt_sem)

    @pl.when(i % 2 == 0)
    def _():
        tile(buf_a, sem.at[0], buf_b, sem.at[1])

    @pl.when(i % 2 == 1)
    def _():
        tile(buf_b, sem.at[1], buf_a, sem.at[0])


def _moe_combine(xt, ys, row_tiles, gate_tok, sw_gate, sw_up, sw_down, g, b, *, split):
    tt = COMBINE_TILE
    n_tiles = N_TOK // tt
    row = lambda i: (i, 0)
    const = lambda i: (0, 0)
    slots = pl.BlockSpec((1, 1, tt * TOP_K), lambda i: (i, 0, 0), memory_space=pltpu.SMEM)
    slots_next = pl.BlockSpec((1, 1, tt * TOP_K), lambda i: (jnp.minimum(i + 1, n_tiles - 1), 0, 0),
                              memory_space=pltpu.SMEM)
    row_tiles = row_tiles.reshape(n_tiles, 1, tt * TOP_K)
    if split:
        n_first = N_PROMPT_TOK // tt
        assert N_SAMPLE_TOK == tt
        out_specs = [pl.BlockSpec((tt, D_MODEL), lambda i: (jnp.minimum(i, n_first - 1), 0)),
                     pl.BlockSpec((tt, D_MODEL), const)]
        out_shape = [jax.ShapeDtypeStruct((N_PROMPT_TOK, D_MODEL), F32),
                     jax.ShapeDtypeStruct((N_SAMPLE_TOK, D_MODEL), F32)]
    else:
        n_first = None
        out_specs = [pl.BlockSpec((tt, D_MODEL), row)]
        out_shape = [jax.ShapeDtypeStruct((N_TOK, D_MODEL), F32)]
    return pl.pallas_call(
        functools.partial(_combine_kernel, tt=tt, n_first=n_first),
        grid=(n_tiles,),
        in_specs=[slots, slots_next,
                  pl.BlockSpec((tt, D_MODEL), row),
                  pl.BlockSpec((tt, TOP_K), row),
                  pl.BlockSpec((D_MODEL, EXPERT_FF), const),
                  pl.BlockSpec((D_MODEL, EXPERT_FF), const),
                  pl.BlockSpec((EXPERT_FF, D_MODEL), const),
                  pl.BlockSpec((1, D_MODEL), const),
                  pl.BlockSpec((1, D_MODEL), const),
                  pl.BlockSpec(memory_space=pl.ANY)],
        out_specs=out_specs,
        scratch_shapes=[pltpu.VMEM((TOP_K, tt * PACK_ROWS, LANES), jnp.uint32),
                        pltpu.VMEM((TOP_K, tt * PACK_ROWS, LANES), jnp.uint32),
                        pltpu.VMEM((tt, D_MODEL), F32), pltpu.SemaphoreType.DMA((2,))],
        out_shape=out_shape,
        compiler_params=_params("arbitrary"),
        name="moe_combine",
    )(row_tiles, row_tiles, xt, gate_tok, sw_gate.astype(BF16), sw_up.astype(BF16), sw_down.astype(BF16), g, b, ys)


def _moe_layer(xt, layer, w_router, bias, w_gate, w_up, w_down, sw_gate, sw_up, sw_down, g, b, *, split):
    eidx, gate, rank, counts, xq = _moe_route(xt, w_router, bias)

    counts = counts[:, 0]
    pcounts = (counts + MOE_BLOCK - 1) // MOE_BLOCK * MOE_BLOCK
    pends = jnp.cumsum(pcounts)
    pstarts = (pends - pcounts).astype(jnp.int32)
    n_used = (pends[-1] // MOE_BLOCK).astype(jnp.int32)
    blk = jnp.minimum(jnp.arange(MOE_N_BLOCKS, dtype=jnp.int32), n_used - 1) * MOE_BLOCK
    block_e = jnp.sum((pends[None, :] <= blk[:, None]).astype(jnp.int32), axis=1)
    block_e = jnp.minimum(block_e, N_EXPERTS - 1)
    after = pends[block_e] // MOE_BLOCK
    next_e = jnp.where(after < n_used, block_e[jnp.minimum(after, MOE_N_BLOCKS - 1)], -1).astype(jnp.int32)

    e_iota = jnp.arange(N_EXPERTS, dtype=jnp.int32)[None, :, None]
    rows = rank + jnp.sum(jnp.where(eidx[:, None, :] == e_iota, pstarts[None, :, None], 0), axis=1)
    row_tok = rows.T
    row_src = _moe_rowmap(row_tok)
    ys = _moe_experts(xq, row_src, block_e, next_e, n_used[None], layer, w_gate, w_up, w_down)
    return _moe_combine(xt, ys, row_tok, gate.T, sw_gate, sw_up, sw_down, g, b, split=split)


def kernel(x_prompt, x_sample, state_pool, cache_swa_k, cache_swa_v, state_mlstm_c, state_mlstm_n, state_mlstm_m, pool_w, pool_scale, swa_w_qkv, swa_w_o, swa_sinks, mlstm_w_in, mlstm_b_gates, mlstm_norm_g, mlstm_w_out, ln_g, ln_b, moe_w_router, moe_router_bias, moe_w_gate, moe_w_up, moe_w_down, moe_shared_w_gate, moe_shared_w_up, moe_shared_w_down):
    d = D_MODEL
    xt = None
    pool_p, pool_s = [], []
    swk_p, swv_p, swk_s, swv_s = [], [], [], []
    mc_p, mn_p, mm_p, mc_s, mn_s, mm_s = [], [], [], [], [], []
    for i in range(DEPTH):
        kind, slot = i % N_MIXERS, i // N_MIXERS
        g0, b0 = ln_g[i, 0][None], ln_b[i, 0][None]
        if kind == 0:
            if i == 0:
                xp, xs = x_prompt.reshape(N_PROMPT_TOK, d), x_sample
            else:
                xp, xs = xt, xt[N_PROMPT_TOK:].reshape(DEC_BATCH, DEC_SEQ, d)
            xt, sp, ss = _pool_layer(xp, xs, state_pool[slot], pool_w[slot], pool_scale[slot][None], g0, b0)
            pool_p.append(sp)
            pool_s.append(ss)
        elif kind == 1:
            xt, kp, vp, ks, vs = _swa_layer(xt, cache_swa_k[slot], cache_swa_v[slot], swa_w_qkv[slot],
                                            swa_w_o[slot], swa_sinks[slot], g0, b0)
            swk_p.append(kp)
            swv_p.append(vp)
            swk_s.append(ks)
            swv_s.append(vs)
        else:
            xt, cp, np_, mp, cs, ns, ms = _mlstm_layer(
                xt, state_mlstm_c[slot], state_mlstm_n[slot], state_mlstm_m[slot], mlstm_w_in[slot],
                mlstm_b_gates[slot], mlstm_norm_g[slot][None], mlstm_w_out[slot], g0, b0)
            mc_p.append(cp)
            mn_p.append(np_)
            mm_p.append(mp)
            mc_s.append(cs)
            mn_s.append(ns)
            mm_s.append(ms)
        xt = _moe_layer(xt, i, moe_w_router[i], moe_router_bias[i], moe_w_gate, moe_w_up, moe_w_down,
                        moe_shared_w_gate[i], moe_shared_w_up[i], moe_shared_w_down[i],
                        ln_g[i, 1][None], ln_b[i, 1][None], split=(i == DEPTH - 1))
        if i < DEPTH - 1:
            xt = xt[0]
    y_p = xt[0].reshape(BATCH, SEQ, d)
    y_s = xt[1].reshape(DEC_BATCH, DEC_SEQ, d)
    return (y_p, y_s, jnp.stack(pool_p), jnp.stack(pool_s), jnp.stack(swk_p), jnp.stack(swv_p),
            jnp.stack(swk_s), jnp.stack(swv_s), jnp.stack(mc_p), jnp.stack(mn_p), jnp.stack(mm_p),
            jnp.stack(mc_s), jnp.stack(mn_s), jnp.stack(mm_s))
```

```python
import functools
import math

import jax
import jax.numpy as jnp
from jax import lax
from jax.experimental import pallas as pl
from jax.experimental.pallas import tpu as pltpu

F32 = jnp.float32
BF16 = jnp.bfloat16

D_MODEL = 2048
BATCH = 2
SEQ = 4096
DEPTH = 4
DEC_BATCH = 32
DEC_SEQ = 4
PAST_LEN = 16384
N_PROMPT_TOK = BATCH * SEQ
N_SAMPLE_TOK = DEC_BATCH * DEC_SEQ
N_TOK = N_PROMPT_TOK + N_SAMPLE_TOK

N_MIXERS = 3
DN_ALPHA = (2.0 * DEPTH) ** 0.25
LN_EPS = 1e-5

POOL_WINDOWS = (2, 4, 8, 16)
POOL_GROUP_DIM = D_MODEL // len(POOL_WINDOWS)
POOL_STATE = max(POOL_WINDOWS) - 1
POOL_HALO = POOL_STATE + 1

SWA_WINDOW = 128
SWA_HEAD_DIM = 64
SWA_HEADS = D_MODEL // SWA_HEAD_DIM
SWA_KV_HEADS = SWA_HEADS // 8
SWA_GROUP = SWA_HEADS // SWA_KV_HEADS
SWA_KV_DIM = SWA_KV_HEADS * SWA_HEAD_DIM

MLSTM_HEADS = 8
MLSTM_HEAD_DIM = D_MODEL // MLSTM_HEADS
MLSTM_CHUNK = 64
MLSTM_SAMPLE_CHUNK = 8

N_EXPERTS = 64
TOP_K = 8
N_EXPERT_GROUPS = 8
TOPK_GROUPS = 4
EXPERT_FF = D_MODEL // 4
ROUTED_SCALE = 2.5

VMEM_LIMIT_BYTES = 56 * 1024 * 1024

ROW_TILE = 320
ROUTER_TILE = 640
POOL_TILE = 512
MOE_BLOCK = 512
MOE_N_BLOCKS = N_TOK * TOP_K // MOE_BLOCK + N_EXPERTS
MOE_ROWS = MOE_N_BLOCKS * MOE_BLOCK
DISPATCH_TILE = 320
COMBINE_TILE = 128
LANES = 128
PACK_ROWS = D_MODEL // (2 * LANES)

_NT = (((1,), (1,)), ((), ()))
_TN = (((0,), (0,)), ((), ()))


def _params(*semantics):
    return pltpu.CompilerParams(dimension_semantics=semantics, vmem_limit_bytes=VMEM_LIMIT_BYTES)


def _layer_norm(z, g, b):
    mu = jnp.mean(z, axis=-1, keepdims=True)
    zc = z - mu
    var = jnp.mean(zc * zc, axis=-1, keepdims=True)
    return zc * lax.rsqrt(var + LN_EPS) * g + b


def _silu(x):
    return x * jax.nn.sigmoid(x)


def _matmul_kernel(x_ref, w_ref, o_ref):
    o_ref[...] = jnp.dot(x_ref[...].astype(BF16), w_ref[...], preferred_element_type=F32)


def _matmul(x, w, *, tm, tn):
    m, k = x.shape
    n = w.shape[1]
    return pl.pallas_call(
        _matmul_kernel,
        grid=(n // tn, m // tm),
        in_specs=[pl.BlockSpec((tm, k), lambda j, i: (i, 0)),
                  pl.BlockSpec((k, tn), lambda j, i: (0, j))],
        out_specs=pl.BlockSpec((tm, tn), lambda j, i: (i, j)),
        out_shape=jax.ShapeDtypeStruct((m, n), F32),
        compiler_params=_params("parallel", "parallel"),
        name="matmul",
    )(x, w)


def _matmul_ln_kernel(a_ref, w_ref, res_ref, g_ref, b_ref, o_ref):
    y = jnp.dot(a_ref[...].astype(BF16), w_ref[...], preferred_element_type=F32)
    o_ref[...] = _layer_norm(DN_ALPHA * res_ref[...] + y, g_ref[...], b_ref[...])


def _matmul_ln(a, w, res, g, b):
    m, k = a.shape
    row = lambda i: (i, 0)
    const = lambda i: (0, 0)
    return pl.pallas_call(
        _matmul_ln_kernel,
        grid=(m // ROW_TILE,),
        in_specs=[pl.BlockSpec((ROW_TILE, k), row),
                  pl.BlockSpec((k, D_MODEL), const),
                  pl.BlockSpec((ROW_TILE, D_MODEL), row),
                  pl.BlockSpec((1, D_MODEL), const),
                  pl.BlockSpec((1, D_MODEL), const)],
        out_specs=pl.BlockSpec((ROW_TILE, D_MODEL), row),
        out_shape=jax.ShapeDtypeStruct((m, D_MODEL), F32),
        compiler_params=_params("parallel"),
        name="matmul_ln",
    )(a, w, res, g, b)


def _pool_core(zbuf, w_ref, scale_ref, g_ref, b_ref, o_ref, *, tt, n_before):
    avail = n_before + lax.broadcasted_iota(jnp.int32, (tt, 1), 0) + 1
    for grp, win in enumerate(POOL_WINDOWS):
        c0, c1 = grp * POOL_GROUP_DIM, (grp + 1) * POOL_GROUP_DIM
        xg = zbuf[POOL_HALO:POOL_HALO + tt, c0:c1]
        total = xg
        for back in range(1, win):
            total = total + zbuf[POOL_HALO - back:POOL_HALO - back + tt, c0:c1]
        count = jnp.minimum(win, avail).astype(F32)
        diff = total / count - xg
        y = jnp.dot(diff.astype(BF16), w_ref[grp], preferred_element_type=F32) * scale_ref[:, c0:c1]
        o_ref[:, c0:c1] = DN_ALPHA * xg + y
    o_ref[...] = _layer_norm(o_ref[...], g_ref[...], b_ref[...])


def _pool_prompt_kernel(x_ref, halo_ref, w_ref, scale_ref, g_ref, b_ref, o_ref, zbuf, *, tt):
    i = pl.program_id(1)
    zbuf[0:POOL_HALO, :] = jnp.where(i == 0, 0.0, halo_ref[...])
    zbuf[POOL_HALO:POOL_HALO + tt, :] = x_ref[...]
    _pool_core(zbuf, w_ref, scale_ref, g_ref, b_ref, o_ref, tt=tt, n_before=i * tt)


def _pool_sample_kernel(z_ref, w_ref, scale_ref, g_ref, b_ref, o_ref, *, tt):
    _pool_core(z_ref.at[0], w_ref, scale_ref, g_ref, b_ref, o_ref.at[0], tt=tt, n_before=PAST_LEN)


def _pool_layer(xp, xs, state, w, scale, g, b):
    tt = POOL_TILE
    tiles = SEQ // tt
    halo_per_tile = tt // POOL_HALO
    const2 = lambda bb, i: (0, 0)
    w_bf = w.astype(BF16)
    out = pl.pallas_call(
        functools.partial(_pool_prompt_kernel, tt=tt),
        grid=(BATCH, tiles),
        in_specs=[pl.BlockSpec((tt, D_MODEL), lambda bb, i: (bb * tiles + i, 0)),
                  pl.BlockSpec((POOL_HALO, D_MODEL),
                               lambda bb, i: (jnp.maximum((bb * tiles + i) * halo_per_tile - 1, 0), 0)),
                  pl.BlockSpec(w_bf.shape, lambda bb, i: (0, 0, 0)),
                  pl.BlockSpec((1, D_MODEL), const2),
                  pl.BlockSpec((1, D_MODEL), const2),
                  pl.BlockSpec((1, D_MODEL), const2)],
        out_specs=pl.BlockSpec((tt, D_MODEL), lambda bb, i: (bb * tiles + i, 0)),
        out_shape=jax.ShapeDtypeStruct((N_TOK, D_MODEL), F32),
        scratch_shapes=[pltpu.VMEM((POOL_HALO + tt, D_MODEL), F32)],
        compiler_params=_params("parallel", "arbitrary"),
        name="pool_prompt",
    )(xp, xp, w_bf, scale, g, b)

    ts = 16
    zs = jnp.concatenate([jnp.zeros((DEC_BATCH, 1, D_MODEL), F32), state, xs,
                          jnp.zeros((DEC_BATCH, ts - DEC_SEQ, D_MODEL), F32)], axis=1)
    const1 = lambda bb: (0, 0)
    out_s = pl.pallas_call(
        functools.partial(_pool_sample_kernel, tt=ts),
        grid=(DEC_BATCH,),
        in_specs=[pl.BlockSpec((1, POOL_HALO + ts, D_MODEL), lambda bb: (bb, 0, 0)),
                  pl.BlockSpec(w_bf.shape, lambda bb: (0, 0, 0)),
                  pl.BlockSpec((1, D_MODEL), const1),
                  pl.BlockSpec((1, D_MODEL), const1),
                  pl.BlockSpec((1, D_MODEL), const1)],
        out_specs=pl.BlockSpec((1, ts, D_MODEL), lambda bb: (bb, 0, 0)),
        out_shape=jax.ShapeDtypeStruct((DEC_BATCH, ts, D_MODEL), F32),
        compiler_params=_params("parallel"),
        name="pool_sample",
    )(zs, w_bf, scale, g, b)
    out = lax.dynamic_update_slice(out, out_s[:, :DEC_SEQ].reshape(N_SAMPLE_TOK, D_MODEL), (N_PROMPT_TOK, 0))

    new_p = jnp.stack([xp[(bb + 1) * SEQ - POOL_STATE:(bb + 1) * SEQ] for bb in range(BATCH)])
    new_s = jnp.concatenate([state, xs], axis=1)[:, DEC_SEQ:]
    return out, new_p, new_s


def _alibi_slope(h):
    return 2.0 ** (-8.0 * (h + 1.0) / SWA_HEADS)


def _attn_core(q_ref, k_all, v_all, sink_ref, o_ref, *, rows, first_block):
    w, dh = SWA_WINDOW, SWA_HEAD_DIM
    qi = lax.broadcasted_iota(jnp.int32, (rows, 2 * w), 0)
    sj = lax.broadcasted_iota(jnp.int32, (rows, 2 * w), 1)
    dist = (w + qi) - sj
    valid = (dist >= 0) & (dist <= w)
    if first_block is not None:
        valid = valid & ((sj >= w) | jnp.logical_not(first_block))
    masked_dist = jnp.where(valid, dist.astype(F32), jnp.inf)
    for kv in range(SWA_KV_HEADS):
        c0, c1 = kv * dh, (kv + 1) * dh
        heads = range(kv * SWA_GROUP, (kv + 1) * SWA_GROUP)
        q = jnp.concatenate([q_ref[:, h * dh:(h + 1) * dh] for h in heads], axis=0)
        q = (q * (dh ** -0.5)).astype(BF16)
        s_all = lax.dot_general(q, k_all[:, c0:c1], _NT, preferred_element_type=F32)
        probs, dens = [], []
        for j, h in enumerate(heads):
            sink = sink_ref[h]
            s = s_all[j * rows:(j + 1) * rows] - _alibi_slope(h) * masked_dist
            m = jnp.maximum(jnp.max(s, axis=1, keepdims=True), sink)
            e = jnp.exp(s - m)
            dens.append(jnp.sum(e, axis=1, keepdims=True) + jnp.exp(sink - m))
            probs.append(e.astype(BF16))
        o_all = jnp.dot(jnp.concatenate(probs, axis=0), v_all[:, c0:c1], preferred_element_type=F32)
        for j, h in enumerate(heads):
            o_ref[:, h * dh:(h + 1) * dh] = o_all[j * rows:(j + 1) * rows] / dens[j]


def _attn_prompt_kernel(sink_ref, q_ref, kp_ref, kc_ref, vp_ref, vc_ref, o_ref):
    n = pl.program_id(1)
    k_all = jnp.concatenate([kp_ref[...], kc_ref[...]], axis=0).astype(BF16)
    v_all = jnp.concatenate([vp_ref[...], vc_ref[...]], axis=0).astype(BF16)
    _attn_core(q_ref, k_all, v_all, sink_ref, o_ref, rows=SWA_WINDOW, first_block=(n == 0))


def _attn_sample_kernel(sink_ref, q_ref, k_ref, v_ref, o_ref, *, rows):
    _attn_core(q_ref.at[0], k_ref[0].astype(BF16), v_ref[0].astype(BF16), sink_ref, o_ref.at[0],
               rows=rows, first_block=None)


def _swa_layer(xt, cache_k, cache_v, w_qkv, w_o, sinks, g, b):
    w = SWA_WINDOW
    qkv = _matmul(xt, w_qkv.astype(BF16), tm=640, tn=1280)
    nb = SEQ // w
    k_col = D_MODEL // SWA_KV_DIM
    cur = lambda bb, n, *_: (bb * nb + n, k_col)
    prev = lambda bb, n, *_: (jnp.maximum(bb * nb + n - 1, 0), k_col)
    cur_v = lambda bb, n, *_: (bb * nb + n, k_col + 1)
    prev_v = lambda bb, n, *_: (jnp.maximum(bb * nb + n - 1, 0), k_col + 1)
    o = pl.pallas_call(
        _attn_prompt_kernel,
        grid_spec=pltpu.PrefetchScalarGridSpec(
            num_scalar_prefetch=1,
            grid=(BATCH, nb),
            in_specs=[pl.BlockSpec((w, D_MODEL), lambda bb, n, *_: (bb * nb + n, 0)),
                      pl.BlockSpec((w, SWA_KV_DIM), prev),
                      pl.BlockSpec((w, SWA_KV_DIM), cur),
                      pl.BlockSpec((w, SWA_KV_DIM), prev_v),
                      pl.BlockSpec((w, SWA_KV_DIM), cur_v)],
            out_specs=pl.BlockSpec((w, D_MODEL), lambda bb, n, *_: (bb * nb + n, 0))),
        out_shape=jax.ShapeDtypeStruct((N_TOK, D_MODEL), F32),
        compiler_params=_params("parallel", "arbitrary"),
        name="attn_prompt",
    )(sinks, qkv, qkv, qkv, qkv, qkv)

    rows = 16
    qkv_s = qkv[N_PROMPT_TOK:].reshape(DEC_BATCH, DEC_SEQ, -1)
    q_s = jnp.pad(qkv_s[..., :D_MODEL], ((0, 0), (0, rows - DEC_SEQ), (0, 0)))
    k_new = qkv_s[..., D_MODEL:D_MODEL + SWA_KV_DIM]
    v_new = qkv_s[..., D_MODEL + SWA_KV_DIM:]
    kz = jnp.concatenate([cache_k.reshape(DEC_BATCH, w, SWA_KV_DIM), k_new], axis=1)
    vz = jnp.concatenate([cache_v.reshape(DEC_BATCH, w, SWA_KV_DIM), v_new], axis=1)
    pad_keys = ((0, 0), (0, w - DEC_SEQ), (0, 0))
    o_s = pl.pallas_call(
        functools.partial(_attn_sample_kernel, rows=rows),
        grid_spec=pltpu.PrefetchScalarGridSpec(
            num_scalar_prefetch=1,
            grid=(DEC_BATCH,),
            in_specs=[pl.BlockSpec((1, rows, D_MODEL), lambda bb, *_: (bb, 0, 0)),
                      pl.BlockSpec((1, 2 * w, SWA_KV_DIM), lambda bb, *_: (bb, 0, 0)),
                      pl.BlockSpec((1, 2 * w, SWA_KV_DIM), lambda bb, *_: (bb, 0, 0))],
            out_specs=pl.BlockSpec((1, rows, D_MODEL), lambda bb, *_: (bb, 0, 0))),
        out_shape=jax.ShapeDtypeStruct((DEC_BATCH, rows, D_MODEL), F32),
        compiler_params=_params("parallel"),
        name="attn_sample",
    )(sinks, q_s, jnp.pad(kz, pad_keys), jnp.pad(vz, pad_keys))
    o = lax.dynamic_update_slice(o, o_s[:, :DEC_SEQ].reshape(N_SAMPLE_TOK, D_MODEL), (N_PROMPT_TOK, 0))

    out = _matmul_ln(o, w_o.astype(BF16), xt, g, b)

    kv_shape = (SWA_WINDOW, SWA_KV_HEADS, SWA_HEAD_DIM)
    kv_p = jnp.stack([qkv[(bb + 1) * SEQ - w:(bb + 1) * SEQ, D_MODEL:] for bb in range(BATCH)])
    new_k_p = kv_p[..., :SWA_KV_DIM].reshape((BATCH,) + kv_shape)
    new_v_p = kv_p[..., SWA_KV_DIM:].reshape((BATCH,) + kv_shape)
    new_k_s = kz[:, DEC_SEQ:].reshape((DEC_BATCH,) + kv_shape)
    new_v_s = vz[:, DEC_SEQ:].reshape((DEC_BATCH,) + kv_shape)
    return out, new_k_p, new_v_p, new_k_s, new_v_s


def _log_sigmoid(x):
    return jnp.minimum(x, 0.0) - jnp.log(1.0 + jnp.exp(-jnp.abs(x)))


def _mlstm_kernel(bias_ref, q_ref, k_ref, v_ref, og_ref, gates_ref, c0_ref, n0_ref, m0_ref, ng_ref,
                  h_ref, c_ref, n_ref, m_ref, c_scr, n_scr, m_scr, *, chunk, n_valid):
    step = pl.program_id(1)

    @pl.when(step == 0)
    def _():
        c_scr[...] = c0_ref[0]
        n_scr[...] = n0_ref[0]
        m_scr[...] = m0_ref[0]

    for head in range(MLSTM_HEADS):
        _mlstm_head(head, step, bias_ref, q_ref, k_ref, v_ref, og_ref, gates_ref, ng_ref, h_ref, c_scr, n_scr, m_scr,
                    chunk=chunk, n_valid=n_valid)

    @pl.when(step == pl.num_programs(1) - 1)
    def _():
        c_ref[0] = c_scr[...]
        n_ref[0] = n_scr[...]
        m_ref[0] = m_scr[...]


def _mlstm_head(head, step, bias_ref, q_ref, k_ref, v_ref, og_ref, gates_ref, ng_ref, h_ref, c_scr, n_scr, m_scr,
                *, chunk, n_valid):
    ln = chunk
    cols = slice(head * MLSTM_HEAD_DIM, (head + 1) * MLSTM_HEAD_DIM)
    ig = gates_ref[0, head, pl.ds(step, 1), :] + bias_ref[head]
    lf = _log_sigmoid(gates_ref[0, MLSTM_HEADS + head, pl.ds(step, 1), :] + bias_ref[MLSTM_HEADS + head])
    if n_valid < ln:
        col = lax.broadcasted_iota(jnp.int32, (1, ln), 1)
        ig = jnp.where(col < n_valid, ig, -1e30)
        lf = jnp.where(col < n_valid, lf, 0.0)

    ri = lax.broadcasted_iota(jnp.int32, (ln, ln), 0)
    ci = lax.broadcasted_iota(jnp.int32, (ln, ln), 1)
    eye = ri == ci
    causal = ci <= ri

    def to_col(row):
        return jnp.sum(jnp.where(eye, row, 0.0), axis=1, keepdims=True)

    f_col = to_col(lf)
    b_col = jnp.sum(jnp.where(causal, lf, 0.0), axis=1, keepdims=True)
    b_row = jnp.sum(jnp.where(ri <= ci, f_col, 0.0), axis=0, keepdims=True)
    b_last = jnp.sum(lf, axis=1, keepdims=True)
    m_prev = m_scr[head, :, 0:1]
    a_col = b_col + m_prev
    dmat = jnp.where(causal, b_col - b_row + ig, -jnp.inf)
    mt = jnp.maximum(a_col, jnp.max(dmat, axis=1, keepdims=True))
    w_inter = jnp.exp(a_col - mt)

    q = q_ref[:, cols]
    k = k_ref[:, cols] * (MLSTM_HEAD_DIM ** -0.5)
    v = v_ref[:, cols]
    qb = q.astype(BF16)
    kb = k.astype(BF16)
    c_prev = c_scr[head]
    n_prev = n_scr[head]
    scores = lax.dot_general(qb, kb, _NT, preferred_element_type=F32)
    w_intra = jnp.exp(dmat - mt) * scores
    num = (w_inter * lax.dot_general(qb, c_prev.astype(BF16), _NT, preferred_element_type=F32)
           + jnp.dot(w_intra.astype(BF16), v.astype(BF16), preferred_element_type=F32))
    den = (w_inter * jnp.sum(q * n_prev, axis=1, keepdims=True)
           + jnp.sum(w_intra, axis=1, keepdims=True))
    h = num / jnp.maximum(jnp.abs(den), jnp.exp(-mt))

    g_row = b_last - b_row + ig
    m_new = jnp.maximum(b_last + m_prev, jnp.max(g_row, axis=1, keepdims=True))
    decay = jnp.exp(b_last + m_prev - m_new)
    wg_col = to_col(jnp.exp(g_row - m_new))
    c_new = decay * c_prev + lax.dot_general((v * wg_col).astype(BF16), kb, _TN, preferred_element_type=F32)
    n_new = decay * n_prev + jnp.sum(wg_col * k, axis=0, keepdims=True)
    c_scr[head] = c_new
    n_scr[head] = n_new
    m_scr[head] = jnp.broadcast_to(m_new, (1, LANES))

    mu = jnp.mean(h, axis=1, keepdims=True)
    hc = h - mu
    var = jnp.mean(hc * hc, axis=1, keepdims=True)
    h_ref[:, cols] = hc * lax.rsqrt(var + LN_EPS) * ng_ref[:, cols] * jax.nn.sigmoid(og_ref[:, cols])


def _mlstm_scan(proj, gates, b_gates, norm_g, c0, n0, m0, *, batch, n_chunks, chunk, n_valid, out_rows):
    d, dh, nh = D_MODEL, MLSTM_HEAD_DIM, MLSTM_HEADS
    rows = lambda off: (lambda bb, cc, *_: (bb * n_chunks + cc, off))
    per_seq = lambda bb, cc, *_: (bb, 0, 0, 0)
    return pl.pallas_call(
        functools.partial(_mlstm_kernel, chunk=chunk, n_valid=n_valid),
        grid_spec=pltpu.PrefetchScalarGridSpec(
            num_scalar_prefetch=1,
            grid=(batch, n_chunks),
            in_specs=[pl.BlockSpec((chunk, d), rows(0)),
                      pl.BlockSpec((chunk, d), rows(1)),
                      pl.BlockSpec((chunk, d), rows(2)),
                      pl.BlockSpec((chunk, d), rows(3)),
                      pl.BlockSpec((1, 2 * nh, n_chunks, chunk), per_seq),
                      pl.BlockSpec((1, nh, dh, dh), per_seq),
                      pl.BlockSpec((1, nh, 1, dh), per_seq),
                      pl.BlockSpec((1, nh, 1, LANES), per_seq),
                      pl.BlockSpec((1, d), lambda bb, cc, *_: (0, 0))],
            out_specs=[pl.BlockSpec((chunk, d), lambda bb, cc, *_: (bb * n_chunks + cc, 0)),
                       pl.BlockSpec((1, nh, dh, dh), per_seq),
                       pl.BlockSpec((1, nh, 1, dh), per_seq),
                       pl.BlockSpec((1, nh, 1, LANES), per_seq)],
            scratch_shapes=[pltpu.VMEM((nh, dh, dh), F32), pltpu.VMEM((nh, 1, dh), F32),
                            pltpu.VMEM((nh, 1, LANES), F32)]),
        out_shape=[jax.ShapeDtypeStruct((out_rows, d), F32),
                   jax.ShapeDtypeStruct((batch, nh, dh, dh), F32),
                   jax.ShapeDtypeStruct((batch, nh, 1, dh), F32),
                   jax.ShapeDtypeStruct((batch, nh, 1, LANES), F32)],
        compiler_params=_params("parallel", "arbitrary"),
        name="mlstm_scan",
    )(b_gates, proj, proj, proj, proj, gates, c0, n0[:, :, None, :],
      jnp.broadcast_to(m0[:, :, None, None], (batch, nh, 1, LANES)), norm_g)


def _mlstm_layer(xt, c0_s, n0_s, m0_s, w_in, b_gates, norm_g, w_out, g, b):
    d, nh, dh = D_MODEL, MLSTM_HEADS, MLSTM_HEAD_DIM
    proj = _matmul(xt, w_in[:, :4 * d].astype(BF16), tm=640, tn=1024)
    w_gates = jnp.pad(w_in[:, 4 * d:], ((0, 0), (0, 128 - 2 * nh)))
    gate_pre = _matmul(xt, w_gates.astype(BF16), tm=640, tn=128)[:, :2 * nh]

    nc = SEQ // MLSTM_CHUNK
    gates_p = gate_pre[:N_PROMPT_TOK].reshape(BATCH, nc, MLSTM_CHUNK, 2 * nh).transpose(0, 3, 1, 2)
    zeros = lambda *s: jnp.zeros(s, F32)
    hn, c_p, n_p, m_p = _mlstm_scan(
        proj, gates_p, b_gates, norm_g, zeros(BATCH, nh, dh, dh), zeros(BATCH, nh, dh), zeros(BATCH, nh),
        batch=BATCH, n_chunks=nc, chunk=MLSTM_CHUNK, n_valid=MLSTM_CHUNK, out_rows=N_TOK)

    ls = MLSTM_SAMPLE_CHUNK
    pad_t = ((0, 0), (0, ls - DEC_SEQ), (0, 0))
    proj_s = jnp.pad(proj[N_PROMPT_TOK:].reshape(DEC_BATCH, DEC_SEQ, 4 * d), pad_t).reshape(DEC_BATCH * ls, 4 * d)
    gates_s = jnp.pad(gate_pre[N_PROMPT_TOK:].reshape(DEC_BATCH, DEC_SEQ, 2 * nh), pad_t)
    gates_s = gates_s.transpose(0, 2, 1)[:, :, None, :]
    hn_s, c_s, n_s, m_s = _mlstm_scan(
        proj_s, gates_s, b_gates, norm_g, c0_s, n0_s, m0_s,
        batch=DEC_BATCH, n_chunks=1, chunk=ls, n_valid=DEC_SEQ, out_rows=DEC_BATCH * ls)
    hn_s = hn_s.reshape(DEC_BATCH, ls, d)[:, :DEC_SEQ].reshape(N_SAMPLE_TOK, d)
    hn = lax.dynamic_update_slice(hn, hn_s, (N_PROMPT_TOK, 0))

    out = _matmul_ln(hn, w_out.astype(BF16), xt, g, b)
    return (out, c_p, n_p[:, :, 0], m_p[:, :, 0, 0], c_s, n_s[:, :, 0], m_s[:, :, 0, 0])


def _router_kernel(x_ref, wt_ref, bias_ref, tri_ref, eidx_ref, gate_ref, rank_ref, count_ref, xq_ref, count_scr,
                   *, tt):
    neg = -jnp.inf
    x = x_ref[...]
    logits = lax.dot_general(wt_ref[...].astype(BF16), x.astype(BF16), _NT, preferred_element_type=F32)
    s = jax.nn.sigmoid(logits)
    sb = s + bias_ref[...]
    per_group = N_EXPERTS // N_EXPERT_GROUPS

    sb3 = sb.reshape(N_EXPERT_GROUPS, per_group, tt)
    i3 = lax.broadcasted_iota(jnp.int32, sb3.shape, 1)
    m1 = jnp.max(sb3, axis=1, keepdims=True)
    first = jnp.min(jnp.where(sb3 == m1, i3, per_group), axis=1, keepdims=True)
    m2 = jnp.max(jnp.where(i3 == first, neg, sb3), axis=1, keepdims=True)
    gscore = (m1 + m2).reshape(N_EXPERT_GROUPS, tt)

    gi = lax.broadcasted_iota(jnp.int32, gscore.shape, 0)
    gsel = jnp.zeros(gscore.shape, jnp.bool_)
    cur = gscore
    for _ in range(TOPK_GROUPS):
        mx = jnp.max(cur, axis=0, keepdims=True)
        pick = gi == jnp.min(jnp.where(cur == mx, gi, N_EXPERT_GROUPS), axis=0, keepdims=True)
        gsel = gsel | pick
        cur = jnp.where(pick, neg, cur)
    emask = jnp.broadcast_to(gsel.reshape(N_EXPERT_GROUPS, 1, tt), sb3.shape).reshape(N_EXPERTS, tt)

    ei = lax.broadcasted_iota(jnp.int32, sb.shape, 0)
    sel = jnp.zeros(sb.shape, jnp.bool_)
    cur = jnp.where(emask, sb, neg)
    picks = []
    for _ in range(TOP_K):
        mx = jnp.max(cur, axis=0, keepdims=True)
        idx = jnp.min(jnp.where(cur == mx, ei, N_EXPERTS), axis=0, keepdims=True)
        pick = ei == idx
        picks.append((idx, pick))
        sel = sel | pick
        cur = jnp.where(pick, neg, cur)
    s_sel = jnp.where(sel, s, 0.0)
    gate = s_sel / jnp.sum(s_sel, axis=0, keepdims=True) * ROUTED_SCALE

    @pl.when(pl.program_id(0) == 0)
    def _():
        count_scr[...] = jnp.zeros_like(count_scr)

    sel_f = sel.astype(F32)
    incl = jnp.dot(sel_f.astype(BF16), tri_ref[...], preferred_element_type=F32)
    rank = count_scr[:, 0:1] + incl - sel_f
    count_scr[...] = count_scr[...] + jnp.sum(sel_f, axis=1, keepdims=True)
    count_ref[...] = count_scr[...].astype(jnp.int32)

    for kk, (idx, pick) in enumerate(picks):
        eidx_ref[kk:kk + 1, :] = idx
        gate_ref[kk:kk + 1, :] = jnp.sum(jnp.where(pick, gate, 0.0), axis=0, keepdims=True)
        rank_ref[kk:kk + 1, :] = jnp.sum(jnp.where(pick, rank, 0.0), axis=0, keepdims=True).astype(jnp.int32)

    for sl in range(PACK_ROWS):
        c0 = 2 * LANES * sl
        xq_ref[pl.ds(sl, tt, stride=PACK_ROWS), :] = _pack_bf16_pair(x[:, c0:c0 + LANES],
                                                                     x[:, c0 + LANES:c0 + 2 * LANES])


def _pack_bf16_pair(hi, lo):
    hb = lax.bitcast_convert_type(hi.astype(BF16).astype(F32), jnp.uint32)
    lb = lax.bitcast_convert_type(lo.astype(BF16).astype(F32), jnp.uint32)
    return hb | (lb >> 16)


def _unpack_bf16_pair(word):
    hi = lax.bitcast_convert_type(word & jnp.uint32(0xFFFF0000), F32)
    lo = lax.bitcast_convert_type(word << 16, F32)
    return hi, lo


def _moe_route(xt, w_router, bias):
    tt = ROUTER_TILE
    col = lambda i: (0, i)
    const = lambda i: (0, 0)
    tri = (jnp.arange(tt)[:, None] <= jnp.arange(tt)[None, :]).astype(BF16)
    return pl.pallas_call(
        functools.partial(_router_kernel, tt=tt),
        grid=(N_TOK // tt,),
        in_specs=[pl.BlockSpec((tt, D_MODEL), lambda i: (i, 0)),
                  pl.BlockSpec((N_EXPERTS, D_MODEL), const),
                  pl.BlockSpec((N_EXPERTS, 1), const),
                  pl.BlockSpec((tt, tt), const)],
        out_specs=[pl.BlockSpec((TOP_K, tt), col), pl.BlockSpec((TOP_K, tt), col), pl.BlockSpec((TOP_K, tt), col),
                   pl.BlockSpec((N_EXPERTS, LANES), const),
                   pl.BlockSpec((tt * PACK_ROWS, LANES), lambda i: (i, 0))],
        out_shape=[jax.ShapeDtypeStruct((TOP_K, N_TOK), jnp.int32),
                   jax.ShapeDtypeStruct((TOP_K, N_TOK), F32),
                   jax.ShapeDtypeStruct((TOP_K, N_TOK), jnp.int32),
                   jax.ShapeDtypeStruct((N_EXPERTS, LANES), jnp.int32),
                   jax.ShapeDtypeStruct((N_TOK * PACK_ROWS, LANES), jnp.uint32)],
        scratch_shapes=[pltpu.VMEM((N_EXPERTS, LANES), F32)],
        compiler_params=_params("arbitrary"),
        name="moe_router",
    )(xt, w_router.T, bias[:, None], tri)


def _sorted_row(pstart_ref, eidx_ref, rank_ref, j):
    return pstart_ref[eidx_ref[0, 0, j]] + rank_ref[0, 0, j]


def _dispatch_kernel(pstart_ref, eidx_ref, rank_ref, xq_ref, xs_hbm, row_ref, sem, *, tt):
    def issue(t, carry):
        src = xq_ref.at[pl.ds(pl.multiple_of(t * PACK_ROWS, PACK_ROWS), PACK_ROWS)]
        for kk in range(TOP_K):
            row = _sorted_row(pstart_ref, eidx_ref, rank_ref, t * TOP_K + kk)
            row_ref[0, 0, t * TOP_K + kk] = row
            row = pl.multiple_of(row * PACK_ROWS, PACK_ROWS)
            pltpu.make_async_copy(src, xs_hbm.at[pl.ds(row, PACK_ROWS)], sem).start()
        return carry
    lax.fori_loop(0, tt, issue, 0)
    for _ in range(TOP_K):
        pltpu.make_async_copy(xq_ref, xs_hbm.at[pl.ds(0, tt * PACK_ROWS)], sem).wait()


def _moe_dispatch(xq, pstarts, eidx_tok, rank_tok):
    tt = DISPATCH_TILE
    n_tiles = N_TOK // tt
    slots = pl.BlockSpec((1, 1, tt * TOP_K), lambda i, *_: (i, 0, 0), memory_space=pltpu.SMEM)
    return pl.pallas_call(
        functools.partial(_dispatch_kernel, tt=tt),
        grid_spec=pltpu.PrefetchScalarGridSpec(
            num_scalar_prefetch=1,
            grid=(n_tiles,),
            in_specs=[slots, slots, pl.BlockSpec((tt * PACK_ROWS, LANES), lambda i, *_: (i, 0))],
            out_specs=[pl.BlockSpec(memory_space=pl.ANY), slots],
            scratch_shapes=[pltpu.SemaphoreType.DMA(())]),
        out_shape=[jax.ShapeDtypeStruct((MOE_ROWS * PACK_ROWS, LANES), jnp.uint32),
                   jax.ShapeDtypeStruct((n_tiles, 1, tt * TOP_K), jnp.int32)],
        compiler_params=_params("arbitrary"),
        name="moe_dispatch",
    )(pstarts, eidx_tok.reshape(n_tiles, 1, tt * TOP_K), rank_tok.reshape(n_tiles, 1, tt * TOP_K), xq)


def _experts_kernel(be_ref, next_ref, nvalid_ref, nused_ref, xs_ref, wg_hbm, wu_hbm, wd_hbm, y_ref,
                    wg_f, wu_f, wd_f, wgu_s, wd_s, sem, *, layer):
    i = pl.program_id(0)
    tm = MOE_BLOCK

    def weight_copies(e):
        return (pltpu.make_async_copy(wg_hbm.at[layer, e], wg_f, sem.at[0]),
                pltpu.make_async_copy(wu_hbm.at[layer, e], wu_f, sem.at[1]),
                pltpu.make_async_copy(wd_hbm.at[layer, e], wd_f, sem.at[2]))

    def swiglu_rows(n_rows):
        parts = []
        for sl in range(PACK_ROWS):
            parts.extend(_unpack_bf16_pair(xs_ref[pl.ds(sl, n_rows, stride=PACK_ROWS), :]))
        x = jnp.concatenate(parts, axis=1).astype(BF16)
        gate_up = jnp.dot(x, wgu_s[...], preferred_element_type=F32)
        hidden = _silu(gate_up[:, :EXPERT_FF]) * gate_up[:, EXPERT_FF:]
        y = jnp.dot(hidden.astype(BF16), wd_s[...], preferred_element_type=F32)
        for sl in range(PACK_ROWS):
            c0 = 2 * LANES * sl
            y_ref[pl.ds(sl, n_rows, stride=PACK_ROWS), :] = _pack_bf16_pair(y[:, c0:c0 + LANES],
                                                                            y[:, c0 + LANES:c0 + 2 * LANES])

    @pl.when(i == 0)
    def _():
        for cp in weight_copies(be_ref[0]):
            cp.start()

    @pl.when(i < nused_ref[0])
    def _():
        first_of_expert = (i == 0) | (be_ref[i] != be_ref[jnp.maximum(i - 1, 0)])

        @pl.when(first_of_expert)
        def _():
            for cp in weight_copies(be_ref[i]):
                cp.wait()
            wgu_s[:, :EXPERT_FF] = wg_f[...].astype(BF16)
            wgu_s[:, EXPERT_FF:] = wu_f[...].astype(BF16)
            wd_s[...] = wd_f[...].astype(BF16)

            @pl.when(next_ref[i] >= 0)
            def _():
                for cp in weight_copies(next_ref[i]):
                    cp.start()

        @pl.when(nvalid_ref[i] > tm // 2)
        def _():
            swiglu_rows(tm)

        @pl.when(nvalid_ref[i] <= tm // 2)
        def _():
            swiglu_rows(tm // 2)


def _moe_experts(xs, block_e, next_e, n_valid, n_used, layer, w_gate, w_up, w_down):
    last = lambda i, be, nx, nv, nu: (jnp.minimum(i, nu[0] - 1), 0)
    return pl.pallas_call(
        functools.partial(_experts_kernel, layer=layer),
        grid_spec=pltpu.PrefetchScalarGridSpec(
            num_scalar_prefetch=4,
            grid=(MOE_N_BLOCKS,),
            in_specs=[pl.BlockSpec((MOE_BLOCK * PACK_ROWS, LANES), last),
                      pl.BlockSpec(memory_space=pl.ANY),
                      pl.BlockSpec(memory_space=pl.ANY),
                      pl.BlockSpec(memory_space=pl.ANY)],
            out_specs=pl.BlockSpec((MOE_BLOCK * PACK_ROWS, LANES), last),
            scratch_shapes=[pltpu.VMEM((D_MODEL, EXPERT_FF), F32), pltpu.VMEM((D_MODEL, EXPERT_FF), F32),
                            pltpu.VMEM((EXPERT_FF, D_MODEL), F32),
                            pltpu.VMEM((D_MODEL, 2 * EXPERT_FF), BF16),
                            pltpu.VMEM((EXPERT_FF, D_MODEL), BF16),
                            pltpu.SemaphoreType.DMA((3,))]),
        out_shape=jax.ShapeDtypeStruct((MOE_ROWS * PACK_ROWS, LANES), jnp.uint32),
        compiler_params=_params("arbitrary"),
        name="moe_experts",
    )(block_e, next_e, n_valid, n_used, xs, w_gate, w_up, w_down)


def _combine_kernel(row_ref, row_next_ref, x_ref, gate_ref, swg_ref, swu_ref, swd_ref, g_ref, b_ref, y_hbm, *rest,
                    tt, n_first):
    out_refs, (buf_a, buf_b, acc, sem) = rest[:-4], rest[-4:]
    i = pl.program_id(0)
    last = pl.num_programs(0) - 1

    def start_fetch(rows, into, into_sem, t):
        dst = pl.ds(t * PACK_ROWS, PACK_ROWS)
        for kk in range(TOP_K):
            src_row = pl.multiple_of(rows[0, 0, t * TOP_K + kk] * PACK_ROWS, PACK_ROWS)
            pltpu.make_async_copy(y_hbm.at[pl.ds(src_row, PACK_ROWS)], into.at[kk, dst], into_sem).start()

    def wait_fetch(into, into_sem):
        for kk in range(TOP_K):
            pltpu.make_async_copy(y_hbm.at[pl.ds(0, tt * PACK_ROWS)], into.at[kk], into_sem).wait()

    @pl.when(i == 0)
    def _():
        def body(t, carry):
            dst = pl.ds(pl.multiple_of(t * PACK_ROWS, PACK_ROWS), PACK_ROWS)
            for kk in range(TOP_K):
                src_row = pl.multiple_of(row_ref[0, 0, t * TOP_K + kk] * PACK_ROWS, PACK_ROWS)
                pltpu.make_async_copy(y_hbm.at[pl.ds(src_row, PACK_ROWS)], buf_a.at[kk, dst], sem.at[0]).start()
            return carry
        lax.fori_loop(0, tt, body, 0)

    def tile(cur, cur_sem, nxt, nxt_sem):
        per_phase = tt // (PACK_ROWS + 2)
        tokens = iter(range(tt))

        def issue(n):
            for _ in range(n):
                t = next(tokens, None)
                if t is not None:
                    start_fetch(row_next_ref, nxt, nxt_sem, t)

        x = x_ref[...]
        xb = x.astype(BF16)
        issue(per_phase)
        hidden = (_silu(jnp.dot(xb, swg_ref[...], preferred_element_type=F32))
                  * jnp.dot(xb, swu_ref[...], preferred_element_type=F32))
        issue(per_phase)
        shared = jnp.dot(hidden.astype(BF16), swd_ref[...], preferred_element_type=F32)
        wait_fetch(cur, cur_sem)
        gate = gate_ref[...]
        for sl in range(PACK_ROWS):
            issue(per_phase if sl < PACK_ROWS - 1 else tt)
            routed_hi = routed_lo = None
            for kk in range(TOP_K):
                hi, lo = _unpack_bf16_pair(cur[kk, pl.ds(sl, tt, stride=PACK_ROWS), :])
                g_k = gate[:, kk:kk + 1]
                routed_hi = g_k * hi if kk == 0 else routed_hi + g_k * hi
                routed_lo = g_k * lo if kk == 0 else routed_lo + g_k * lo
            for half, routed in enumerate((routed_hi, routed_lo)):
                cols = slice((2 * sl + half) * LANES, (2 * sl + half + 1) * LANES)
                acc[:, cols] = DN_ALPHA * x[:, cols] + (routed + shared[:, cols])
        res = _layer_norm(acc[...], g_ref[...], b_ref[...])
        if n_first is None:
            out_refs[0][...] = res
        else:
            @pl.when(i < n_first)
            def _():
                out_refs[0][...] = res

            @pl.when(i >= n_first)
            def _():
                out_refs[1][...] = res

        @pl.when(i == last)
        def _():
            wait_fetch(nxt, nxt_sem)

    @pl.when(i % 2 == 0)
    def _():
        tile(buf_a, sem.at[0], buf_b, sem.at[1])

    @pl.when(i % 2 == 1)
    def _():
        tile(buf_b, sem.at[1], buf_a, sem.at[0])


def _moe_combine(xt, ys, row_tiles, gate_tok, sw_gate, sw_up, sw_down, g, b, *, split):
    tt = COMBINE_TILE
    n_tiles = N_TOK // tt
    row = lambda i: (i, 0)
    const = lambda i: (0, 0)
    slots = pl.BlockSpec((1, 1, tt * TOP_K), lambda i: (i, 0, 0), memory_space=pltpu.SMEM)
    slots_next = pl.BlockSpec((1, 1, tt * TOP_K), lambda i: (jnp.minimum(i + 1, n_tiles - 1), 0, 0),
                              memory_space=pltpu.SMEM)
    row_tiles = row_tiles.reshape(n_tiles, 1, tt * TOP_K)
    if split:
        n_first = N_PROMPT_TOK // tt
        assert N_SAMPLE_TOK == tt
        out_specs = [pl.BlockSpec((tt, D_MODEL), lambda i: (jnp.minimum(i, n_first - 1), 0)),
                     pl.BlockSpec((tt, D_MODEL), const)]
        out_shape = [jax.ShapeDtypeStruct((N_PROMPT_TOK, D_MODEL), F32),
                     jax.ShapeDtypeStruct((N_SAMPLE_TOK, D_MODEL), F32)]
    else:
        n_first = None
        out_specs = [pl.BlockSpec((tt, D_MODEL), row)]
        out_shape = [jax.ShapeDtypeStruct((N_TOK, D_MODEL), F32)]
    return pl.pallas_call(
        functools.partial(_combine_kernel, tt=tt, n_first=n_first),
        grid=(n_tiles,),
        in_specs=[slots, slots_next,
                  pl.BlockSpec((tt, D_MODEL), row),
                  pl.BlockSpec((tt, TOP_K), row),
                  pl.BlockSpec((D_MODEL, EXPERT_FF), const),
                  pl.BlockSpec((D_MODEL, EXPERT_FF), const),
                  pl.BlockSpec((EXPERT_FF, D_MODEL), const),
                  pl.BlockSpec((1, D_MODEL), const),
                  pl.BlockSpec((1, D_MODEL), const),
                  pl.BlockSpec(memory_space=pl.ANY)],
        out_specs=out_specs,
        scratch_shapes=[pltpu.VMEM((TOP_K, tt * PACK_ROWS, LANES), jnp.uint32),
                        pltpu.VMEM((TOP_K, tt * PACK_ROWS, LANES), jnp.uint32),
                        pltpu.VMEM((tt, D_MODEL), F32), pltpu.SemaphoreType.DMA((2,))],
        out_shape=out_shape,
        compiler_params=_params("arbitrary"),
        name="moe_combine",
    )(row_tiles, row_tiles, xt, gate_tok, sw_gate.astype(BF16), sw_up.astype(BF16), sw_down.astype(BF16), g, b, ys)


def _moe_layer(xt, layer, w_router, bias, w_gate, w_up, w_down, sw_gate, sw_up, sw_down, g, b, *, split):
    eidx, gate, rank, counts, xq = _moe_route(xt, w_router, bias)

    counts = counts[:, 0]
    pcounts = (counts + MOE_BLOCK - 1) // MOE_BLOCK * MOE_BLOCK
    pends = jnp.cumsum(pcounts)
    pstarts = (pends - pcounts).astype(jnp.int32)
    n_used = (pends[-1] // MOE_BLOCK).astype(jnp.int32)
    blk = jnp.minimum(jnp.arange(MOE_N_BLOCKS, dtype=jnp.int32), n_used - 1) * MOE_BLOCK
    block_e = jnp.sum((pends[None, :] <= blk[:, None]).astype(jnp.int32), axis=1)
    block_e = jnp.minimum(block_e, N_EXPERTS - 1)
    after = pends[block_e] // MOE_BLOCK
    next_e = jnp.where(after < n_used, block_e[jnp.minimum(after, MOE_N_BLOCKS - 1)], -1).astype(jnp.int32)

    n_valid = jnp.clip(counts[block_e] - (blk - pstarts[block_e]), 0, MOE_BLOCK).astype(jnp.int32)

    eidx_tok, rank_tok = eidx.T, rank.T
    xs, rows = _moe_dispatch(xq, pstarts, eidx_tok, rank_tok)
    ys = _moe_experts(xs, block_e, next_e, n_valid, n_used[None], layer, w_gate, w_up, w_down)
    return _moe_combine(xt, ys, rows, gate.T, sw_gate, sw_up, sw_down, g, b, split=split)


def kernel(x_prompt, x_sample, state_pool, cache_swa_k, cache_swa_v, state_mlstm_c, state_mlstm_n, state_mlstm_m, pool_w, pool_scale, swa_w_qkv, swa_w_o, swa_sinks, mlstm_w_in, mlstm_b_gates, mlstm_norm_g, mlstm_w_out, ln_g, ln_b, moe_w_router, moe_router_bias, moe_w_gate, moe_w_up, moe_w_down, moe_shared_w_gate, moe_shared_w_up, moe_shared_w_down):
    d = D_MODEL
    xt = None
    pool_p, pool_s = [], []
    swk_p, swv_p, swk_s, swv_s = [], [], [], []
    mc_p, mn_p, mm_p, mc_s, mn_s, mm_s = [], [], [], [], [], []
    for i in range(DEPTH):
        kind, slot = i % N_MIXERS, i // N_MIXERS
        g0, b0 = ln_g[i, 0][None], ln_b[i, 0][None]
        if kind == 0:
            if i == 0:
                xp, xs = x_prompt.reshape(N_PROMPT_TOK, d), x_sample
            else:
                xp, xs = xt, xt[N_PROMPT_TOK:].reshape(DEC_BATCH, DEC_SEQ, d)
            xt, sp, ss = _pool_layer(xp, xs, state_pool[slot], pool_w[slot], pool_scale[slot][None], g0, b0)
            pool_p.append(sp)
            pool_s.append(ss)
        elif kind == 1:
            xt, kp, vp, ks, vs = _swa_layer(xt, cache_swa_k[slot], cache_swa_v[slot], swa_w_qkv[slot],
                                            swa_w_o[slot], swa_sinks[slot], g0, b0)
            swk_p.append(kp)
            swv_p.append(vp)
            swk_s.append(ks)
            swv_s.append(vs)
        else:
            xt, cp, np_, mp, cs, ns, ms = _mlstm_layer(
                xt, state_mlstm_c[slot], state_mlstm_n[slot], state_mlstm_m[slot], mlstm_w_in[slot],
                mlstm_b_gates[slot], mlstm_norm_g[slot][None], mlstm_w_out[slot], g0, b0)
            mc_p.append(cp)
            mn_p.append(np_)
            mm_p.append(mp)
            mc_s.append(cs)
            mn_s.append(ns)
            mm_s.append(ms)
        xt = _moe_layer(xt, i, moe_w_router[i], moe_router_bias[i], moe_w_gate, moe_w_up, moe_w_down,
                        moe_shared_w_gate[i], moe_shared_w_up[i], moe_shared_w_down[i],
                        ln_g[i, 1][None], ln_b[i, 1][None], split=(i == DEPTH - 1))
        if i < DEPTH - 1:
            xt = xt[0]
    y_p = xt[0].reshape(BATCH, SEQ, d)
    y_s = xt[1].reshape(DEC_BATCH, DEC_SEQ, d)
    return (y_p, y_s, jnp.stack(pool_p), jnp.stack(pool_s), jnp.stack(swk_p), jnp.stack(swv_p),
            jnp.stack(swk_s), jnp.stack(swv_s), jnp.stack(mc_p), jnp.stack(mn_p), jnp.stack(mm_p),
            jnp.stack(mc_s), jnp.stack(mn_s), jnp.stack(mm_s))
```

```python
import functools
import math

import jax
import jax.numpy as jnp
from jax import lax
from jax.experimental import pallas as pl
from jax.experimental.pallas import tpu as pltpu

F32 = jnp.float32
BF16 = jnp.bfloat16

D_MODEL = 2048
BATCH = 2
SEQ = 4096
DEPTH = 4
DEC_BATCH = 32
DEC_SEQ = 4
PAST_LEN = 16384
N_PROMPT_TOK = BATCH * SEQ
N_SAMPLE_TOK = DEC_BATCH * DEC_SEQ
N_TOK = N_PROMPT_TOK + N_SAMPLE_TOK

N_MIXERS = 3
DN_ALPHA = (2.0 * DEPTH) ** 0.25
LN_EPS = 1e-5

POOL_WINDOWS = (2, 4, 8, 16)
POOL_GROUP_DIM = D_MODEL // len(POOL_WINDOWS)
POOL_STATE = max(POOL_WINDOWS) - 1
POOL_HALO = POOL_STATE + 1

SWA_WINDOW = 128
SWA_HEAD_DIM = 64
SWA_HEADS = D_MODEL // SWA_HEAD_DIM
SWA_KV_HEADS = SWA_HEADS // 8
SWA_GROUP = SWA_HEADS // SWA_KV_HEADS
SWA_KV_DIM = SWA_KV_HEADS * SWA_HEAD_DIM

MLSTM_HEADS = 8
MLSTM_HEAD_DIM = D_MODEL // MLSTM_HEADS
MLSTM_CHUNK = 64
MLSTM_SAMPLE_CHUNK = 8

N_EXPERTS = 64
TOP_K = 8
N_EXPERT_GROUPS = 8
TOPK_GROUPS = 4
EXPERT_FF = D_MODEL // 4
ROUTED_SCALE = 2.5

VMEM_LIMIT_BYTES = 56 * 1024 * 1024

ROW_TILE = 320
ROUTER_TILE = 640
POOL_TILE = 512
MOE_BLOCK = 512
MOE_N_BLOCKS = N_TOK * TOP_K // MOE_BLOCK + N_EXPERTS
MOE_ROWS = MOE_N_BLOCKS * MOE_BLOCK
DISPATCH_TILE = 320
COMBINE_TILE = 128
LANES = 128
PACK_ROWS = D_MODEL // (2 * LANES)

_NT = (((1,), (1,)), ((), ()))
_TN = (((0,), (0,)), ((), ()))


def _params(*semantics):
    return pltpu.CompilerParams(dimension_semantics=semantics, vmem_limit_bytes=VMEM_LIMIT_BYTES)


def _layer_norm(z, g, b):
    mu = jnp.mean(z, axis=-1, keepdims=True)
    zc = z - mu
    var = jnp.mean(zc * zc, axis=-1, keepdims=True)
    return zc * lax.rsqrt(var + LN_EPS) * g + b


def _silu(x):
    return x * jax.nn.sigmoid(x)


def _matmul_kernel(x_ref, w_ref, o_ref):
    o_ref[...] = jnp.dot(x_ref[...].astype(BF16), w_ref[...], preferred_element_type=F32)


def _matmul(x, w, *, tm, tn):
    m, k = x.shape
    n = w.shape[1]
    return pl.pallas_call(
        _matmul_kernel,
        grid=(n // tn, m // tm),
        in_specs=[pl.BlockSpec((tm, k), lambda j, i: (i, 0)),
                  pl.BlockSpec((k, tn), lambda j, i: (0, j))],
        out_specs=pl.BlockSpec((tm, tn), lambda j, i: (i, j)),
        out_shape=jax.ShapeDtypeStruct((m, n), F32),
        compiler_params=_params("parallel", "parallel"),
        name="matmul",
    )(x, w)


def _matmul_ln_kernel(a_ref, w_ref, res_ref, g_ref, b_ref, o_ref):
    y = jnp.dot(a_ref[...].astype(BF16), w_ref[...], preferred_element_type=F32)
    o_ref[...] = _layer_norm(DN_ALPHA * res_ref[...] + y, g_ref[...], b_ref[...])


def _matmul_ln(a, w, res, g, b):
    m, k = a.shape
    row = lambda i: (i, 0)
    const = lambda i: (0, 0)
    return pl.pallas_call(
        _matmul_ln_kernel,
        grid=(m // ROW_TILE,),
        in_specs=[pl.BlockSpec((ROW_TILE, k), row),
                  pl.BlockSpec((k, D_MODEL), const),
                  pl.BlockSpec((ROW_TILE, D_MODEL), row),
                  pl.BlockSpec((1, D_MODEL), const),
                  pl.BlockSpec((1, D_MODEL), const)],
        out_specs=pl.BlockSpec((ROW_TILE, D_MODEL), row),
        out_shape=jax.ShapeDtypeStruct((m, D_MODEL), F32),
        compiler_params=_params("parallel"),
        name="matmul_ln",
    )(a, w, res, g, b)


def _pool_core(zbuf, w_ref, scale_ref, g_ref, b_ref, o_ref, *, tt, n_before):
    avail = n_before + lax.broadcasted_iota(jnp.int32, (tt, 1), 0) + 1
    for grp, win in enumerate(POOL_WINDOWS):
        c0, c1 = grp * POOL_GROUP_DIM, (grp + 1) * POOL_GROUP_DIM
        xg = zbuf[POOL_HALO:POOL_HALO + tt, c0:c1]
        total = xg
        for back in range(1, win):
            total = total + zbuf[POOL_HALO - back:POOL_HALO - back + tt, c0:c1]
        count = jnp.minimum(win, avail).astype(F32)
        diff = total / count - xg
        y = jnp.dot(diff.astype(BF16), w_ref[grp], preferred_element_type=F32) * scale_ref[:, c0:c1]
        o_ref[:, c0:c1] = DN_ALPHA * xg + y
    o_ref[...] = _layer_norm(o_ref[...], g_ref[...], b_ref[...])


def _pool_prompt_kernel(x_ref, halo_ref, w_ref, scale_ref, g_ref, b_ref, o_ref, zbuf, *, tt):
    i = pl.program_id(1)
    zbuf[0:POOL_HALO, :] = jnp.where(i == 0, 0.0, halo_ref[...])
    zbuf[POOL_HALO:POOL_HALO + tt, :] = x_ref[...]
    _pool_core(zbuf, w_ref, scale_ref, g_ref, b_ref, o_ref, tt=tt, n_before=i * tt)


def _pool_sample_kernel(z_ref, w_ref, scale_ref, g_ref, b_ref, o_ref, *, tt):
    _pool_core(z_ref.at[0], w_ref, scale_ref, g_ref, b_ref, o_ref.at[0], tt=tt, n_before=PAST_LEN)


def _pool_layer(xp, xs, state, w, scale, g, b):
    tt = POOL_TILE
    tiles = SEQ // tt
    halo_per_tile = tt // POOL_HALO
    const2 = lambda bb, i: (0, 0)
    w_bf = w.astype(BF16)
    out = pl.pallas_call(
        functools.partial(_pool_prompt_kernel, tt=tt),
        grid=(BATCH, tiles),
        in_specs=[pl.BlockSpec((tt, D_MODEL), lambda bb, i: (bb * tiles + i, 0)),
                  pl.BlockSpec((POOL_HALO, D_MODEL),
                               lambda bb, i: (jnp.maximum((bb * tiles + i) * halo_per_tile - 1, 0), 0)),
                  pl.BlockSpec(w_bf.shape, lambda bb, i: (0, 0, 0)),
                  pl.BlockSpec((1, D_MODEL), const2),
                  pl.BlockSpec((1, D_MODEL), const2),
                  pl.BlockSpec((1, D_MODEL), const2)],
        out_specs=pl.BlockSpec((tt, D_MODEL), lambda bb, i: (bb * tiles + i, 0)),
        out_shape=jax.ShapeDtypeStruct((N_TOK, D_MODEL), F32),
        scratch_shapes=[pltpu.VMEM((POOL_HALO + tt, D_MODEL), F32)],
        compiler_params=_params("parallel", "arbitrary"),
        name="pool_prompt",
    )(xp, xp, w_bf, scale, g, b)

    ts = 16
    zs = jnp.concatenate([jnp.zeros((DEC_BATCH, 1, D_MODEL), F32), state, xs,
                          jnp.zeros((DEC_BATCH, ts - DEC_SEQ, D_MODEL), F32)], axis=1)
    const1 = lambda bb: (0, 0)
    out_s = pl.pallas_call(
        functools.partial(_pool_sample_kernel, tt=ts),
        grid=(DEC_BATCH,),
        in_specs=[pl.BlockSpec((1, POOL_HALO + ts, D_MODEL), lambda bb: (bb, 0, 0)),
                  pl.BlockSpec(w_bf.shape, lambda bb: (0, 0, 0)),
                  pl.BlockSpec((1, D_MODEL), const1),
                  pl.BlockSpec((1, D_MODEL), const1),
                  pl.BlockSpec((1, D_MODEL), const1)],
        out_specs=pl.BlockSpec((1, ts, D_MODEL), lambda bb: (bb, 0, 0)),
        out_shape=jax.ShapeDtypeStruct((DEC_BATCH, ts, D_MODEL), F32),
        compiler_params=_params("parallel"),
        name="pool_sample",
    )(zs, w_bf, scale, g, b)
    out = lax.dynamic_update_slice(out, out_s[:, :DEC_SEQ].reshape(N_SAMPLE_TOK, D_MODEL), (N_PROMPT_TOK, 0))

    new_p = jnp.stack([xp[(bb + 1) * SEQ - POOL_STATE:(bb + 1) * SEQ] for bb in range(BATCH)])
    new_s = jnp.concatenate([state, xs], axis=1)[:, DEC_SEQ:]
    return out, new_p, new_s


def _alibi_slope(h):
    return 2.0 ** (-8.0 * (h + 1.0) / SWA_HEADS)


def _attn_core(q_ref, k_all, v_all, sink_ref, o_ref, *, rows, first_block):
    w, dh = SWA_WINDOW, SWA_HEAD_DIM
    qi = lax.broadcasted_iota(jnp.int32, (rows, 2 * w), 0)
    sj = lax.broadcasted_iota(jnp.int32, (rows, 2 * w), 1)
    dist = (w + qi) - sj
    valid = (dist >= 0) & (dist <= w)
    if first_block is not None:
        valid = valid & ((sj >= w) | jnp.logical_not(first_block))
    masked_dist = jnp.where(valid, dist.astype(F32), jnp.inf)
    for kv in range(SWA_KV_HEADS):
        c0, c1 = kv * dh, (kv + 1) * dh
        heads = range(kv * SWA_GROUP, (kv + 1) * SWA_GROUP)
        q = jnp.concatenate([q_ref[:, h * dh:(h + 1) * dh] for h in heads], axis=0)
        q = (q * (dh ** -0.5)).astype(BF16)
        s_all = lax.dot_general(q, k_all[:, c0:c1], _NT, preferred_element_type=F32)
        probs, dens = [], []
        for j, h in enumerate(heads):
            sink = sink_ref[h]
            s = s_all[j * rows:(j + 1) * rows] - _alibi_slope(h) * masked_dist
            m = jnp.maximum(jnp.max(s, axis=1, keepdims=True), sink)
            e = jnp.exp(s - m)
            dens.append(jnp.sum(e, axis=1, keepdims=True) + jnp.exp(sink - m))
            probs.append(e.astype(BF16))
        o_all = jnp.dot(jnp.concatenate(probs, axis=0), v_all[:, c0:c1], preferred_element_type=F32)
        for j, h in enumerate(heads):
            o_ref[:, h * dh:(h + 1) * dh] = o_all[j * rows:(j + 1) * rows] / dens[j]


def _attn_prompt_kernel(sink_ref, q_ref, kp_ref, kc_ref, vp_ref, vc_ref, o_ref):
    n = pl.program_id(1)
    k_all = jnp.concatenate([kp_ref[...], kc_ref[...]], axis=0).astype(BF16)
    v_all = jnp.concatenate([vp_ref[...], vc_ref[...]], axis=0).astype(BF16)
    _attn_core(q_ref, k_all, v_all, sink_ref, o_ref, rows=SWA_WINDOW, first_block=(n == 0))


def _attn_sample_kernel(sink_ref, q_ref, k_ref, v_ref, o_ref, *, rows):
    _attn_core(q_ref.at[0], k_ref[0].astype(BF16), v_ref[0].astype(BF16), sink_ref, o_ref.at[0],
               rows=rows, first_block=None)


def _swa_layer(xt, cache_k, cache_v, w_qkv, w_o, sinks, g, b):
    w = SWA_WINDOW
    qkv = _matmul(xt, w_qkv.astype(BF16), tm=640, tn=1280)
    nb = SEQ // w
    k_col = D_MODEL // SWA_KV_DIM
    cur = lambda bb, n, *_: (bb * nb + n, k_col)
    prev = lambda bb, n, *_: (jnp.maximum(bb * nb + n - 1, 0), k_col)
    cur_v = lambda bb, n, *_: (bb * nb + n, k_col + 1)
    prev_v = lambda bb, n, *_: (jnp.maximum(bb * nb + n - 1, 0), k_col + 1)
    o = pl.pallas_call(
        _attn_prompt_kernel,
        grid_spec=pltpu.PrefetchScalarGridSpec(
            num_scalar_prefetch=1,
            grid=(BATCH, nb),
            in_specs=[pl.BlockSpec((w, D_MODEL), lambda bb, n, *_: (bb * nb + n, 0)),
                      pl.BlockSpec((w, SWA_KV_DIM), prev),
                      pl.BlockSpec((w, SWA_KV_DIM), cur),
                      pl.BlockSpec((w, SWA_KV_DIM), prev_v),
                      pl.BlockSpec((w, SWA_KV_DIM), cur_v)],
            out_specs=pl.BlockSpec((w, D_MODEL), lambda bb, n, *_: (bb * nb + n, 0))),
        out_shape=jax.ShapeDtypeStruct((N_TOK, D_MODEL), F32),
        compiler_params=_params("parallel", "arbitrary"),
        name="attn_prompt",
    )(sinks, qkv, qkv, qkv, qkv, qkv)

    rows = 16
    qkv_s = qkv[N_PROMPT_TOK:].reshape(DEC_BATCH, DEC_SEQ, -1)
    q_s = jnp.pad(qkv_s[..., :D_MODEL], ((0, 0), (0, rows - DEC_SEQ), (0, 0)))
    k_new = qkv_s[..., D_MODEL:D_MODEL + SWA_KV_DIM]
    v_new = qkv_s[..., D_MODEL + SWA_KV_DIM:]
    kz = jnp.concatenate([cache_k.reshape(DEC_BATCH, w, SWA_KV_DIM), k_new], axis=1)
    vz = jnp.concatenate([cache_v.reshape(DEC_BATCH, w, SWA_KV_DIM), v_new], axis=1)
    pad_keys = ((0, 0), (0, w - DEC_SEQ), (0, 0))
    o_s = pl.pallas_call(
        functools.partial(_attn_sample_kernel, rows=rows),
        grid_spec=pltpu.PrefetchScalarGridSpec(
            num_scalar_prefetch=1,
            grid=(DEC_BATCH,),
            in_specs=[pl.BlockSpec((1, rows, D_MODEL), lambda bb, *_: (bb, 0, 0)),
                      pl.BlockSpec((1, 2 * w, SWA_KV_DIM), lambda bb, *_: (bb, 0, 0)),
                      pl.BlockSpec((1, 2 * w, SWA_KV_DIM), lambda bb, *_: (bb, 0, 0))],
            out_specs=pl.BlockSpec((1, rows, D_MODEL), lambda bb, *_: (bb, 0, 0))),
        out_shape=jax.ShapeDtypeStruct((DEC_BATCH, rows, D_MODEL), F32),
        compiler_params=_params("parallel"),
        name="attn_sample",
    )(sinks, q_s, jnp.pad(kz, pad_keys), jnp.pad(vz, pad_keys))
    o = lax.dynamic_update_slice(o, o_s[:, :DEC_SEQ].reshape(N_SAMPLE_TOK, D_MODEL), (N_PROMPT_TOK, 0))

    out = _matmul_ln(o, w_o.astype(BF16), xt, g, b)

    kv_shape = (SWA_WINDOW, SWA_KV_HEADS, SWA_HEAD_DIM)
    kv_p = jnp.stack([qkv[(bb + 1) * SEQ - w:(bb + 1) * SEQ, D_MODEL:] for bb in range(BATCH)])
    new_k_p = kv_p[..., :SWA_KV_DIM].reshape((BATCH,) + kv_shape)
    new_v_p = kv_p[..., SWA_KV_DIM:].reshape((BATCH,) + kv_shape)
    new_k_s = kz[:, DEC_SEQ:].reshape((DEC_BATCH,) + kv_shape)
    new_v_s = vz[:, DEC_SEQ:].reshape((DEC_BATCH,) + kv_shape)
    return out, new_k_p, new_v_p, new_k_s, new_v_s


def _log_sigmoid(x):
    return jnp.minimum(x, 0.0) - jnp.log(1.0 + jnp.exp(-jnp.abs(x)))


def _mlstm_kernel(bias_ref, q_ref, k_ref, v_ref, og_ref, gates_ref, c0_ref, n0_ref, m0_ref, ng_ref,
                  h_ref, c_ref, n_ref, m_ref, c_scr, n_scr, m_scr, *, chunk, n_valid):
    step = pl.program_id(1)

    @pl.when(step == 0)
    def _():
        c_scr[...] = c0_ref[0]
        n_scr[...] = n0_ref[0]
        m_scr[...] = m0_ref[0]

    for head in range(MLSTM_HEADS):
        _mlstm_head(head, step, bias_ref, q_ref, k_ref, v_ref, og_ref, gates_ref, ng_ref, h_ref, c_scr, n_scr, m_scr,
                    chunk=chunk, n_valid=n_valid)

    @pl.when(step == pl.num_programs(1) - 1)
    def _():
        c_ref[0] = c_scr[...]
        n_ref[0] = n_scr[...]
        m_ref[0] = m_scr[...]


def _mlstm_head(head, step, bias_ref, q_ref, k_ref, v_ref, og_ref, gates_ref, ng_ref, h_ref, c_scr, n_scr, m_scr,
                *, chunk, n_valid):
    ln = chunk
    cols = slice(head * MLSTM_HEAD_DIM, (head + 1) * MLSTM_HEAD_DIM)
    ig = gates_ref[0, head, pl.ds(step, 1), :] + bias_ref[head]
    lf = _log_sigmoid(gates_ref[0, MLSTM_HEADS + head, pl.ds(step, 1), :] + bias_ref[MLSTM_HEADS + head])
    if n_valid < ln:
        col = lax.broadcasted_iota(jnp.int32, (1, ln), 1)
        ig = jnp.where(col < n_valid, ig, -1e30)
        lf = jnp.where(col < n_valid, lf, 0.0)

    ri = lax.broadcasted_iota(jnp.int32, (ln, ln), 0)
    ci = lax.broadcasted_iota(jnp.int32, (ln, ln), 1)
    eye = ri == ci
    causal = ci <= ri

    def to_col(row):
        return jnp.sum(jnp.where(eye, row, 0.0), axis=1, keepdims=True)

    f_col = to_col(lf)
    b_col = jnp.sum(jnp.where(causal, lf, 0.0), axis=1, keepdims=True)
    b_row = jnp.sum(jnp.where(ri <= ci, f_col, 0.0), axis=0, keepdims=True)
    b_last = jnp.sum(lf, axis=1, keepdims=True)
    m_prev = m_scr[head, :, 0:1]
    a_col = b_col + m_prev
    dmat = jnp.where(causal, b_col - b_row + ig, -jnp.inf)
    mt = jnp.maximum(a_col, jnp.max(dmat, axis=1, keepdims=True))
    w_inter = jnp.exp(a_col - mt)

    q = q_ref[:, cols]
    k = k_ref[:, cols] * (MLSTM_HEAD_DIM ** -0.5)
    v = v_ref[:, cols]
    qb = q.astype(BF16)
    kb = k.astype(BF16)
    c_prev = c_scr[head]
    n_prev = n_scr[head]
    scores = lax.dot_general(qb, kb, _NT, preferred_element_type=F32)
    w_intra = jnp.exp(dmat - mt) * scores
    num = (w_inter * lax.dot_general(qb, c_prev.astype(BF16), _NT, preferred_element_type=F32)
           + jnp.dot(w_intra.astype(BF16), v.astype(BF16), preferred_element_type=F32))
    den = (w_inter * jnp.sum(q * n_prev, axis=1, keepdims=True)
           + jnp.sum(w_intra, axis=1, keepdims=True))
    h = num / jnp.maximum(jnp.abs(den), jnp.exp(-mt))

    g_row = b_last - b_row + ig
    m_new = jnp.maximum(b_last + m_prev, jnp.max(g_row, axis=1, keepdims=True))
    decay = jnp.exp(b_last + m_prev - m_new)
    wg_col = to_col(jnp.exp(g_row - m_new))
    c_new = decay * c_prev + lax.dot_general((v * wg_col).astype(BF16), kb, _TN, preferred_element_type=F32)
    n_new = decay * n_prev + jnp.sum(wg_col * k, axis=0, keepdims=True)
    c_scr[head] = c_new
    n_scr[head] = n_new
    m_scr[head] = jnp.broadcast_to(m_new, (1, LANES))

    mu = jnp.mean(h, axis=1, keepdims=True)
    hc = h - mu
    var = jnp.mean(hc * hc, axis=1, keepdims=True)
    h_ref[:, cols] = hc * lax.rsqrt(var + LN_EPS) * ng_ref[:, cols] * jax.nn.sigmoid(og_ref[:, cols])


def _mlstm_scan(proj, gates, b_gates, norm_g, c0, n0, m0, *, batch, n_chunks, chunk, n_valid, out_rows):
    d, dh, nh = D_MODEL, MLSTM_HEAD_DIM, MLSTM_HEADS
    rows = lambda off: (lambda bb, cc, *_: (bb * n_chunks + cc, off))
    per_seq = lambda bb, cc, *_: (bb, 0, 0, 0)
    return pl.pallas_call(
        functools.partial(_mlstm_kernel, chunk=chunk, n_valid=n_valid),
        grid_spec=pltpu.PrefetchScalarGridSpec(
            num_scalar_prefetch=1,
            grid=(batch, n_chunks),
            in_specs=[pl.BlockSpec((chunk, d), rows(0)),
                      pl.BlockSpec((chunk, d), rows(1)),
                      pl.BlockSpec((chunk, d), rows(2)),
                      pl.BlockSpec((chunk, d), rows(3)),
                      pl.BlockSpec((1, 2 * nh, n_chunks, chunk), per_seq),
                      pl.BlockSpec((1, nh, dh, dh), per_seq),
                      pl.BlockSpec((1, nh, 1, dh), per_seq),
                      pl.BlockSpec((1, nh, 1, LANES), per_seq),
                      pl.BlockSpec((1, d), lambda bb, cc, *_: (0, 0))],
            out_specs=[pl.BlockSpec((chunk, d), lambda bb, cc, *_: (bb * n_chunks + cc, 0)),
                       pl.BlockSpec((1, nh, dh, dh), per_seq),
                       pl.BlockSpec((1, nh, 1, dh), per_seq),
                       pl.BlockSpec((1, nh, 1, LANES), per_seq)],
            scratch_shapes=[pltpu.VMEM((nh, dh, dh), F32), pltpu.VMEM((nh, 1, dh), F32),
                            pltpu.VMEM((nh, 1, LANES), F32)]),
        out_shape=[jax.ShapeDtypeStruct((out_rows, d), F32),
                   jax.ShapeDtypeStruct((batch, nh, dh, dh), F32),
                   jax.ShapeDtypeStruct((batch, nh, 1, dh), F32),
                   jax.ShapeDtypeStruct((batch, nh, 1, LANES), F32)],
        compiler_params=_params("parallel", "arbitrary"),
        name="mlstm_scan",
    )(b_gates, proj, proj, proj, proj, gates, c0, n0[:, :, None, :],
      jnp.broadcast_to(m0[:, :, None, None], (batch, nh, 1, LANES)), norm_g)


def _mlstm_layer(xt, c0_s, n0_s, m0_s, w_in, b_gates, norm_g, w_out, g, b):
    d, nh, dh = D_MODEL, MLSTM_HEADS, MLSTM_HEAD_DIM
    proj = _matmul(xt, w_in[:, :4 * d].astype(BF16), tm=640, tn=1024)
    w_gates = jnp.pad(w_in[:, 4 * d:], ((0, 0), (0, 128 - 2 * nh)))
    gate_pre = _matmul(xt, w_gates.astype(BF16), tm=640, tn=128)[:, :2 * nh]

    nc = SEQ // MLSTM_CHUNK
    gates_p = gate_pre[:N_PROMPT_TOK].reshape(BATCH, nc, MLSTM_CHUNK, 2 * nh).transpose(0, 3, 1, 2)
    zeros = lambda *s: jnp.zeros(s, F32)
    hn, c_p, n_p, m_p = _mlstm_scan(
        proj, gates_p, b_gates, norm_g, zeros(BATCH, nh, dh, dh), zeros(BATCH, nh, dh), zeros(BATCH, nh),
        batch=BATCH, n_chunks=nc, chunk=MLSTM_CHUNK, n_valid=MLSTM_CHUNK, out_rows=N_TOK)

    ls = MLSTM_SAMPLE_CHUNK
    pad_t = ((0, 0), (0, ls - DEC_SEQ), (0, 0))
    proj_s = jnp.pad(proj[N_PROMPT_TOK:].reshape(DEC_BATCH, DEC_SEQ, 4 * d), pad_t).reshape(DEC_BATCH * ls, 4 * d)
    gates_s = jnp.pad(gate_pre[N_PROMPT_TOK:].reshape(DEC_BATCH, DEC_SEQ, 2 * nh), pad_t)
    gates_s = gates_s.transpose(0, 2, 1)[:, :, None, :]
    hn_s, c_s, n_s, m_s = _mlstm_scan(
        proj_s, gates_s, b_gates, norm_g, c0_s, n0_s, m0_s,
        batch=DEC_BATCH, n_chunks=1, chunk=ls, n_valid=DEC_SEQ, out_rows=DEC_BATCH * ls)
    hn_s = hn_s.reshape(DEC_BATCH, ls, d)[:, :DEC_SEQ].reshape(N_SAMPLE_TOK, d)
    hn = lax.dynamic_update_slice(hn, hn_s, (N_PROMPT_TOK, 0))

    out = _matmul_ln(hn, w_out.astype(BF16), xt, g, b)
    return (out, c_p, n_p[:, :, 0], m_p[:, :, 0, 0], c_s, n_s[:, :, 0], m_s[:, :, 0, 0])


def _router_kernel(x_ref, wt_ref, bias_ref, tri_ref, eidx_ref, gate_ref, rank_ref, count_ref, xq_ref, count_scr,
                   *, tt):
    neg = -jnp.inf
    x = x_ref[...]
    logits = lax.dot_general(wt_ref[...].astype(BF16), x.astype(BF16), _NT, preferred_element_type=F32)
    s = jax.nn.sigmoid(logits)
    sb = s + bias_ref[...]
    per_group = N_EXPERTS // N_EXPERT_GROUPS

    sb3 = sb.reshape(N_EXPERT_GROUPS, per_group, tt)
    i3 = lax.broadcasted_iota(jnp.int32, sb3.shape, 1)
    m1 = jnp.max(sb3, axis=1, keepdims=True)
    first = jnp.min(jnp.where(sb3 == m1, i3, per_group), axis=1, keepdims=True)
    m2 = jnp.max(jnp.where(i3 == first, neg, sb3), axis=1, keepdims=True)
    gscore = (m1 + m2).reshape(N_EXPERT_GROUPS, tt)

    gi = lax.broadcasted_iota(jnp.int32, gscore.shape, 0)
    gsel = jnp.zeros(gscore.shape, jnp.bool_)
    cur = gscore
    for _ in range(TOPK_GROUPS):
        mx = jnp.max(cur, axis=0, keepdims=True)
        pick = gi == jnp.min(jnp.where(cur == mx, gi, N_EXPERT_GROUPS), axis=0, keepdims=True)
        gsel = gsel | pick
        cur = jnp.where(pick, neg, cur)
    emask = jnp.broadcast_to(gsel.reshape(N_EXPERT_GROUPS, 1, tt), sb3.shape).reshape(N_EXPERTS, tt)

    ei = lax.broadcasted_iota(jnp.int32, sb.shape, 0)
    sel = jnp.zeros(sb.shape, jnp.bool_)
    cur = jnp.where(emask, sb, neg)
    picks = []
    for _ in range(TOP_K):
        mx = jnp.max(cur, axis=0, keepdims=True)
        idx = jnp.min(jnp.where(cur == mx, ei, N_EXPERTS), axis=0, keepdims=True)
        pick = ei == idx
        picks.append((idx, pick))
        sel = sel | pick
        cur = jnp.where(pick, neg, cur)
    s_sel = jnp.where(sel, s, 0.0)
    gate = s_sel / jnp.sum(s_sel, axis=0, keepdims=True) * ROUTED_SCALE

    @pl.when(pl.program_id(0) == 0)
    def _():
        count_scr[...] = jnp.zeros_like(count_scr)

    sel_f = sel.astype(F32)
    incl = jnp.dot(sel_f.astype(BF16), tri_ref[...], preferred_element_type=F32)
    rank = count_scr[:, 0:1] + incl - sel_f
    count_scr[...] = count_scr[...] + jnp.sum(sel_f, axis=1, keepdims=True)
    count_ref[...] = count_scr[...].astype(jnp.int32)

    for kk, (idx, pick) in enumerate(picks):
        eidx_ref[kk:kk + 1, :] = idx
        gate_ref[kk:kk + 1, :] = jnp.sum(jnp.where(pick, gate, 0.0), axis=0, keepdims=True)
        rank_ref[kk:kk + 1, :] = jnp.sum(jnp.where(pick, rank, 0.0), axis=0, keepdims=True).astype(jnp.int32)

    for sl in range(PACK_ROWS):
        c0 = 2 * LANES * sl
        xq_ref[pl.ds(sl, tt, stride=PACK_ROWS), :] = _pack_bf16_pair(x[:, c0:c0 + LANES],
                                                                     x[:, c0 + LANES:c0 + 2 * LANES])


def _pack_bf16_pair(hi, lo):
    hb = lax.bitcast_convert_type(hi.astype(BF16).astype(F32), jnp.uint32)
    lb = lax.bitcast_convert_type(lo.astype(BF16).astype(F32), jnp.uint32)
    return hb | (lb >> 16)


def _unpack_bf16_pair(word):
    hi = lax.bitcast_convert_type(word & jnp.uint32(0xFFFF0000), F32)
    lo = lax.bitcast_convert_type(word << 16, F32)
    return hi, lo


def _moe_route(xt, w_router, bias):
    tt = ROUTER_TILE
    col = lambda i: (0, i)
    const = lambda i: (0, 0)
    tri = (jnp.arange(tt)[:, None] <= jnp.arange(tt)[None, :]).astype(BF16)
    return pl.pallas_call(
        functools.partial(_router_kernel, tt=tt),
        grid=(N_TOK // tt,),
        in_specs=[pl.BlockSpec((tt, D_MODEL), lambda i: (i, 0)),
                  pl.BlockSpec((N_EXPERTS, D_MODEL), const),
                  pl.BlockSpec((N_EXPERTS, 1), const),
                  pl.BlockSpec((tt, tt), const)],
        out_specs=[pl.BlockSpec((TOP_K, tt), col), pl.BlockSpec((TOP_K, tt), col), pl.BlockSpec((TOP_K, tt), col),
                   pl.BlockSpec((N_EXPERTS, LANES), const),
                   pl.BlockSpec((tt * PACK_ROWS, LANES), lambda i: (i, 0))],
        out_shape=[jax.ShapeDtypeStruct((TOP_K, N_TOK), jnp.int32),
                   jax.ShapeDtypeStruct((TOP_K, N_TOK), F32),
                   jax.ShapeDtypeStruct((TOP_K, N_TOK), jnp.int32),
                   jax.ShapeDtypeStruct((N_EXPERTS, LANES), jnp.int32),
                   jax.ShapeDtypeStruct((N_TOK * PACK_ROWS, LANES), jnp.uint32)],
        scratch_shapes=[pltpu.VMEM((N_EXPERTS, LANES), F32)],
        compiler_params=_params("arbitrary"),
        name="moe_router",
    )(xt, w_router.T, bias[:, None], tri)


def _sorted_row(pstart_ref, eidx_ref, rank_ref, j):
    return pstart_ref[eidx_ref[0, 0, j]] + rank_ref[0, 0, j]


def _dispatch_kernel(pstart_ref, eidx_ref, rank_ref, xq_ref, xs_hbm, row_ref, sem, *, tt):
    def issue(t, carry):
        src = xq_ref.at[pl.ds(pl.multiple_of(t * PACK_ROWS, PACK_ROWS), PACK_ROWS)]
        for kk in range(TOP_K):
            row = _sorted_row(pstart_ref, eidx_ref, rank_ref, t * TOP_K + kk)
            row_ref[0, 0, t * TOP_K + kk] = row
            row = pl.multiple_of(row * PACK_ROWS, PACK_ROWS)
            pltpu.make_async_copy(src, xs_hbm.at[pl.ds(row, PACK_ROWS)], sem).start()
        return carry
    lax.fori_loop(0, tt, issue, 0)
    for _ in range(TOP_K):
        pltpu.make_async_copy(xq_ref, xs_hbm.at[pl.ds(0, tt * PACK_ROWS)], sem).wait()


def _moe_dispatch(xq, pstarts, eidx_tok, rank_tok):
    tt = DISPATCH_TILE
    n_tiles = N_TOK // tt
    slots = pl.BlockSpec((1, 1, tt * TOP_K), lambda i, *_: (i, 0, 0), memory_space=pltpu.SMEM)
    return pl.pallas_call(
        functools.partial(_dispatch_kernel, tt=tt),
        grid_spec=pltpu.PrefetchScalarGridSpec(
            num_scalar_prefetch=1,
            grid=(n_tiles,),
            in_specs=[slots, slots, pl.BlockSpec((tt * PACK_ROWS, LANES), lambda i, *_: (i, 0))],
            out_specs=[pl.BlockSpec(memory_space=pl.ANY), slots],
            scratch_shapes=[pltpu.SemaphoreType.DMA(())]),
        out_shape=[jax.ShapeDtypeStruct((MOE_ROWS * PACK_ROWS, LANES), jnp.uint32),
                   jax.ShapeDtypeStruct((n_tiles, 1, tt * TOP_K), jnp.int32)],
        compiler_params=_params("arbitrary"),
        name="moe_dispatch",
    )(pstarts, eidx_tok.reshape(n_tiles, 1, tt * TOP_K), rank_tok.reshape(n_tiles, 1, tt * TOP_K), xq)


def _experts_kernel(be_ref, next_ref, nvalid_ref, nused_ref, xs_ref, wg_hbm, wu_hbm, wd_hbm, y_ref,
                    wg_f, wu_f, wd_f, wgu_s, wd_s, sem, *, layer):
    i = pl.program_id(0)
    tm = MOE_BLOCK

    def weight_copies(e):
        return (pltpu.make_async_copy(wg_hbm.at[layer, e], wg_f, sem.at[0]),
                pltpu.make_async_copy(wu_hbm.at[layer, e], wu_f, sem.at[1]),
                pltpu.make_async_copy(wd_hbm.at[layer, e], wd_f, sem.at[2]))

    def swiglu_rows(n_rows):
        parts = []
        for sl in range(PACK_ROWS):
            parts.extend(_unpack_bf16_pair(xs_ref[pl.ds(sl, n_rows, stride=PACK_ROWS), :]))
        x = jnp.concatenate(parts, axis=1).astype(BF16)
        gate_up = jnp.dot(x, wgu_s[...], preferred_element_type=F32)
        hidden = _silu(gate_up[:, :EXPERT_FF]) * gate_up[:, EXPERT_FF:]
        y = jnp.dot(hidden.astype(BF16), wd_s[...], preferred_element_type=F32)
        for sl in range(PACK_ROWS):
            c0 = 2 * LANES * sl
            y_ref[pl.ds(sl, n_rows, stride=PACK_ROWS), :] = _pack_bf16_pair(y[:, c0:c0 + LANES],
                                                                            y[:, c0 + LANES:c0 + 2 * LANES])

    @pl.when(i == 0)
    def _():
        for cp in weight_copies(be_ref[0]):
            cp.start()

    @pl.when(i < nused_ref[0])
    def _():
        first_of_expert = (i == 0) | (be_ref[i] != be_ref[jnp.maximum(i - 1, 0)])

        @pl.when(first_of_expert)
        def _():
            for cp in weight_copies(be_ref[i]):
                cp.wait()
            wgu_s[:, :EXPERT_FF] = wg_f[...].astype(BF16)
            wgu_s[:, EXPERT_FF:] = wu_f[...].astype(BF16)
            wd_s[...] = wd_f[...].astype(BF16)

            @pl.when(next_ref[i] >= 0)
            def _():
                for cp in weight_copies(next_ref[i]):
                    cp.start()

        quarter = tm // 4
        n_quarters = (nvalid_ref[i] + quarter - 1) // quarter
        for nq in range(1, 5):
            @pl.when(n_quarters == nq if nq > 1 else n_quarters <= 1)
            def _(nq=nq):
                swiglu_rows(nq * quarter)


def _moe_experts(xs, block_e, next_e, n_valid, n_used, layer, w_gate, w_up, w_down):
    last = lambda i, be, nx, nv, nu: (jnp.minimum(i, nu[0] - 1), 0)
    return pl.pallas_call(
        functools.partial(_experts_kernel, layer=layer),
        grid_spec=pltpu.PrefetchScalarGridSpec(
            num_scalar_prefetch=4,
            grid=(MOE_N_BLOCKS,),
            in_specs=[pl.BlockSpec((MOE_BLOCK * PACK_ROWS, LANES), last),
                      pl.BlockSpec(memory_space=pl.ANY),
                      pl.BlockSpec(memory_space=pl.ANY),
                      pl.BlockSpec(memory_space=pl.ANY)],
            out_specs=pl.BlockSpec((MOE_BLOCK * PACK_ROWS, LANES), last),
            scratch_shapes=[pltpu.VMEM((D_MODEL, EXPERT_FF), F32), pltpu.VMEM((D_MODEL, EXPERT_FF), F32),
                            pltpu.VMEM((EXPERT_FF, D_MODEL), F32),
                            pltpu.VMEM((D_MODEL, 2 * EXPERT_FF), BF16),
                            pltpu.VMEM((EXPERT_FF, D_MODEL), BF16),
                            pltpu.SemaphoreType.DMA((3,))]),
        out_shape=jax.ShapeDtypeStruct((MOE_ROWS * PACK_ROWS, LANES), jnp.uint32),
        compiler_params=_params("arbitrary"),
        name="moe_experts",
    )(block_e, next_e, n_valid, n_used, xs, w_gate, w_up, w_down)


def _combine_kernel(row_ref, row_next_ref, x_ref, gate_ref, swg_ref, swu_ref, swd_ref, g_ref, b_ref, y_hbm, *rest,
                    tt, n_first):
    out_refs, (buf_a, buf_b, acc, sem) = rest[:-4], rest[-4:]
    i = pl.program_id(0)
    last = pl.num_programs(0) - 1

    def start_fetch(rows, into, into_sem, t):
        dst = pl.ds(t * PACK_ROWS, PACK_ROWS)
        for kk in range(TOP_K):
            src_row = pl.multiple_of(rows[0, 0, t * TOP_K + kk] * PACK_ROWS, PACK_ROWS)
            pltpu.make_async_copy(y_hbm.at[pl.ds(src_row, PACK_ROWS)], into.at[kk, dst], into_sem).start()

    def wait_fetch(into, into_sem):
        for kk in range(TOP_K):
            pltpu.make_async_copy(y_hbm.at[pl.ds(0, tt * PACK_ROWS)], into.at[kk], into_sem).wait()

    @pl.when(i == 0)
    def _():
        def body(t, carry):
            dst = pl.ds(pl.multiple_of(t * PACK_ROWS, PACK_ROWS), PACK_ROWS)
            for kk in range(TOP_K):
                src_row = pl.multiple_of(row_ref[0, 0, t * TOP_K + kk] * PACK_ROWS, PACK_ROWS)
                pltpu.make_async_copy(y_hbm.at[pl.ds(src_row, PACK_ROWS)], buf_a.at[kk, dst], sem.at[0]).start()
            return carry
        lax.fori_loop(0, tt, body, 0)

    def tile(cur, cur_sem, nxt, nxt_sem):
        per_phase = tt // (PACK_ROWS + 2)
        tokens = iter(range(tt))

        def issue(n):
            for _ in range(n):
                t = next(tokens, None)
                if t is not None:
                    start_fetch(row_next_ref, nxt, nxt_sem, t)

        x = x_ref[...]
        xb = x.astype(BF16)
        issue(per_phase)
        hidden = (_silu(jnp.dot(xb, swg_ref[...], preferred_element_type=F32))
                  * jnp.dot(xb, swu_ref[...], preferred_element_type=F32))
        issue(per_phase)
        shared = jnp.dot(hidden.astype(BF16), swd_ref[...], preferred_element_type=F32)
        wait_fetch(cur, cur_sem)
        gate = gate_ref[...]
        for sl in range(PACK_ROWS):
            issue(per_phase if sl < PACK_ROWS - 1 else tt)
            routed_hi = routed_lo = None
            for kk in range(TOP_K):
                hi, lo = _unpack_bf16_pair(cur[kk, pl.ds(sl, tt, stride=PACK_ROWS), :])
                g_k = gate[:, kk:kk + 1]
                routed_hi = g_k * hi if kk == 0 else routed_hi + g_k * hi
                routed_lo = g_k * lo if kk == 0 else routed_lo + g_k * lo
            for half, routed in enumerate((routed_hi, routed_lo)):
                cols = slice((2 * sl + half) * LANES, (2 * sl + half + 1) * LANES)
                acc[:, cols] = DN_ALPHA * x[:, cols] + (routed + shared[:, cols])
        res = _layer_norm(acc[...], g_ref[...], b_ref[...])
        if n_first is None:
            out_refs[0][...] = res
        else:
            @pl.when(i < n_first)
            def _():
                out_refs[0][...] = res

            @pl.when(i >= n_first)
            def _():
                out_refs[1][...] = res

        @pl.when(i == last)
        def _():
            wait_fetch(nxt, nxt_sem)

    @pl.when(i % 2 == 0)
    def _():
        tile(buf_a, sem.at[0], buf_b, sem.at[1])

    @pl.when(i % 2 == 1)
    def _():
        tile(buf_b, sem.at[1], buf_a, sem.at[0])


def _moe_combine(xt, ys, row_tiles, gate_tok, sw_gate, sw_up, sw_down, g, b, *, split):
    tt = COMBINE_TILE
    n_tiles = N_TOK // tt
    row = lambda i: (i, 0)
    const = lambda i: (0, 0)
    slots = pl.BlockSpec((1, 1, tt * TOP_K), lambda i: (i, 0, 0), memory_space=pltpu.SMEM)
    slots_next = pl.BlockSpec((1, 1, tt * TOP_K), lambda i: (jnp.minimum(i + 1, n_tiles - 1), 0, 0),
                              memory_space=pltpu.SMEM)
    row_tiles = row_tiles.reshape(n_tiles, 1, tt * TOP_K)
    if split:
        n_first = N_PROMPT_TOK // tt
        assert N_SAMPLE_TOK == tt
        out_specs = [pl.BlockSpec((tt, D_MODEL), lambda i: (jnp.minimum(i, n_first - 1), 0)),
                     pl.BlockSpec((tt, D_MODEL), const)]
        out_shape = [jax.ShapeDtypeStruct((N_PROMPT_TOK, D_MODEL), F32),
                     jax.ShapeDtypeStruct((N_SAMPLE_TOK, D_MODEL), F32)]
    else:
        n_first = None
        out_specs = [pl.BlockSpec((tt, D_MODEL), row)]
        out_shape = [jax.ShapeDtypeStruct((N_TOK, D_MODEL), F32)]
    return pl.pallas_call(
        functools.partial(_combine_kernel, tt=tt, n_first=n_first),
        grid=(n_tiles,),
        in_specs=[slots, slots_next,
                  pl.BlockSpec((tt, D_MODEL), row),
                  pl.BlockSpec((tt, TOP_K), row),
                  pl.BlockSpec((D_MODEL, EXPERT_FF), const),
                  pl.BlockSpec((D_MODEL, EXPERT_FF), const),
                  pl.BlockSpec((EXPERT_FF, D_MODEL), const),
                  pl.BlockSpec((1, D_MODEL), const),
                  pl.BlockSpec((1, D_MODEL), const),
                  pl.BlockSpec(memory_space=pl.ANY)],
        out_specs=out_specs,
        scratch_shapes=[pltpu.VMEM((TOP_K, tt * PACK_ROWS, LANES), jnp.uint32),
                        pltpu.VMEM((TOP_K, tt * PACK_ROWS, LANES), jnp.uint32),
                        pltpu.VMEM((tt, D_MODEL), F32), pltpu.SemaphoreType.DMA((2,))],
        out_shape=out_shape,
        compiler_params=_params("arbitrary"),
        name="moe_combine",
    )(row_tiles, row_tiles, xt, gate_tok, sw_gate.astype(BF16), sw_up.astype(BF16), sw_down.astype(BF16), g, b, ys)


def _moe_layer(xt, layer, w_router, bias, w_gate, w_up, w_down, sw_gate, sw_up, sw_down, g, b, *, split):
    eidx, gate, rank, counts, xq = _moe_route(xt, w_router, bias)

    counts = counts[:, 0]
    pcounts = (counts + MOE_BLOCK - 1) // MOE_BLOCK * MOE_BLOCK
    pends = jnp.cumsum(pcounts)
    pstarts = (pends - pcounts).astype(jnp.int32)
    n_used = (pends[-1] // MOE_BLOCK).astype(jnp.int32)
    blk = jnp.minimum(jnp.arange(MOE_N_BLOCKS, dtype=jnp.int32), n_used - 1) * MOE_BLOCK
    block_e = jnp.sum((pends[None, :] <= blk[:, None]).astype(jnp.int32), axis=1)
    block_e = jnp.minimum(block_e, N_EXPERTS - 1)
    after = pends[block_e] // MOE_BLOCK
    next_e = jnp.where(after < n_used, block_e[jnp.minimum(after, MOE_N_BLOCKS - 1)], -1).astype(jnp.int32)

    n_valid = jnp.clip(counts[block_e] - (blk - pstarts[block_e]), 0, MOE_BLOCK).astype(jnp.int32)

    eidx_tok, rank_tok = eidx.T, rank.T
    xs, rows = _moe_dispatch(xq, pstarts, eidx_tok, rank_tok)
    ys = _moe_experts(xs, block_e, next_e, n_valid, n_used[None], layer, w_gate, w_up, w_down)
    return _moe_combine(xt, ys, rows, gate.T, sw_gate, sw_up, sw_down, g, b, split=split)


def kernel(x_prompt, x_sample, state_pool, cache_swa_k, cache_swa_v, state_mlstm_c, state_mlstm_n, state_mlstm_m, pool_w, pool_scale, swa_w_qkv, swa_w_o, swa_sinks, mlstm_w_in, mlstm_b_gates, mlstm_norm_g, mlstm_w_out, ln_g, ln_b, moe_w_router, moe_router_bias, moe_w_gate, moe_w_up, moe_w_down, moe_shared_w_gate, moe_shared_w_up, moe_shared_w_down):
    d = D_MODEL
    xt = None
    pool_p, pool_s = [], []
    swk_p, swv_p, swk_s, swv_s = [], [], [], []
    mc_p, mn_p, mm_p, mc_s, mn_s, mm_s = [], [], [], [], [], []
    for i in range(DEPTH):
        kind, slot = i % N_MIXERS, i // N_MIXERS
        g0, b0 = ln_g[i, 0][None], ln_b[i, 0][None]
        if kind == 0:
            if i == 0:
                xp, xs = x_prompt.reshape(N_PROMPT_TOK, d), x_sample
            else:
                xp, xs = xt, xt[N_PROMPT_TOK:].reshape(DEC_BATCH, DEC_SEQ, d)
            xt, sp, ss = _pool_layer(xp, xs, state_pool[slot], pool_w[slot], pool_scale[slot][None], g0, b0)
            pool_p.append(sp)
            pool_s.append(ss)
        elif kind == 1:
            xt, kp, vp, ks, vs = _swa_layer(xt, cache_swa_k[slot], cache_swa_v[slot], swa_w_qkv[slot],
                                            swa_w_o[slot], swa_sinks[slot], g0, b0)
            swk_p.append(kp)
            swv_p.append(vp)
            swk_s.append(ks)
            swv_s.append(vs)
        else:
            xt, cp, np_, mp, cs, ns, ms = _mlstm_layer(
                xt, state_mlstm_c[slot], state_mlstm_n[slot], state_mlstm_m[slot], mlstm_w_in[slot],
                mlstm_b_gates[slot], mlstm_norm_g[slot][None], mlstm_w_out[slot], g0, b0)
            mc_p.append(cp)
            mn_p.append(np_)
            mm_p.append(mp)
            mc_s.append(cs)
            mn_s.append(ns)
            mm_s.append(ms)
        xt = _moe_layer(xt, i, moe_w_router[i], moe_router_bias[i], moe_w_gate, moe_w_up, moe_w_down,
                        moe_shared_w_gate[i], moe_shared_w_up[i], moe_shared_w_down[i],
                        ln_g[i, 1][None], ln_b[i, 1][None], split=(i == DEPTH - 1))
        if i < DEPTH - 1:
            xt = xt[0]
    y_p = xt[0].reshape(BATCH, SEQ, d)
    y_s = xt[1].reshape(DEC_BATCH, DEC_SEQ, d)
    return (y_p, y_s, jnp.stack(pool_p), jnp.stack(pool_s), jnp.stack(swk_p), jnp.stack(swv_p),
            jnp.stack(swk_s), jnp.stack(swv_s), jnp.stack(mc_p), jnp.stack(mn_p), jnp.stack(mm_p),
            jnp.stack(mc_s), jnp.stack(mn_s), jnp.stack(mm_s))
```

```python
import functools
import math

import jax
import jax.numpy as jnp
from jax import lax
from jax.experimental import pallas as pl
from jax.experimental.pallas import tpu as pltpu

F32 = jnp.float32
BF16 = jnp.bfloat16

D_MODEL = 2048
BATCH = 2
SEQ = 4096
DEPTH = 4
DEC_BATCH = 32
DEC_SEQ = 4
PAST_LEN = 16384
N_PROMPT_TOK = BATCH * SEQ
N_SAMPLE_TOK = DEC_BATCH * DEC_SEQ
N_TOK = N_PROMPT_TOK + N_SAMPLE_TOK

N_MIXERS = 3
DN_ALPHA = (2.0 * DEPTH) ** 0.25
LN_EPS = 1e-5

POOL_WINDOWS = (2, 4, 8, 16)
POOL_GROUP_DIM = D_MODEL // len(POOL_WINDOWS)
POOL_STATE = max(POOL_WINDOWS) - 1
POOL_HALO = POOL_STATE + 1

SWA_WINDOW = 128
SWA_HEAD_DIM = 64
SWA_HEADS = D_MODEL // SWA_HEAD_DIM
SWA_KV_HEADS = SWA_HEADS // 8
SWA_GROUP = SWA_HEADS // SWA_KV_HEADS
SWA_KV_DIM = SWA_KV_HEADS * SWA_HEAD_DIM

MLSTM_HEADS = 8
MLSTM_HEAD_DIM = D_MODEL // MLSTM_HEADS
MLSTM_CHUNK = 64
MLSTM_PHASES = 5
MLSTM_SAMPLE_CHUNK = 8

N_EXPERTS = 64
TOP_K = 8
N_EXPERT_GROUPS = 8
TOPK_GROUPS = 4
EXPERT_FF = D_MODEL // 4
ROUTED_SCALE = 2.5

VMEM_LIMIT_BYTES = 56 * 1024 * 1024

ROW_TILE = 320
ROUTER_TILE = 640
POOL_TILE = 512
MOE_BLOCK = 512
MOE_N_BLOCKS = N_TOK * TOP_K // MOE_BLOCK + N_EXPERTS
MOE_ROWS = MOE_N_BLOCKS * MOE_BLOCK
DISPATCH_TILE = 320
COMBINE_TILE = 128
LANES = 128
PACK_ROWS = D_MODEL // (2 * LANES)

_NT = (((1,), (1,)), ((), ()))
_TN = (((0,), (0,)), ((), ()))


def _params(*semantics):
    return pltpu.CompilerParams(dimension_semantics=semantics, vmem_limit_bytes=VMEM_LIMIT_BYTES)


def _layer_norm(z, g, b):
    mu = jnp.mean(z, axis=-1, keepdims=True)
    zc = z - mu
    var = jnp.mean(zc * zc, axis=-1, keepdims=True)
    return zc * lax.rsqrt(var + LN_EPS) * g + b


def _silu(x):
    return x * jax.nn.sigmoid(x)


def _matmul_kernel(x_ref, w_ref, o_ref):
    o_ref[...] = jnp.dot(x_ref[...].astype(BF16), w_ref[...], preferred_element_type=F32)


def _matmul(x, w, *, tm, tn):
    m, k = x.shape
    n = w.shape[1]
    return pl.pallas_call(
        _matmul_kernel,
        grid=(n // tn, m // tm),
        in_specs=[pl.BlockSpec((tm, k), lambda j, i: (i, 0)),
                  pl.BlockSpec((k, tn), lambda j, i: (0, j))],
        out_specs=pl.BlockSpec((tm, tn), lambda j, i: (i, j)),
        out_shape=jax.ShapeDtypeStruct((m, n), F32),
        compiler_params=_params("parallel", "parallel"),
        name="matmul",
    )(x, w)


def _matmul_ln_kernel(a_ref, w_ref, res_ref, g_ref, b_ref, o_ref):
    y = jnp.dot(a_ref[...], w_ref[...], preferred_element_type=F32)
    o_ref[...] = _layer_norm(DN_ALPHA * res_ref[...] + y, g_ref[...], b_ref[...])


def _matmul_ln(a, w, res, g, b):
    m, k = a.shape
    row = lambda i: (i, 0)
    const = lambda i: (0, 0)
    return pl.pallas_call(
        _matmul_ln_kernel,
        grid=(m // ROW_TILE,),
        in_specs=[pl.BlockSpec((ROW_TILE, k), row),
                  pl.BlockSpec((k, D_MODEL), const),
                  pl.BlockSpec((ROW_TILE, D_MODEL), row),
                  pl.BlockSpec((1, D_MODEL), const),
                  pl.BlockSpec((1, D_MODEL), const)],
        out_specs=pl.BlockSpec((ROW_TILE, D_MODEL), row),
        out_shape=jax.ShapeDtypeStruct((m, D_MODEL), F32),
        compiler_params=_params("parallel"),
        name="matmul_ln",
    )(a, w, res, g, b)


def _pool_core(zbuf, w_ref, scale_ref, g_ref, b_ref, o_ref, sums, *, tt, n_before):
    rows = POOL_HALO + tt
    avail = n_before + lax.broadcasted_iota(jnp.int32, (tt, 1), 0) + 1
    for grp, win in enumerate(POOL_WINDOWS):
        c0, c1 = grp * POOL_GROUP_DIM, (grp + 1) * POOL_GROUP_DIM
        xg = zbuf[POOL_HALO:POOL_HALO + tt, c0:c1]
        src, src_cols, span, level = zbuf, slice(c0, c1), 1, 0
        while span < win:
            dst = sums[level % 2]
            dst[0:span, :] = src[0:span, src_cols]
            dst[span:rows, :] = src[span:rows, src_cols] + src[0:rows - span, src_cols]
            src, src_cols, span, level = dst, slice(None), 2 * span, level + 1
        total = src[POOL_HALO:POOL_HALO + tt, src_cols]
        count = jnp.minimum(win, avail).astype(F32)
        diff = total / count - xg
        y = jnp.dot(diff.astype(BF16), w_ref[grp], preferred_element_type=F32) * scale_ref[:, c0:c1]
        o_ref[:, c0:c1] = DN_ALPHA * xg + y
    o_ref[...] = _layer_norm(o_ref[...], g_ref[...], b_ref[...])


def _pool_prompt_kernel(x_ref, halo_ref, w_ref, scale_ref, g_ref, b_ref, o_ref, zbuf, sum_a, sum_b, *, tt):
    i = pl.program_id(1)
    zbuf[0:POOL_HALO, :] = jnp.where(i == 0, 0.0, halo_ref[...])
    zbuf[POOL_HALO:POOL_HALO + tt, :] = x_ref[...]
    _pool_core(zbuf, w_ref, scale_ref, g_ref, b_ref, o_ref, (sum_a, sum_b), tt=tt, n_before=i * tt)


def _pool_sample_kernel(z_ref, w_ref, scale_ref, g_ref, b_ref, o_ref, sum_a, sum_b, *, tt):
    _pool_core(z_ref.at[0], w_ref, scale_ref, g_ref, b_ref, o_ref.at[0], (sum_a, sum_b), tt=tt, n_before=PAST_LEN)


def _pool_layer(xp, xs, state, w, scale, g, b):
    tt = POOL_TILE
    tiles = SEQ // tt
    halo_per_tile = tt // POOL_HALO
    const2 = lambda bb, i: (0, 0)
    w_bf = w.astype(BF16)
    out = pl.pallas_call(
        functools.partial(_pool_prompt_kernel, tt=tt),
        grid=(BATCH, tiles),
        in_specs=[pl.BlockSpec((tt, D_MODEL), lambda bb, i: (bb * tiles + i, 0)),
                  pl.BlockSpec((POOL_HALO, D_MODEL),
                               lambda bb, i: (jnp.maximum((bb * tiles + i) * halo_per_tile - 1, 0), 0)),
                  pl.BlockSpec(w_bf.shape, lambda bb, i: (0, 0, 0)),
                  pl.BlockSpec((1, D_MODEL), const2),
                  pl.BlockSpec((1, D_MODEL), const2),
                  pl.BlockSpec((1, D_MODEL), const2)],
        out_specs=pl.BlockSpec((tt, D_MODEL), lambda bb, i: (bb * tiles + i, 0)),
        out_shape=jax.ShapeDtypeStruct((N_TOK, D_MODEL), F32),
        scratch_shapes=[pltpu.VMEM((POOL_HALO + tt, D_MODEL), F32),
                        pltpu.VMEM((POOL_HALO + tt, POOL_GROUP_DIM), F32),
                        pltpu.VMEM((POOL_HALO + tt, POOL_GROUP_DIM), F32)],
        compiler_params=_params("parallel", "arbitrary"),
        name="pool_prompt",
    )(xp, xp, w_bf, scale, g, b)

    ts = 16
    zs = jnp.concatenate([jnp.zeros((DEC_BATCH, 1, D_MODEL), F32), state, xs,
                          jnp.zeros((DEC_BATCH, ts - DEC_SEQ, D_MODEL), F32)], axis=1)
    const1 = lambda bb: (0, 0)
    out_s = pl.pallas_call(
        functools.partial(_pool_sample_kernel, tt=ts),
        grid=(DEC_BATCH,),
        in_specs=[pl.BlockSpec((1, POOL_HALO + ts, D_MODEL), lambda bb: (bb, 0, 0)),
                  pl.BlockSpec(w_bf.shape, lambda bb: (0, 0, 0)),
                  pl.BlockSpec((1, D_MODEL), const1),
                  pl.BlockSpec((1, D_MODEL), const1),
                  pl.BlockSpec((1, D_MODEL), const1)],
        out_specs=pl.BlockSpec((1, ts, D_MODEL), lambda bb: (bb, 0, 0)),
        out_shape=jax.ShapeDtypeStruct((DEC_BATCH, ts, D_MODEL), F32),
        scratch_shapes=[pltpu.VMEM((POOL_HALO + ts, POOL_GROUP_DIM), F32),
                        pltpu.VMEM((POOL_HALO + ts, POOL_GROUP_DIM), F32)],
        compiler_params=_params("parallel"),
        name="pool_sample",
    )(zs, w_bf, scale, g, b)
    out = lax.dynamic_update_slice(out, out_s[:, :DEC_SEQ].reshape(N_SAMPLE_TOK, D_MODEL), (N_PROMPT_TOK, 0))

    new_p = jnp.stack([xp[(bb + 1) * SEQ - POOL_STATE:(bb + 1) * SEQ] for bb in range(BATCH)])
    new_s = jnp.concatenate([state, xs], axis=1)[:, DEC_SEQ:]
    return out, new_p, new_s


def _alibi_slope(h):
    return 2.0 ** (-8.0 * (h + 1.0) / SWA_HEADS)


def _attn_core(q_ref, k_all, v_all, sink_ref, o_ref, *, rows, first_block):
    w, dh = SWA_WINDOW, SWA_HEAD_DIM
    qi = lax.broadcasted_iota(jnp.int32, (rows, 2 * w), 0)
    sj = lax.broadcasted_iota(jnp.int32, (rows, 2 * w), 1)
    dist = (w + qi) - sj
    valid = (dist >= 0) & (dist <= w)
    if first_block is not None:
        valid = valid & ((sj >= w) | jnp.logical_not(first_block))
    masked_dist = jnp.where(valid, dist.astype(F32), jnp.inf)
    for kv in range(SWA_KV_HEADS):
        c0, c1 = kv * dh, (kv + 1) * dh
        heads = range(kv * SWA_GROUP, (kv + 1) * SWA_GROUP)
        q = jnp.concatenate([q_ref[:, h * dh:(h + 1) * dh] for h in heads], axis=0)
        q = (q * (dh ** -0.5)).astype(BF16)
        s_all = lax.dot_general(q, k_all[:, c0:c1], _NT, preferred_element_type=F32)
        probs, dens = [], []
        for j, h in enumerate(heads):
            sink = sink_ref[h]
            s = s_all[j * rows:(j + 1) * rows] - _alibi_slope(h) * masked_dist
            m = jnp.maximum(jnp.max(s, axis=1, keepdims=True), sink)
            e = jnp.exp(s - m)
            dens.append(jnp.sum(e, axis=1, keepdims=True) + jnp.exp(sink - m))
            probs.append(e.astype(BF16))
        o_all = jnp.dot(jnp.concatenate(probs, axis=0), v_all[:, c0:c1], preferred_element_type=F32)
        for j, h in enumerate(heads):
            o_ref[:, h * dh:(h + 1) * dh] = (o_all[j * rows:(j + 1) * rows] / dens[j]).astype(o_ref.dtype)


def _attn_prompt_kernel(sink_ref, q_ref, kp_ref, kc_ref, vp_ref, vc_ref, o_ref):
    n = pl.program_id(1)
    k_all = jnp.concatenate([kp_ref[...], kc_ref[...]], axis=0).astype(BF16)
    v_all = jnp.concatenate([vp_ref[...], vc_ref[...]], axis=0).astype(BF16)
    _attn_core(q_ref, k_all, v_all, sink_ref, o_ref, rows=SWA_WINDOW, first_block=(n == 0))


def _attn_sample_kernel(sink_ref, q_ref, k_ref, v_ref, o_ref, *, rows):
    _attn_core(q_ref.at[0], k_ref[0].astype(BF16), v_ref[0].astype(BF16), sink_ref, o_ref.at[0],
               rows=rows, first_block=None)


def _swa_layer(xt, cache_k, cache_v, w_qkv, w_o, sinks, g, b):
    w = SWA_WINDOW
    qkv = _matmul(xt, w_qkv.astype(BF16), tm=640, tn=1280)
    nb = SEQ // w
    k_col = D_MODEL // SWA_KV_DIM
    cur = lambda bb, n, *_: (bb * nb + n, k_col)
    prev = lambda bb, n, *_: (jnp.maximum(bb * nb + n - 1, 0), k_col)
    cur_v = lambda bb, n, *_: (bb * nb + n, k_col + 1)
    prev_v = lambda bb, n, *_: (jnp.maximum(bb * nb + n - 1, 0), k_col + 1)
    o = pl.pallas_call(
        _attn_prompt_kernel,
        grid_spec=pltpu.PrefetchScalarGridSpec(
            num_scalar_prefetch=1,
            grid=(BATCH, nb),
            in_specs=[pl.BlockSpec((w, D_MODEL), lambda bb, n, *_: (bb * nb + n, 0)),
                      pl.BlockSpec((w, SWA_KV_DIM), prev),
                      pl.BlockSpec((w, SWA_KV_DIM), cur),
                      pl.BlockSpec((w, SWA_KV_DIM), prev_v),
                      pl.BlockSpec((w, SWA_KV_DIM), cur_v)],
            out_specs=pl.BlockSpec((w, D_MODEL), lambda bb, n, *_: (bb * nb + n, 0))),
        out_shape=jax.ShapeDtypeStruct((N_TOK, D_MODEL), BF16),
        compiler_params=_params("parallel", "arbitrary"),
        name="attn_prompt",
    )(sinks, qkv, qkv, qkv, qkv, qkv)

    rows = 16
    qkv_s = qkv[N_PROMPT_TOK:].reshape(DEC_BATCH, DEC_SEQ, -1)
    q_s = jnp.pad(qkv_s[..., :D_MODEL], ((0, 0), (0, rows - DEC_SEQ), (0, 0)))
    k_new = qkv_s[..., D_MODEL:D_MODEL + SWA_KV_DIM]
    v_new = qkv_s[..., D_MODEL + SWA_KV_DIM:]
    kz = jnp.concatenate([cache_k.reshape(DEC_BATCH, w, SWA_KV_DIM), k_new], axis=1)
    vz = jnp.concatenate([cache_v.reshape(DEC_BATCH, w, SWA_KV_DIM), v_new], axis=1)
    pad_keys = ((0, 0), (0, w - DEC_SEQ), (0, 0))
    o_s = pl.pallas_call(
        functools.partial(_attn_sample_kernel, rows=rows),
        grid_spec=pltpu.PrefetchScalarGridSpec(
            num_scalar_prefetch=1,
            grid=(DEC_BATCH,),
            in_specs=[pl.BlockSpec((1, rows, D_MODEL), lambda bb, *_: (bb, 0, 0)),
                      pl.BlockSpec((1, 2 * w, SWA_KV_DIM), lambda bb, *_: (bb, 0, 0)),
                      pl.BlockSpec((1, 2 * w, SWA_KV_DIM), lambda bb, *_: (bb, 0, 0))],
            out_specs=pl.BlockSpec((1, rows, D_MODEL), lambda bb, *_: (bb, 0, 0))),
        out_shape=jax.ShapeDtypeStruct((DEC_BATCH, rows, D_MODEL), F32),
        compiler_params=_params("parallel"),
        name="attn_sample",
    )(sinks, q_s, jnp.pad(kz, pad_keys), jnp.pad(vz, pad_keys))
    o = lax.dynamic_update_slice(o, o_s[:, :DEC_SEQ].reshape(N_SAMPLE_TOK, D_MODEL).astype(BF16),
                                 (N_PROMPT_TOK, 0))

    out = _matmul_ln(o, w_o.astype(BF16), xt, g, b)

    kv_shape = (SWA_WINDOW, SWA_KV_HEADS, SWA_HEAD_DIM)
    kv_p = jnp.stack([qkv[(bb + 1) * SEQ - w:(bb + 1) * SEQ, D_MODEL:] for bb in range(BATCH)])
    new_k_p = kv_p[..., :SWA_KV_DIM].reshape((BATCH,) + kv_shape)
    new_v_p = kv_p[..., SWA_KV_DIM:].reshape((BATCH,) + kv_shape)
    new_k_s = kz[:, DEC_SEQ:].reshape((DEC_BATCH,) + kv_shape)
    new_v_s = vz[:, DEC_SEQ:].reshape((DEC_BATCH,) + kv_shape)
    return out, new_k_p, new_v_p, new_k_s, new_v_s


def _log_sigmoid(x):
    return jnp.minimum(x, 0.0) - jnp.log(1.0 + jnp.exp(-jnp.abs(x)))


def _mlstm_kernel(bias_ref, q_ref, k_ref, v_ref, og_ref, gates_ref, c0_ref, n0_ref, m0_ref, ng_ref,
                  h_ref, c_ref, n_ref, m_ref, c_scr, n_scr, m_scr, *, chunk, n_valid, group):
    step = pl.program_id(1)

    @pl.when(step == 0)
    def _():
        c_scr[...] = c0_ref[0]
        n_scr[...] = n0_ref[0]
        m_scr[...] = m0_ref[0]

    heads = [_mlstm_head(head, step, bias_ref, q_ref, k_ref, v_ref, og_ref, gates_ref, ng_ref, h_ref, c_scr, n_scr,
                         m_scr, chunk=chunk, n_valid=n_valid) for head in range(MLSTM_HEADS)]
    for h0 in range(0, MLSTM_HEADS, group):
        for _ in range(MLSTM_PHASES):
            for head_phases in heads[h0:h0 + group]:
                next(head_phases)

    @pl.when(step == pl.num_programs(1) - 1)
    def _():
        c_ref[0] = c_scr[...]
        n_ref[0] = n_scr[...]
        m_ref[0] = m_scr[...]


def _mlstm_head(head, step, bias_ref, q_ref, k_ref, v_ref, og_ref, gates_ref, ng_ref, h_ref, c_scr, n_scr, m_scr,
                *, chunk, n_valid):
    ln = chunk
    cols = slice(head * MLSTM_HEAD_DIM, (head + 1) * MLSTM_HEAD_DIM)
    ig = gates_ref[0, head, pl.ds(step, 1), :] + bias_ref[head]
    lf = _log_sigmoid(gates_ref[0, MLSTM_HEADS + head, pl.ds(step, 1), :] + bias_ref[MLSTM_HEADS + head])
    if n_valid < ln:
        col = lax.broadcasted_iota(jnp.int32, (1, ln), 1)
        ig = jnp.where(col < n_valid, ig, -1e30)
        lf = jnp.where(col < n_valid, lf, 0.0)

    ri = lax.broadcasted_iota(jnp.int32, (ln, ln), 0)
    ci = lax.broadcasted_iota(jnp.int32, (ln, ln), 1)
    eye = ri == ci
    causal = ci <= ri

    def to_col(row):
        return jnp.sum(jnp.where(eye, row, 0.0), axis=1, keepdims=True)

    f_col = to_col(lf)
    b_col = jnp.sum(jnp.where(causal, lf, 0.0), axis=1, keepdims=True)
    b_row = jnp.sum(jnp.where(ri <= ci, f_col, 0.0), axis=0, keepdims=True)
    b_last = jnp.sum(lf, axis=1, keepdims=True)
    m_prev = m_scr[head, :, 0:1]
    a_col = b_col + m_prev
    dmat = jnp.where(causal, b_col - b_row + ig, -jnp.inf)
    mt = jnp.maximum(a_col, jnp.max(dmat, axis=1, keepdims=True))
    w_inter = jnp.exp(a_col - mt)
    yield

    q = q_ref[:, cols]
    k = k_ref[:, cols] * (MLSTM_HEAD_DIM ** -0.5)
    v = v_ref[:, cols]
    qb = q.astype(BF16)
    kb = k.astype(BF16)
    c_prev = c_scr[head]
    n_prev = n_scr[head]
    scores = lax.dot_general(qb, kb, _NT, preferred_element_type=F32)
    yield
    w_intra = jnp.exp(dmat - mt) * scores
    num = (w_inter * lax.dot_general(qb, c_prev.astype(BF16), _NT, preferred_element_type=F32)
           + jnp.dot(w_intra.astype(BF16), v.astype(BF16), preferred_element_type=F32))
    den = (w_inter * jnp.sum(q * n_prev, axis=1, keepdims=True)
           + jnp.sum(w_intra, axis=1, keepdims=True))
    h = num / jnp.maximum(jnp.abs(den), jnp.exp(-mt))
    yield

    g_row = b_last - b_row + ig
    m_new = jnp.maximum(b_last + m_prev, jnp.max(g_row, axis=1, keepdims=True))
    decay = jnp.exp(b_last + m_prev - m_new)
    wg_col = to_col(jnp.exp(g_row - m_new))
    c_new = decay * c_prev + lax.dot_general((v * wg_col).astype(BF16), kb, _TN, preferred_element_type=F32)
    n_new = decay * n_prev + jnp.sum(wg_col * k, axis=0, keepdims=True)
    c_scr[head] = c_new
    n_scr[head] = n_new
    m_scr[head] = jnp.broadcast_to(m_new, (1, LANES))
    yield

    mu = jnp.mean(h, axis=1, keepdims=True)
    hc = h - mu
    var = jnp.mean(hc * hc, axis=1, keepdims=True)
    hn = hc * lax.rsqrt(var + LN_EPS) * ng_ref[:, cols] * jax.nn.sigmoid(og_ref[:, cols])
    h_ref[:, cols] = hn.astype(h_ref.dtype)
    yield


def _mlstm_scan(proj, gates, b_gates, norm_g, c0, n0, m0, *, batch, n_chunks, chunk, n_valid, out_rows, out_dtype,
                group):
    d, dh, nh = D_MODEL, MLSTM_HEAD_DIM, MLSTM_HEADS
    rows = lambda off: (lambda bb, cc, *_: (bb * n_chunks + cc, off))
    per_seq = lambda bb, cc, *_: (bb, 0, 0, 0)
    return pl.pallas_call(
        functools.partial(_mlstm_kernel, chunk=chunk, n_valid=n_valid, group=group),
        grid_spec=pltpu.PrefetchScalarGridSpec(
            num_scalar_prefetch=1,
            grid=(batch, n_chunks),
            in_specs=[pl.BlockSpec((chunk, d), rows(0)),
                      pl.BlockSpec((chunk, d), rows(1)),
                      pl.BlockSpec((chunk, d), rows(2)),
                      pl.BlockSpec((chunk, d), rows(3)),
                      pl.BlockSpec((1, 2 * nh, n_chunks, chunk), per_seq),
                      pl.BlockSpec((1, nh, dh, dh), per_seq),
                      pl.BlockSpec((1, nh, 1, dh), per_seq),
                      pl.BlockSpec((1, nh, 1, LANES), per_seq),
                      pl.BlockSpec((1, d), lambda bb, cc, *_: (0, 0))],
            out_specs=[pl.BlockSpec((chunk, d), lambda bb, cc, *_: (bb * n_chunks + cc, 0)),
                       pl.BlockSpec((1, nh, dh, dh), per_seq),
                       pl.BlockSpec((1, nh, 1, dh), per_seq),
                       pl.BlockSpec((1, nh, 1, LANES), per_seq)],
            scratch_shapes=[pltpu.VMEM((nh, dh, dh), F32), pltpu.VMEM((nh, 1, dh), F32),
                            pltpu.VMEM((nh, 1, LANES), F32)]),
        out_shape=[jax.ShapeDtypeStruct((out_rows, d), out_dtype),
                   jax.ShapeDtypeStruct((batch, nh, dh, dh), F32),
                   jax.ShapeDtypeStruct((batch, nh, 1, dh), F32),
                   jax.ShapeDtypeStruct((batch, nh, 1, LANES), F32)],
        compiler_params=_params("parallel", "arbitrary"),
        name="mlstm_scan",
    )(b_gates, proj, proj, proj, proj, gates, c0, n0[:, :, None, :],
      jnp.broadcast_to(m0[:, :, None, None], (batch, nh, 1, LANES)), norm_g)


def _mlstm_layer(xt, c0_s, n0_s, m0_s, w_in, b_gates, norm_g, w_out, g, b):
    d, nh, dh = D_MODEL, MLSTM_HEADS, MLSTM_HEAD_DIM
    proj = _matmul(xt, w_in[:, :4 * d].astype(BF16), tm=640, tn=1024)
    w_gates = jnp.pad(w_in[:, 4 * d:], ((0, 0), (0, 128 - 2 * nh)))
    gate_pre = _matmul(xt, w_gates.astype(BF16), tm=640, tn=128)[:, :2 * nh]

    nc = SEQ // MLSTM_CHUNK
    gates_p = gate_pre[:N_PROMPT_TOK].reshape(BATCH, nc, MLSTM_CHUNK, 2 * nh).transpose(0, 3, 1, 2)
    zeros = lambda *s: jnp.zeros(s, F32)
    hn, c_p, n_p, m_p = _mlstm_scan(
        proj, gates_p, b_gates, norm_g, zeros(BATCH, nh, dh, dh), zeros(BATCH, nh, dh), zeros(BATCH, nh),
        batch=BATCH, n_chunks=nc, chunk=MLSTM_CHUNK, n_valid=MLSTM_CHUNK, out_rows=N_TOK, out_dtype=BF16, group=2)

    ls = MLSTM_SAMPLE_CHUNK
    pad_t = ((0, 0), (0, ls - DEC_SEQ), (0, 0))
    proj_s = jnp.pad(proj[N_PROMPT_TOK:].reshape(DEC_BATCH, DEC_SEQ, 4 * d), pad_t).reshape(DEC_BATCH * ls, 4 * d)
    gates_s = jnp.pad(gate_pre[N_PROMPT_TOK:].reshape(DEC_BATCH, DEC_SEQ, 2 * nh), pad_t)
    gates_s = gates_s.transpose(0, 2, 1)[:, :, None, :]
    hn_s, c_s, n_s, m_s = _mlstm_scan(
        proj_s, gates_s, b_gates, norm_g, c0_s, n0_s, m0_s,
        batch=DEC_BATCH, n_chunks=1, chunk=ls, n_valid=DEC_SEQ, out_rows=DEC_BATCH * ls, out_dtype=F32,
        group=MLSTM_HEADS)
    hn_s = hn_s.reshape(DEC_BATCH, ls, d)[:, :DEC_SEQ].reshape(N_SAMPLE_TOK, d).astype(BF16)
    hn = lax.dynamic_update_slice(hn, hn_s, (N_PROMPT_TOK, 0))

    out = _matmul_ln(hn, w_out.astype(BF16), xt, g, b)
    return (out, c_p, n_p[:, :, 0], m_p[:, :, 0, 0], c_s, n_s[:, :, 0], m_s[:, :, 0, 0])


def _router_kernel(x_ref, wt_ref, bias_ref, tri_ref, eidx_ref, gate_ref, rank_ref, count_ref, xq_ref, count_scr,
                   *, tt):
    neg = -jnp.inf
    x = x_ref[...]
    logits = lax.dot_general(wt_ref[...].astype(BF16), x.astype(BF16), _NT, preferred_element_type=F32)
    s = jax.nn.sigmoid(logits)
    sb = s + bias_ref[...]
    per_group = N_EXPERTS // N_EXPERT_GROUPS

    sb3 = sb.reshape(N_EXPERT_GROUPS, per_group, tt)
    i3 = lax.broadcasted_iota(jnp.int32, sb3.shape, 1)
    m1 = jnp.max(sb3, axis=1, keepdims=True)
    first = jnp.min(jnp.where(sb3 == m1, i3, per_group), axis=1, keepdims=True)
    m2 = jnp.max(jnp.where(i3 == first, neg, sb3), axis=1, keepdims=True)
    gscore = (m1 + m2).reshape(N_EXPERT_GROUPS, tt)

    gi = lax.broadcasted_iota(jnp.int32, gscore.shape, 0)
    gsel = jnp.zeros(gscore.shape, jnp.bool_)
    cur = gscore
    for _ in range(TOPK_GROUPS):
        mx = jnp.max(cur, axis=0, keepdims=True)
        pick = gi == jnp.min(jnp.where(cur == mx, gi, N_EXPERT_GROUPS), axis=0, keepdims=True)
        gsel = gsel | pick
        cur = jnp.where(pick, neg, cur)
    emask = jnp.broadcast_to(gsel.reshape(N_EXPERT_GROUPS, 1, tt), sb3.shape).reshape(N_EXPERTS, tt)

    ei = lax.broadcasted_iota(jnp.int32, sb.shape, 0)
    sel = jnp.zeros(sb.shape, jnp.bool_)
    cur = jnp.where(emask, sb, neg)
    picks = []
    for _ in range(TOP_K):
        mx = jnp.max(cur, axis=0, keepdims=True)
        idx = jnp.min(jnp.where(cur == mx, ei, N_EXPERTS), axis=0, keepdims=True)
        pick = ei == idx
        picks.append((idx, pick))
        sel = sel | pick
        cur = jnp.where(pick, neg, cur)
    s_sel = jnp.where(sel, s, 0.0)
    gate = s_sel / jnp.sum(s_sel, axis=0, keepdims=True) * ROUTED_SCALE

    @pl.when(pl.program_id(0) == 0)
    def _():
        count_scr[...] = jnp.zeros_like(count_scr)

    sel_f = sel.astype(F32)
    incl = jnp.dot(sel_f.astype(BF16), tri_ref[...], preferred_element_type=F32)
    rank = count_scr[:, 0:1] + incl - sel_f
    count_scr[...] = count_scr[...] + jnp.sum(sel_f, axis=1, keepdims=True)
    count_ref[...] = count_scr[...].astype(jnp.int32)

    for kk, (idx, pick) in enumerate(picks):
        eidx_ref[kk:kk + 1, :] = idx
        gate_ref[kk:kk + 1, :] = jnp.sum(jnp.where(pick, gate, 0.0), axis=0, keepdims=True)
        rank_ref[kk:kk + 1, :] = jnp.sum(jnp.where(pick, rank, 0.0), axis=0, keepdims=True).astype(jnp.int32)

    for sl in range(PACK_ROWS):
        c0 = 2 * LANES * sl
        xq_ref[pl.ds(sl, tt, stride=PACK_ROWS), :] = _pack_bf16_pair(x[:, c0:c0 + LANES],
                                                                     x[:, c0 + LANES:c0 + 2 * LANES])


def _pack_bf16_pair(hi, lo):
    hb = lax.bitcast_convert_type(hi.astype(BF16).astype(F32), jnp.uint32)
    lb = lax.bitcast_convert_type(lo.astype(BF16).astype(F32), jnp.uint32)
    return hb | (lb >> 16)


def _unpack_bf16_pair(word):
    hi = lax.bitcast_convert_type(word & jnp.uint32(0xFFFF0000), F32)
    lo = lax.bitcast_convert_type(word << 16, F32)
    return hi, lo


def _moe_route(xt, w_router, bias):
    tt = ROUTER_TILE
    col = lambda i: (0, i)
    const = lambda i: (0, 0)
    tri = (jnp.arange(tt)[:, None] <= jnp.arange(tt)[None, :]).astype(BF16)
    return pl.pallas_call(
        functools.partial(_router_kernel, tt=tt),
        grid=(N_TOK // tt,),
        in_specs=[pl.BlockSpec((tt, D_MODEL), lambda i: (i, 0)),
                  pl.BlockSpec((N_EXPERTS, D_MODEL), const),
                  pl.BlockSpec((N_EXPERTS, 1), const),
                  pl.BlockSpec((tt, tt), const)],
        out_specs=[pl.BlockSpec((TOP_K, tt), col), pl.BlockSpec((TOP_K, tt), col), pl.BlockSpec((TOP_K, tt), col),
                   pl.BlockSpec((N_EXPERTS, LANES), const),
                   pl.BlockSpec((tt * PACK_ROWS, LANES), lambda i: (i, 0))],
        out_shape=[jax.ShapeDtypeStruct((TOP_K, N_TOK), jnp.int32),
                   jax.ShapeDtypeStruct((TOP_K, N_TOK), F32),
                   jax.ShapeDtypeStruct((TOP_K, N_TOK), jnp.int32),
                   jax.ShapeDtypeStruct((N_EXPERTS, LANES), jnp.int32),
                   jax.ShapeDtypeStruct((N_TOK * PACK_ROWS, LANES), jnp.uint32)],
        scratch_shapes=[pltpu.VMEM((N_EXPERTS, LANES), F32)],
        compiler_params=_params("arbitrary"),
        name="moe_router",
    )(xt, w_router.T, bias[:, None], tri)


def _sorted_row(pstart_ref, eidx_ref, rank_ref, j):
    return pstart_ref[eidx_ref[0, 0, j]] + rank_ref[0, 0, j]


def _dispatch_kernel(pstart_ref, eidx_ref, rank_ref, xq_ref, xs_hbm, row_ref, sem, *, tt):
    def issue(t, carry):
        src = xq_ref.at[pl.ds(pl.multiple_of(t * PACK_ROWS, PACK_ROWS), PACK_ROWS)]
        for kk in range(TOP_K):
            row = _sorted_row(pstart_ref, eidx_ref, rank_ref, t * TOP_K + kk)
            row_ref[0, 0, t * TOP_K + kk] = row
            row = pl.multiple_of(row * PACK_ROWS, PACK_ROWS)
            pltpu.make_async_copy(src, xs_hbm.at[pl.ds(row, PACK_ROWS)], sem).start()
        return carry
    lax.fori_loop(0, tt, issue, 0)
    for _ in range(TOP_K):
        pltpu.make_async_copy(xq_ref, xs_hbm.at[pl.ds(0, tt * PACK_ROWS)], sem).wait()


def _moe_dispatch(xq, pstarts, eidx_tok, rank_tok):
    tt = DISPATCH_TILE
    n_tiles = N_TOK // tt
    slots = pl.BlockSpec((1, 1, tt * TOP_K), lambda i, *_: (i, 0, 0), memory_space=pltpu.SMEM)
    return pl.pallas_call(
        functools.partial(_dispatch_kernel, tt=tt),
        grid_spec=pltpu.PrefetchScalarGridSpec(
            num_scalar_prefetch=1,
            grid=(n_tiles,),
            in_specs=[slots, slots, pl.BlockSpec((tt * PACK_ROWS, LANES), lambda i, *_: (i, 0))],
            out_specs=[pl.BlockSpec(memory_space=pl.ANY), slots],
            scratch_shapes=[pltpu.SemaphoreType.DMA(())]),
        out_shape=[jax.ShapeDtypeStruct((MOE_ROWS * PACK_ROWS, LANES), jnp.uint32),
                   jax.ShapeDtypeStruct((n_tiles, 1, tt * TOP_K), jnp.int32)],
        compiler_params=_params("arbitrary"),
        name="moe_dispatch",
    )(pstarts, eidx_tok.reshape(n_tiles, 1, tt * TOP_K), rank_tok.reshape(n_tiles, 1, tt * TOP_K), xq)


def _experts_kernel(be_ref, next_ref, nvalid_ref, nused_ref, xs_ref, wg_hbm, wu_hbm, wd_hbm, y_ref,
                    wg_f, wu_f, wd_f, wgu_s, wd_s, sem, *, layer):
    i = pl.program_id(0)
    tm = MOE_BLOCK

    def weight_copies(e):
        return (pltpu.make_async_copy(wg_hbm.at[layer, e], wg_f, sem.at[0]),
                pltpu.make_async_copy(wu_hbm.at[layer, e], wu_f, sem.at[1]),
                pltpu.make_async_copy(wd_hbm.at[layer, e], wd_f, sem.at[2]))

    def swiglu_rows(n_rows):
        parts = []
        for sl in range(PACK_ROWS):
            parts.extend(_unpack_bf16_pair(xs_ref[pl.ds(sl, n_rows, stride=PACK_ROWS), :]))
        x = jnp.concatenate(parts, axis=1).astype(BF16)
        gate_up = jnp.dot(x, wgu_s[...], preferred_element_type=F32)
        hidden = _silu(gate_up[:, :EXPERT_FF]) * gate_up[:, EXPERT_FF:]
        y = jnp.dot(hidden.astype(BF16), wd_s[...], preferred_element_type=F32)
        for sl in range(PACK_ROWS):
            c0 = 2 * LANES * sl
            y_ref[pl.ds(sl, n_rows, stride=PACK_ROWS), :] = _pack_bf16_pair(y[:, c0:c0 + LANES],
                                                                            y[:, c0 + LANES:c0 + 2 * LANES])

    @pl.when(i == 0)
    def _():
        for cp in weight_copies(be_ref[0]):
            cp.start()

    @pl.when(i < nused_ref[0])
    def _():
        first_of_expert = (i == 0) | (be_ref[i] != be_ref[jnp.maximum(i - 1, 0)])

        @pl.when(first_of_expert)
        def _():
            for cp in weight_copies(be_ref[i]):
                cp.wait()
            wgu_s[:, :EXPERT_FF] = wg_f[...].astype(BF16)
            wgu_s[:, EXPERT_FF:] = wu_f[...].astype(BF16)
            wd_s[...] = wd_f[...].astype(BF16)

            @pl.when(next_ref[i] >= 0)
            def _():
                for cp in weight_copies(next_ref[i]):
                    cp.start()

        @pl.when(nvalid_ref[i] > tm // 2)
        def _():
            swiglu_rows(tm)

        @pl.when(nvalid_ref[i] <= tm // 2)
        def _():
            swiglu_rows(tm // 2)


def _moe_experts(xs, block_e, next_e, n_valid, n_used, layer, w_gate, w_up, w_down):
    last = lambda i, be, nx, nv, nu: (jnp.minimum(i, nu[0] - 1), 0)
    return pl.pallas_call(
        functools.partial(_experts_kernel, layer=layer),
        grid_spec=pltpu.PrefetchScalarGridSpec(
            num_scalar_prefetch=4,
            grid=(MOE_N_BLOCKS,),
            in_specs=[pl.BlockSpec((MOE_BLOCK * PACK_ROWS, LANES), last),
                      pl.BlockSpec(memory_space=pl.ANY),
                      pl.BlockSpec(memory_space=pl.ANY),
                      pl.BlockSpec(memory_space=pl.ANY)],
            out_specs=pl.BlockSpec((MOE_BLOCK * PACK_ROWS, LANES), last),
            scratch_shapes=[pltpu.VMEM((D_MODEL, EXPERT_FF), F32), pltpu.VMEM((D_MODEL, EXPERT_FF), F32),
                            pltpu.VMEM((EXPERT_FF, D_MODEL), F32),
                            pltpu.VMEM((D_MODEL, 2 * EXPERT_FF), BF16),
                            pltpu.VMEM((EXPERT_FF, D_MODEL), BF16),
                            pltpu.SemaphoreType.DMA((3,))]),
        out_shape=jax.ShapeDtypeStruct((MOE_ROWS * PACK_ROWS, LANES), jnp.uint32),
        compiler_params=_params("arbitrary"),
        name="moe_experts",
    )(block_e, next_e, n_valid, n_used, xs, w_gate, w_up, w_down)


def _combine_kernel(row_ref, row_next_ref, x_ref, gate_ref, swg_ref, swu_ref, swd_ref, g_ref, b_ref, y_hbm, *rest,
                    tt, n_first):
    out_refs, (buf_a, buf_b, acc, sem) = rest[:-4], rest[-4:]
    i = pl.program_id(0)
    last = pl.num_programs(0) - 1

    def start_fetch(rows, into, into_sem, t):
        dst = pl.ds(t * PACK_ROWS, PACK_ROWS)
        for kk in range(TOP_K):
            src_row = pl.multiple_of(rows[0, 0, t * TOP_K + kk] * PACK_ROWS, PACK_ROWS)
            pltpu.make_async_copy(y_hbm.at[pl.ds(src_row, PACK_ROWS)], into.at[kk, dst], into_sem).start()

    def wait_fetch(into, into_sem):
        for kk in range(TOP_K):
            pltpu.make_async_copy(y_hbm.at[pl.ds(0, tt * PACK_ROWS)], into.at[kk], into_sem).wait()

    @pl.when(i == 0)
    def _():
        def body(t, carry):
            dst = pl.ds(pl.multiple_of(t * PACK_ROWS, PACK_ROWS), PACK_ROWS)
            for kk in range(TOP_K):
                src_row = pl.multiple_of(row_ref[0, 0, t * TOP_K + kk] * PACK_ROWS, PACK_ROWS)
                pltpu.make_async_copy(y_hbm.at[pl.ds(src_row, PACK_ROWS)], buf_a.at[kk, dst], sem.at[0]).start()
            return carry
        lax.fori_loop(0, tt, body, 0)

    def tile(cur, cur_sem, nxt, nxt_sem):
        per_phase = tt // (PACK_ROWS + 2)
        tokens = iter(range(tt))

        def issue(n):
            for _ in range(n):
                t = next(tokens, None)
                if t is not None:
                    start_fetch(row_next_ref, nxt, nxt_sem, t)

        x = x_ref[...]
        xb = x.astype(BF16)
        issue(per_phase)
        hidden = (_silu(jnp.dot(xb, swg_ref[...], preferred_element_type=F32))
                  * jnp.dot(xb, swu_ref[...], preferred_element_type=F32))
        issue(per_phase)
        shared = jnp.dot(hidden.astype(BF16), swd_ref[...], preferred_element_type=F32)
        wait_fetch(cur, cur_sem)
        gate = gate_ref[...]
        for sl in range(PACK_ROWS):
            issue(per_phase if sl < PACK_ROWS - 1 else tt)
            routed_hi = routed_lo = None
            for kk in range(TOP_K):
                hi, lo = _unpack_bf16_pair(cur[kk, pl.ds(sl, tt, stride=PACK_ROWS), :])
                g_k = gate[:, kk:kk + 1]
                routed_hi = g_k * hi if kk == 0 else routed_hi + g_k * hi
                routed_lo = g_k * lo if kk == 0 else routed_lo + g_k * lo
            for half, routed in enumerate((routed_hi, routed_lo)):
                cols = slice((2 * sl + half) * LANES, (2 * sl + half + 1) * LANES)
                acc[:, cols] = DN_ALPHA * x[:, cols] + (routed + shared[:, cols])
        res = _layer_norm(acc[...], g_ref[...], b_ref[...])
        if n_first is None:
            out_refs[0][...] = res
        else:
            @pl.when(i < n_first)
            def _():
                out_refs[0][...] = res

            @pl.when(i >= n_first)
            def _():
                out_refs[1][...] = res

        @pl.when(i == last)
        def _():
            wait_fetch(nxt, nxt_sem)

    @pl.when(i % 2 == 0)
    def _():
        tile(buf_a, sem.at[0], buf_b, sem.at[1])

    @pl.when(i % 2 == 1)
    def _():
        tile(buf_b, sem.at[1], buf_a, sem.at[0])


def _moe_combine(xt, ys, row_tiles, gate_tok, sw_gate, sw_up, sw_down, g, b, *, split):
    tt = COMBINE_TILE
    n_tiles = N_TOK // tt
    row = lambda i: (i, 0)
    const = lambda i: (0, 0)
    slots = pl.BlockSpec((1, 1, tt * TOP_K), lambda i: (i, 0, 0), memory_space=pltpu.SMEM)
    slots_next = pl.BlockSpec((1, 1, tt * TOP_K), lambda i: (jnp.minimum(i + 1, n_tiles - 1), 0, 0),
                              memory_space=pltpu.SMEM)
    row_tiles = row_tiles.reshape(n_tiles, 1, tt * TOP_K)
    if split:
        n_first = N_PROMPT_TOK // tt
        assert N_SAMPLE_TOK == tt
        out_specs = [pl.BlockSpec((tt, D_MODEL), lambda i: (jnp.minimum(i, n_first - 1), 0)),
                     pl.BlockSpec((tt, D_MODEL), const)]
        out_shape = [jax.ShapeDtypeStruct((N_PROMPT_TOK, D_MODEL), F32),
                     jax.ShapeDtypeStruct((N_SAMPLE_TOK, D_MODEL), F32)]
    else:
        n_first = None
        out_specs = [pl.BlockSpec((tt, D_MODEL), row)]
        out_shape = [jax.ShapeDtypeStruct((N_TOK, D_MODEL), F32)]
    return pl.pallas_call(
        functools.partial(_combine_kernel, tt=tt, n_first=n_first),
        grid=(n_tiles,),
        in_specs=[slots, slots_next,
                  pl.BlockSpec((tt, D_MODEL), row),
                  pl.BlockSpec((tt, TOP_K), row),
                  pl.BlockSpec((D_MODEL, EXPERT_FF), const),
                  pl.BlockSpec((D_MODEL, EXPERT_FF), const),
                  pl.BlockSpec((EXPERT_FF, D_MODEL), const),
                  pl.BlockSpec((1, D_MODEL), const),
                  pl.BlockSpec((1, D_MODEL), const),
                  pl.BlockSpec(memory_space=pl.ANY)],
        out_specs=out_specs,
        scratch_shapes=[pltpu.VMEM((TOP_K, tt * PACK_ROWS, LANES), jnp.uint32),
                        pltpu.VMEM((TOP_K, tt * PACK_ROWS, LANES), jnp.uint32),
                        pltpu.VMEM((tt, D_MODEL), F32), pltpu.SemaphoreType.DMA((2,))],
        out_shape=out_shape,
        compiler_params=_params("arbitrary"),
        name="moe_combine",
    )(row_tiles, row_tiles, xt, gate_tok, sw_gate.astype(BF16), sw_up.astype(BF16), sw_down.astype(BF16), g, b, ys)


def _moe_layer(xt, layer, w_router, bias, w_gate, w_up, w_down, sw_gate, sw_up, sw_down, g, b, *, split):
    eidx, gate, rank, counts, xq = _moe_route(xt, w_router, bias)

    counts = counts[:, 0]
    pcounts = (counts + MOE_BLOCK - 1) // MOE_BLOCK * MOE_BLOCK
    pends = jnp.cumsum(pcounts)
    pstarts = (pends - pcounts).astype(jnp.int32)
    n_used = (pends[-1] // MOE_BLOCK).astype(jnp.int32)
    blk = jnp.minimum(jnp.arange(MOE_N_BLOCKS, dtype=jnp.int32), n_used - 1) * MOE_BLOCK
    block_e = jnp.sum((pends[None, :] <= blk[:, None]).astype(jnp.int32), axis=1)
    block_e = jnp.minimum(block_e, N_EXPERTS - 1)
    after = pends[block_e] // MOE_BLOCK
    next_e = jnp.where(after < n_used, block_e[jnp.minimum(after, MOE_N_BLOCKS - 1)], -1).astype(jnp.int32)

    n_valid = jnp.clip(counts[block_e] - (blk - pstarts[block_e]), 0, MOE_BLOCK).astype(jnp.int32)

    eidx_tok, rank_tok = eidx.T, rank.T
    xs, rows = _moe_dispatch(xq, pstarts, eidx_tok, rank_tok)
    ys = _moe_experts(xs, block_e, next_e, n_valid, n_used[None], layer, w_gate, w_up, w_down)
    return _moe_combine(xt, ys, rows, gate.T, sw_gate, sw_up, sw_down, g, b, split=split)


def kernel(x_prompt, x_sample, state_pool, cache_swa_k, cache_swa_v, state_mlstm_c, state_mlstm_n, state_mlstm_m, pool_w, pool_scale, swa_w_qkv, swa_w_o, swa_sinks, mlstm_w_in, mlstm_b_gates, mlstm_norm_g, mlstm_w_out, ln_g, ln_b, moe_w_router, moe_router_bias, moe_w_gate, moe_w_up, moe_w_down, moe_shared_w_gate, moe_shared_w_up, moe_shared_w_down):
    d = D_MODEL
    xt = None
    pool_p, pool_s = [], []
    swk_p, swv_p, swk_s, swv_s = [], [], [], []
    mc_p, mn_p, mm_p, mc_s, mn_s, mm_s = [], [], [], [], [], []
    for i in range(DEPTH):
        kind, slot = i % N_MIXERS, i // N_MIXERS
        g0, b0 = ln_g[i, 0][None], ln_b[i, 0][None]
        if kind == 0:
            if i == 0:
                xp, xs = x_prompt.reshape(N_PROMPT_TOK, d), x_sample
            else:
                xp, xs = xt, xt[N_PROMPT_TOK:].reshape(DEC_BATCH, DEC_SEQ, d)
            xt, sp, ss = _pool_layer(xp, xs, state_pool[slot], pool_w[slot], pool_scale[slot][None], g0, b0)
            pool_p.append(sp)
            pool_s.append(ss)
        elif kind == 1:
            xt, kp, vp, ks, vs = _swa_layer(xt, cache_swa_k[slot], cache_swa_v[slot], swa_w_qkv[slot],
                                            swa_w_o[slot], swa_sinks[slot], g0, b0)
            swk_p.append(kp)
            swv_p.append(vp)
            swk_s.append(ks)
            swv_s.append(vs)
        else:
            xt, cp, np_, mp, cs, ns, ms = _mlstm_layer(
                xt, state_mlstm_c[slot], state_mlstm_n[slot], state_mlstm_m[slot], mlstm_w_in[slot],
                mlstm_b_gates[slot], mlstm_norm_g[slot][None], mlstm_w_out[slot], g0, b0)
            mc_p.append(cp)
            mn_p.append(np_)
            mm_p.append(mp)
            mc_s.append(cs)
            mn_s.append(ns)
            mm_s.append(ms)
        xt = _moe_layer(xt, i, moe_w_router[i], moe_router_bias[i], moe_w_gate, moe_w_up, moe_w_down,
                        moe_shared_w_gate[i], moe_shared_w_up[i], moe_shared_w_down[i],
                        ln_g[i, 1][None], ln_b[i, 1][None], split=(i == DEPTH - 1))
        if i < DEPTH - 1:
            xt = xt[0]
    y_p = xt[0].reshape(BATCH, SEQ, d)
    y_s = xt[1].reshape(DEC_BATCH, DEC_SEQ, d)
    return (y_p, y_s, jnp.stack(pool_p), jnp.stack(pool_s), jnp.stack(swk_p), jnp.stack(swv_p),
            jnp.stack(swk_s), jnp.stack(swv_s), jnp.stack(mc_p), jnp.stack(mn_p), jnp.stack(mm_p),
            jnp.stack(mc_s), jnp.stack(mn_s), jnp.stack(mm_s))
```

```python
import functools
import math

import jax
import jax.numpy as jnp
from jax import lax
from jax.experimental import pallas as pl
from jax.experimental.pallas import tpu as pltpu

F32 = jnp.float32
BF16 = jnp.bfloat16

D_MODEL = 2048
BATCH = 2
SEQ = 4096
DEPTH = 4
DEC_BATCH = 32
DEC_SEQ = 4
PAST_LEN = 16384
N_PROMPT_TOK = BATCH * SEQ
N_SAMPLE_TOK = DEC_BATCH * DEC_SEQ
N_TOK = N_PROMPT_TOK + N_SAMPLE_TOK

N_MIXERS = 3
DN_ALPHA = (2.0 * DEPTH) ** 0.25
LN_EPS = 1e-5

POOL_WINDOWS = (2, 4, 8, 16)
POOL_GROUP_DIM = D_MODEL // len(POOL_WINDOWS)
POOL_STATE = max(POOL_WINDOWS) - 1
POOL_HALO = POOL_STATE + 1

SWA_WINDOW = 128
SWA_HEAD_DIM = 64
SWA_HEADS = D_MODEL // SWA_HEAD_DIM
SWA_KV_HEADS = SWA_HEADS // 8
SWA_GROUP = SWA_HEADS // SWA_KV_HEADS
SWA_KV_DIM = SWA_KV_HEADS * SWA_HEAD_DIM

MLSTM_HEADS = 8
MLSTM_HEAD_DIM = D_MODEL // MLSTM_HEADS
MLSTM_CHUNK = 64
MLSTM_PHASES = 5
MLSTM_SAMPLE_CHUNK = 8

N_EXPERTS = 64
TOP_K = 8
N_EXPERT_GROUPS = 8
TOPK_GROUPS = 4
EXPERT_FF = D_MODEL // 4
ROUTED_SCALE = 2.5

VMEM_LIMIT_BYTES = 56 * 1024 * 1024

ROW_TILE = 832
ROUTER_TILE = 640
POOL_TILE = 512
MOE_BLOCK = 512
MOE_N_BLOCKS = N_TOK * TOP_K // MOE_BLOCK + N_EXPERTS
MOE_ROWS = MOE_N_BLOCKS * MOE_BLOCK
DISPATCH_TILE = 640
COMBINE_TILE = 128
LANES = 128
PACK_ROWS = D_MODEL // (2 * LANES)

_NT = (((1,), (1,)), ((), ()))
_TN = (((0,), (0,)), ((), ()))


def _params(*semantics):
    return pltpu.CompilerParams(dimension_semantics=semantics, vmem_limit_bytes=VMEM_LIMIT_BYTES)


def _layer_norm(z, g, b):
    mu = jnp.mean(z, axis=-1, keepdims=True)
    zc = z - mu
    var = jnp.mean(zc * zc, axis=-1, keepdims=True)
    return zc * lax.rsqrt(var + LN_EPS) * g + b


def _silu(x):
    return x * jax.nn.sigmoid(x)


def _matmul_kernel(x_ref, w_ref, o_ref):
    o_ref[...] = jnp.dot(x_ref[...].astype(BF16), w_ref[...], preferred_element_type=F32)


def _matmul(x, w, *, tm, tn):
    m, k = x.shape
    n = w.shape[1]
    return pl.pallas_call(
        _matmul_kernel,
        grid=(n // tn, m // tm),
        in_specs=[pl.BlockSpec((tm, k), lambda j, i: (i, 0)),
                  pl.BlockSpec((k, tn), lambda j, i: (0, j))],
        out_specs=pl.BlockSpec((tm, tn), lambda j, i: (i, j)),
        out_shape=jax.ShapeDtypeStruct((m, n), F32),
        compiler_params=_params("parallel", "parallel"),
        name="matmul",
    )(x, w)


def _matmul_ln_kernel(a_ref, w_ref, res_ref, g_ref, b_ref, o_ref):
    y = jnp.dot(a_ref[...], w_ref[...], preferred_element_type=F32)
    o_ref[...] = _layer_norm(DN_ALPHA * res_ref[...] + y, g_ref[...], b_ref[...])


def _matmul_ln(a, w, res, g, b):
    m, k = a.shape
    row = lambda i: (i, 0)
    const = lambda i: (0, 0)
    return pl.pallas_call(
        _matmul_ln_kernel,
        grid=(m // ROW_TILE,),
        in_specs=[pl.BlockSpec((ROW_TILE, k), row),
                  pl.BlockSpec((k, D_MODEL), const),
                  pl.BlockSpec((ROW_TILE, D_MODEL), row),
                  pl.BlockSpec((1, D_MODEL), const),
                  pl.BlockSpec((1, D_MODEL), const)],
        out_specs=pl.BlockSpec((ROW_TILE, D_MODEL), row),
        out_shape=jax.ShapeDtypeStruct((m, D_MODEL), F32),
        compiler_params=_params("parallel"),
        name="matmul_ln",
    )(a, w, res, g, b)


def _pool_core(zbuf, w_ref, scale_ref, g_ref, b_ref, o_ref, sums, *, tt, n_before):
    rows = POOL_HALO + tt
    avail = n_before + lax.broadcasted_iota(jnp.int32, (tt, 1), 0) + 1
    for grp, win in enumerate(POOL_WINDOWS):
        c0, c1 = grp * POOL_GROUP_DIM, (grp + 1) * POOL_GROUP_DIM
        xg = zbuf[POOL_HALO:POOL_HALO + tt, c0:c1]
        src, src_cols, span, level = zbuf, slice(c0, c1), 1, 0
        while span < win:
            dst = sums[level % 2]
            dst[0:span, :] = src[0:span, src_cols]
            dst[span:rows, :] = src[span:rows, src_cols] + src[0:rows - span, src_cols]
            src, src_cols, span, level = dst, slice(None), 2 * span, level + 1
        total = src[POOL_HALO:POOL_HALO + tt, src_cols]
        count = jnp.minimum(win, avail).astype(F32)
        diff = total / count - xg
        y = jnp.dot(diff.astype(BF16), w_ref[grp], preferred_element_type=F32) * scale_ref[:, c0:c1]
        o_ref[:, c0:c1] = DN_ALPHA * xg + y
    o_ref[...] = _layer_norm(o_ref[...], g_ref[...], b_ref[...])


def _pool_prompt_kernel(x_ref, halo_ref, w_ref, scale_ref, g_ref, b_ref, o_ref, zbuf, sum_a, sum_b, *, tt):
    i = pl.program_id(1)
    zbuf[0:POOL_HALO, :] = jnp.where(i == 0, 0.0, halo_ref[...])
    zbuf[POOL_HALO:POOL_HALO + tt, :] = x_ref[...]
    _pool_core(zbuf, w_ref, scale_ref, g_ref, b_ref, o_ref, (sum_a, sum_b), tt=tt, n_before=i * tt)


def _pool_sample_kernel(z_ref, w_ref, scale_ref, g_ref, b_ref, o_ref, sum_a, sum_b, *, tt):
    _pool_core(z_ref.at[0], w_ref, scale_ref, g_ref, b_ref, o_ref.at[0], (sum_a, sum_b), tt=tt, n_before=PAST_LEN)


def _pool_layer(xp, xs, state, w, scale, g, b):
    tt = POOL_TILE
    tiles = SEQ // tt
    halo_per_tile = tt // POOL_HALO
    const2 = lambda bb, i: (0, 0)
    w_bf = w.astype(BF16)
    out = pl.pallas_call(
        functools.partial(_pool_prompt_kernel, tt=tt),
        grid=(BATCH, tiles),
        in_specs=[pl.BlockSpec((tt, D_MODEL), lambda bb, i: (bb * tiles + i, 0)),
                  pl.BlockSpec((POOL_HALO, D_MODEL),
                               lambda bb, i: (jnp.maximum((bb * tiles + i) * halo_per_tile - 1, 0), 0)),
                  pl.BlockSpec(w_bf.shape, lambda bb, i: (0, 0, 0)),
                  pl.BlockSpec((1, D_MODEL), const2),
                  pl.BlockSpec((1, D_MODEL), const2),
                  pl.BlockSpec((1, D_MODEL), const2)],
        out_specs=pl.BlockSpec((tt, D_MODEL), lambda bb, i: (bb * tiles + i, 0)),
        out_shape=jax.ShapeDtypeStruct((N_TOK, D_MODEL), F32),
        scratch_shapes=[pltpu.VMEM((POOL_HALO + tt, D_MODEL), F32),
                        pltpu.VMEM((POOL_HALO + tt, POOL_GROUP_DIM), F32),
                        pltpu.VMEM((POOL_HALO + tt, POOL_GROUP_DIM), F32)],
        compiler_params=_params("parallel", "arbitrary"),
        name="pool_prompt",
    )(xp, xp, w_bf, scale, g, b)

    ts = 16
    zs = jnp.concatenate([jnp.zeros((DEC_BATCH, 1, D_MODEL), F32), state, xs,
                          jnp.zeros((DEC_BATCH, ts - DEC_SEQ, D_MODEL), F32)], axis=1)
    const1 = lambda bb: (0, 0)
    out_s = pl.pallas_call(
        functools.partial(_pool_sample_kernel, tt=ts),
        grid=(DEC_BATCH,),
        in_specs=[pl.BlockSpec((1, POOL_HALO + ts, D_MODEL), lambda bb: (bb, 0, 0)),
                  pl.BlockSpec(w_bf.shape, lambda bb: (0, 0, 0)),
                  pl.BlockSpec((1, D_MODEL), const1),
                  pl.BlockSpec((1, D_MODEL), const1),
                  pl.BlockSpec((1, D_MODEL), const1)],
        out_specs=pl.BlockSpec((1, ts, D_MODEL), lambda bb: (bb, 0, 0)),
        out_shape=jax.ShapeDtypeStruct((DEC_BATCH, ts, D_MODEL), F32),
        scratch_shapes=[pltpu.VMEM((POOL_HALO + ts, POOL_GROUP_DIM), F32),
                        pltpu.VMEM((POOL_HALO + ts, POOL_GROUP_DIM), F32)],
        compiler_params=_params("parallel"),
        name="pool_sample",
    )(zs, w_bf, scale, g, b)
    out = lax.dynamic_update_slice(out, out_s[:, :DEC_SEQ].reshape(N_SAMPLE_TOK, D_MODEL), (N_PROMPT_TOK, 0))

    new_p = jnp.stack([xp[(bb + 1) * SEQ - POOL_STATE:(bb + 1) * SEQ] for bb in range(BATCH)])
    new_s = jnp.concatenate([state, xs], axis=1)[:, DEC_SEQ:]
    return out, new_p, new_s


def _alibi_slope(h):
    return 2.0 ** (-8.0 * (h + 1.0) / SWA_HEADS)


def _attn_core(q_ref, k_all, v_all, sink_ref, o_ref, *, rows, first_block):
    w, dh = SWA_WINDOW, SWA_HEAD_DIM
    qi = lax.broadcasted_iota(jnp.int32, (rows, 2 * w), 0)
    sj = lax.broadcasted_iota(jnp.int32, (rows, 2 * w), 1)
    dist = (w + qi) - sj
    valid = (dist >= 0) & (dist <= w)
    if first_block is not None:
        valid = valid & ((sj >= w) | jnp.logical_not(first_block))
    masked_dist = jnp.where(valid, dist.astype(F32), jnp.inf)
    for kv in range(SWA_KV_HEADS):
        c0, c1 = kv * dh, (kv + 1) * dh
        heads = range(kv * SWA_GROUP, (kv + 1) * SWA_GROUP)
        q = jnp.concatenate([q_ref[:, h * dh:(h + 1) * dh] for h in heads], axis=0)
        q = (q * (dh ** -0.5)).astype(BF16)
        s_all = lax.dot_general(q, k_all[:, c0:c1], _NT, preferred_element_type=F32)
        probs, dens = [], []
        for j, h in enumerate(heads):
            sink = sink_ref[h]
            s = s_all[j * rows:(j + 1) * rows] - _alibi_slope(h) * masked_dist
            m = jnp.maximum(jnp.max(s, axis=1, keepdims=True), sink)
            e = jnp.exp(s - m)
            dens.append(jnp.sum(e, axis=1, keepdims=True) + jnp.exp(sink - m))
            probs.append(e.astype(BF16))
        o_all = jnp.dot(jnp.concatenate(probs, axis=0), v_all[:, c0:c1], preferred_element_type=F32)
        for j, h in enumerate(heads):
            o_ref[:, h * dh:(h + 1) * dh] = (o_all[j * rows:(j + 1) * rows] / dens[j]).astype(o_ref.dtype)


def _attn_prompt_kernel(sink_ref, q_ref, kp_ref, kc_ref, vp_ref, vc_ref, o_ref):
    n = pl.program_id(1)
    k_all = jnp.concatenate([kp_ref[...], kc_ref[...]], axis=0).astype(BF16)
    v_all = jnp.concatenate([vp_ref[...], vc_ref[...]], axis=0).astype(BF16)
    _attn_core(q_ref, k_all, v_all, sink_ref, o_ref, rows=SWA_WINDOW, first_block=(n == 0))


def _attn_sample_kernel(sink_ref, q_ref, k_ref, v_ref, o_ref, *, rows):
    _attn_core(q_ref.at[0], k_ref[0].astype(BF16), v_ref[0].astype(BF16), sink_ref, o_ref.at[0],
               rows=rows, first_block=None)


def _swa_layer(xt, cache_k, cache_v, w_qkv, w_o, sinks, g, b):
    w = SWA_WINDOW
    qkv = _matmul(xt, w_qkv.astype(BF16), tm=640, tn=1280)
    nb = SEQ // w
    k_col = D_MODEL // SWA_KV_DIM
    cur = lambda bb, n, *_: (bb * nb + n, k_col)
    prev = lambda bb, n, *_: (jnp.maximum(bb * nb + n - 1, 0), k_col)
    cur_v = lambda bb, n, *_: (bb * nb + n, k_col + 1)
    prev_v = lambda bb, n, *_: (jnp.maximum(bb * nb + n - 1, 0), k_col + 1)
    o = pl.pallas_call(
        _attn_prompt_kernel,
        grid_spec=pltpu.PrefetchScalarGridSpec(
            num_scalar_prefetch=1,
            grid=(BATCH, nb),
            in_specs=[pl.BlockSpec((w, D_MODEL), lambda bb, n, *_: (bb * nb + n, 0)),
                      pl.BlockSpec((w, SWA_KV_DIM), prev),
                      pl.BlockSpec((w, SWA_KV_DIM), cur),
                      pl.BlockSpec((w, SWA_KV_DIM), prev_v),
                      pl.BlockSpec((w, SWA_KV_DIM), cur_v)],
            out_specs=pl.BlockSpec((w, D_MODEL), lambda bb, n, *_: (bb * nb + n, 0))),
        out_shape=jax.ShapeDtypeStruct((N_TOK, D_MODEL), BF16),
        compiler_params=_params("parallel", "arbitrary"),
        name="attn_prompt",
    )(sinks, qkv, qkv, qkv, qkv, qkv)

    rows = 16
    qkv_s = qkv[N_PROMPT_TOK:].reshape(DEC_BATCH, DEC_SEQ, -1)
    q_s = jnp.pad(qkv_s[..., :D_MODEL], ((0, 0), (0, rows - DEC_SEQ), (0, 0)))
    k_new = qkv_s[..., D_MODEL:D_MODEL + SWA_KV_DIM]
    v_new = qkv_s[..., D_MODEL + SWA_KV_DIM:]
    kz = jnp.concatenate([cache_k.reshape(DEC_BATCH, w, SWA_KV_DIM), k_new], axis=1)
    vz = jnp.concatenate([cache_v.reshape(DEC_BATCH, w, SWA_KV_DIM), v_new], axis=1)
    pad_keys = ((0, 0), (0, w - DEC_SEQ), (0, 0))
    o_s = pl.pallas_call(
        functools.partial(_attn_sample_kernel, rows=rows),
        grid_spec=pltpu.PrefetchScalarGridSpec(
            num_scalar_prefetch=1,
            grid=(DEC_BATCH,),
            in_specs=[pl.BlockSpec((1, rows, D_MODEL), lambda bb, *_: (bb, 0, 0)),
                      pl.BlockSpec((1, 2 * w, SWA_KV_DIM), lambda bb, *_: (bb, 0, 0)),
                      pl.BlockSpec((1, 2 * w, SWA_KV_DIM), lambda bb, *_: (bb, 0, 0))],
            out_specs=pl.BlockSpec((1, rows, D_MODEL), lambda bb, *_: (bb, 0, 0))),
        out_shape=jax.ShapeDtypeStruct((DEC_BATCH, rows, D_MODEL), F32),
        compiler_params=_params("parallel"),
        name="attn_sample",
    )(sinks, q_s, jnp.pad(kz, pad_keys), jnp.pad(vz, pad_keys))
    o = lax.dynamic_update_slice(o, o_s[:, :DEC_SEQ].reshape(N_SAMPLE_TOK, D_MODEL).astype(BF16),
                                 (N_PROMPT_TOK, 0))

    out = _matmul_ln(o, w_o.astype(BF16), xt, g, b)

    kv_shape = (SWA_WINDOW, SWA_KV_HEADS, SWA_HEAD_DIM)
    kv_p = jnp.stack([qkv[(bb + 1) * SEQ - w:(bb + 1) * SEQ, D_MODEL:] for bb in range(BATCH)])
    new_k_p = kv_p[..., :SWA_KV_DIM].reshape((BATCH,) + kv_shape)
    new_v_p = kv_p[..., SWA_KV_DIM:].reshape((BATCH,) + kv_shape)
    new_k_s = kz[:, DEC_SEQ:].reshape((DEC_BATCH,) + kv_shape)
    new_v_s = vz[:, DEC_SEQ:].reshape((DEC_BATCH,) + kv_shape)
    return out, new_k_p, new_v_p, new_k_s, new_v_s


def _log_sigmoid(x):
    return jnp.minimum(x, 0.0) - jnp.log(1.0 + jnp.exp(-jnp.abs(x)))


def _mlstm_kernel(bias_ref, q_ref, k_ref, v_ref, og_ref, gates_ref, c0_ref, n0_ref, m0_ref, ng_ref,
                  h_ref, c_ref, n_ref, m_ref, c_scr, n_scr, m_scr, *, chunk, n_valid, group):
    step = pl.program_id(1)

    @pl.when(step == 0)
    def _():
        c_scr[...] = c0_ref[0]
        n_scr[...] = n0_ref[0]
        m_scr[...] = m0_ref[0]

    heads = [_mlstm_head(head, step, bias_ref, q_ref, k_ref, v_ref, og_ref, gates_ref, ng_ref, h_ref, c_scr, n_scr,
                         m_scr, chunk=chunk, n_valid=n_valid) for head in range(MLSTM_HEADS)]
    for h0 in range(0, MLSTM_HEADS, group):
        for _ in range(MLSTM_PHASES):
            for head_phases in heads[h0:h0 + group]:
                next(head_phases)

    @pl.when(step == pl.num_programs(1) - 1)
    def _():
        c_ref[0] = c_scr[...]
        n_ref[0] = n_scr[...]
        m_ref[0] = m_scr[...]


def _mlstm_head(head, step, bias_ref, q_ref, k_ref, v_ref, og_ref, gates_ref, ng_ref, h_ref, c_scr, n_scr, m_scr,
                *, chunk, n_valid):
    ln = chunk
    cols = slice(head * MLSTM_HEAD_DIM, (head + 1) * MLSTM_HEAD_DIM)
    ig = gates_ref[0, head, pl.ds(step, 1), :] + bias_ref[head]
    lf = _log_sigmoid(gates_ref[0, MLSTM_HEADS + head, pl.ds(step, 1), :] + bias_ref[MLSTM_HEADS + head])
    if n_valid < ln:
        col = lax.broadcasted_iota(jnp.int32, (1, ln), 1)
        ig = jnp.where(col < n_valid, ig, -1e30)
        lf = jnp.where(col < n_valid, lf, 0.0)

    ri = lax.broadcasted_iota(jnp.int32, (ln, ln), 0)
    ci = lax.broadcasted_iota(jnp.int32, (ln, ln), 1)
    eye = ri == ci
    causal = ci <= ri

    def to_col(row):
        return jnp.sum(jnp.where(eye, row, 0.0), axis=1, keepdims=True)

    f_col = to_col(lf)
    b_col = jnp.sum(jnp.where(causal, lf, 0.0), axis=1, keepdims=True)
    b_row = jnp.sum(jnp.where(ri <= ci, f_col, 0.0), axis=0, keepdims=True)
    b_last = jnp.sum(lf, axis=1, keepdims=True)
    m_prev = m_scr[head, :, 0:1]
    a_col = b_col + m_prev
    dmat = jnp.where(causal, b_col - b_row + ig, -jnp.inf)
    mt = jnp.maximum(a_col, jnp.max(dmat, axis=1, keepdims=True))
    w_inter = jnp.exp(a_col - mt)
    yield

    q = q_ref[:, cols]
    k = k_ref[:, cols] * (MLSTM_HEAD_DIM ** -0.5)
    v = v_ref[:, cols]
    qb = q.astype(BF16)
    kb = k.astype(BF16)
    c_prev = c_scr[head]
    n_prev = n_scr[head]
    scores = lax.dot_general(qb, kb, _NT, preferred_element_type=F32)
    yield
    w_intra = jnp.exp(dmat - mt) * scores
    num = (w_inter * lax.dot_general(qb, c_prev.astype(BF16), _NT, preferred_element_type=F32)
           + jnp.dot(w_intra.astype(BF16), v.astype(BF16), preferred_element_type=F32))
    den = (w_inter * jnp.sum(q * n_prev, axis=1, keepdims=True)
           + jnp.sum(w_intra, axis=1, keepdims=True))
    h = num / jnp.maximum(jnp.abs(den), jnp.exp(-mt))
    yield

    g_row = b_last - b_row + ig
    m_new = jnp.maximum(b_last + m_prev, jnp.max(g_row, axis=1, keepdims=True))
    decay = jnp.exp(b_last + m_prev - m_new)
    wg_col = to_col(jnp.exp(g_row - m_new))
    c_new = decay * c_prev + lax.dot_general((v * wg_col).astype(BF16), kb, _TN, preferred_element_type=F32)
    n_new = decay * n_prev + jnp.sum(wg_col * k, axis=0, keepdims=True)
    c_scr[head] = c_new
    n_scr[head] = n_new
    m_scr[head] = jnp.broadcast_to(m_new, (1, LANES))
    yield

    mu = jnp.mean(h, axis=1, keepdims=True)
    hc = h - mu
    var = jnp.mean(hc * hc, axis=1, keepdims=True)
    hn = hc * lax.rsqrt(var + LN_EPS) * ng_ref[:, cols] * jax.nn.sigmoid(og_ref[:, cols])
    h_ref[:, cols] = hn.astype(h_ref.dtype)
    yield


def _mlstm_scan(proj, gates, b_gates, norm_g, c0, n0, m0, *, batch, n_chunks, chunk, n_valid, out_rows, out_dtype,
                group):
    d, dh, nh = D_MODEL, MLSTM_HEAD_DIM, MLSTM_HEADS
    rows = lambda off: (lambda bb, cc, *_: (bb * n_chunks + cc, off))
    per_seq = lambda bb, cc, *_: (bb, 0, 0, 0)
    return pl.pallas_call(
        functools.partial(_mlstm_kernel, chunk=chunk, n_valid=n_valid, group=group),
        grid_spec=pltpu.PrefetchScalarGridSpec(
            num_scalar_prefetch=1,
            grid=(batch, n_chunks),
            in_specs=[pl.BlockSpec((chunk, d), rows(0)),
                      pl.BlockSpec((chunk, d), rows(1)),
                      pl.BlockSpec((chunk, d), rows(2)),
                      pl.BlockSpec((chunk, d), rows(3)),
                      pl.BlockSpec((1, 2 * nh, n_chunks, chunk), per_seq),
                      pl.BlockSpec((1, nh, dh, dh), per_seq),
                      pl.BlockSpec((1, nh, 1, dh), per_seq),
                      pl.BlockSpec((1, nh, 1, LANES), per_seq),
                      pl.BlockSpec((1, d), lambda bb, cc, *_: (0, 0))],
            out_specs=[pl.BlockSpec((chunk, d), lambda bb, cc, *_: (bb * n_chunks + cc, 0)),
                       pl.BlockSpec((1, nh, dh, dh), per_seq),
                       pl.BlockSpec((1, nh, 1, dh), per_seq),
                       pl.BlockSpec((1, nh, 1, LANES), per_seq)],
            scratch_shapes=[pltpu.VMEM((nh, dh, dh), F32), pltpu.VMEM((nh, 1, dh), F32),
                            pltpu.VMEM((nh, 1, LANES), F32)]),
        out_shape=[jax.ShapeDtypeStruct((out_rows, d), out_dtype),
                   jax.ShapeDtypeStruct((batch, nh, dh, dh), F32),
                   jax.ShapeDtypeStruct((batch, nh, 1, dh), F32),
                   jax.ShapeDtypeStruct((batch, nh, 1, LANES), F32)],
        compiler_params=_params("parallel", "arbitrary"),
        name="mlstm_scan",
    )(b_gates, proj, proj, proj, proj, gates, c0, n0[:, :, None, :],
      jnp.broadcast_to(m0[:, :, None, None], (batch, nh, 1, LANES)), norm_g)


def _mlstm_layer(xt, c0_s, n0_s, m0_s, w_in, b_gates, norm_g, w_out, g, b):
    d, nh, dh = D_MODEL, MLSTM_HEADS, MLSTM_HEAD_DIM
    proj = _matmul(xt, w_in[:, :4 * d].astype(BF16), tm=640, tn=2048)
    w_gates = jnp.pad(w_in[:, 4 * d:], ((0, 0), (0, 128 - 2 * nh)))
    gate_pre = _matmul(xt, w_gates.astype(BF16), tm=640, tn=128)[:, :2 * nh]

    nc = SEQ // MLSTM_CHUNK
    gates_p = gate_pre[:N_PROMPT_TOK].reshape(BATCH, nc, MLSTM_CHUNK, 2 * nh).transpose(0, 3, 1, 2)
    zeros = lambda *s: jnp.zeros(s, F32)
    hn, c_p, n_p, m_p = _mlstm_scan(
        proj, gates_p, b_gates, norm_g, zeros(BATCH, nh, dh, dh), zeros(BATCH, nh, dh), zeros(BATCH, nh),
        batch=BATCH, n_chunks=nc, chunk=MLSTM_CHUNK, n_valid=MLSTM_CHUNK, out_rows=N_TOK, out_dtype=BF16, group=2)

    ls = MLSTM_SAMPLE_CHUNK
    pad_t = ((0, 0), (0, ls - DEC_SEQ), (0, 0))
    proj_s = jnp.pad(proj[N_PROMPT_TOK:].reshape(DEC_BATCH, DEC_SEQ, 4 * d), pad_t).reshape(DEC_BATCH * ls, 4 * d)
    gates_s = jnp.pad(gate_pre[N_PROMPT_TOK:].reshape(DEC_BATCH, DEC_SEQ, 2 * nh), pad_t)
    gates_s = gates_s.transpose(0, 2, 1)[:, :, None, :]
    hn_s, c_s, n_s, m_s = _mlstm_scan(
        proj_s, gates_s, b_gates, norm_g, c0_s, n0_s, m0_s,
        batch=DEC_BATCH, n_chunks=1, chunk=ls, n_valid=DEC_SEQ, out_rows=DEC_BATCH * ls, out_dtype=F32,
        group=MLSTM_HEADS)
    hn_s = hn_s.reshape(DEC_BATCH, ls, d)[:, :DEC_SEQ].reshape(N_SAMPLE_TOK, d).astype(BF16)
    hn = lax.dynamic_update_slice(hn, hn_s, (N_PROMPT_TOK, 0))

    out = _matmul_ln(hn, w_out.astype(BF16), xt, g, b)
    return (out, c_p, n_p[:, :, 0], m_p[:, :, 0, 0], c_s, n_s[:, :, 0], m_s[:, :, 0, 0])


def _router_kernel(x_ref, wt_ref, bias_ref, tri_ref, eidx_ref, gate_ref, rank_ref, count_ref, xq_ref, count_scr,
                   *, tt):
    neg = -jnp.inf
    x = x_ref[...]
    logits = lax.dot_general(wt_ref[...].astype(BF16), x.astype(BF16), _NT, preferred_element_type=F32)
    s = jax.nn.sigmoid(logits)
    sb = s + bias_ref[...]
    per_group = N_EXPERTS // N_EXPERT_GROUPS

    sb3 = sb.reshape(N_EXPERT_GROUPS, per_group, tt)
    i3 = lax.broadcasted_iota(jnp.int32, sb3.shape, 1)
    m1 = jnp.max(sb3, axis=1, keepdims=True)
    first = jnp.min(jnp.where(sb3 == m1, i3, per_group), axis=1, keepdims=True)
    m2 = jnp.max(jnp.where(i3 == first, neg, sb3), axis=1, keepdims=True)
    gscore = (m1 + m2).reshape(N_EXPERT_GROUPS, tt)

    gi = lax.broadcasted_iota(jnp.int32, gscore.shape, 0)
    gsel = jnp.zeros(gscore.shape, jnp.bool_)
    cur = gscore
    for _ in range(TOPK_GROUPS):
        mx = jnp.max(cur, axis=0, keepdims=True)
        pick = gi == jnp.min(jnp.where(cur == mx, gi, N_EXPERT_GROUPS), axis=0, keepdims=True)
        gsel = gsel | pick
        cur = jnp.where(pick, neg, cur)
    emask = jnp.broadcast_to(gsel.reshape(N_EXPERT_GROUPS, 1, tt), sb3.shape).reshape(N_EXPERTS, tt)

    ei = lax.broadcasted_iota(jnp.int32, sb.shape, 0)
    sel = jnp.zeros(sb.shape, jnp.bool_)
    cur = jnp.where(emask, sb, neg)
    picks = []
    for _ in range(TOP_K):
        mx = jnp.max(cur, axis=0, keepdims=True)
        idx = jnp.min(jnp.where(cur == mx, ei, N_EXPERTS), axis=0, keepdims=True)
        pick = ei == idx
        picks.append((idx, pick))
        sel = sel | pick
        cur = jnp.where(pick, neg, cur)
    s_sel = jnp.where(sel, s, 0.0)
    gate = s_sel / jnp.sum(s_sel, axis=0, keepdims=True) * ROUTED_SCALE

    @pl.when(pl.program_id(0) == 0)
    def _():
        count_scr[...] = jnp.zeros_like(count_scr)

    sel_f = sel.astype(F32)
    incl = jnp.dot(sel_f.astype(BF16), tri_ref[...], preferred_element_type=F32)
    rank = count_scr[:, 0:1] + incl - sel_f
    count_scr[...] = count_scr[...] + jnp.sum(sel_f, axis=1, keepdims=True)
    count_ref[...] = count_scr[...].astype(jnp.int32)

    for kk, (idx, pick) in enumerate(picks):
        eidx_ref[kk:kk + 1, :] = idx
        gate_ref[kk:kk + 1, :] = jnp.sum(jnp.where(pick, gate, 0.0), axis=0, keepdims=True)
        rank_ref[kk:kk + 1, :] = jnp.sum(jnp.where(pick, rank, 0.0), axis=0, keepdims=True).astype(jnp.int32)

    for sl in range(PACK_ROWS):
        c0 = 2 * LANES * sl
        xq_ref[pl.ds(sl, tt, stride=PACK_ROWS), :] = _pack_bf16_pair(x[:, c0:c0 + LANES],
                                                                     x[:, c0 + LANES:c0 + 2 * LANES])


def _pack_bf16_pair(hi, lo):
    hb = lax.bitcast_convert_type(hi.astype(BF16).astype(F32), jnp.uint32)
    lb = lax.bitcast_convert_type(lo.astype(BF16).astype(F32), jnp.uint32)
    return hb | (lb >> 16)


def _unpack_bf16_pair(word):
    hi = lax.bitcast_convert_type(word & jnp.uint32(0xFFFF0000), F32)
    lo = lax.bitcast_convert_type(word << 16, F32)
    return hi, lo


def _moe_route(xt, w_router, bias):
    tt = ROUTER_TILE
    col = lambda i: (0, i)
    const = lambda i: (0, 0)
    tri = (jnp.arange(tt)[:, None] <= jnp.arange(tt)[None, :]).astype(BF16)
    return pl.pallas_call(
        functools.partial(_router_kernel, tt=tt),
        grid=(N_TOK // tt,),
        in_specs=[pl.BlockSpec((tt, D_MODEL), lambda i: (i, 0)),
                  pl.BlockSpec((N_EXPERTS, D_MODEL), const),
                  pl.BlockSpec((N_EXPERTS, 1), const),
                  pl.BlockSpec((tt, tt), const)],
        out_specs=[pl.BlockSpec((TOP_K, tt), col), pl.BlockSpec((TOP_K, tt), col), pl.BlockSpec((TOP_K, tt), col),
                   pl.BlockSpec((N_EXPERTS, LANES), const),
                   pl.BlockSpec((tt * PACK_ROWS, LANES), lambda i: (i, 0))],
        out_shape=[jax.ShapeDtypeStruct((TOP_K, N_TOK), jnp.int32),
                   jax.ShapeDtypeStruct((TOP_K, N_TOK), F32),
                   jax.ShapeDtypeStruct((TOP_K, N_TOK), jnp.int32),
                   jax.ShapeDtypeStruct((N_EXPERTS, LANES), jnp.int32),
                   jax.ShapeDtypeStruct((N_TOK * PACK_ROWS, LANES), jnp.uint32)],
        scratch_shapes=[pltpu.VMEM((N_EXPERTS, LANES), F32)],
        compiler_params=_params("arbitrary"),
        name="moe_router",
    )(xt, w_router.T, bias[:, None], tri)


def _sorted_row(pstart_ref, eidx_ref, rank_ref, j):
    return pstart_ref[eidx_ref[0, 0, j]] + rank_ref[0, 0, j]


def _dispatch_kernel(pstart_ref, eidx_ref, rank_ref, xq_ref, xs_hbm, row_ref, sem, *, tt):
    def issue(t, carry):
        src = xq_ref.at[pl.ds(pl.multiple_of(t * PACK_ROWS, PACK_ROWS), PACK_ROWS)]
        for kk in range(TOP_K):
            row = _sorted_row(pstart_ref, eidx_ref, rank_ref, t * TOP_K + kk)
            row_ref[0, 0, t * TOP_K + kk] = row
            row = pl.multiple_of(row * PACK_ROWS, PACK_ROWS)
            pltpu.make_async_copy(src, xs_hbm.at[pl.ds(row, PACK_ROWS)], sem).start()
        return carry
    lax.fori_loop(0, tt, issue, 0)
    for _ in range(TOP_K):
        pltpu.make_async_copy(xq_ref, xs_hbm.at[pl.ds(0, tt * PACK_ROWS)], sem).wait()


def _moe_dispatch(xq, pstarts, eidx_tok, rank_tok):
    tt = DISPATCH_TILE
    n_tiles = N_TOK // tt
    slots = pl.BlockSpec((1, 1, tt * TOP_K), lambda i, *_: (i, 0, 0), memory_space=pltpu.SMEM)
    return pl.pallas_call(
        functools.partial(_dispatch_kernel, tt=tt),
        grid_spec=pltpu.PrefetchScalarGridSpec(
            num_scalar_prefetch=1,
            grid=(n_tiles,),
            in_specs=[slots, slots, pl.BlockSpec((tt * PACK_ROWS, LANES), lambda i, *_: (i, 0))],
            out_specs=[pl.BlockSpec(memory_space=pl.ANY), slots],
            scratch_shapes=[pltpu.SemaphoreType.DMA(())]),
        out_shape=[jax.ShapeDtypeStruct((MOE_ROWS * PACK_ROWS, LANES), jnp.uint32),
                   jax.ShapeDtypeStruct((n_tiles, 1, tt * TOP_K), jnp.int32)],
        compiler_params=_params("arbitrary"),
        name="moe_dispatch",
    )(pstarts, eidx_tok.reshape(n_tiles, 1, tt * TOP_K), rank_tok.reshape(n_tiles, 1, tt * TOP_K), xq)


def _experts_kernel(be_ref, next_ref, nvalid_ref, nused_ref, xs_ref, wg_hbm, wu_hbm, wd_hbm, y_ref,
                    wg_f, wu_f, wd_f, wgu_s, wd_s, sem, *, layer):
    i = pl.program_id(0)
    tm = MOE_BLOCK

    def weight_copies(e):
        return (pltpu.make_async_copy(wg_hbm.at[layer, e], wg_f, sem.at[0]),
                pltpu.make_async_copy(wu_hbm.at[layer, e], wu_f, sem.at[1]),
                pltpu.make_async_copy(wd_hbm.at[layer, e], wd_f, sem.at[2]))

    def swiglu_rows(n_rows):
        parts = []
        for sl in range(PACK_ROWS):
            parts.extend(_unpack_bf16_pair(xs_ref[pl.ds(sl, n_rows, stride=PACK_ROWS), :]))
        x = jnp.concatenate(parts, axis=1).astype(BF16)
        gate_up = jnp.dot(x, wgu_s[...], preferred_element_type=F32)
        hidden = _silu(gate_up[:, :EXPERT_FF]) * gate_up[:, EXPERT_FF:]
        y = jnp.dot(hidden.astype(BF16), wd_s[...], preferred_element_type=F32)
        for sl in range(PACK_ROWS):
            c0 = 2 * LANES * sl
            y_ref[pl.ds(sl, n_rows, stride=PACK_ROWS), :] = _pack_bf16_pair(y[:, c0:c0 + LANES],
                                                                            y[:, c0 + LANES:c0 + 2 * LANES])

    @pl.when(i == 0)
    def _():
        for cp in weight_copies(be_ref[0]):
            cp.start()

    @pl.when(i < nused_ref[0])
    def _():
        first_of_expert = (i == 0) | (be_ref[i] != be_ref[jnp.maximum(i - 1, 0)])

        @pl.when(first_of_expert)
        def _():
            for cp in weight_copies(be_ref[i]):
                cp.wait()
            wgu_s[:, :EXPERT_FF] = wg_f[...].astype(BF16)
            wgu_s[:, EXPERT_FF:] = wu_f[...].astype(BF16)
            wd_s[...] = wd_f[...].astype(BF16)

            @pl.when(next_ref[i] >= 0)
            def _():
                for cp in weight_copies(next_ref[i]):
                    cp.start()

        @pl.when(nvalid_ref[i] > tm // 2)
        def _():
            swiglu_rows(tm)

        @pl.when(nvalid_ref[i] <= tm // 2)
        def _():
            swiglu_rows(tm // 2)


def _moe_experts(xs, block_e, next_e, n_valid, n_used, layer, w_gate, w_up, w_down):
    last = lambda i, be, nx, nv, nu: (jnp.minimum(i, nu[0] - 1), 0)
    return pl.pallas_call(
        functools.partial(_experts_kernel, layer=layer),
        grid_spec=pltpu.PrefetchScalarGridSpec(
            num_scalar_prefetch=4,
            grid=(MOE_N_BLOCKS,),
            in_specs=[pl.BlockSpec((MOE_BLOCK * PACK_ROWS, LANES), last),
                      pl.BlockSpec(memory_space=pl.ANY),
                      pl.BlockSpec(memory_space=pl.ANY),
                      pl.BlockSpec(memory_space=pl.ANY)],
            out_specs=pl.BlockSpec((MOE_BLOCK * PACK_ROWS, LANES), last),
            scratch_shapes=[pltpu.VMEM((D_MODEL, EXPERT_FF), F32), pltpu.VMEM((D_MODEL, EXPERT_FF), F32),
                            pltpu.VMEM((EXPERT_FF, D_MODEL), F32),
                            pltpu.VMEM((D_MODEL, 2 * EXPERT_FF), BF16),
                            pltpu.VMEM((EXPERT_FF, D_MODEL), BF16),
                            pltpu.SemaphoreType.DMA((3,))]),
        out_shape=jax.ShapeDtypeStruct((MOE_ROWS * PACK_ROWS, LANES), jnp.uint32),
        compiler_params=_params("arbitrary"),
        name="moe_experts",
    )(block_e, next_e, n_valid, n_used, xs, w_gate, w_up, w_down)


def _combine_kernel(row_ref, row_next_ref, x_ref, gate_ref, swg_ref, swu_ref, swd_ref, g_ref, b_ref, y_hbm, *rest,
                    tt, n_first):
    out_refs, (buf_a, buf_b, acc, sem) = rest[:-4], rest[-4:]
    i = pl.program_id(0)
    last = pl.num_programs(0) - 1

    def start_fetch(rows, into, into_sem, t):
        dst = pl.ds(t * PACK_ROWS, PACK_ROWS)
        for kk in range(TOP_K):
            src_row = pl.multiple_of(rows[0, 0, t * TOP_K + kk] * PACK_ROWS, PACK_ROWS)
            pltpu.make_async_copy(y_hbm.at[pl.ds(src_row, PACK_ROWS)], into.at[kk, dst], into_sem).start()

    def wait_fetch(into, into_sem):
        for kk in range(TOP_K):
            pltpu.make_async_copy(y_hbm.at[pl.ds(0, tt * PACK_ROWS)], into.at[kk], into_sem).wait()

    @pl.when(i == 0)
    def _():
        def body(t, carry):
            dst = pl.ds(pl.multiple_of(t * PACK_ROWS, PACK_ROWS), PACK_ROWS)
            for kk in range(TOP_K):
                src_row = pl.multiple_of(row_ref[0, 0, t * TOP_K + kk] * PACK_ROWS, PACK_ROWS)
                pltpu.make_async_copy(y_hbm.at[pl.ds(src_row, PACK_ROWS)], buf_a.at[kk, dst], sem.at[0]).start()
            return carry
        lax.fori_loop(0, tt, body, 0)

    def tile(cur, cur_sem, nxt, nxt_sem):
        per_phase = tt // (PACK_ROWS + 2)
        tokens = iter(range(tt))

        def issue(n):
            for _ in range(n):
                t = next(tokens, None)
                if t is not None:
                    start_fetch(row_next_ref, nxt, nxt_sem, t)

        x = x_ref[...]
        xb = x.astype(BF16)
        issue(per_phase)
        hidden = (_silu(jnp.dot(xb, swg_ref[...], preferred_element_type=F32))
                  * jnp.dot(xb, swu_ref[...], preferred_element_type=F32))
        issue(per_phase)
        shared = jnp.dot(hidden.astype(BF16), swd_ref[...], preferred_element_type=F32)
        wait_fetch(cur, cur_sem)
        gate = gate_ref[...]
        for sl in range(PACK_ROWS):
            issue(per_phase if sl < PACK_ROWS - 1 else tt)
            routed_hi = routed_lo = None
            for kk in range(TOP_K):
                hi, lo = _unpack_bf16_pair(cur[kk, pl.ds(sl, tt, stride=PACK_ROWS), :])
                g_k = gate[:, kk:kk + 1]
                routed_hi = g_k * hi if kk == 0 else routed_hi + g_k * hi
                routed_lo = g_k * lo if kk == 0 else routed_lo + g_k * lo
            for half, routed in enumerate((routed_hi, routed_lo)):
                cols = slice((2 * sl + half) * LANES, (2 * sl + half + 1) * LANES)
                acc[:, cols] = DN_ALPHA * x[:, cols] + (routed + shared[:, cols])
        res = _layer_norm(acc[...], g_ref[...], b_ref[...])
        if n_first is None:
            out_refs[0][...] = res
        else:
            @pl.when(i < n_first)
            def _():
                out_refs[0][...] = res

            @pl.when(i >= n_first)
            def _():
                out_refs[1][...] = res

        @pl.when(i == last)
        def _():
            wait_fetch(nxt, nxt_sem)

    @pl.when(i % 2 == 0)
    def _():
        tile(buf_a, sem.at[0], buf_b, sem.at[1])

    @pl.when(i % 2 == 1)
    def _():
        tile(buf_b, sem.at[1], buf_a, sem.at[0])


def _moe_combine(xt, ys, row_tiles, gate_tok, sw_gate, sw_up, sw_down, g, b, *, split):
    tt = COMBINE_TILE
    n_tiles = N_TOK // tt
    row = lambda i: (i, 0)
    const = lambda i: (0, 0)
    slots = pl.BlockSpec((1, 1, tt * TOP_K), lambda i: (i, 0, 0), memory_space=pltpu.SMEM)
    slots_next = pl.BlockSpec((1, 1, tt * TOP_K), lambda i: (jnp.minimum(i + 1, n_tiles - 1), 0, 0),
                              memory_space=pltpu.SMEM)
    row_tiles = row_tiles.reshape(n_tiles, 1, tt * TOP_K)
    if split:
        n_first = N_PROMPT_TOK // tt
        assert N_SAMPLE_TOK == tt
        out_specs = [pl.BlockSpec((tt, D_MODEL), lambda i: (jnp.minimum(i, n_first - 1), 0)),
                     pl.BlockSpec((tt, D_MODEL), const)]
        out_shape = [jax.ShapeDtypeStruct((N_PROMPT_TOK, D_MODEL), F32),
                     jax.ShapeDtypeStruct((N_SAMPLE_TOK, D_MODEL), F32)]
    else:
        n_first = None
        out_specs = [pl.BlockSpec((tt, D_MODEL), row)]
        out_shape = [jax.ShapeDtypeStruct((N_TOK, D_MODEL), F32)]
    return pl.pallas_call(
        functools.partial(_combine_kernel, tt=tt, n_first=n_first),
        grid=(n_tiles,),
        in_specs=[slots, slots_next,
                  pl.BlockSpec((tt, D_MODEL), row),
                  pl.BlockSpec((tt, TOP_K), row),
                  pl.BlockSpec((D_MODEL, EXPERT_FF), const),
                  pl.BlockSpec((D_MODEL, EXPERT_FF), const),
                  pl.BlockSpec((EXPERT_FF, D_MODEL), const),
                  pl.BlockSpec((1, D_MODEL), const),
                  pl.BlockSpec((1, D_MODEL), const),
                  pl.BlockSpec(memory_space=pl.ANY)],
        out_specs=out_specs,
        scratch_shapes=[pltpu.VMEM((TOP_K, tt * PACK_ROWS, LANES), jnp.uint32),
                        pltpu.VMEM((TOP_K, tt * PACK_ROWS, LANES), jnp.uint32),
                        pltpu.VMEM((tt, D_MODEL), F32), pltpu.SemaphoreType.DMA((2,))],
        out_shape=out_shape,
        compiler_params=_params("arbitrary"),
        name="moe_combine",
    )(row_tiles, row_tiles, xt, gate_tok, sw_gate.astype(BF16), sw_up.astype(BF16), sw_down.astype(BF16), g, b, ys)


def _moe_layer(xt, layer, w_router, bias, w_gate, w_up, w_down, sw_gate, sw_up, sw_down, g, b, *, split):
    eidx, gate, rank, counts, xq = _moe_route(xt, w_router, bias)

    counts = counts[:, 0]
    pcounts = (counts + MOE_BLOCK - 1) // MOE_BLOCK * MOE_BLOCK
    pends = jnp.cumsum(pcounts)
    pstarts = (pends - pcounts).astype(jnp.int32)
    n_used = (pends[-1] // MOE_BLOCK).astype(jnp.int32)
    blk = jnp.minimum(jnp.arange(MOE_N_BLOCKS, dtype=jnp.int32), n_used - 1) * MOE_BLOCK
    block_e = jnp.sum((pends[None, :] <= blk[:, None]).astype(jnp.int32), axis=1)
    block_e = jnp.minimum(block_e, N_EXPERTS - 1)
    after = pends[block_e] // MOE_BLOCK
    next_e = jnp.where(after < n_used, block_e[jnp.minimum(after, MOE_N_BLOCKS - 1)], -1).astype(jnp.int32)

    n_valid = jnp.clip(counts[block_e] - (blk - pstarts[block_e]), 0, MOE_BLOCK).astype(jnp.int32)

    eidx_tok, rank_tok = eidx.T, rank.T
    xs, rows = _moe_dispatch(xq, pstarts, eidx_tok, rank_tok)
    ys = _moe_experts(xs, block_e, next_e, n_valid, n_used[None], layer, w_gate, w_up, w_down)
    return _moe_combine(xt, ys, rows, gate.T, sw_gate, sw_up, sw_down, g, b, split=split)


def kernel(x_prompt, x_sample, state_pool, cache_swa_k, cache_swa_v, state_mlstm_c, state_mlstm_n, state_mlstm_m, pool_w, pool_scale, swa_w_qkv, swa_w_o, swa_sinks, mlstm_w_in, mlstm_b_gates, mlstm_norm_g, mlstm_w_out, ln_g, ln_b, moe_w_router, moe_router_bias, moe_w_gate, moe_w_up, moe_w_down, moe_shared_w_gate, moe_shared_w_up, moe_shared_w_down):
    d = D_MODEL
    xt = None
    pool_p, pool_s = [], []
    swk_p, swv_p, swk_s, swv_s = [], [], [], []
    mc_p, mn_p, mm_p, mc_s, mn_s, mm_s = [], [], [], [], [], []
    for i in range(DEPTH):
        kind, slot = i % N_MIXERS, i // N_MIXERS
        g0, b0 = ln_g[i, 0][None], ln_b[i, 0][None]
        if kind == 0:
            if i == 0:
                xp, xs = x_prompt.reshape(N_PROMPT_TOK, d), x_sample
            else:
                xp, xs = xt, xt[N_PROMPT_TOK:].reshape(DEC_BATCH, DEC_SEQ, d)
            xt, sp, ss = _pool_layer(xp, xs, state_pool[slot], pool_w[slot], pool_scale[slot][None], g0, b0)
            pool_p.append(sp)
            pool_s.append(ss)
        elif kind == 1:
            xt, kp, vp, ks, vs = _swa_layer(xt, cache_swa_k[slot], cache_swa_v[slot], swa_w_qkv[slot],
                                            swa_w_o[slot], swa_sinks[slot], g0, b0)
            swk_p.append(kp)
            swv_p.append(vp)
            swk_s.append(ks)
            swv_s.append(vs)
        else:
            xt, cp, np_, mp, cs, ns, ms = _mlstm_layer(
                xt, state_mlstm_c[slot], state_mlstm_n[slot], state_mlstm_m[slot], mlstm_w_in[slot],
                mlstm_b_gates[slot], mlstm_norm_g[slot][None], mlstm_w_out[slot], g0, b0)
            mc_p.append(cp)
            mn_p.append(np_)
            mm_p.append(mp)
            mc_s.append(cs)
            mn_s.append(ns)
            mm_s.append(ms)
        xt = _moe_layer(xt, i, moe_w_router[i], moe_router_bias[i], moe_w_gate, moe_w_up, moe_w_down,
                        moe_shared_w_gate[i], moe_shared_w_up[i], moe_shared_w_down[i],
                        ln_g[i, 1][None], ln_b[i, 1][None], split=(i == DEPTH - 1))
        if i < DEPTH - 1:
            xt = xt[0]
    y_p = xt[0].reshape(BATCH, SEQ, d)
    y_s = xt[1].reshape(DEC_BATCH, DEC_SEQ, d)
    return (y_p, y_s, jnp.stack(pool_p), jnp.stack(pool_s), jnp.stack(swk_p), jnp.stack(swv_p),
            jnp.stack(swk_s), jnp.stack(swv_s), jnp.stack(mc_p), jnp.stack(mn_p), jnp.stack(mm_p),
            jnp.stack(mc_s), jnp.stack(mn_s), jnp.stack(mm_s))
```

```python
import functools
import math

import jax
import jax.numpy as jnp
from jax import lax
from jax.experimental import pallas as pl
from jax.experimental.pallas import tpu as pltpu

F32 = jnp.float32
BF16 = jnp.bfloat16

D_MODEL = 2048
BATCH = 2
SEQ = 4096
DEPTH = 4
DEC_BATCH = 32
DEC_SEQ = 4
PAST_LEN = 16384
N_PROMPT_TOK = BATCH * SEQ
N_SAMPLE_TOK = DEC_BATCH * DEC_SEQ
N_TOK = N_PROMPT_TOK + N_SAMPLE_TOK

N_MIXERS = 3
DN_ALPHA = (2.0 * DEPTH) ** 0.25
LN_EPS = 1e-5

POOL_WINDOWS = (2, 4, 8, 16)
POOL_GROUP_DIM = D_MODEL // len(POOL_WINDOWS)
POOL_STATE = max(POOL_WINDOWS) - 1
POOL_HALO = POOL_STATE + 1

SWA_WINDOW = 128
SWA_HEAD_DIM = 64
SWA_HEADS = D_MODEL // SWA_HEAD_DIM
SWA_KV_HEADS = SWA_HEADS // 8
SWA_GROUP = SWA_HEADS // SWA_KV_HEADS
SWA_KV_DIM = SWA_KV_HEADS * SWA_HEAD_DIM

MLSTM_HEADS = 8
MLSTM_HEAD_DIM = D_MODEL // MLSTM_HEADS
MLSTM_CHUNK = 64
MLSTM_PHASES = 5
MLSTM_SAMPLE_CHUNK = 8

N_EXPERTS = 64
TOP_K = 8
N_EXPERT_GROUPS = 8
TOPK_GROUPS = 4
EXPERT_FF = D_MODEL // 4
ROUTED_SCALE = 2.5

VMEM_LIMIT_BYTES = 56 * 1024 * 1024

ROW_TILE = 832
ROUTER_TILE = 640
POOL_TILE = 512
MOE_BLOCK = 512
MOE_N_BLOCKS = N_TOK * TOP_K // MOE_BLOCK + N_EXPERTS
MOE_ROWS = MOE_N_BLOCKS * MOE_BLOCK
DISPATCH_TILE = 640
COMBINE_TILE = 128
LANES = 128
PACK_ROWS = D_MODEL // (2 * LANES)

_NT = (((1,), (1,)), ((), ()))
_TN = (((0,), (0,)), ((), ()))


def _params(*semantics):
    return pltpu.CompilerParams(dimension_semantics=semantics, vmem_limit_bytes=VMEM_LIMIT_BYTES)


def _layer_norm(z, g, b):
    mu = jnp.mean(z, axis=-1, keepdims=True)
    zc = z - mu
    var = jnp.mean(zc * zc, axis=-1, keepdims=True)
    return zc * lax.rsqrt(var + LN_EPS) * g + b


def _silu(x):
    return x * jax.nn.sigmoid(x)


def _matmul_kernel(x_ref, w_ref, o_ref):
    o_ref[...] = jnp.dot(x_ref[...].astype(BF16), w_ref[...], preferred_element_type=F32)


def _matmul(x, w, *, tm, tn):
    m, k = x.shape
    n = w.shape[1]
    return pl.pallas_call(
        _matmul_kernel,
        grid=(n // tn, m // tm),
        in_specs=[pl.BlockSpec((tm, k), lambda j, i: (i, 0)),
                  pl.BlockSpec((k, tn), lambda j, i: (0, j))],
        out_specs=pl.BlockSpec((tm, tn), lambda j, i: (i, j)),
        out_shape=jax.ShapeDtypeStruct((m, n), F32),
        compiler_params=_params("parallel", "parallel"),
        name="matmul",
    )(x, w)


def _matmul_ln_kernel(a_ref, w_ref, res_ref, g_ref, b_ref, o_ref):
    y = jnp.dot(a_ref[...], w_ref[...], preferred_element_type=F32)
    o_ref[...] = _layer_norm(DN_ALPHA * res_ref[...] + y, g_ref[...], b_ref[...])


def _matmul_ln(a, w, res, g, b):
    m, k = a.shape
    row = lambda i: (i, 0)
    const = lambda i: (0, 0)
    return pl.pallas_call(
        _matmul_ln_kernel,
        grid=(m // ROW_TILE,),
        in_specs=[pl.BlockSpec((ROW_TILE, k), row),
                  pl.BlockSpec((k, D_MODEL), const),
                  pl.BlockSpec((ROW_TILE, D_MODEL), row),
                  pl.BlockSpec((1, D_MODEL), const),
                  pl.BlockSpec((1, D_MODEL), const)],
        out_specs=pl.BlockSpec((ROW_TILE, D_MODEL), row),
        out_shape=jax.ShapeDtypeStruct((m, D_MODEL), F32),
        compiler_params=_params("parallel"),
        name="matmul_ln",
    )(a, w, res, g, b)


def _pool_core(zbuf, w_ref, scale_ref, g_ref, b_ref, o_ref, sums, *, tt, n_before):
    rows = POOL_HALO + tt
    avail = n_before + lax.broadcasted_iota(jnp.int32, (tt, 1), 0) + 1
    for grp, win in enumerate(POOL_WINDOWS):
        c0, c1 = grp * POOL_GROUP_DIM, (grp + 1) * POOL_GROUP_DIM
        xg = zbuf[POOL_HALO:POOL_HALO + tt, c0:c1]
        src, src_cols, span, level = zbuf, slice(c0, c1), 1, 0
        while span < win:
            dst = sums[level % 2]
            dst[0:span, :] = src[0:span, src_cols]
            dst[span:rows, :] = src[span:rows, src_cols] + src[0:rows - span, src_cols]
            src, src_cols, span, level = dst, slice(None), 2 * span, level + 1
        total = src[POOL_HALO:POOL_HALO + tt, src_cols]
        count = jnp.minimum(win, avail).astype(F32)
        diff = total / count - xg
        y = jnp.dot(diff.astype(BF16), w_ref[grp], preferred_element_type=F32) * scale_ref[:, c0:c1]
        o_ref[:, c0:c1] = DN_ALPHA * xg + y
    o_ref[...] = _layer_norm(o_ref[...], g_ref[...], b_ref[...])


def _pool_prompt_kernel(x_ref, halo_ref, w_ref, scale_ref, g_ref, b_ref, o_ref, zbuf, sum_a, sum_b, *, tt):
    i = pl.program_id(1)
    zbuf[0:POOL_HALO, :] = jnp.where(i == 0, 0.0, halo_ref[...])
    zbuf[POOL_HALO:POOL_HALO + tt, :] = x_ref[...]
    _pool_core(zbuf, w_ref, scale_ref, g_ref, b_ref, o_ref, (sum_a, sum_b), tt=tt, n_before=i * tt)


def _pool_sample_kernel(z_ref, w_ref, scale_ref, g_ref, b_ref, o_ref, sum_a, sum_b, *, tt):
    _pool_core(z_ref.at[0], w_ref, scale_ref, g_ref, b_ref, o_ref.at[0], (sum_a, sum_b), tt=tt, n_before=PAST_LEN)


def _pool_layer(xp, xs, state, w, scale, g, b):
    tt = POOL_TILE
    tiles = SEQ // tt
    halo_per_tile = tt // POOL_HALO
    const2 = lambda bb, i: (0, 0)
    w_bf = w.astype(BF16)
    out = pl.pallas_call(
        functools.partial(_pool_prompt_kernel, tt=tt),
        grid=(BATCH, tiles),
        in_specs=[pl.BlockSpec((tt, D_MODEL), lambda bb, i: (bb * tiles + i, 0)),
                  pl.BlockSpec((POOL_HALO, D_MODEL),
                               lambda bb, i: (jnp.maximum((bb * tiles + i) * halo_per_tile - 1, 0), 0)),
                  pl.BlockSpec(w_bf.shape, lambda bb, i: (0, 0, 0)),
                  pl.BlockSpec((1, D_MODEL), const2),
                  pl.BlockSpec((1, D_MODEL), const2),
                  pl.BlockSpec((1, D_MODEL), const2)],
        out_specs=pl.BlockSpec((tt, D_MODEL), lambda bb, i: (bb * tiles + i, 0)),
        out_shape=jax.ShapeDtypeStruct((N_TOK, D_MODEL), F32),
        scratch_shapes=[pltpu.VMEM((POOL_HALO + tt, D_MODEL), F32),
                        pltpu.VMEM((POOL_HALO + tt, POOL_GROUP_DIM), F32),
                        pltpu.VMEM((POOL_HALO + tt, POOL_GROUP_DIM), F32)],
        compiler_params=_params("parallel", "arbitrary"),
        name="pool_prompt",
    )(xp, xp, w_bf, scale, g, b)

    ts = 16
    zs = jnp.concatenate([jnp.zeros((DEC_BATCH, 1, D_MODEL), F32), state, xs,
                          jnp.zeros((DEC_BATCH, ts - DEC_SEQ, D_MODEL), F32)], axis=1)
    const1 = lambda bb: (0, 0)
    out_s = pl.pallas_call(
        functools.partial(_pool_sample_kernel, tt=ts),
        grid=(DEC_BATCH,),
        in_specs=[pl.BlockSpec((1, POOL_HALO + ts, D_MODEL), lambda bb: (bb, 0, 0)),
                  pl.BlockSpec(w_bf.shape, lambda bb: (0, 0, 0)),
                  pl.BlockSpec((1, D_MODEL), const1),
                  pl.BlockSpec((1, D_MODEL), const1),
                  pl.BlockSpec((1, D_MODEL), const1)],
        out_specs=pl.BlockSpec((1, ts, D_MODEL), lambda bb: (bb, 0, 0)),
        out_shape=jax.ShapeDtypeStruct((DEC_BATCH, ts, D_MODEL), F32),
        scratch_shapes=[pltpu.VMEM((POOL_HALO + ts, POOL_GROUP_DIM), F32),
                        pltpu.VMEM((POOL_HALO + ts, POOL_GROUP_DIM), F32)],
        compiler_params=_params("parallel"),
        name="pool_sample",
    )(zs, w_bf, scale, g, b)
    out = lax.dynamic_update_slice(out, out_s[:, :DEC_SEQ].reshape(N_SAMPLE_TOK, D_MODEL), (N_PROMPT_TOK, 0))

    new_p = jnp.stack([xp[(bb + 1) * SEQ - POOL_STATE:(bb + 1) * SEQ] for bb in range(BATCH)])
    new_s = jnp.concatenate([state, xs], axis=1)[:, DEC_SEQ:]
    return out, new_p, new_s


def _alibi_slope(h):
    return 2.0 ** (-8.0 * (h + 1.0) / SWA_HEADS)


def _attn_core(q_ref, k_all, v_all, sink_ref, o_ref, *, rows, first_block):
    w, dh = SWA_WINDOW, SWA_HEAD_DIM
    qi = lax.broadcasted_iota(jnp.int32, (rows, 2 * w), 0)
    sj = lax.broadcasted_iota(jnp.int32, (rows, 2 * w), 1)
    dist = (w + qi) - sj
    valid = (dist >= 0) & (dist <= w)
    if first_block is not None:
        valid = valid & ((sj >= w) | jnp.logical_not(first_block))
    masked_dist = jnp.where(valid, dist.astype(F32), jnp.inf)
    for kv in range(SWA_KV_HEADS):
        c0, c1 = kv * dh, (kv + 1) * dh
        heads = range(kv * SWA_GROUP, (kv + 1) * SWA_GROUP)
        q = jnp.concatenate([q_ref[:, h * dh:(h + 1) * dh] for h in heads], axis=0)
        q = (q * (dh ** -0.5)).astype(BF16)
        s_all = lax.dot_general(q, k_all[:, c0:c1], _NT, preferred_element_type=F32)
        probs, dens = [], []
        for j, h in enumerate(heads):
            sink = sink_ref[h]
            s = s_all[j * rows:(j + 1) * rows] - _alibi_slope(h) * masked_dist
            m = jnp.maximum(jnp.max(s, axis=1, keepdims=True), sink)
            e = jnp.exp(s - m)
            dens.append(jnp.sum(e, axis=1, keepdims=True) + jnp.exp(sink - m))
            probs.append(e.astype(BF16))
        o_all = jnp.dot(jnp.concatenate(probs, axis=0), v_all[:, c0:c1], preferred_element_type=F32)
        for j, h in enumerate(heads):
            o_ref[:, h * dh:(h + 1) * dh] = (o_all[j * rows:(j + 1) * rows] / dens[j]).astype(o_ref.dtype)


def _attn_prompt_kernel(sink_ref, q_ref, kp_ref, kc_ref, vp_ref, vc_ref, o_ref):
    n = pl.program_id(1)
    k_all = jnp.concatenate([kp_ref[...], kc_ref[...]], axis=0).astype(BF16)
    v_all = jnp.concatenate([vp_ref[...], vc_ref[...]], axis=0).astype(BF16)
    _attn_core(q_ref, k_all, v_all, sink_ref, o_ref, rows=SWA_WINDOW, first_block=(n == 0))


def _attn_sample_kernel(sink_ref, q_ref, k_ref, v_ref, o_ref, *, rows):
    _attn_core(q_ref.at[0], k_ref[0].astype(BF16), v_ref[0].astype(BF16), sink_ref, o_ref.at[0],
               rows=rows, first_block=None)


def _swa_layer(xt, cache_k, cache_v, w_qkv, w_o, sinks, g, b):
    w = SWA_WINDOW
    qkv = _matmul(xt, w_qkv.astype(BF16), tm=640, tn=2560)
    nb = SEQ // w
    k_col = D_MODEL // SWA_KV_DIM
    cur = lambda bb, n, *_: (bb * nb + n, k_col)
    prev = lambda bb, n, *_: (jnp.maximum(bb * nb + n - 1, 0), k_col)
    cur_v = lambda bb, n, *_: (bb * nb + n, k_col + 1)
    prev_v = lambda bb, n, *_: (jnp.maximum(bb * nb + n - 1, 0), k_col + 1)
    o = pl.pallas_call(
        _attn_prompt_kernel,
        grid_spec=pltpu.PrefetchScalarGridSpec(
            num_scalar_prefetch=1,
            grid=(BATCH, nb),
            in_specs=[pl.BlockSpec((w, D_MODEL), lambda bb, n, *_: (bb * nb + n, 0)),
                      pl.BlockSpec((w, SWA_KV_DIM), prev),
                      pl.BlockSpec((w, SWA_KV_DIM), cur),
                      pl.BlockSpec((w, SWA_KV_DIM), prev_v),
                      pl.BlockSpec((w, SWA_KV_DIM), cur_v)],
            out_specs=pl.BlockSpec((w, D_MODEL), lambda bb, n, *_: (bb * nb + n, 0))),
        out_shape=jax.ShapeDtypeStruct((N_TOK, D_MODEL), BF16),
        compiler_params=_params("parallel", "arbitrary"),
        name="attn_prompt",
    )(sinks, qkv, qkv, qkv, qkv, qkv)

    rows = 16
    qkv_s = qkv[N_PROMPT_TOK:].reshape(DEC_BATCH, DEC_SEQ, -1)
    q_s = jnp.pad(qkv_s[..., :D_MODEL], ((0, 0), (0, rows - DEC_SEQ), (0, 0)))
    k_new = qkv_s[..., D_MODEL:D_MODEL + SWA_KV_DIM]
    v_new = qkv_s[..., D_MODEL + SWA_KV_DIM:]
    kz = jnp.concatenate([cache_k.reshape(DEC_BATCH, w, SWA_KV_DIM), k_new], axis=1)
    vz = jnp.concatenate([cache_v.reshape(DEC_BATCH, w, SWA_KV_DIM), v_new], axis=1)
    pad_keys = ((0, 0), (0, w - DEC_SEQ), (0, 0))
    o_s = pl.pallas_call(
        functools.partial(_attn_sample_kernel, rows=rows),
        grid_spec=pltpu.PrefetchScalarGridSpec(
            num_scalar_prefetch=1,
            grid=(DEC_BATCH,),
            in_specs=[pl.BlockSpec((1, rows, D_MODEL), lambda bb, *_: (bb, 0, 0)),
                      pl.BlockSpec((1, 2 * w, SWA_KV_DIM), lambda bb, *_: (bb, 0, 0)),
                      pl.BlockSpec((1, 2 * w, SWA_KV_DIM), lambda bb, *_: (bb, 0, 0))],
            out_specs=pl.BlockSpec((1, rows, D_MODEL), lambda bb, *_: (bb, 0, 0))),
        out_shape=jax.ShapeDtypeStruct((DEC_BATCH, rows, D_MODEL), F32),
        compiler_params=_params("parallel"),
        name="attn_sample",
    )(sinks, q_s, jnp.pad(kz, pad_keys), jnp.pad(vz, pad_keys))
    o = lax.dynamic_update_slice(o, o_s[:, :DEC_SEQ].reshape(N_SAMPLE_TOK, D_MODEL).astype(BF16),
                                 (N_PROMPT_TOK, 0))

    out = _matmul_ln(o, w_o.astype(BF16), xt, g, b)

    kv_shape = (SWA_WINDOW, SWA_KV_HEADS, SWA_HEAD_DIM)
    kv_p = jnp.stack([qkv[(bb + 1) * SEQ - w:(bb + 1) * SEQ, D_MODEL:] for bb in range(BATCH)])
    new_k_p = kv_p[..., :SWA_KV_DIM].reshape((BATCH,) + kv_shape)
    new_v_p = kv_p[..., SWA_KV_DIM:].reshape((BATCH,) + kv_shape)
    new_k_s = kz[:, DEC_SEQ:].reshape((DEC_BATCH,) + kv_shape)
    new_v_s = vz[:, DEC_SEQ:].reshape((DEC_BATCH,) + kv_shape)
    return out, new_k_p, new_v_p, new_k_s, new_v_s


def _log_sigmoid(x):
    return jnp.minimum(x, 0.0) - jnp.log(1.0 + jnp.exp(-jnp.abs(x)))


def _mlstm_kernel(bias_ref, q_ref, k_ref, v_ref, og_ref, gates_ref, c0_ref, n0_ref, m0_ref, ng_ref,
                  h_ref, c_ref, n_ref, m_ref, c_scr, n_scr, m_scr, *, chunk, n_valid, group):
    step = pl.program_id(1)

    @pl.when(step == 0)
    def _():
        c_scr[...] = c0_ref[0]
        n_scr[...] = n0_ref[0]
        m_scr[...] = m0_ref[0]

    heads = [_mlstm_head(head, step, bias_ref, q_ref, k_ref, v_ref, og_ref, gates_ref, ng_ref, h_ref, c_scr, n_scr,
                         m_scr, chunk=chunk, n_valid=n_valid) for head in range(MLSTM_HEADS)]
    for h0 in range(0, MLSTM_HEADS, group):
        for _ in range(MLSTM_PHASES):
            for head_phases in heads[h0:h0 + group]:
                next(head_phases)

    @pl.when(step == pl.num_programs(1) - 1)
    def _():
        c_ref[0] = c_scr[...]
        n_ref[0] = n_scr[...]
        m_ref[0] = m_scr[...]


def _mlstm_head(head, step, bias_ref, q_ref, k_ref, v_ref, og_ref, gates_ref, ng_ref, h_ref, c_scr, n_scr, m_scr,
                *, chunk, n_valid):
    ln = chunk
    cols = slice(head * MLSTM_HEAD_DIM, (head + 1) * MLSTM_HEAD_DIM)
    ig = gates_ref[0, head, pl.ds(step, 1), :] + bias_ref[head]
    lf = _log_sigmoid(gates_ref[0, MLSTM_HEADS + head, pl.ds(step, 1), :] + bias_ref[MLSTM_HEADS + head])
    if n_valid < ln:
        col = lax.broadcasted_iota(jnp.int32, (1, ln), 1)
        ig = jnp.where(col < n_valid, ig, -1e30)
        lf = jnp.where(col < n_valid, lf, 0.0)

    ri = lax.broadcasted_iota(jnp.int32, (ln, ln), 0)
    ci = lax.broadcasted_iota(jnp.int32, (ln, ln), 1)
    eye = ri == ci
    causal = ci <= ri

    def to_col(row):
        return jnp.sum(jnp.where(eye, row, 0.0), axis=1, keepdims=True)

    f_col = to_col(lf)
    b_col = jnp.sum(jnp.where(causal, lf, 0.0), axis=1, keepdims=True)
    b_row = jnp.sum(jnp.where(ri <= ci, f_col, 0.0), axis=0, keepdims=True)
    b_last = jnp.sum(lf, axis=1, keepdims=True)
    m_prev = m_scr[head, :, 0:1]
    a_col = b_col + m_prev
    dmat = jnp.where(causal, b_col - b_row + ig, -jnp.inf)
    mt = jnp.maximum(a_col, jnp.max(dmat, axis=1, keepdims=True))
    w_inter = jnp.exp(a_col - mt)
    yield

    q = q_ref[:, cols]
    k = k_ref[:, cols] * (MLSTM_HEAD_DIM ** -0.5)
    v = v_ref[:, cols]
    qb = q.astype(BF16)
    kb = k.astype(BF16)
    c_prev = c_scr[head]
    n_prev = n_scr[head]
    scores = lax.dot_general(qb, kb, _NT, preferred_element_type=F32)
    yield
    w_intra = jnp.exp(dmat - mt) * scores
    num = (w_inter * lax.dot_general(qb, c_prev.astype(BF16), _NT, preferred_element_type=F32)
           + jnp.dot(w_intra.astype(BF16), v.astype(BF16), preferred_element_type=F32))
    den = (w_inter * jnp.sum(q * n_prev, axis=1, keepdims=True)
           + jnp.sum(w_intra, axis=1, keepdims=True))
    h = num / jnp.maximum(jnp.abs(den), jnp.exp(-mt))
    yield

    g_row = b_last - b_row + ig
    m_new = jnp.maximum(b_last + m_prev, jnp.max(g_row, axis=1, keepdims=True))
    decay = jnp.exp(b_last + m_prev - m_new)
    wg_col = to_col(jnp.exp(g_row - m_new))
    c_new = decay * c_prev + lax.dot_general((v * wg_col).astype(BF16), kb, _TN, preferred_element_type=F32)
    n_new = decay * n_prev + jnp.sum(wg_col * k, axis=0, keepdims=True)
    c_scr[head] = c_new
    n_scr[head] = n_new
    m_scr[head] = jnp.broadcast_to(m_new, (1, LANES))
    yield

    mu = jnp.mean(h, axis=1, keepdims=True)
    hc = h - mu
    var = jnp.mean(hc * hc, axis=1, keepdims=True)
    hn = hc * lax.rsqrt(var + LN_EPS) * ng_ref[:, cols] * jax.nn.sigmoid(og_ref[:, cols])
    h_ref[:, cols] = hn.astype(h_ref.dtype)
    yield


def _mlstm_scan(proj, gates, b_gates, norm_g, c0, n0, m0, *, batch, n_chunks, chunk, n_valid, out_rows, out_dtype,
                group):
    d, dh, nh = D_MODEL, MLSTM_HEAD_DIM, MLSTM_HEADS
    rows = lambda off: (lambda bb, cc, *_: (bb * n_chunks + cc, off))
    per_seq = lambda bb, cc, *_: (bb, 0, 0, 0)
    return pl.pallas_call(
        functools.partial(_mlstm_kernel, chunk=chunk, n_valid=n_valid, group=group),
        grid_spec=pltpu.PrefetchScalarGridSpec(
            num_scalar_prefetch=1,
            grid=(batch, n_chunks),
            in_specs=[pl.BlockSpec((chunk, d), rows(0)),
                      pl.BlockSpec((chunk, d), rows(1)),
                      pl.BlockSpec((chunk, d), rows(2)),
                      pl.BlockSpec((chunk, d), rows(3)),
                      pl.BlockSpec((1, 2 * nh, n_chunks, chunk), per_seq),
                      pl.BlockSpec((1, nh, dh, dh), per_seq),
                      pl.BlockSpec((1, nh, 1, dh), per_seq),
                      pl.BlockSpec((1, nh, 1, LANES), per_seq),
                      pl.BlockSpec((1, d), lambda bb, cc, *_: (0, 0))],
            out_specs=[pl.BlockSpec((chunk, d), lambda bb, cc, *_: (bb * n_chunks + cc, 0)),
                       pl.BlockSpec((1, nh, dh, dh), per_seq),
                       pl.BlockSpec((1, nh, 1, dh), per_seq),
                       pl.BlockSpec((1, nh, 1, LANES), per_seq)],
            scratch_shapes=[pltpu.VMEM((nh, dh, dh), F32), pltpu.VMEM((nh, 1, dh), F32),
                            pltpu.VMEM((nh, 1, LANES), F32)]),
        out_shape=[jax.ShapeDtypeStruct((out_rows, d), out_dtype),
                   jax.ShapeDtypeStruct((batch, nh, dh, dh), F32),
                   jax.ShapeDtypeStruct((batch, nh, 1, dh), F32),
                   jax.ShapeDtypeStruct((batch, nh, 1, LANES), F32)],
        compiler_params=_params("parallel", "arbitrary"),
        name="mlstm_scan",
    )(b_gates, proj, proj, proj, proj, gates, c0, n0[:, :, None, :],
      jnp.broadcast_to(m0[:, :, None, None], (batch, nh, 1, LANES)), norm_g)


def _mlstm_layer(xt, c0_s, n0_s, m0_s, w_in, b_gates, norm_g, w_out, g, b):
    d, nh, dh = D_MODEL, MLSTM_HEADS, MLSTM_HEAD_DIM
    proj = _matmul(xt, w_in[:, :4 * d].astype(BF16), tm=640, tn=2048)
    w_gates = jnp.pad(w_in[:, 4 * d:], ((0, 0), (0, 128 - 2 * nh)))
    gate_pre = _matmul(xt, w_gates.astype(BF16), tm=640, tn=128)[:, :2 * nh]

    nc = SEQ // MLSTM_CHUNK
    gates_p = gate_pre[:N_PROMPT_TOK].reshape(BATCH, nc, MLSTM_CHUNK, 2 * nh).transpose(0, 3, 1, 2)
    zeros = lambda *s: jnp.zeros(s, F32)
    hn, c_p, n_p, m_p = _mlstm_scan(
        proj, gates_p, b_gates, norm_g, zeros(BATCH, nh, dh, dh), zeros(BATCH, nh, dh), zeros(BATCH, nh),
        batch=BATCH, n_chunks=nc, chunk=MLSTM_CHUNK, n_valid=MLSTM_CHUNK, out_rows=N_TOK, out_dtype=BF16, group=2)

    ls = MLSTM_SAMPLE_CHUNK
    pad_t = ((0, 0), (0, ls - DEC_SEQ), (0, 0))
    proj_s = jnp.pad(proj[N_PROMPT_TOK:].reshape(DEC_BATCH, DEC_SEQ, 4 * d), pad_t).reshape(DEC_BATCH * ls, 4 * d)
    gates_s = jnp.pad(gate_pre[N_PROMPT_TOK:].reshape(DEC_BATCH, DEC_SEQ, 2 * nh), pad_t)
    gates_s = gates_s.transpose(0, 2, 1)[:, :, None, :]
    hn_s, c_s, n_s, m_s = _mlstm_scan(
        proj_s, gates_s, b_gates, norm_g, c0_s, n0_s, m0_s,
        batch=DEC_BATCH, n_chunks=1, chunk=ls, n_valid=DEC_SEQ, out_rows=DEC_BATCH * ls, out_dtype=F32,
        group=MLSTM_HEADS)
    hn_s = hn_s.reshape(DEC_BATCH, ls, d)[:, :DEC_SEQ].reshape(N_SAMPLE_TOK, d).astype(BF16)
    hn = lax.dynamic_update_slice(hn, hn_s, (N_PROMPT_TOK, 0))

    out = _matmul_ln(hn, w_out.astype(BF16), xt, g, b)
    return (out, c_p, n_p[:, :, 0], m_p[:, :, 0, 0], c_s, n_s[:, :, 0], m_s[:, :, 0, 0])


def _router_kernel(x_ref, wt_ref, bias_ref, tri_ref, eidx_ref, gate_ref, rank_ref, count_ref, xq_ref, count_scr,
                   *, tt):
    neg = -jnp.inf
    x = x_ref[...]
    logits = lax.dot_general(wt_ref[...].astype(BF16), x.astype(BF16), _NT, preferred_element_type=F32)
    s = jax.nn.sigmoid(logits)
    sb = s + bias_ref[...]
    per_group = N_EXPERTS // N_EXPERT_GROUPS

    sb3 = sb.reshape(N_EXPERT_GROUPS, per_group, tt)
    i3 = lax.broadcasted_iota(jnp.int32, sb3.shape, 1)
    m1 = jnp.max(sb3, axis=1, keepdims=True)
    first = jnp.min(jnp.where(sb3 == m1, i3, per_group), axis=1, keepdims=True)
    m2 = jnp.max(jnp.where(i3 == first, neg, sb3), axis=1, keepdims=True)
    gscore = (m1 + m2).reshape(N_EXPERT_GROUPS, tt)

    gi = lax.broadcasted_iota(jnp.int32, gscore.shape, 0)
    gsel = jnp.zeros(gscore.shape, jnp.bool_)
    cur = gscore
    for _ in range(TOPK_GROUPS):
        mx = jnp.max(cur, axis=0, keepdims=True)
        pick = gi == jnp.min(jnp.where(cur == mx, gi, N_EXPERT_GROUPS), axis=0, keepdims=True)
        gsel = gsel | pick
        cur = jnp.where(pick, neg, cur)
    emask = jnp.broadcast_to(gsel.reshape(N_EXPERT_GROUPS, 1, tt), sb3.shape).reshape(N_EXPERTS, tt)

    ei = lax.broadcasted_iota(jnp.int32, sb.shape, 0)
    sel = jnp.zeros(sb.shape, jnp.bool_)
    cur = jnp.where(emask, sb, neg)
    picks = []
    for _ in range(TOP_K):
        mx = jnp.max(cur, axis=0, keepdims=True)
        idx = jnp.min(jnp.where(cur == mx, ei, N_EXPERTS), axis=0, keepdims=True)
        pick = ei == idx
        picks.append((idx, pick))
        sel = sel | pick
        cur = jnp.where(pick, neg, cur)
    s_sel = jnp.where(sel, s, 0.0)
    gate = s_sel / jnp.sum(s_sel, axis=0, keepdims=True) * ROUTED_SCALE

    @pl.when(pl.program_id(0) == 0)
    def _():
        count_scr[...] = jnp.zeros_like(count_scr)

    sel_f = sel.astype(F32)
    incl = jnp.dot(sel_f.astype(BF16), tri_ref[...], preferred_element_type=F32)
    rank = count_scr[:, 0:1] + incl - sel_f
    count_scr[...] = count_scr[...] + jnp.sum(sel_f, axis=1, keepdims=True)
    count_ref[...] = count_scr[...].astype(jnp.int32)

    for kk, (idx, pick) in enumerate(picks):
        eidx_ref[kk:kk + 1, :] = idx
        gate_ref[kk:kk + 1, :] = jnp.sum(jnp.where(pick, gate, 0.0), axis=0, keepdims=True)
        rank_ref[kk:kk + 1, :] = jnp.sum(jnp.where(pick, rank, 0.0), axis=0, keepdims=True).astype(jnp.int32)

    for sl in range(PACK_ROWS):
        c0 = 2 * LANES * sl
        xq_ref[pl.ds(sl, tt, stride=PACK_ROWS), :] = _pack_bf16_pair(x[:, c0:c0 + LANES],
                                                                     x[:, c0 + LANES:c0 + 2 * LANES])


def _pack_bf16_pair(hi, lo):
    hb = lax.bitcast_convert_type(hi.astype(BF16).astype(F32), jnp.uint32)
    lb = lax.bitcast_convert_type(lo.astype(BF16).astype(F32), jnp.uint32)
    return hb | (lb >> 16)


def _unpack_bf16_pair(word):
    hi = lax.bitcast_convert_type(word & jnp.uint32(0xFFFF0000), F32)
    lo = lax.bitcast_convert_type(word << 16, F32)
    return hi, lo


def _moe_route(xt, w_router, bias):
    tt = ROUTER_TILE
    col = lambda i: (0, i)
    const = lambda i: (0, 0)
    tri = (jnp.arange(tt)[:, None] <= jnp.arange(tt)[None, :]).astype(BF16)
    return pl.pallas_call(
        functools.partial(_router_kernel, tt=tt),
        grid=(N_TOK // tt,),
        in_specs=[pl.BlockSpec((tt, D_MODEL), lambda i: (i, 0)),
                  pl.BlockSpec((N_EXPERTS, D_MODEL), const),
                  pl.BlockSpec((N_EXPERTS, 1), const),
                  pl.BlockSpec((tt, tt), const)],
        out_specs=[pl.BlockSpec((TOP_K, tt), col), pl.BlockSpec((TOP_K, tt), col), pl.BlockSpec((TOP_K, tt), col),
                   pl.BlockSpec((N_EXPERTS, LANES), const),
                   pl.BlockSpec((tt * PACK_ROWS, LANES), lambda i: (i, 0))],
        out_shape=[jax.ShapeDtypeStruct((TOP_K, N_TOK), jnp.int32),
                   jax.ShapeDtypeStruct((TOP_K, N_TOK), F32),
                   jax.ShapeDtypeStruct((TOP_K, N_TOK), jnp.int32),
                   jax.ShapeDtypeStruct((N_EXPERTS, LANES), jnp.int32),
                   jax.ShapeDtypeStruct((N_TOK * PACK_ROWS, LANES), jnp.uint32)],
        scratch_shapes=[pltpu.VMEM((N_EXPERTS, LANES), F32)],
        compiler_params=_params("arbitrary"),
        name="moe_router",
    )(xt, w_router.T, bias[:, None], tri)


def _sorted_row(pstart_ref, eidx_ref, rank_ref, j):
    return pstart_ref[eidx_ref[0, 0, j]] + rank_ref[0, 0, j]


def _dispatch_kernel(pstart_ref, eidx_ref, rank_ref, xq_ref, xs_hbm, row_ref, sem, *, tt):
    def issue(t, carry):
        src = xq_ref.at[pl.ds(pl.multiple_of(t * PACK_ROWS, PACK_ROWS), PACK_ROWS)]
        for kk in range(TOP_K):
            row = _sorted_row(pstart_ref, eidx_ref, rank_ref, t * TOP_K + kk)
            row_ref[0, 0, t * TOP_K + kk] = row
            row = pl.multiple_of(row * PACK_ROWS, PACK_ROWS)
            pltpu.make_async_copy(src, xs_hbm.at[pl.ds(row, PACK_ROWS)], sem).start()
        return carry
    lax.fori_loop(0, tt, issue, 0)
    for _ in range(TOP_K):
        pltpu.make_async_copy(xq_ref, xs_hbm.at[pl.ds(0, tt * PACK_ROWS)], sem).wait()


def _moe_dispatch(xq, pstarts, eidx_tok, rank_tok):
    tt = DISPATCH_TILE
    n_tiles = N_TOK // tt
    slots = pl.BlockSpec((1, 1, tt * TOP_K), lambda i, *_: (i, 0, 0), memory_space=pltpu.SMEM)
    return pl.pallas_call(
        functools.partial(_dispatch_kernel, tt=tt),
        grid_spec=pltpu.PrefetchScalarGridSpec(
            num_scalar_prefetch=1,
            grid=(n_tiles,),
            in_specs=[slots, slots, pl.BlockSpec((tt * PACK_ROWS, LANES), lambda i, *_: (i, 0))],
            out_specs=[pl.BlockSpec(memory_space=pl.ANY), slots],
            scratch_shapes=[pltpu.SemaphoreType.DMA(())]),
        out_shape=[jax.ShapeDtypeStruct((MOE_ROWS * PACK_ROWS, LANES), jnp.uint32),
                   jax.ShapeDtypeStruct((n_tiles, 1, tt * TOP_K), jnp.int32)],
        compiler_params=_params("arbitrary"),
        name="moe_dispatch",
    )(pstarts, eidx_tok.reshape(n_tiles, 1, tt * TOP_K), rank_tok.reshape(n_tiles, 1, tt * TOP_K), xq)


def _experts_kernel(be_ref, next_ref, nvalid_ref, nused_ref, xs_ref, wg_hbm, wu_hbm, wd_hbm, y_ref,
                    wg_f, wu_f, wd_f, wgu_s, wd_s, sem, *, layer):
    i = pl.program_id(0)
    tm = MOE_BLOCK

    def weight_copies(e):
        return (pltpu.make_async_copy(wg_hbm.at[layer, e], wg_f, sem.at[0]),
                pltpu.make_async_copy(wu_hbm.at[layer, e], wu_f, sem.at[1]),
                pltpu.make_async_copy(wd_hbm.at[layer, e], wd_f, sem.at[2]))

    def swiglu_rows(n_rows):
        parts = []
        for sl in range(PACK_ROWS):
            parts.extend(_unpack_bf16_pair(xs_ref[pl.ds(sl, n_rows, stride=PACK_ROWS), :]))
        x = jnp.concatenate(parts, axis=1).astype(BF16)
        gate_up = jnp.dot(x, wgu_s[...], preferred_element_type=F32)
        hidden = _silu(gate_up[:, :EXPERT_FF]) * gate_up[:, EXPERT_FF:]
        y = jnp.dot(hidden.astype(BF16), wd_s[...], preferred_element_type=F32)
        for sl in range(PACK_ROWS):
            c0 = 2 * LANES * sl
            y_ref[pl.ds(sl, n_rows, stride=PACK_ROWS), :] = _pack_bf16_pair(y[:, c0:c0 + LANES],
                                                                            y[:, c0 + LANES:c0 + 2 * LANES])

    @pl.when(i == 0)
    def _():
        for cp in weight_copies(be_ref[0]):
            cp.start()

    @pl.when(i < nused_ref[0])
    def _():
        first_of_expert = (i == 0) | (be_ref[i] != be_ref[jnp.maximum(i - 1, 0)])

        @pl.when(first_of_expert)
        def _():
            for cp in weight_copies(be_ref[i]):
                cp.wait()
            wgu_s[:, :EXPERT_FF] = wg_f[...].astype(BF16)
            wgu_s[:, EXPERT_FF:] = wu_f[...].astype(BF16)
            wd_s[...] = wd_f[...].astype(BF16)

            @pl.when(next_ref[i] >= 0)
            def _():
                for cp in weight_copies(next_ref[i]):
                    cp.start()

        @pl.when(nvalid_ref[i] > tm // 2)
        def _():
            swiglu_rows(tm)

        @pl.when(nvalid_ref[i] <= tm // 2)
        def _():
            swiglu_rows(tm // 2)


def _moe_experts(xs, block_e, next_e, n_valid, n_used, layer, w_gate, w_up, w_down):
    last = lambda i, be, nx, nv, nu: (jnp.minimum(i, nu[0] - 1), 0)
    return pl.pallas_call(
        functools.partial(_experts_kernel, layer=layer),
        grid_spec=pltpu.PrefetchScalarGridSpec(
            num_scalar_prefetch=4,
            grid=(MOE_N_BLOCKS,),
            in_specs=[pl.BlockSpec((MOE_BLOCK * PACK_ROWS, LANES), last),
                      pl.BlockSpec(memory_space=pl.ANY),
                      pl.BlockSpec(memory_space=pl.ANY),
                      pl.BlockSpec(memory_space=pl.ANY)],
            out_specs=pl.BlockSpec((MOE_BLOCK * PACK_ROWS, LANES), last),
            scratch_shapes=[pltpu.VMEM((D_MODEL, EXPERT_FF), F32), pltpu.VMEM((D_MODEL, EXPERT_FF), F32),
                            pltpu.VMEM((EXPERT_FF, D_MODEL), F32),
                            pltpu.VMEM((D_MODEL, 2 * EXPERT_FF), BF16),
                            pltpu.VMEM((EXPERT_FF, D_MODEL), BF16),
                            pltpu.SemaphoreType.DMA((3,))]),
        out_shape=jax.ShapeDtypeStruct((MOE_ROWS * PACK_ROWS, LANES), jnp.uint32),
        compiler_params=_params("arbitrary"),
        name="moe_experts",
    )(block_e, next_e, n_valid, n_used, xs, w_gate, w_up, w_down)


def _combine_kernel(row_ref, row_next_ref, x_ref, gate_ref, swg_ref, swu_ref, swd_ref, g_ref, b_ref, y_hbm, *rest,
                    tt, n_first):
    out_refs, (buf_a, buf_b, acc, sem) = rest[:-4], rest[-4:]
    i = pl.program_id(0)
    last = pl.num_programs(0) - 1

    def start_fetch(rows, into, into_sem, t):
        dst = pl.ds(t * PACK_ROWS, PACK_ROWS)
        for kk in range(TOP_K):
            src_row = pl.multiple_of(rows[0, 0, t * TOP_K + kk] * PACK_ROWS, PACK_ROWS)
            pltpu.make_async_copy(y_hbm.at[pl.ds(src_row, PACK_ROWS)], into.at[kk, dst], into_sem).start()

    def wait_fetch(into, into_sem):
        for kk in range(TOP_K):
            pltpu.make_async_copy(y_hbm.at[pl.ds(0, tt * PACK_ROWS)], into.at[kk], into_sem).wait()

    @pl.when(i == 0)
    def _():
        def body(t, carry):
            dst = pl.ds(pl.multiple_of(t * PACK_ROWS, PACK_ROWS), PACK_ROWS)
            for kk in range(TOP_K):
                src_row = pl.multiple_of(row_ref[0, 0, t * TOP_K + kk] * PACK_ROWS, PACK_ROWS)
                pltpu.make_async_copy(y_hbm.at[pl.ds(src_row, PACK_ROWS)], buf_a.at[kk, dst], sem.at[0]).start()
            return carry
        lax.fori_loop(0, tt, body, 0)

    def tile(cur, cur_sem, nxt, nxt_sem):
        per_phase = tt // (PACK_ROWS + 2)
        tokens = iter(range(tt))

        def issue(n):
            for _ in range(n):
                t = next(tokens, None)
                if t is not None:
                    start_fetch(row_next_ref, nxt, nxt_sem, t)

        x = x_ref[...]
        xb = x.astype(BF16)
        issue(per_phase)
        hidden = (_silu(jnp.dot(xb, swg_ref[...], preferred_element_type=F32))
                  * jnp.dot(xb, swu_ref[...], preferred_element_type=F32))
        issue(per_phase)
        shared = jnp.dot(hidden.astype(BF16), swd_ref[...], preferred_element_type=F32)
        wait_fetch(cur, cur_sem)
        gate = gate_ref[...]
        for sl in range(PACK_ROWS):
            issue(per_phase if sl < PACK_ROWS - 1 else tt)
            routed_hi = routed_lo = None
            for kk in range(TOP_K):
                hi, lo = _unpack_bf16_pair(cur[kk, pl.ds(sl, tt, stride=PACK_ROWS), :])
                g_k = gate[:, kk:kk + 1]
                routed_hi = g_k * hi if kk == 0 else routed_hi + g_k * hi
                routed_lo = g_k * lo if kk == 0 else routed_lo + g_k * lo
            for half, routed in enumerate((routed_hi, routed_lo)):
                cols = slice((2 * sl + half) * LANES, (2 * sl + half + 1) * LANES)
                acc[:, cols] = DN_ALPHA * x[:, cols] + (routed + shared[:, cols])
        res = _layer_norm(acc[...], g_ref[...], b_ref[...])
        if n_first is None:
            out_refs[0][...] = res
        else:
            @pl.when(i < n_first)
            def _():
                out_refs[0][...] = res

            @pl.when(i >= n_first)
            def _():
                out_refs[1][...] = res

        @pl.when(i == last)
        def _():
            wait_fetch(nxt, nxt_sem)

    @pl.when(i % 2 == 0)
    def _():
        tile(buf_a, sem.at[0], buf_b, sem.at[1])

    @pl.when(i % 2 == 1)
    def _():
        tile(buf_b, sem.at[1], buf_a, sem.at[0])


def _moe_combine(xt, ys, row_tiles, gate_tok, sw_gate, sw_up, sw_down, g, b, *, split):
    tt = COMBINE_TILE
    n_tiles = N_TOK // tt
    row = lambda i: (i, 0)
    const = lambda i: (0, 0)
    slots = pl.BlockSpec((1, 1, tt * TOP_K), lambda i: (i, 0, 0), memory_space=pltpu.SMEM)
    slots_next = pl.BlockSpec((1, 1, tt * TOP_K), lambda i: (jnp.minimum(i + 1, n_tiles - 1), 0, 0),
                              memory_space=pltpu.SMEM)
    row_tiles = row_tiles.reshape(n_tiles, 1, tt * TOP_K)
    if split:
        n_first = N_PROMPT_TOK // tt
        assert N_SAMPLE_TOK == tt
        out_specs = [pl.BlockSpec((tt, D_MODEL), lambda i: (jnp.minimum(i, n_first - 1), 0)),
                     pl.BlockSpec((tt, D_MODEL), const)]
        out_shape = [jax.ShapeDtypeStruct((N_PROMPT_TOK, D_MODEL), F32),
                     jax.ShapeDtypeStruct((N_SAMPLE_TOK, D_MODEL), F32)]
    else:
        n_first = None
        out_specs = [pl.BlockSpec((tt, D_MODEL), row)]
        out_shape = [jax.ShapeDtypeStruct((N_TOK, D_MODEL), F32)]
    return pl.pallas_call(
        functools.partial(_combine_kernel, tt=tt, n_first=n_first),
        grid=(n_tiles,),
        in_specs=[slots, slots_next,
                  pl.BlockSpec((tt, D_MODEL), row),
                  pl.BlockSpec((tt, TOP_K), row),
                  pl.BlockSpec((D_MODEL, EXPERT_FF), const),
                  pl.BlockSpec((D_MODEL, EXPERT_FF), const),
                  pl.BlockSpec((EXPERT_FF, D_MODEL), const),
                  pl.BlockSpec((1, D_MODEL), const),
                  pl.BlockSpec((1, D_MODEL), const),
                  pl.BlockSpec(memory_space=pl.ANY)],
        out_specs=out_specs,
        scratch_shapes=[pltpu.VMEM((TOP_K, tt * PACK_ROWS, LANES), jnp.uint32),
                        pltpu.VMEM((TOP_K, tt * PACK_ROWS, LANES), jnp.uint32),
                        pltpu.VMEM((tt, D_MODEL), F32), pltpu.SemaphoreType.DMA((2,))],
        out_shape=out_shape,
        compiler_params=_params("arbitrary"),
        name="moe_combine",
    )(row_tiles, row_tiles, xt, gate_tok, sw_gate.astype(BF16), sw_up.astype(BF16), sw_down.astype(BF16), g, b, ys)


def _moe_layer(xt, layer, w_router, bias, w_gate, w_up, w_down, sw_gate, sw_up, sw_down, g, b, *, split):
    eidx, gate, rank, counts, xq = _moe_route(xt, w_router, bias)

    counts = counts[:, 0]
    pcounts = (counts + MOE_BLOCK - 1) // MOE_BLOCK * MOE_BLOCK
    pends = jnp.cumsum(pcounts)
    pstarts = (pends - pcounts).astype(jnp.int32)
    n_used = (pends[-1] // MOE_BLOCK).astype(jnp.int32)
    blk = jnp.minimum(jnp.arange(MOE_N_BLOCKS, dtype=jnp.int32), n_used - 1) * MOE_BLOCK
    block_e = jnp.sum((pends[None, :] <= blk[:, None]).astype(jnp.int32), axis=1)
    block_e = jnp.minimum(block_e, N_EXPERTS - 1)
    after = pends[block_e] // MOE_BLOCK
    next_e = jnp.where(after < n_used, block_e[jnp.minimum(after, MOE_N_BLOCKS - 1)], -1).astype(jnp.int32)

    n_valid = jnp.clip(counts[block_e] - (blk - pstarts[block_e]), 0, MOE_BLOCK).astype(jnp.int32)

    eidx_tok, rank_tok = eidx.T, rank.T
    xs, rows = _moe_dispatch(xq, pstarts, eidx_tok, rank_tok)
    ys = _moe_experts(xs, block_e, next_e, n_valid, n_used[None], layer, w_gate, w_up, w_down)
    return _moe_combine(xt, ys, rows, gate.T, sw_gate, sw_up, sw_down, g, b, split=split)


def kernel(x_prompt, x_sample, state_pool, cache_swa_k, cache_swa_v, state_mlstm_c, state_mlstm_n, state_mlstm_m, pool_w, pool_scale, swa_w_qkv, swa_w_o, swa_sinks, mlstm_w_in, mlstm_b_gates, mlstm_norm_g, mlstm_w_out, ln_g, ln_b, moe_w_router, moe_router_bias, moe_w_gate, moe_w_up, moe_w_down, moe_shared_w_gate, moe_shared_w_up, moe_shared_w_down):
    d = D_MODEL
    xt = None
    pool_p, pool_s = [], []
    swk_p, swv_p, swk_s, swv_s = [], [], [], []
    mc_p, mn_p, mm_p, mc_s, mn_s, mm_s = [], [], [], [], [], []
    for i in range(DEPTH):
        kind, slot = i % N_MIXERS, i // N_MIXERS
        g0, b0 = ln_g[i, 0][None], ln_b[i, 0][None]
        if kind == 0:
            if i == 0:
                xp, xs = x_prompt.reshape(N_PROMPT_TOK, d), x_sample
            else:
                xp, xs = xt, xt[N_PROMPT_TOK:].reshape(DEC_BATCH, DEC_SEQ, d)
            xt, sp, ss = _pool_layer(xp, xs, state_pool[slot], pool_w[slot], pool_scale[slot][None], g0, b0)
            pool_p.append(sp)
            pool_s.append(ss)
        elif kind == 1:
            xt, kp, vp, ks, vs = _swa_layer(xt, cache_swa_k[slot], cache_swa_v[slot], swa_w_qkv[slot],
                                            swa_w_o[slot], swa_sinks[slot], g0, b0)
            swk_p.append(kp)
            swv_p.append(vp)
            swk_s.append(ks)
            swv_s.append(vs)
        else:
            xt, cp, np_, mp, cs, ns, ms = _mlstm_layer(
                xt, state_mlstm_c[slot], state_mlstm_n[slot], state_mlstm_m[slot], mlstm_w_in[slot],
                mlstm_b_gates[slot], mlstm_norm_g[slot][None], mlstm_w_out[slot], g0, b0)
            mc_p.append(cp)
            mn_p.append(np_)
            mm_p.append(mp)
            mc_s.append(cs)
            mn_s.append(ns)
            mm_s.append(ms)
        xt = _moe_layer(xt, i, moe_w_router[i], moe_router_bias[i], moe_w_gate, moe_w_up, moe_w_down,
                        moe_shared_w_gate[i], moe_shared_w_up[i], moe_shared_w_down[i],
                        ln_g[i, 1][None], ln_b[i, 1][None], split=(i == DEPTH - 1))
        if i < DEPTH - 1:
            xt = xt[0]
    y_p = xt[0].reshape(BATCH, SEQ, d)
    y_s = xt[1].reshape(DEC_BATCH, DEC_SEQ, d)
    return (y_p, y_s, jnp.stack(pool_p), jnp.stack(pool_s), jnp.stack(swk_p), jnp.stack(swv_p),
            jnp.stack(swk_s), jnp.stack(swv_s), jnp.stack(mc_p), jnp.stack(mn_p), jnp.stack(mm_p),
            jnp.stack(mc_s), jnp.stack(mn_s), jnp.stack(mm_s))
```

```python
import functools
import math

import jax
import jax.numpy as jnp
from jax import lax
from jax.experimental import pallas as pl
from jax.experimental.pallas import tpu as pltpu

F32 = jnp.float32
BF16 = jnp.bfloat16

D_MODEL = 2048
BATCH = 2
SEQ = 4096
DEPTH = 4
DEC_BATCH = 32
DEC_SEQ = 4
PAST_LEN = 16384
N_PROMPT_TOK = BATCH * SEQ
N_SAMPLE_TOK = DEC_BATCH * DEC_SEQ
N_TOK = N_PROMPT_TOK + N_SAMPLE_TOK

N_MIXERS = 3
DN_ALPHA = (2.0 * DEPTH) ** 0.25
LN_EPS = 1e-5

POOL_WINDOWS = (2, 4, 8, 16)
POOL_GROUP_DIM = D_MODEL // len(POOL_WINDOWS)
POOL_STATE = max(POOL_WINDOWS) - 1
POOL_HALO = POOL_STATE + 1

SWA_WINDOW = 128
SWA_HEAD_DIM = 64
SWA_HEADS = D_MODEL // SWA_HEAD_DIM
SWA_KV_HEADS = SWA_HEADS // 8
SWA_GROUP = SWA_HEADS // SWA_KV_HEADS
SWA_KV_DIM = SWA_KV_HEADS * SWA_HEAD_DIM

MLSTM_HEADS = 8
MLSTM_HEAD_DIM = D_MODEL // MLSTM_HEADS
MLSTM_CHUNK = 64
MLSTM_PHASES = 5
MLSTM_SAMPLE_CHUNK = 8

N_EXPERTS = 64
TOP_K = 8
N_EXPERT_GROUPS = 8
TOPK_GROUPS = 4
EXPERT_FF = D_MODEL // 4
ROUTED_SCALE = 2.5

VMEM_LIMIT_BYTES = 56 * 1024 * 1024

ROW_TILE = 832
ROUTER_TILE = 640
POOL_TILE = 512
MOE_BLOCK = 512
MOE_N_BLOCKS = N_TOK * TOP_K // MOE_BLOCK + N_EXPERTS
MOE_ROWS = MOE_N_BLOCKS * MOE_BLOCK
DISPATCH_TILE = 640
COMBINE_TILE = 128
LANES = 128
PACK_ROWS = D_MODEL // (2 * LANES)

_NT = (((1,), (1,)), ((), ()))
_TN = (((0,), (0,)), ((), ()))


def _params(*semantics):
    return pltpu.CompilerParams(dimension_semantics=semantics, vmem_limit_bytes=VMEM_LIMIT_BYTES)


def _layer_norm(z, g, b):
    mu = jnp.mean(z, axis=-1, keepdims=True)
    zc = z - mu
    var = jnp.mean(zc * zc, axis=-1, keepdims=True)
    return zc * lax.rsqrt(var + LN_EPS) * g + b


def _silu(x):
    return x * jax.nn.sigmoid(x)


def _matmul_kernel(x_ref, w_ref, o_ref):
    o_ref[...] = jnp.dot(x_ref[...].astype(BF16), w_ref[...], preferred_element_type=F32)


def _matmul(x, w, *, tm, tn):
    m, k = x.shape
    n = w.shape[1]
    return pl.pallas_call(
        _matmul_kernel,
        grid=(n // tn, m // tm),
        in_specs=[pl.BlockSpec((tm, k), lambda j, i: (i, 0)),
                  pl.BlockSpec((k, tn), lambda j, i: (0, j))],
        out_specs=pl.BlockSpec((tm, tn), lambda j, i: (i, j)),
        out_shape=jax.ShapeDtypeStruct((m, n), F32),
        compiler_params=_params("parallel", "parallel"),
        name="matmul",
    )(x, w)


def _matmul_ln_kernel(a_ref, w_ref, res_ref, g_ref, b_ref, o_ref):
    y = jnp.dot(a_ref[...], w_ref[...], preferred_element_type=F32)
    o_ref[...] = _layer_norm(DN_ALPHA * res_ref[...] + y, g_ref[...], b_ref[...])


def _matmul_ln(a, w, res, g, b):
    m, k = a.shape
    row = lambda i: (i, 0)
    const = lambda i: (0, 0)
    return pl.pallas_call(
        _matmul_ln_kernel,
        grid=(m // ROW_TILE,),
        in_specs=[pl.BlockSpec((ROW_TILE, k), row),
                  pl.BlockSpec((k, D_MODEL), const),
                  pl.BlockSpec((ROW_TILE, D_MODEL), row),
                  pl.BlockSpec((1, D_MODEL), const),
                  pl.BlockSpec((1, D_MODEL), const)],
        out_specs=pl.BlockSpec((ROW_TILE, D_MODEL), row),
        out_shape=jax.ShapeDtypeStruct((m, D_MODEL), F32),
        compiler_params=_params("parallel"),
        name="matmul_ln",
    )(a, w, res, g, b)


def _pool_core(zbuf, w_ref, scale_ref, g_ref, b_ref, o_ref, sums, *, tt, n_before):
    rows = POOL_HALO + tt
    avail = n_before + lax.broadcasted_iota(jnp.int32, (tt, 1), 0) + 1
    for grp, win in enumerate(POOL_WINDOWS):
        c0, c1 = grp * POOL_GROUP_DIM, (grp + 1) * POOL_GROUP_DIM
        xg = zbuf[POOL_HALO:POOL_HALO + tt, c0:c1]
        src, src_cols, span, level = zbuf, slice(c0, c1), 1, 0
        while span < win:
            dst = sums[level % 2]
            dst[0:span, :] = src[0:span, src_cols]
            dst[span:rows, :] = src[span:rows, src_cols] + src[0:rows - span, src_cols]
            src, src_cols, span, level = dst, slice(None), 2 * span, level + 1
        total = src[POOL_HALO:POOL_HALO + tt, src_cols]
        count = jnp.minimum(win, avail).astype(F32)
        diff = total / count - xg
        y = jnp.dot(diff.astype(BF16), w_ref[grp], preferred_element_type=F32) * scale_ref[:, c0:c1]
        o_ref[:, c0:c1] = DN_ALPHA * xg + y
    o_ref[...] = _layer_norm(o_ref[...], g_ref[...], b_ref[...])


def _pool_prompt_kernel(x_ref, halo_ref, w_ref, scale_ref, g_ref, b_ref, o_ref, zbuf, sum_a, sum_b, *, tt):
    i = pl.program_id(1)
    zbuf[0:POOL_HALO, :] = jnp.where(i == 0, 0.0, halo_ref[...])
    zbuf[POOL_HALO:POOL_HALO + tt, :] = x_ref[...]
    _pool_core(zbuf, w_ref, scale_ref, g_ref, b_ref, o_ref, (sum_a, sum_b), tt=tt, n_before=i * tt)


def _pool_sample_kernel(z_ref, w_ref, scale_ref, g_ref, b_ref, o_ref, sum_a, sum_b, *, tt):
    _pool_core(z_ref.at[0], w_ref, scale_ref, g_ref, b_ref, o_ref.at[0], (sum_a, sum_b), tt=tt, n_before=PAST_LEN)


def _pool_layer(xp, xs, state, w, scale, g, b):
    tt = POOL_TILE
    tiles = SEQ // tt
    halo_per_tile = tt // POOL_HALO
    const2 = lambda bb, i: (0, 0)
    w_bf = w.astype(BF16)
    out = pl.pallas_call(
        functools.partial(_pool_prompt_kernel, tt=tt),
        grid=(BATCH, tiles),
        in_specs=[pl.BlockSpec((tt, D_MODEL), lambda bb, i: (bb * tiles + i, 0)),
                  pl.BlockSpec((POOL_HALO, D_MODEL),
                               lambda bb, i: (jnp.maximum((bb * tiles + i) * halo_per_tile - 1, 0), 0)),
                  pl.BlockSpec(w_bf.shape, lambda bb, i: (0, 0, 0)),
                  pl.BlockSpec((1, D_MODEL), const2),
                  pl.BlockSpec((1, D_MODEL), const2),
                  pl.BlockSpec((1, D_MODEL), const2)],
        out_specs=pl.BlockSpec((tt, D_MODEL), lambda bb, i: (bb * tiles + i, 0)),
        out_shape=jax.ShapeDtypeStruct((N_TOK, D_MODEL), F32),
        scratch_shapes=[pltpu.VMEM((POOL_HALO + tt, D_MODEL), F32),
                        pltpu.VMEM((POOL_HALO + tt, POOL_GROUP_DIM), F32),
                        pltpu.VMEM((POOL_HALO + tt, POOL_GROUP_DIM), F32)],
        compiler_params=_params("parallel", "arbitrary"),
        name="pool_prompt",
    )(xp, xp, w_bf, scale, g, b)

    ts = 16
    zs = jnp.concatenate([jnp.zeros((DEC_BATCH, 1, D_MODEL), F32), state, xs,
                          jnp.zeros((DEC_BATCH, ts - DEC_SEQ, D_MODEL), F32)], axis=1)
    const1 = lambda bb: (0, 0)
    out_s = pl.pallas_call(
        functools.partial(_pool_sample_kernel, tt=ts),
        grid=(DEC_BATCH,),
        in_specs=[pl.BlockSpec((1, POOL_HALO + ts, D_MODEL), lambda bb: (bb, 0, 0)),
                  pl.BlockSpec(w_bf.shape, lambda bb: (0, 0, 0)),
                  pl.BlockSpec((1, D_MODEL), const1),
                  pl.BlockSpec((1, D_MODEL), const1),
                  pl.BlockSpec((1, D_MODEL), const1)],
        out_specs=pl.BlockSpec((1, ts, D_MODEL), lambda bb: (bb, 0, 0)),
        out_shape=jax.ShapeDtypeStruct((DEC_BATCH, ts, D_MODEL), F32),
        scratch_shapes=[pltpu.VMEM((POOL_HALO + ts, POOL_GROUP_DIM), F32),
                        pltpu.VMEM((POOL_HALO + ts, POOL_GROUP_DIM), F32)],
        compiler_params=_params("parallel"),
        name="pool_sample",
    )(zs, w_bf, scale, g, b)
    out = lax.dynamic_update_slice(out, out_s[:, :DEC_SEQ].reshape(N_SAMPLE_TOK, D_MODEL), (N_PROMPT_TOK, 0))

    new_p = jnp.stack([xp[(bb + 1) * SEQ - POOL_STATE:(bb + 1) * SEQ] for bb in range(BATCH)])
    new_s = jnp.concatenate([state, xs], axis=1)[:, DEC_SEQ:]
    return out, new_p, new_s


def _alibi_slope(h):
    return 2.0 ** (-8.0 * (h + 1.0) / SWA_HEADS)


def _attn_core(q_ref, k_all, v_all, sink_ref, o_ref, *, rows, first_block):
    w, dh = SWA_WINDOW, SWA_HEAD_DIM
    qi = lax.broadcasted_iota(jnp.int32, (rows, 2 * w), 0)
    sj = lax.broadcasted_iota(jnp.int32, (rows, 2 * w), 1)
    dist = (w + qi) - sj
    valid = (dist >= 0) & (dist <= w)
    if first_block is not None:
        valid = valid & ((sj >= w) | jnp.logical_not(first_block))
    masked_dist = jnp.where(valid, dist.astype(F32), jnp.inf)
    for kv in range(SWA_KV_HEADS):
        c0, c1 = kv * dh, (kv + 1) * dh
        heads = range(kv * SWA_GROUP, (kv + 1) * SWA_GROUP)
        q = jnp.concatenate([q_ref[:, h * dh:(h + 1) * dh] for h in heads], axis=0)
        q = (q * (dh ** -0.5)).astype(BF16)
        s_all = lax.dot_general(q, k_all[:, c0:c1], _NT, preferred_element_type=F32)
        probs, dens = [], []
        for j, h in enumerate(heads):
            sink = sink_ref[h]
            s = s_all[j * rows:(j + 1) * rows] - _alibi_slope(h) * masked_dist
            m = jnp.maximum(jnp.max(s, axis=1, keepdims=True), sink)
            e = jnp.exp(s - m)
            dens.append(jnp.sum(e, axis=1, keepdims=True) + jnp.exp(sink - m))
            probs.append(e.astype(BF16))
        o_all = jnp.dot(jnp.concatenate(probs, axis=0), v_all[:, c0:c1], preferred_element_type=F32)
        for j, h in enumerate(heads):
            o_ref[:, h * dh:(h + 1) * dh] = (o_all[j * rows:(j + 1) * rows] / dens[j]).astype(o_ref.dtype)


def _attn_prompt_kernel(sink_ref, q_ref, kp_ref, kc_ref, vp_ref, vc_ref, o_ref):
    n = pl.program_id(1)
    k_all = jnp.concatenate([kp_ref[...], kc_ref[...]], axis=0).astype(BF16)
    v_all = jnp.concatenate([vp_ref[...], vc_ref[...]], axis=0).astype(BF16)
    _attn_core(q_ref, k_all, v_all, sink_ref, o_ref, rows=SWA_WINDOW, first_block=(n == 0))


def _attn_sample_kernel(sink_ref, q_ref, k_ref, v_ref, o_ref, *, rows):
    _attn_core(q_ref.at[0], k_ref[0].astype(BF16), v_ref[0].astype(BF16), sink_ref, o_ref.at[0],
               rows=rows, first_block=None)


def _swa_layer(xt, cache_k, cache_v, w_qkv, w_o, sinks, g, b):
    w = SWA_WINDOW
    qkv = _matmul(xt, w_qkv.astype(BF16), tm=640, tn=2560)
    nb = SEQ // w
    k_col = D_MODEL // SWA_KV_DIM
    cur = lambda bb, n, *_: (bb * nb + n, k_col)
    prev = lambda bb, n, *_: (jnp.maximum(bb * nb + n - 1, 0), k_col)
    cur_v = lambda bb, n, *_: (bb * nb + n, k_col + 1)
    prev_v = lambda bb, n, *_: (jnp.maximum(bb * nb + n - 1, 0), k_col + 1)
    o = pl.pallas_call(
        _attn_prompt_kernel,
        grid_spec=pltpu.PrefetchScalarGridSpec(
            num_scalar_prefetch=1,
            grid=(BATCH, nb),
            in_specs=[pl.BlockSpec((w, D_MODEL), lambda bb, n, *_: (bb * nb + n, 0)),
                      pl.BlockSpec((w, SWA_KV_DIM), prev),
                      pl.BlockSpec((w, SWA_KV_DIM), cur),
                      pl.BlockSpec((w, SWA_KV_DIM), prev_v),
                      pl.BlockSpec((w, SWA_KV_DIM), cur_v)],
            out_specs=pl.BlockSpec((w, D_MODEL), lambda bb, n, *_: (bb * nb + n, 0))),
        out_shape=jax.ShapeDtypeStruct((N_TOK, D_MODEL), BF16),
        compiler_params=_params("parallel", "arbitrary"),
        name="attn_prompt",
    )(sinks, qkv, qkv, qkv, qkv, qkv)

    rows = 16
    qkv_s = qkv[N_PROMPT_TOK:].reshape(DEC_BATCH, DEC_SEQ, -1)
    q_s = jnp.pad(qkv_s[..., :D_MODEL], ((0, 0), (0, rows - DEC_SEQ), (0, 0)))
    k_new = qkv_s[..., D_MODEL:D_MODEL + SWA_KV_DIM]
    v_new = qkv_s[..., D_MODEL + SWA_KV_DIM:]
    kz = jnp.concatenate([cache_k.reshape(DEC_BATCH, w, SWA_KV_DIM), k_new], axis=1)
    vz = jnp.concatenate([cache_v.reshape(DEC_BATCH, w, SWA_KV_DIM), v_new], axis=1)
    pad_keys = ((0, 0), (0, w - DEC_SEQ), (0, 0))
    o_s = pl.pallas_call(
        functools.partial(_attn_sample_kernel, rows=rows),
        grid_spec=pltpu.PrefetchScalarGridSpec(
            num_scalar_prefetch=1,
            grid=(DEC_BATCH,),
            in_specs=[pl.BlockSpec((1, rows, D_MODEL), lambda bb, *_: (bb, 0, 0)),
                      pl.BlockSpec((1, 2 * w, SWA_KV_DIM), lambda bb, *_: (bb, 0, 0)),
                      pl.BlockSpec((1, 2 * w, SWA_KV_DIM), lambda bb, *_: (bb, 0, 0))],
            out_specs=pl.BlockSpec((1, rows, D_MODEL), lambda bb, *_: (bb, 0, 0))),
        out_shape=jax.ShapeDtypeStruct((DEC_BATCH, rows, D_MODEL), F32),
        compiler_params=_params("parallel"),
        name="attn_sample",
    )(sinks, q_s, jnp.pad(kz, pad_keys), jnp.pad(vz, pad_keys))
    o = lax.dynamic_update_slice(o, o_s[:, :DEC_SEQ].reshape(N_SAMPLE_TOK, D_MODEL).astype(BF16),
                                 (N_PROMPT_TOK, 0))

    out = _matmul_ln(o, w_o.astype(BF16), xt, g, b)

    kv_shape = (SWA_WINDOW, SWA_KV_HEADS, SWA_HEAD_DIM)
    kv_p = jnp.stack([qkv[(bb + 1) * SEQ - w:(bb + 1) * SEQ, D_MODEL:] for bb in range(BATCH)])
    new_k_p = kv_p[..., :SWA_KV_DIM].reshape((BATCH,) + kv_shape)
    new_v_p = kv_p[..., SWA_KV_DIM:].reshape((BATCH,) + kv_shape)
    new_k_s = kz[:, DEC_SEQ:].reshape((DEC_BATCH,) + kv_shape)
    new_v_s = vz[:, DEC_SEQ:].reshape((DEC_BATCH,) + kv_shape)
    return out, new_k_p, new_v_p, new_k_s, new_v_s


def _log_sigmoid(x):
    return jnp.minimum(x, 0.0) - jnp.log(1.0 + jnp.exp(-jnp.abs(x)))


def _mlstm_kernel(bias_ref, q_ref, k_ref, v_ref, og_ref, gates_ref, c0_ref, n0_ref, m0_ref, ng_ref,
                  h_ref, c_ref, n_ref, m_ref, c_scr, n_scr, m_scr, *, chunk, n_valid, group):
    step = pl.program_id(1)

    @pl.when(step == 0)
    def _():
        c_scr[...] = c0_ref[0]
        n_scr[...] = n0_ref[0]
        m_scr[...] = m0_ref[0]

    heads = [_mlstm_head(head, step, bias_ref, q_ref, k_ref, v_ref, og_ref, gates_ref, ng_ref, h_ref, c_scr, n_scr,
                         m_scr, chunk=chunk, n_valid=n_valid) for head in range(MLSTM_HEADS)]
    for h0 in range(0, MLSTM_HEADS, group):
        for _ in range(MLSTM_PHASES):
            for head_phases in heads[h0:h0 + group]:
                next(head_phases)

    @pl.when(step == pl.num_programs(1) - 1)
    def _():
        c_ref[0] = c_scr[...]
        n_ref[0] = n_scr[...]
        m_ref[0] = m_scr[...]


def _mlstm_head(head, step, bias_ref, q_ref, k_ref, v_ref, og_ref, gates_ref, ng_ref, h_ref, c_scr, n_scr, m_scr,
                *, chunk, n_valid):
    ln = chunk
    cols = slice(head * MLSTM_HEAD_DIM, (head + 1) * MLSTM_HEAD_DIM)
    ig = gates_ref[0, head, pl.ds(step, 1), :] + bias_ref[head]
    lf = _log_sigmoid(gates_ref[0, MLSTM_HEADS + head, pl.ds(step, 1), :] + bias_ref[MLSTM_HEADS + head])
    if n_valid < ln:
        col = lax.broadcasted_iota(jnp.int32, (1, ln), 1)
        ig = jnp.where(col < n_valid, ig, -1e30)
        lf = jnp.where(col < n_valid, lf, 0.0)

    ri = lax.broadcasted_iota(jnp.int32, (ln, ln), 0)
    ci = lax.broadcasted_iota(jnp.int32, (ln, ln), 1)
    eye = ri == ci
    causal = ci <= ri

    def to_col(row):
        return jnp.sum(jnp.where(eye, row, 0.0), axis=1, keepdims=True)

    f_col = to_col(lf)
    b_col = jnp.sum(jnp.where(causal, lf, 0.0), axis=1, keepdims=True)
    b_row = jnp.sum(jnp.where(ri <= ci, f_col, 0.0), axis=0, keepdims=True)
    b_last = jnp.sum(lf, axis=1, keepdims=True)
    m_prev = m_scr[head, :, 0:1]
    a_col = b_col + m_prev
    dmat = jnp.where(causal, b_col - b_row + ig, -jnp.inf)
    mt = jnp.maximum(a_col, jnp.max(dmat, axis=1, keepdims=True))
    w_inter = jnp.exp(a_col - mt)
    yield

    q = q_ref[:, cols]
    k = k_ref[:, cols] * (MLSTM_HEAD_DIM ** -0.5)
    v = v_ref[:, cols]
    qb = q.astype(BF16)
    kb = k.astype(BF16)
    c_prev = c_scr[head]
    n_prev = n_scr[head]
    scores = lax.dot_general(qb, kb, _NT, preferred_element_type=F32)
    yield
    w_intra = jnp.exp(dmat - mt) * scores
    num = (w_inter * lax.dot_general(qb, c_prev.astype(BF16), _NT, preferred_element_type=F32)
           + jnp.dot(w_intra.astype(BF16), v.astype(BF16), preferred_element_type=F32))
    den = (w_inter * jnp.sum(q * n_prev, axis=1, keepdims=True)
           + jnp.sum(w_intra, axis=1, keepdims=True))
    h = num / jnp.maximum(jnp.abs(den), jnp.exp(-mt))
    yield

    g_row = b_last - b_row + ig
    m_new = jnp.maximum(b_last + m_prev, jnp.max(g_row, axis=1, keepdims=True))
    decay = jnp.exp(b_last + m_prev - m_new)
    wg_col = to_col(jnp.exp(g_row - m_new))
    c_new = decay * c_prev + lax.dot_general((v * wg_col).astype(BF16), kb, _TN, preferred_element_type=F32)
    n_new = decay * n_prev + jnp.sum(wg_col * k, axis=0, keepdims=True)
    c_scr[head] = c_new
    n_scr[head] = n_new
    m_scr[head] = jnp.broadcast_to(m_new, (1, LANES))
    yield

    mu = jnp.mean(h, axis=1, keepdims=True)
    hc = h - mu
    var = jnp.mean(hc * hc, axis=1, keepdims=True)
    hn = hc * lax.rsqrt(var + LN_EPS) * ng_ref[:, cols] * jax.nn.sigmoid(og_ref[:, cols])
    h_ref[:, cols] = hn.astype(h_ref.dtype)
    yield


def _mlstm_scan(proj, gates, b_gates, norm_g, c0, n0, m0, *, batch, n_chunks, chunk, n_valid, out_rows, out_dtype,
                group):
    d, dh, nh = D_MODEL, MLSTM_HEAD_DIM, MLSTM_HEADS
    rows = lambda off: (lambda bb, cc, *_: (bb * n_chunks + cc, off))
    per_seq = lambda bb, cc, *_: (bb, 0, 0, 0)
    return pl.pallas_call(
        functools.partial(_mlstm_kernel, chunk=chunk, n_valid=n_valid, group=group),
        grid_spec=pltpu.PrefetchScalarGridSpec(
            num_scalar_prefetch=1,
            grid=(batch, n_chunks),
            in_specs=[pl.BlockSpec((chunk, d), rows(0)),
                      pl.BlockSpec((chunk, d), rows(1)),
                      pl.BlockSpec((chunk, d), rows(2)),
                      pl.BlockSpec((chunk, d), rows(3)),
                      pl.BlockSpec((1, 2 * nh, n_chunks, chunk), per_seq),
                      pl.BlockSpec((1, nh, dh, dh), per_seq),
                      pl.BlockSpec((1, nh, 1, dh), per_seq),
                      pl.BlockSpec((1, nh, 1, LANES), per_seq),
                      pl.BlockSpec((1, d), lambda bb, cc, *_: (0, 0))],
            out_specs=[pl.BlockSpec((chunk, d), lambda bb, cc, *_: (bb * n_chunks + cc, 0)),
                       pl.BlockSpec((1, nh, dh, dh), per_seq),
                       pl.BlockSpec((1, nh, 1, dh), per_seq),
                       pl.BlockSpec((1, nh, 1, LANES), per_seq)],
            scratch_shapes=[pltpu.VMEM((nh, dh, dh), F32), pltpu.VMEM((nh, 1, dh), F32),
                            pltpu.VMEM((nh, 1, LANES), F32)]),
        out_shape=[jax.ShapeDtypeStruct((out_rows, d), out_dtype),
                   jax.ShapeDtypeStruct((batch, nh, dh, dh), F32),
                   jax.ShapeDtypeStruct((batch, nh, 1, dh), F32),
                   jax.ShapeDtypeStruct((batch, nh, 1, LANES), F32)],
        compiler_params=_params("parallel", "arbitrary"),
        name="mlstm_scan",
    )(b_gates, proj, proj, proj, proj, gates, c0, n0[:, :, None, :],
      jnp.broadcast_to(m0[:, :, None, None], (batch, nh, 1, LANES)), norm_g)


def _mlstm_layer(xt, c0_s, n0_s, m0_s, w_in, b_gates, norm_g, w_out, g, b):
    d, nh, dh = D_MODEL, MLSTM_HEADS, MLSTM_HEAD_DIM
    proj = _matmul(xt, w_in[:, :4 * d].astype(BF16), tm=640, tn=2048)
    w_gates = jnp.pad(w_in[:, 4 * d:], ((0, 0), (0, 128 - 2 * nh)))
    gate_pre = _matmul(xt, w_gates.astype(BF16), tm=640, tn=128)[:, :2 * nh]

    nc = SEQ // MLSTM_CHUNK
    gates_p = gate_pre[:N_PROMPT_TOK].reshape(BATCH, nc, MLSTM_CHUNK, 2 * nh).transpose(0, 3, 1, 2)
    zeros = lambda *s: jnp.zeros(s, F32)
    hn, c_p, n_p, m_p = _mlstm_scan(
        proj, gates_p, b_gates, norm_g, zeros(BATCH, nh, dh, dh), zeros(BATCH, nh, dh), zeros(BATCH, nh),
        batch=BATCH, n_chunks=nc, chunk=MLSTM_CHUNK, n_valid=MLSTM_CHUNK, out_rows=N_TOK, out_dtype=BF16, group=2)

    ls = MLSTM_SAMPLE_CHUNK
    pad_t = ((0, 0), (0, ls - DEC_SEQ), (0, 0))
    proj_s = jnp.pad(proj[N_PROMPT_TOK:].reshape(DEC_BATCH, DEC_SEQ, 4 * d), pad_t).reshape(DEC_BATCH * ls, 4 * d)
    gates_s = jnp.pad(gate_pre[N_PROMPT_TOK:].reshape(DEC_BATCH, DEC_SEQ, 2 * nh), pad_t)
    gates_s = gates_s.transpose(0, 2, 1)[:, :, None, :]
    hn_s, c_s, n_s, m_s = _mlstm_scan(
        proj_s, gates_s, b_gates, norm_g, c0_s, n0_s, m0_s,
        batch=DEC_BATCH, n_chunks=1, chunk=ls, n_valid=DEC_SEQ, out_rows=DEC_BATCH * ls, out_dtype=F32,
        group=MLSTM_HEADS)
    hn_s = hn_s.reshape(DEC_BATCH, ls, d)[:, :DEC_SEQ].reshape(N_SAMPLE_TOK, d).astype(BF16)
    hn = lax.dynamic_update_slice(hn, hn_s, (N_PROMPT_TOK, 0))

    out = _matmul_ln(hn, w_out.astype(BF16), xt, g, b)
    return (out, c_p, n_p[:, :, 0], m_p[:, :, 0, 0], c_s, n_s[:, :, 0], m_s[:, :, 0, 0])


def _router_kernel(x_ref, wt_ref, bias_ref, tri_ref, eidx_ref, gate_ref, rank_ref, count_ref, xq_ref, count_scr,
                   *, tt):
    neg = -jnp.inf
    x = x_ref[...]
    logits = lax.dot_general(wt_ref[...].astype(BF16), x.astype(BF16), _NT, preferred_element_type=F32)
    s = jax.nn.sigmoid(logits)
    sb = s + bias_ref[...]
    per_group = N_EXPERTS // N_EXPERT_GROUPS

    sb3 = sb.reshape(N_EXPERT_GROUPS, per_group, tt)
    i3 = lax.broadcasted_iota(jnp.int32, sb3.shape, 1)
    m1 = jnp.max(sb3, axis=1, keepdims=True)
    first = jnp.min(jnp.where(sb3 == m1, i3, per_group), axis=1, keepdims=True)
    m2 = jnp.max(jnp.where(i3 == first, neg, sb3), axis=1, keepdims=True)
    gscore = (m1 + m2).reshape(N_EXPERT_GROUPS, tt)

    gi = lax.broadcasted_iota(jnp.int32, gscore.shape, 0)
    gsel = jnp.zeros(gscore.shape, jnp.bool_)
    cur = gscore
    for _ in range(TOPK_GROUPS):
        mx = jnp.max(cur, axis=0, keepdims=True)
        pick = gi == jnp.min(jnp.where(cur == mx, gi, N_EXPERT_GROUPS), axis=0, keepdims=True)
        gsel = gsel | pick
        cur = jnp.where(pick, neg, cur)
    emask = jnp.broadcast_to(gsel.reshape(N_EXPERT_GROUPS, 1, tt), sb3.shape).reshape(N_EXPERTS, tt)

    ei = lax.broadcasted_iota(jnp.int32, sb.shape, 0)
    sel = jnp.zeros(sb.shape, jnp.bool_)
    cur = jnp.where(emask, sb, neg)
    picks = []
    for _ in range(TOP_K):
        mx = jnp.max(cur, axis=0, keepdims=True)
        idx = jnp.min(jnp.where(cur == mx, ei, N_EXPERTS), axis=0, keepdims=True)
        pick = ei == idx
        picks.append((idx, pick))
        sel = sel | pick
        cur = jnp.where(pick, neg, cur)
    s_sel = jnp.where(sel, s, 0.0)
    gate = s_sel / jnp.sum(s_sel, axis=0, keepdims=True) * ROUTED_SCALE

    @pl.when(pl.program_id(0) == 0)
    def _():
        count_scr[...] = jnp.zeros_like(count_scr)

    sel_f = sel.astype(F32)
    incl = jnp.dot(sel_f.astype(BF16), tri_ref[...], preferred_element_type=F32)
    rank = count_scr[:, 0:1] + incl - sel_f
    count_scr[...] = count_scr[...] + jnp.sum(sel_f, axis=1, keepdims=True)
    count_ref[...] = count_scr[...].astype(jnp.int32)

    for kk, (idx, pick) in enumerate(picks):
        eidx_ref[kk:kk + 1, :] = idx
        gate_ref[kk:kk + 1, :] = jnp.sum(jnp.where(pick, gate, 0.0), axis=0, keepdims=True)
        rank_ref[kk:kk + 1, :] = jnp.sum(jnp.where(pick, rank, 0.0), axis=0, keepdims=True).astype(jnp.int32)

    for sl in range(PACK_ROWS):
        c0 = 2 * LANES * sl
        xq_ref[pl.ds(sl, tt, stride=PACK_ROWS), :] = _pack_bf16_pair(x[:, c0:c0 + LANES],
                                                                     x[:, c0 + LANES:c0 + 2 * LANES])


def _pack_bf16_pair(hi, lo):
    hb = lax.bitcast_convert_type(hi.astype(BF16).astype(F32), jnp.uint32)
    lb = lax.bitcast_convert_type(lo.astype(BF16).astype(F32), jnp.uint32)
    return hb | (lb >> 16)


def _unpack_bf16_pair(word):
    hi = lax.bitcast_convert_type(word & jnp.uint32(0xFFFF0000), F32)
    lo = lax.bitcast_convert_type(word << 16, F32)
    return hi, lo


def _moe_route(xt, w_router, bias):
    tt = ROUTER_TILE
    col = lambda i: (0, i)
    const = lambda i: (0, 0)
    tri = (jnp.arange(tt)[:, None] <= jnp.arange(tt)[None, :]).astype(BF16)
    return pl.pallas_call(
        functools.partial(_router_kernel, tt=tt),
        grid=(N_TOK // tt,),
        in_specs=[pl.BlockSpec((tt, D_MODEL), lambda i: (i, 0)),
                  pl.BlockSpec((N_EXPERTS, D_MODEL), const),
                  pl.BlockSpec((N_EXPERTS, 1), const),
                  pl.BlockSpec((tt, tt), const)],
        out_specs=[pl.BlockSpec((TOP_K, tt), col), pl.BlockSpec((TOP_K, tt), col), pl.BlockSpec((TOP_K, tt), col),
                   pl.BlockSpec((N_EXPERTS, LANES), const),
                   pl.BlockSpec((tt * PACK_ROWS, LANES), lambda i: (i, 0))],
        out_shape=[jax.ShapeDtypeStruct((TOP_K, N_TOK), jnp.int32),
                   jax.ShapeDtypeStruct((TOP_K, N_TOK), F32),
                   jax.ShapeDtypeStruct((TOP_K, N_TOK), jnp.int32),
                   jax.ShapeDtypeStruct((N_EXPERTS, LANES), jnp.int32),
                   jax.ShapeDtypeStruct((N_TOK * PACK_ROWS, LANES), jnp.uint32)],
        scratch_shapes=[pltpu.VMEM((N_EXPERTS, LANES), F32)],
        compiler_params=_params("arbitrary"),
        name="moe_router",
    )(xt, w_router.T, bias[:, None], tri)


def _sorted_row(pstart_ref, eidx_ref, rank_ref, j):
    return pstart_ref[eidx_ref[0, 0, j]] + rank_ref[0, 0, j]


def _dispatch_kernel(pstart_ref, eidx_ref, rank_ref, xq_ref, xs_hbm, row_ref, sem, *, tt):
    def issue(t, carry):
        src = xq_ref.at[pl.ds(pl.multiple_of(t * PACK_ROWS, PACK_ROWS), PACK_ROWS)]
        for kk in range(TOP_K):
            row = _sorted_row(pstart_ref, eidx_ref, rank_ref, t * TOP_K + kk)
            row_ref[0, 0, t * TOP_K + kk] = row
            row = pl.multiple_of(row * PACK_ROWS, PACK_ROWS)
            pltpu.make_async_copy(src, xs_hbm.at[pl.ds(row, PACK_ROWS)], sem).start(priority=kk % 2)
        return carry
    lax.fori_loop(0, tt, issue, 0)
    for _ in range(TOP_K):
        pltpu.make_async_copy(xq_ref, xs_hbm.at[pl.ds(0, tt * PACK_ROWS)], sem).wait()


def _moe_dispatch(xq, pstarts, eidx_tok, rank_tok):
    tt = DISPATCH_TILE
    n_tiles = N_TOK // tt
    slots = pl.BlockSpec((1, 1, tt * TOP_K), lambda i, *_: (i, 0, 0), memory_space=pltpu.SMEM)
    return pl.pallas_call(
        functools.partial(_dispatch_kernel, tt=tt),
        grid_spec=pltpu.PrefetchScalarGridSpec(
            num_scalar_prefetch=1,
            grid=(n_tiles,),
            in_specs=[slots, slots, pl.BlockSpec((tt * PACK_ROWS, LANES), lambda i, *_: (i, 0))],
            out_specs=[pl.BlockSpec(memory_space=pl.ANY), slots],
            scratch_shapes=[pltpu.SemaphoreType.DMA(())]),
        out_shape=[jax.ShapeDtypeStruct((MOE_ROWS * PACK_ROWS, LANES), jnp.uint32),
                   jax.ShapeDtypeStruct((n_tiles, 1, tt * TOP_K), jnp.int32)],
        compiler_params=_params("arbitrary"),
        name="moe_dispatch",
    )(pstarts, eidx_tok.reshape(n_tiles, 1, tt * TOP_K), rank_tok.reshape(n_tiles, 1, tt * TOP_K), xq)


def _experts_kernel(be_ref, next_ref, nvalid_ref, nused_ref, xs_ref, wg_hbm, wu_hbm, wd_hbm, y_ref,
                    wg_f, wu_f, wd_f, wgu_s, wd_s, sem, *, layer):
    i = pl.program_id(0)
    tm = MOE_BLOCK

    def weight_copies(e):
        return (pltpu.make_async_copy(wg_hbm.at[layer, e], wg_f, sem.at[0]),
                pltpu.make_async_copy(wu_hbm.at[layer, e], wu_f, sem.at[1]),
                pltpu.make_async_copy(wd_hbm.at[layer, e], wd_f, sem.at[2]))

    def swiglu_rows(n_rows):
        parts = []
        for sl in range(PACK_ROWS):
            parts.extend(_unpack_bf16_pair(xs_ref[pl.ds(sl, n_rows, stride=PACK_ROWS), :]))
        x = jnp.concatenate(parts, axis=1).astype(BF16)
        gate_up = jnp.dot(x, wgu_s[...], preferred_element_type=F32)
        hidden = _silu(gate_up[:, :EXPERT_FF]) * gate_up[:, EXPERT_FF:]
        y = jnp.dot(hidden.astype(BF16), wd_s[...], preferred_element_type=F32)
        for sl in range(PACK_ROWS):
            c0 = 2 * LANES * sl
            y_ref[pl.ds(sl, n_rows, stride=PACK_ROWS), :] = _pack_bf16_pair(y[:, c0:c0 + LANES],
                                                                            y[:, c0 + LANES:c0 + 2 * LANES])

    @pl.when(i == 0)
    def _():
        for cp in weight_copies(be_ref[0]):
            cp.start()

    @pl.when(i < nused_ref[0])
    def _():
        first_of_expert = (i == 0) | (be_ref[i] != be_ref[jnp.maximum(i - 1, 0)])

        @pl.when(first_of_expert)
        def _():
            for cp in weight_copies(be_ref[i]):
                cp.wait()
            wgu_s[:, :EXPERT_FF] = wg_f[...].astype(BF16)
            wgu_s[:, EXPERT_FF:] = wu_f[...].astype(BF16)
            wd_s[...] = wd_f[...].astype(BF16)

            @pl.when(next_ref[i] >= 0)
            def _():
                for cp in weight_copies(next_ref[i]):
                    cp.start()

        @pl.when(nvalid_ref[i] > tm // 2)
        def _():
            swiglu_rows(tm)

        @pl.when(nvalid_ref[i] <= tm // 2)
        def _():
            swiglu_rows(tm // 2)


def _moe_experts(xs, block_e, next_e, n_valid, n_used, layer, w_gate, w_up, w_down):
    last = lambda i, be, nx, nv, nu: (jnp.minimum(i, nu[0] - 1), 0)
    return pl.pallas_call(
        functools.partial(_experts_kernel, layer=layer),
        grid_spec=pltpu.PrefetchScalarGridSpec(
            num_scalar_prefetch=4,
            grid=(MOE_N_BLOCKS,),
            in_specs=[pl.BlockSpec((MOE_BLOCK * PACK_ROWS, LANES), last),
                      pl.BlockSpec(memory_space=pl.ANY),
                      pl.BlockSpec(memory_space=pl.ANY),
                      pl.BlockSpec(memory_space=pl.ANY)],
            out_specs=pl.BlockSpec((MOE_BLOCK * PACK_ROWS, LANES), last),
            scratch_shapes=[pltpu.VMEM((D_MODEL, EXPERT_FF), F32), pltpu.VMEM((D_MODEL, EXPERT_FF), F32),
                            pltpu.VMEM((EXPERT_FF, D_MODEL), F32),
                            pltpu.VMEM((D_MODEL, 2 * EXPERT_FF), BF16),
                            pltpu.VMEM((EXPERT_FF, D_MODEL), BF16),
                            pltpu.SemaphoreType.DMA((3,))]),
        out_shape=jax.ShapeDtypeStruct((MOE_ROWS * PACK_ROWS, LANES), jnp.uint32),
        compiler_params=_params("arbitrary"),
        name="moe_experts",
    )(block_e, next_e, n_valid, n_used, xs, w_gate, w_up, w_down)


def _combine_kernel(row_ref, row_next_ref, x_ref, gate_ref, swg_ref, swu_ref, swd_ref, g_ref, b_ref, y_hbm, *rest,
                    tt, n_first):
    out_refs, (buf_a, buf_b, acc, sem) = rest[:-4], rest[-4:]
    i = pl.program_id(0)
    last = pl.num_programs(0) - 1

    def start_fetch(rows, into, into_sem, t):
        dst = pl.ds(t * PACK_ROWS, PACK_ROWS)
        for kk in range(TOP_K):
            src_row = pl.multiple_of(rows[0, 0, t * TOP_K + kk] * PACK_ROWS, PACK_ROWS)
            pltpu.make_async_copy(y_hbm.at[pl.ds(src_row, PACK_ROWS)], into.at[kk, dst],
                                  into_sem).start(priority=kk % 2)

    def wait_fetch(into, into_sem):
        for kk in range(TOP_K):
            pltpu.make_async_copy(y_hbm.at[pl.ds(0, tt * PACK_ROWS)], into.at[kk], into_sem).wait()

    @pl.when(i == 0)
    def _():
        def body(t, carry):
            dst = pl.ds(pl.multiple_of(t * PACK_ROWS, PACK_ROWS), PACK_ROWS)
            for kk in range(TOP_K):
                src_row = pl.multiple_of(row_ref[0, 0, t * TOP_K + kk] * PACK_ROWS, PACK_ROWS)
                pltpu.make_async_copy(y_hbm.at[pl.ds(src_row, PACK_ROWS)], buf_a.at[kk, dst],
                                      sem.at[0]).start(priority=kk % 2)
            return carry
        lax.fori_loop(0, tt, body, 0)

    def tile(cur, cur_sem, nxt, nxt_sem):
        per_phase = tt // (PACK_ROWS + 2)
        tokens = iter(range(tt))

        def issue(n):
            for _ in range(n):
                t = next(tokens, None)
                if t is not None:
                    start_fetch(row_next_ref, nxt, nxt_sem, t)

        x = x_ref[...]
        xb = x.astype(BF16)
        issue(per_phase)
        hidden = (_silu(jnp.dot(xb, swg_ref[...], preferred_element_type=F32))
                  * jnp.dot(xb, swu_ref[...], preferred_element_type=F32))
        issue(per_phase)
        shared = jnp.dot(hidden.astype(BF16), swd_ref[...], preferred_element_type=F32)
        wait_fetch(cur, cur_sem)
        gate = gate_ref[...]
        for sl in range(PACK_ROWS):
            issue(per_phase if sl < PACK_ROWS - 1 else tt)
            routed_hi = routed_lo = None
            for kk in range(TOP_K):
                hi, lo = _unpack_bf16_pair(cur[kk, pl.ds(sl, tt, stride=PACK_ROWS), :])
                g_k = gate[:, kk:kk + 1]
                routed_hi = g_k * hi if kk == 0 else routed_hi + g_k * hi
                routed_lo = g_k * lo if kk == 0 else routed_lo + g_k * lo
            for half, routed in enumerate((routed_hi, routed_lo)):
                cols = slice((2 * sl + half) * LANES, (2 * sl + half + 1) * LANES)
                acc[:, cols] = DN_ALPHA * x[:, cols] + (routed + shared[:, cols])
        res = _layer_norm(acc[...], g_ref[...], b_ref[...])
        if n_first is None:
            out_refs[0][...] = res
        else:
            @pl.when(i < n_first)
            def _():
                out_refs[0][...] = res

            @pl.when(i >= n_first)
            def _():
                out_refs[1][...] = res

        @pl.when(i == last)
        def _():
            wait_fetch(nxt, nxt_sem)

    @pl.when(i % 2 == 0)
    def _():
        tile(buf_a, sem.at[0], buf_b, sem.at[1])

    @pl.when(i % 2 == 1)
    def _():
        tile(buf_b, sem.at[1], buf_a, sem.at[0])


def _moe_combine(xt, ys, row_tiles, gate_tok, sw_gate, sw_up, sw_down, g, b, *, split):
    tt = COMBINE_TILE
    n_tiles = N_TOK // tt
    row = lambda i: (i, 0)
    const = lambda i: (0, 0)
    slots = pl.BlockSpec((1, 1, tt * TOP_K), lambda i: (i, 0, 0), memory_space=pltpu.SMEM)
    slots_next = pl.BlockSpec((1, 1, tt * TOP_K), lambda i: (jnp.minimum(i + 1, n_tiles - 1), 0, 0),
                              memory_space=pltpu.SMEM)
    row_tiles = row_tiles.reshape(n_tiles, 1, tt * TOP_K)
    if split:
        n_first = N_PROMPT_TOK // tt
        assert N_SAMPLE_TOK == tt
        out_specs = [pl.BlockSpec((tt, D_MODEL), lambda i: (jnp.minimum(i, n_first - 1), 0)),
                     pl.BlockSpec((tt, D_MODEL), const)]
        out_shape = [jax.ShapeDtypeStruct((N_PROMPT_TOK, D_MODEL), F32),
                     jax.ShapeDtypeStruct((N_SAMPLE_TOK, D_MODEL), F32)]
    else:
        n_first = None
        out_specs = [pl.BlockSpec((tt, D_MODEL), row)]
        out_shape = [jax.ShapeDtypeStruct((N_TOK, D_MODEL), F32)]
    return pl.pallas_call(
        functools.partial(_combine_kernel, tt=tt, n_first=n_first),
        grid=(n_tiles,),
        in_specs=[slots, slots_next,
                  pl.BlockSpec((tt, D_MODEL), row),
                  pl.BlockSpec((tt, TOP_K), row),
                  pl.BlockSpec((D_MODEL, EXPERT_FF), const),
                  pl.BlockSpec((D_MODEL, EXPERT_FF), const),
                  pl.BlockSpec((EXPERT_FF, D_MODEL), const),
                  pl.BlockSpec((1, D_MODEL), const),
                  pl.BlockSpec((1, D_MODEL), const),
                  pl.BlockSpec(memory_space=pl.ANY)],
        out_specs=out_specs,
        scratch_shapes=[pltpu.VMEM((TOP_K, tt * PACK_ROWS, LANES), jnp.uint32),
                        pltpu.VMEM((TOP_K, tt * PACK_ROWS, LANES), jnp.uint32),
                        pltpu.VMEM((tt, D_MODEL), F32), pltpu.SemaphoreType.DMA((2,))],
        out_shape=out_shape,
        compiler_params=_params("arbitrary"),
        name="moe_combine",
    )(row_tiles, row_tiles, xt, gate_tok, sw_gate.astype(BF16), sw_up.astype(BF16), sw_down.astype(BF16), g, b, ys)


def _moe_layer(xt, layer, w_router, bias, w_gate, w_up, w_down, sw_gate, sw_up, sw_down, g, b, *, split):
    eidx, gate, rank, counts, xq = _moe_route(xt, w_router, bias)

    counts = counts[:, 0]
    pcounts = (counts + MOE_BLOCK - 1) // MOE_BLOCK * MOE_BLOCK
    pends = jnp.cumsum(pcounts)
    pstarts = (pends - pcounts).astype(jnp.int32)
    n_used = (pends[-1] // MOE_BLOCK).astype(jnp.int32)
    blk = jnp.minimum(jnp.arange(MOE_N_BLOCKS, dtype=jnp.int32), n_used - 1) * MOE_BLOCK
    block_e = jnp.sum((pends[None, :] <= blk[:, None]).astype(jnp.int32), axis=1)
    block_e = jnp.minimum(block_e, N_EXPERTS - 1)
    after = pends[block_e] // MOE_BLOCK
    next_e = jnp.where(after < n_used, block_e[jnp.minimum(after, MOE_N_BLOCKS - 1)], -1).astype(jnp.int32)

    n_valid = jnp.clip(counts[block_e] - (blk - pstarts[block_e]), 0, MOE_BLOCK).astype(jnp.int32)

    eidx_tok, rank_tok = eidx.T, rank.T
    xs, rows = _moe_dispatch(xq, pstarts, eidx_tok, rank_tok)
    ys = _moe_experts(xs, block_e, next_e, n_valid, n_used[None], layer, w_gate, w_up, w_down)
    return _moe_combine(xt, ys, rows, gate.T, sw_gate, sw_up, sw_down, g, b, split=split)


def kernel(x_prompt, x_sample, state_pool, cache_swa_k, cache_swa_v, state_mlstm_c, state_mlstm_n, state_mlstm_m, pool_w, pool_scale, swa_w_qkv, swa_w_o, swa_sinks, mlstm_w_in, mlstm_b_gates, mlstm_norm_g, mlstm_w_out, ln_g, ln_b, moe_w_router, moe_router_bias, moe_w_gate, moe_w_up, moe_w_down, moe_shared_w_gate, moe_shared_w_up, moe_shared_w_down):
    d = D_MODEL
    xt = None
    pool_p, pool_s = [], []
    swk_p, swv_p, swk_s, swv_s = [], [], [], []
    mc_p, mn_p, mm_p, mc_s, mn_s, mm_s = [], [], [], [], [], []
    for i in range(DEPTH):
        kind, slot = i % N_MIXERS, i // N_MIXERS
        g0, b0 = ln_g[i, 0][None], ln_b[i, 0][None]
        if kind == 0:
            if i == 0:
                xp, xs = x_prompt.reshape(N_PROMPT_TOK, d), x_sample
            else:
                xp, xs = xt, xt[N_PROMPT_TOK:].reshape(DEC_BATCH, DEC_SEQ, d)
            xt, sp, ss = _pool_layer(xp, xs, state_pool[slot], pool_w[slot], pool_scale[slot][None], g0, b0)
            pool_p.append(sp)
            pool_s.append(ss)
        elif kind == 1:
            xt, kp, vp, ks, vs = _swa_layer(xt, cache_swa_k[slot], cache_swa_v[slot], swa_w_qkv[slot],
                                            swa_w_o[slot], swa_sinks[slot], g0, b0)
            swk_p.append(kp)
            swv_p.append(vp)
            swk_s.append(ks)
            swv_s.append(vs)
        else:
            xt, cp, np_, mp, cs, ns, ms = _mlstm_layer(
                xt, state_mlstm_c[slot], state_mlstm_n[slot], state_mlstm_m[slot], mlstm_w_in[slot],
                mlstm_b_gates[slot], mlstm_norm_g[slot][None], mlstm_w_out[slot], g0, b0)
            mc_p.append(cp)
            mn_p.append(np_)
            mm_p.append(mp)
            mc_s.append(cs)
            mn_s.append(ns)
            mm_s.append(ms)
        xt = _moe_layer(xt, i, moe_w_router[i], moe_router_bias[i], moe_w_gate, moe_w_up, moe_w_down,
                        moe_shared_w_gate[i], moe_shared_w_up[i], moe_shared_w_down[i],
                        ln_g[i, 1][None], ln_b[i, 1][None], split=(i == DEPTH - 1))
        if i < DEPTH - 1:
            xt = xt[0]
    y_p = xt[0].reshape(BATCH, SEQ, d)
    y_s = xt[1].reshape(DEC_BATCH, DEC_SEQ, d)
    return (y_p, y_s, jnp.stack(pool_p), jnp.stack(pool_s), jnp.stack(swk_p), jnp.stack(swv_p),
            jnp.stack(swk_s), jnp.stack(swv_s), jnp.stack(mc_p), jnp.stack(mn_p), jnp.stack(mm_p),
            jnp.stack(mc_s), jnp.stack(mn_s), jnp.stack(mm_s))
```
